```python
import numpy as np
import jax
import jax.numpy as jnp
from jax import lax

D_MODEL = 1024
BATCH = 8
SEQ = 4096
DEPTH = 2

N_GROUPS = 4
HEAD_DIM = 64
GROUP_HEADS = D_MODEL // (N_GROUPS * HEAD_DIM)
GROUP_WIDTH = GROUP_HEADS * HEAD_DIM
MIX_WIDTH = N_GROUPS * GROUP_WIDTH
D_FF = 256 * ((8 * D_MODEL + 3 * 256 - 1) // (3 * 256))
N_ADA = 9
FFN_RESIDUAL_WEIGHT = 0.5
ROPE_THETA = 10000.0
RMS_EPS = 1e-6
NEG_INF = -1e30

MOBA_BLOCK = 256
MOBA_TOPK = 3
MOBA_QBLK = 32

MLA_Q_LORA = D_MODEL // 4
MLA_KV_LORA = D_MODEL // 8
MLA_NOPE = HEAD_DIM
MLA_ROPE = HEAD_DIM // 2
MLA_V = HEAD_DIM
MLA_QBLK = 128

NSA_CMP_LEN = 32
NSA_CMP_STRIDE = 16
NSA_CMP_HIDDEN = 4 * HEAD_DIM
NSA_SEL_BLOCK = 64
NSA_SEL_TOPN = 16
NSA_WINDOW = 512
NSA_FORCE_SCORE = 1e4
NSA_QBLK = 64

DSA_TOPK = 256
DSA_IDX_HEADS = 8
DSA_IDX_DIM = 32
DSA_QBLK = 64

IN_SIZES = (
    GROUP_WIDTH, GROUP_WIDTH, GROUP_WIDTH,
    MLA_Q_LORA, MLA_KV_LORA, MLA_ROPE,
    GROUP_WIDTH, HEAD_DIM, HEAD_DIM, HEAD_DIM, HEAD_DIM,
    HEAD_DIM, HEAD_DIM, 3 * GROUP_HEADS,
    GROUP_WIDTH, HEAD_DIM, HEAD_DIM,
    DSA_IDX_HEADS * DSA_IDX_DIM, DSA_IDX_DIM, DSA_IDX_HEADS,
)
N_IN = sum(IN_SIZES)

kernel_name = 'hybrid_moba_mla_nsa_dsa_macaron'


def rms_norm(x, g):
    x32 = x.astype(jnp.float32)
    y = x32 * lax.rsqrt(jnp.mean(x32 * x32, axis=-1, keepdims=True) + RMS_EPS)
    return (y * g.astype(jnp.float32)).astype(x.dtype)


def modulate(xn, shift, scale):
    return xn * (1 + scale) + shift


def swiglu(h, w_gate, w_up, w_down):
    return (jax.nn.silu(h @ w_gate) * (h @ w_up)) @ w_down


def rope_tables(n_pos, dim, dtype):
    inv_freq = 1.0 / (ROPE_THETA ** (np.arange(0, dim, 2, dtype=np.float32) / dim))
    ang = jnp.arange(n_pos, dtype=jnp.float32)[:, None] * jnp.asarray(inv_freq, jnp.float32)[None, :]
    return jnp.cos(ang).astype(dtype), jnp.sin(ang).astype(dtype)


def apply_rope(x, cos, sin):
    x1, x2 = jnp.split(x, 2, axis=-1)
    return jnp.concatenate([x1 * cos - x2 * sin, x2 * cos + x1 * sin], axis=-1)


def masked_softmax(s, mask):
    p = jax.nn.softmax(jnp.where(mask, s.astype(jnp.float32), NEG_INF), axis=-1)
    return jnp.where(mask, p, 0.0)


def map_query_blocks(fn, n_pos, qblk):
    starts = jnp.arange(n_pos // qblk, dtype=jnp.int32) * qblk
    out = lax.map(fn, starts)
    out = jnp.moveaxis(out, 0, 1)
    return out.reshape(out.shape[0], n_pos, *out.shape[3:])


def moba_attention(q, k, v, cos, sin):
    B, T, H, dh = q.shape
    q = apply_rope(q, cos[:, None], sin[:, None])
    k = apply_rope(k, cos[:, None], sin[:, None])
    nb = -(-T // MOBA_BLOCK)
    pad = nb * MOBA_BLOCK - T

    def to_blocks(t):
        t = jnp.pad(t, ((0, 0), (0, pad), (0, 0), (0, 0)))
        return t.reshape(B, nb, MOBA_BLOCK, H, dh).transpose(0, 3, 1, 2, 4)

    kb, vb = to_blocks(k), to_blocks(v)
    k_mean = jnp.mean(kb.astype(jnp.float32), axis=3).astype(q.dtype)
    topk = min(MOBA_TOPK, nb - 1)
    scale = dh ** -0.5
    b_ix = jnp.arange(B)[:, None, None, None]
    h_ix = jnp.arange(H)[None, :, None, None]
    blk_ids = jnp.arange(nb)
    offs = jnp.arange(MOBA_BLOCK)

    def block_fn(q0):
        qc = lax.dynamic_slice_in_dim(q, q0, MOBA_QBLK, axis=1)
        tq = q0 + jnp.arange(MOBA_QBLK)
        own = q0 // MOBA_BLOCK
        k_own = lax.dynamic_index_in_dim(kb, own, axis=2, keepdims=False)
        v_own = lax.dynamic_index_in_dim(vb, own, axis=2, keepdims=False)
        s_own = jnp.einsum('bqhd,bhkd->bhqk', qc, k_own) * scale
        m_own = jnp.broadcast_to((own * MOBA_BLOCK + offs)[None, :] <= tq[:, None], s_own.shape)
        if topk == 0:
            p = masked_softmax(s_own, m_own).astype(v.dtype)
            return jnp.einsum('bhqk,bhkd->bqhd', p, v_own)
        gate = jnp.einsum('bqhd,bhnd->bhqn', qc, k_mean).astype(jnp.float32)
        gate = jnp.where(blk_ids < own, gate, NEG_INF)
        _, idx = lax.top_k(gate, topk)
        k_sel = kb[b_ix, h_ix, idx]
        v_sel = vb[b_ix, h_ix, idx]
        s_sel = (jnp.einsum('bqhd,bhqnkd->bhqnk', qc, k_sel) * scale).reshape(B, H, MOBA_QBLK, topk * MOBA_BLOCK)
        m_sel = jnp.broadcast_to((idx < own)[..., None], (B, H, MOBA_QBLK, topk, MOBA_BLOCK)).reshape(s_sel.shape)
        p = masked_softmax(jnp.concatenate([s_sel, s_own], axis=-1),
                           jnp.concatenate([m_sel, m_own], axis=-1)).astype(v.dtype)
        p_sel = p[..., :topk * MOBA_BLOCK].reshape(B, H, MOBA_QBLK, topk, MOBA_BLOCK)
        p_own = p[..., topk * MOBA_BLOCK:]
        return (jnp.einsum('bhqnk,bhqnkd->bqhd', p_sel, v_sel)
                + jnp.einsum('bhqk,bhkd->bqhd', p_own, v_own))

    return map_query_blocks(block_fn, T, MOBA_QBLK)


def mla_attention(c_q, c_kv, k_rope, q_norm, w_uq, kv_norm, w_uk, w_uv, cos_r, sin_r):
    B, T, _ = c_q.shape
    H = GROUP_HEADS
    q = (rms_norm(c_q, q_norm) @ w_uq).reshape(B, T, H, MLA_NOPE + MLA_ROPE)
    q_nope = q[..., :MLA_NOPE]
    q_pe = apply_rope(q[..., MLA_NOPE:], cos_r[:, None], sin_r[:, None])
    ckv = rms_norm(c_kv, kv_norm)
    k_nope = (ckv @ w_uk).reshape(B, T, H, MLA_NOPE)
    v = (ckv @ w_uv).reshape(B, T, H, MLA_V)
    k_pe = apply_rope(k_rope, cos_r, sin_r)
    scale = (MLA_NOPE + MLA_ROPE) ** -0.5
    kpos = jnp.arange(T)

    def block_fn(q0):
        qn = lax.dynamic_slice_in_dim(q_nope, q0, MLA_QBLK, axis=1)
        qp = lax.dynamic_slice_in_dim(q_pe, q0, MLA_QBLK, axis=1)
        tq = q0 + jnp.arange(MLA_QBLK)
        s = (jnp.einsum('bqhd,bshd->bhqs', qn, k_nope)
             + jnp.einsum('bqhr,bsr->bhqs', qp, k_pe)) * scale
        p = masked_softmax(s, kpos[None, :] <= tq[:, None]).astype(v.dtype)
        return jnp.einsum('bhqs,bshd->bqhd', p, v)

    return map_query_blocks(block_fn, T, MLA_QBLK)


def nsa_attention(q, kc, vc, ks, vs, kw, vw, gate_logits, pe_k, pe_v,
                  cmp_k_w1, cmp_k_w2, cmp_v_w1, cmp_v_w2, cos, sin):
    B, T, H, dh = q.shape
    q = apply_rope(q, cos[:, None], sin[:, None])
    kc, ks, kw = (apply_rope(t, cos, sin) for t in (kc, ks, kw))
    n_cmp = (T - NSA_CMP_LEN) // NSA_CMP_STRIDE + 1
    cmp_start = np.arange(n_cmp) * NSA_CMP_STRIDE
    win_idx = cmp_start[:, None] + np.arange(NSA_CMP_LEN)[None, :]

    def compress(t, pe, w1, w2):
        blocks = (t[:, win_idx] + pe).reshape(B, n_cmp, NSA_CMP_LEN * dh)
        return jax.nn.silu(blocks @ w1) @ w2

    k_cmp = compress(kc, pe_k, cmp_k_w1, cmp_k_w2)
    v_cmp = compress(vc, pe_v, cmp_v_w1, cmp_v_w2)
    cmp_end = jnp.asarray(cmp_start + NSA_CMP_LEN - 1)
    n_sel = T // NSA_SEL_BLOCK
    sel_start = np.arange(n_sel) * NSA_SEL_BLOCK
    overlap = ((cmp_start[:, None] < sel_start[None, :] + NSA_SEL_BLOCK)
               & (cmp_start[:, None] + NSA_CMP_LEN > sel_start[None, :]))
    cmp_to_sel = jnp.asarray(overlap, jnp.float32)
    topn = min(NSA_SEL_TOPN, n_sel)
    ksb = ks.reshape(B, n_sel, NSA_SEL_BLOCK, dh)
    vsb = vs.reshape(B, n_sel, NSA_SEL_BLOCK, dh)
    kw_pad = jnp.pad(kw, ((0, 0), (NSA_WINDOW, 0), (0, 0)))
    vw_pad = jnp.pad(vw, ((0, 0), (NSA_WINDOW, 0), (0, 0)))
    gates = jax.nn.sigmoid(gate_logits.reshape(B, T, H, 3))
    scale = dh ** -0.5
    b_ix = jnp.arange(B)[:, None, None]
    sel_ids = jnp.arange(n_sel)
    sel_offs = jnp.arange(NSA_SEL_BLOCK)
    win_offs = jnp.arange(NSA_WINDOW + NSA_QBLK)

    def block_fn(q0):
        qc = lax.dynamic_slice_in_dim(q, q0, NSA_QBLK, axis=1)
        tq = q0 + jnp.arange(NSA_QBLK)
        s_c = jnp.einsum('bqhd,bnd->bhqn', qc, k_cmp) * scale
        p_c = masked_softmax(s_c, cmp_end[None, :] <= tq[:, None])
        o_c = jnp.einsum('bhqn,bnd->bqhd', p_c.astype(v_cmp.dtype), v_cmp)
        imp = jnp.einsum('bhqn,ns->bqs', p_c, cmp_to_sel)
        own = tq // NSA_SEL_BLOCK
        causal = sel_ids[None, :] <= own[:, None]
        forced = causal & ((sel_ids[None, :] == 0) | (sel_ids[None, :] >= own[:, None] - 1))
        imp = jnp.where(forced, NSA_FORCE_SCORE, jnp.where(causal, imp, -NSA_FORCE_SCORE))
        _, idx = lax.top_k(imp, topn)
        k_sel = ksb[b_ix, idx]
        v_sel = vsb[b_ix, idx]
        s_s = (jnp.einsum('bqhd,bqnkd->bhqnk', qc, k_sel) * scale).reshape(B, H, NSA_QBLK, topn * NSA_SEL_BLOCK)
        pos = idx[..., None] * NSA_SEL_BLOCK + sel_offs
        m_s = (pos <= tq[None, :, None, None]).reshape(B, 1, NSA_QBLK, topn * NSA_SEL_BLOCK)
        p_s = masked_softmax(s_s, m_s).astype(v_sel.dtype).reshape(B, H, NSA_QBLK, topn, NSA_SEL_BLOCK)
        o_s = jnp.einsum('bhqnk,bqnkd->bqhd', p_s, v_sel)
        kwc = lax.dynamic_slice_in_dim(kw_pad, q0, NSA_WINDOW + NSA_QBLK, axis=1)
        vwc = lax.dynamic_slice_in_dim(vw_pad, q0, NSA_WINDOW + NSA_QBLK, axis=1)
        kpos = q0 - NSA_WINDOW + win_offs
        m_w = ((kpos[None, :] <= tq[:, None]) & (kpos[None, :] > tq[:, None] - NSA_WINDOW)
               & (kpos[None, :] >= 0))
        s_w = jnp.einsum('bqhd,bkd->bhqk', qc, kwc) * scale
        p_w = masked_softmax(s_w, m_w).astype(vwc.dtype)
        o_w = jnp.einsum('bhqk,bkd->bqhd', p_w, vwc)
        g = lax.dynamic_slice_in_dim(gates, q0, NSA_QBLK, axis=1)
        return g[..., 0:1] * o_c + g[..., 1:2] * o_s + g[..., 2:3] * o_w

    return map_query_blocks(block_fn, T, NSA_QBLK)


def dsa_attention(q, k, v, iq, ik, iw, cos, sin):
    B, T, H, dh = q.shape
    q = apply_rope(q, cos[:, None], sin[:, None])
    k = apply_rope(k, cos, sin)
    iq = iq.reshape(B, T, DSA_IDX_HEADS, DSA_IDX_DIM)
    topk = min(DSA_TOPK, T // 4)
    idx_scale = (DSA_IDX_HEADS * DSA_IDX_DIM) ** -0.5
    scale = dh ** -0.5
    b_ix = jnp.arange(B)[:, None, None]
    kpos = jnp.arange(T)

    def block_fn(q0):
        qc = lax.dynamic_slice_in_dim(q, q0, DSA_QBLK, axis=1)
        iqc = lax.dynamic_slice_in_dim(iq, q0, DSA_QBLK, axis=1)
        iwc = lax.dynamic_slice_in_dim(iw, q0, DSA_QBLK, axis=1)
        tq = q0 + jnp.arange(DSA_QBLK)
        score = jnp.einsum('bqh,bqhs->bqs', iwc,
                           jax.nn.relu(jnp.einsum('bqhd,bsd->bqhs', iqc, ik))).astype(jnp.float32) * idx_scale
        score = jnp.where(kpos[None, None, :] <= tq[None, :, None], score, NEG_INF)
        _, idx = lax.top_k(score, topk)
        k_sel = k[b_ix, idx]
        v_sel = v[b_ix, idx]
        s = jnp.einsum('bqhd,bqkd->bhqk', qc, k_sel) * scale
        p = masked_softmax(s, (idx <= tq[None, :, None])[:, None]).astype(v_sel.dtype)
        return jnp.einsum('bhqk,bqkd->bqhd', p, v_sel)

    return map_query_blocks(block_fn, T, DSA_QBLK)


def hybrid_mixer(h, w_in, mla_q_norm, mla_w_uq, mla_kv_norm, mla_w_uk, mla_w_uv,
                 nsa_pe_k, nsa_pe_v, nsa_cmp_k_w1, nsa_cmp_k_w2, nsa_cmp_v_w1, nsa_cmp_v_w2,
                 group_norm, w_out, cos, sin, cos_r, sin_r):
    B, T, _ = h.shape
    splits = np.cumsum(IN_SIZES)[:-1].tolist()
    (mq, mk, mv, cq, ckv, kr, nq, nkc, nvc, nks, nvs, nkw, nvw, ngate,
     dq, dk, dv, diq, dik, diw) = jnp.split(h @ w_in, splits, axis=-1)

    def heads(t):
        return t.reshape(B, T, GROUP_HEADS, HEAD_DIM)

    o_moba = moba_attention(heads(mq), heads(mk), heads(mv), cos, sin)
    o_mla = mla_attention(cq, ckv, kr, mla_q_norm, mla_w_uq, mla_kv_norm, mla_w_uk, mla_w_uv, cos_r, sin_r)
    o_nsa = nsa_attention(heads(nq), nkc, nvc, nks, nvs, nkw, nvw, ngate, nsa_pe_k, nsa_pe_v,
                          nsa_cmp_k_w1, nsa_cmp_k_w2, nsa_cmp_v_w1, nsa_cmp_v_w2, cos, sin)
    o_dsa = dsa_attention(heads(dq), dk, dv, diq, dik, diw, cos, sin)
    groups = [o.reshape(B, T, GROUP_WIDTH) for o in (o_moba, o_mla, o_nsa, o_dsa)]
    y = jnp.concatenate([rms_norm(o, group_norm[i]) for i, o in enumerate(groups)], axis=-1)
    return y @ w_out


def setup_inputs(seed: int = 0) -> dict:
    key = jax.random.key(seed)
    keys = list(jax.random.split(key, 32))

    def nrm(shape, scale):
        return jax.random.normal(keys.pop(), shape, jnp.float32) * scale

    def gain(shape):
        return 1.0 + nrm(shape, 0.05)

    L = DEPTH
    cmp_in = NSA_CMP_LEN * HEAD_DIM
    return {
        'x': nrm((BATCH, SEQ, D_MODEL), 1.0),
        'c': nrm((BATCH, D_MODEL), 1.0),
        'ada_w': nrm((L, D_MODEL, N_ADA * D_MODEL), 0.5 * D_MODEL ** -0.5),
        'ada_b': nrm((L, N_ADA * D_MODEL), 0.02),
        'ffn1_norm': gain((L, D_MODEL)),
        'ffn1_w_gate': nrm((L, D_MODEL, D_FF), D_MODEL ** -0.5),
        'ffn1_w_up': nrm((L, D_MODEL, D_FF), D_MODEL ** -0.5),
        'ffn1_w_down': nrm((L, D_FF, D_MODEL), D_FF ** -0.5),
        'mix_norm': gain((L, D_MODEL)),
        'w_in': nrm((L, D_MODEL, N_IN), D_MODEL ** -0.5),
        'mla_q_norm': gain((L, MLA_Q_LORA)),
        'mla_w_uq': nrm((L, MLA_Q_LORA, GROUP_HEADS * (MLA_NOPE + MLA_ROPE)), MLA_Q_LORA ** -0.5),
        'mla_kv_norm': gain((L, MLA_KV_LORA)),
        'mla_w_uk': nrm((L, MLA_KV_LORA, GROUP_HEADS * MLA_NOPE), MLA_KV_LORA ** -0.5),
        'mla_w_uv': nrm((L, MLA_KV_LORA, GROUP_HEADS * MLA_V), MLA_KV_LORA ** -0.5),
        'nsa_pe_k': nrm((L, NSA_CMP_LEN, HEAD_DIM), 0.5),
        'nsa_pe_v': nrm((L, NSA_CMP_LEN, HEAD_DIM), 0.5),
        'nsa_cmp_k_w1': nrm((L, cmp_in, NSA_CMP_HIDDEN), cmp_in ** -0.5),
        'nsa_cmp_k_w2': nrm((L, NSA_CMP_HIDDEN, HEAD_DIM), NSA_CMP_HIDDEN ** -0.5),
        'nsa_cmp_v_w1': nrm((L, cmp_in, NSA_CMP_HIDDEN), cmp_in ** -0.5),
        'nsa_cmp_v_w2': nrm((L, NSA_CMP_HIDDEN, HEAD_DIM), NSA_CMP_HIDDEN ** -0.5),
        'group_norm': gain((L, N_GROUPS, GROUP_WIDTH)),
        'w_out': nrm((L, MIX_WIDTH, D_MODEL), MIX_WIDTH ** -0.5),
        'ffn2_norm': gain((L, D_MODEL)),
        'ffn2_w_gate': nrm((L, D_MODEL, D_FF), D_MODEL ** -0.5),
        'ffn2_w_up': nrm((L, D_MODEL, D_FF), D_MODEL ** -0.5),
        'ffn2_w_down': nrm((L, D_FF, D_MODEL), D_FF ** -0.5),
        'final_norm': gain((D_MODEL,)),
    }


def reference(x, c, ada_w, ada_b, ffn1_norm, ffn1_w_gate, ffn1_w_up, ffn1_w_down, mix_norm, w_in,
              mla_q_norm, mla_w_uq, mla_kv_norm, mla_w_uk, mla_w_uv,
              nsa_pe_k, nsa_pe_v, nsa_cmp_k_w1, nsa_cmp_k_w2, nsa_cmp_v_w1, nsa_cmp_v_w2,
              group_norm, w_out, ffn2_norm, ffn2_w_gate, ffn2_w_up, ffn2_w_down, final_norm):
    B, T, D = x.shape
    cos, sin = rope_tables(T, HEAD_DIM, x.dtype)
    cos_r, sin_r = rope_tables(T, MLA_ROPE, x.dtype)
    c_act = jax.nn.silu(c)
    for l in range(DEPTH):
        sh1, sc1, g1, sh2, sc2, g2, sh3, sc3, g3 = jnp.split(
            (c_act @ ada_w[l] + ada_b[l])[:, None, :], N_ADA, axis=-1)
        h = modulate(rms_norm(x, ffn1_norm[l]), sh1, sc1)
        x = x + FFN_RESIDUAL_WEIGHT * g1 * swiglu(h, ffn1_w_gate[l], ffn1_w_up[l], ffn1_w_down[l])
        h = modulate(rms_norm(x, mix_norm[l]), sh2, sc2)
        x = x + g2 * hybrid_mixer(h, w_in[l], mla_q_norm[l], mla_w_uq[l], mla_kv_norm[l], mla_w_uk[l], mla_w_uv[l],
                                  nsa_pe_k[l], nsa_pe_v[l], nsa_cmp_k_w1[l], nsa_cmp_k_w2[l],
                                  nsa_cmp_v_w1[l], nsa_cmp_v_w2[l], group_norm[l], w_out[l],
                                  cos, sin, cos_r, sin_r)
        h = modulate(rms_norm(x, ffn2_norm[l]), sh3, sc3)
        x = x + FFN_RESIDUAL_WEIGHT * g3 * swiglu(h, ffn2_w_gate[l], ffn2_w_up[l], ffn2_w_down[l])
    return rms_norm(x, final_norm)
```

```python
import functools
import math

import numpy as np
import jax
import jax.numpy as jnp
from jax import lax
from jax.experimental import pallas as pl
from jax.experimental.pallas import tpu as pltpu

D_MODEL = 1024
N_GROUPS = 4
HEAD_DIM = 64
GROUP_HEADS = D_MODEL // (N_GROUPS * HEAD_DIM)
GROUP_WIDTH = GROUP_HEADS * HEAD_DIM
MIX_WIDTH = N_GROUPS * GROUP_WIDTH
D_FF = 256 * ((8 * D_MODEL + 3 * 256 - 1) // (3 * 256))
N_ADA = 9
FFN_RESIDUAL_WEIGHT = 0.5
ROPE_THETA = 10000.0
RMS_EPS = 1e-6
NEG_INF = -1e30

MOBA_BLOCK = 256
MOBA_TOPK = 3

MLA_Q_LORA = D_MODEL // 4
MLA_KV_LORA = D_MODEL // 8
MLA_NOPE = HEAD_DIM
MLA_ROPE = HEAD_DIM // 2
MLA_V = HEAD_DIM
MLA_QK = MLA_NOPE + MLA_ROPE

NSA_CMP_LEN = 32
NSA_CMP_STRIDE = 16
NSA_CMP_HIDDEN = 4 * HEAD_DIM
NSA_SEL_BLOCK = 64
NSA_SEL_TOPN = 16
NSA_WINDOW = 512
NSA_FORCE_SCORE = 1e4

DSA_TOPK = 256
DSA_IDX_HEADS = 8
DSA_IDX_DIM = 32

IN_NAMES = ("mq", "mk", "mv", "cq", "ckv", "kr", "nq", "nkc", "nvc", "nks", "nvs", "nkw", "nvw",
            "ngate", "dq", "dk", "dv", "diq", "dik", "diw")
IN_SIZES = (
    GROUP_WIDTH, GROUP_WIDTH, GROUP_WIDTH,
    MLA_Q_LORA, MLA_KV_LORA, MLA_ROPE,
    GROUP_WIDTH, HEAD_DIM, HEAD_DIM, HEAD_DIM, HEAD_DIM,
    HEAD_DIM, HEAD_DIM, 3 * GROUP_HEADS,
    GROUP_WIDTH, HEAD_DIM, HEAD_DIM,
    DSA_IDX_HEADS * DSA_IDX_DIM, DSA_IDX_DIM, DSA_IDX_HEADS,
)
N_IN = sum(IN_SIZES)

LANES = 128
MXU_DTYPE = jnp.bfloat16
VMEM_LIMIT = 56 * 1024 * 1024

ATT_TILE = 256
ROW_TILE = 512
FF_CHUNK = 256


def _params(*semantics):
    return pltpu.CompilerParams(dimension_semantics=semantics, vmem_limit_bytes=VMEM_LIMIT)


def _dot(a, b):
    return jnp.dot(a.astype(MXU_DTYPE), b.astype(MXU_DTYPE), preferred_element_type=jnp.float32)


def _dot_nt(a, b):
    return lax.dot_general(a.astype(MXU_DTYPE), b.astype(MXU_DTYPE), (((1,), (1,)), ((), ())),
                           preferred_element_type=jnp.float32)


def _rms(x, g):
    return x * lax.rsqrt(jnp.mean(x * x, axis=-1, keepdims=True) + RMS_EPS) * g


def _silu(x):
    return x * (1.0 / (1.0 + jnp.exp(-x)))


def _iota(shape, dim):
    return lax.broadcasted_iota(jnp.int32, shape, dim)


def _div_pow2(x, n):
    assert n & (n - 1) == 0
    return x >> (n.bit_length() - 1)


def _any(pred):
    return jnp.max(jnp.where(pred, 1.0, 0.0)) > 0.5


def _ada_kernel(c_ref, w_ref, b_ref, o_ref):
    o_ref[0] = _dot(_silu(c_ref[...]), w_ref[0]) + b_ref[0]


def _ada_call(c, ada_w, ada_b):
    L, D, _ = ada_w.shape
    B = c.shape[0]
    out = pl.pallas_call(
        _ada_kernel,
        grid=(L, N_ADA),
        in_specs=[
            pl.BlockSpec((B, D), lambda l, k: (0, 0)),
            pl.BlockSpec((1, D, D), lambda l, k: (l, 0, k)),
            pl.BlockSpec((1, 1, D), lambda l, k: (l, 0, k)),
        ],
        out_specs=pl.BlockSpec((1, B, D), lambda l, k: (l, 0, k)),
        out_shape=jax.ShapeDtypeStruct((L, B, N_ADA * D), jnp.float32),
        compiler_params=_params("arbitrary", "arbitrary"),
        name="ada_proj",
    )(c, ada_w, ada_b.reshape(L, 1, N_ADA * D))
    return out.reshape(L, B, N_ADA, D)


def _ffn_kernel(x_ref, ada_ref, gn_ref, wg_ref, wu_ref, wd_ref, *rest, k0, n_chunks, final):
    if final:
        fg_ref, o_ref, h_scr, acc_scr = rest
    else:
        o_ref, h_scr, acc_scr = rest
    a = ada_ref[0]
    h = _rms(x_ref[...], gn_ref[...]) * (1.0 + a[k0 + 1:k0 + 2]) + a[k0:k0 + 1]
    h_scr[...] = h.astype(MXU_DTYPE)
    acc_scr[...] = jnp.zeros_like(acc_scr)

    def body(j, carry):
        hh = h_scr[...]
        g = jnp.dot(hh, wg_ref[j], preferred_element_type=jnp.float32)
        u = jnp.dot(hh, wu_ref[j], preferred_element_type=jnp.float32)
        act = (_silu(g) * u).astype(MXU_DTYPE)
        acc_scr[...] += jnp.dot(act, wd_ref[j], preferred_element_type=jnp.float32)
        return carry

    lax.fori_loop(0, n_chunks, body, 0)
    y = x_ref[...] + (FFN_RESIDUAL_WEIGHT * a[k0 + 2:k0 + 3]) * acc_scr[...]
    if final:
        y = _rms(y, fg_ref[...])
    o_ref[...] = y


def _ffn_call(x2d, ada_l, norm_g, w_gate, w_up, w_down, k0, tiles_per_batch, final_gain=None):
    N, D = x2d.shape
    F = w_gate.shape[1]
    n_chunks = F // FF_CHUNK
    tm = ROW_TILE
    wg = w_gate.astype(MXU_DTYPE).reshape(D, n_chunks, FF_CHUNK).transpose(1, 0, 2)
    wu = w_up.astype(MXU_DTYPE).reshape(D, n_chunks, FF_CHUNK).transpose(1, 0, 2)
    wd = w_down.astype(MXU_DTYPE).reshape(n_chunks, FF_CHUNK, D)
    const3 = lambda i: (0, 0, 0)
    in_specs = [
        pl.BlockSpec((tm, D), lambda i: (i, 0)),
        pl.BlockSpec((1, N_ADA, D), lambda i: (i // tiles_per_batch, 0, 0)),
        pl.BlockSpec((1, D), lambda i: (0, 0)),
        pl.BlockSpec((n_chunks, D, FF_CHUNK), const3),
        pl.BlockSpec((n_chunks, D, FF_CHUNK), const3),
        pl.BlockSpec((n_chunks, FF_CHUNK, D), const3),
    ]
    args = [x2d, ada_l, norm_g.reshape(1, D), wg, wu, wd]
    final = final_gain is not None
    if final:
        in_specs.append(pl.BlockSpec((1, D), lambda i: (0, 0)))
        args.append(final_gain.reshape(1, D))
    return pl.pallas_call(
        functools.partial(_ffn_kernel, k0=k0, n_chunks=n_chunks, final=final),
        grid=(N // tm,),
        in_specs=in_specs,
        out_specs=pl.BlockSpec((tm, D), lambda i: (i, 0)),
        out_shape=jax.ShapeDtypeStruct((N, D), jnp.float32),
        scratch_shapes=[pltpu.VMEM((tm, D), MXU_DTYPE), pltpu.VMEM((tm, D), jnp.float32)],
        compiler_params=_params("arbitrary"),
        name="ffn",
    )(*args)


PROJ_SEG = 5 * GROUP_WIDTH
PROJ_COLS = 3 * PROJ_SEG + 2 * LANES


def _proj_columns():
    off = dict(zip(IN_NAMES, np.cumsum((0,) + IN_SIZES[:-1]).tolist()))
    size = dict(zip(IN_NAMES, IN_SIZES))
    cols = lambda name: np.arange(off[name], off[name] + size[name])
    zero = lambda n: np.full((n,), N_IN)

    def swap_halves(c, width):
        return c.reshape(-1, 2, width // 2)[:, ::-1, :].reshape(-1)

    seg0 = np.concatenate([cols(n) for n in ("mq", "mk", "nq", "dq", "nkc", "nks", "nkw", "dk")])
    seg1 = swap_halves(seg0, HEAD_DIM)
    seg2 = np.concatenate([cols("mv"), cols("cq"), cols("ckv"), cols("nvc"), cols("nvs"), cols("nvw"),
                           cols("dv"), cols("diq"), cols("dik"), cols("kr"),
                           swap_halves(cols("kr"), MLA_ROPE), zero(LANES - DSA_IDX_DIM - 2 * MLA_ROPE)])
    seg3 = np.concatenate([cols("ngate"), zero(LANES - size["ngate"]), cols("diw"), zero(LANES - size["diw"])])
    assert seg0.size == seg1.size == seg2.size == PROJ_SEG and seg3.size == 2 * LANES
    return np.concatenate([seg0, seg1, seg2, seg3])


def _proj_kernel(x_ref, ada_ref, gn_ref, w_ref, cos_ref, sin_ref, cosr_ref, sinr_ref,
                 qn_ref, wuq_ref, kvn_ref, wukv_ref,
                 mq_ref, mk_ref, mv_ref, lq_ref, lk_ref, lv_ref,
                 nq_ref, nkc_ref, nvc_ref, nks_ref, nvs_ref, nkw_ref, nvw_ref, ng_ref,
                 dq_ref, dk_ref, dv_ref, diq_ref, dik_ref, diw_ref):
    H, hd, gw = GROUP_HEADS, HEAD_DIM, GROUP_WIDTH
    a = ada_ref[0]
    h = (_rms(x_ref[...], gn_ref[...]) * (1.0 + a[4:5]) + a[3:4]).astype(MXU_DTYPE)

    def seg(k, width=PROJ_SEG):
        return jnp.dot(h, w_ref[:, k * PROJ_SEG:k * PROJ_SEG + width], preferred_element_type=jnp.float32)

    reps = PROJ_SEG // LANES
    roped = seg(0) * jnp.tile(cos_ref[...], (1, reps)) + seg(1) * jnp.tile(sin_ref[...], (1, reps))
    for hh in range(H):
        mq_ref[0, hh] = roped[:, hh * hd:(hh + 1) * hd].astype(mq_ref.dtype)
        mk_ref[0, hh] = roped[:, gw + hh * hd:gw + (hh + 1) * hd].astype(mk_ref.dtype)
        nq_ref[0, hh] = roped[:, 2 * gw + hh * hd:2 * gw + (hh + 1) * hd].astype(nq_ref.dtype)
        dq_ref[0, hh] = roped[:, 3 * gw + hh * hd:3 * gw + (hh + 1) * hd].astype(dq_ref.dtype)
    base = 4 * gw
    nkc_ref[0] = roped[:, base:base + hd].astype(nkc_ref.dtype)
    nks_ref[0] = roped[:, base + hd:base + 2 * hd].astype(nks_ref.dtype)
    nkw_ref[0] = roped[:, base + 2 * hd:base + 3 * hd].astype(nkw_ref.dtype)
    dk_ref[0] = roped[:, base + 3 * hd:base + 4 * hd].astype(dk_ref.dtype)

    yb = seg(2)
    for hh in range(H):
        mv_ref[0, hh] = yb[:, hh * hd:(hh + 1) * hd].astype(mv_ref.dtype)
    cq = yb[:, gw:gw + MLA_Q_LORA]
    ckv = yb[:, gw + MLA_Q_LORA:gw + MLA_Q_LORA + MLA_KV_LORA]
    base = gw + MLA_Q_LORA + MLA_KV_LORA
    nvc_ref[0] = yb[:, base:base + hd].astype(nvc_ref.dtype)
    nvs_ref[0] = yb[:, base + hd:base + 2 * hd].astype(nvs_ref.dtype)
    nvw_ref[0] = yb[:, base + 2 * hd:base + 3 * hd].astype(nvw_ref.dtype)
    dv_ref[0] = yb[:, base + 3 * hd:base + 4 * hd].astype(dv_ref.dtype)
    base += 4 * hd
    diq_ref[0] = yb[:, base:base + DSA_IDX_HEADS * DSA_IDX_DIM].astype(diq_ref.dtype)
    base += DSA_IDX_HEADS * DSA_IDX_DIM
    dik_ref[0] = yb[:, base:base + DSA_IDX_DIM].astype(dik_ref.dtype)
    base += DSA_IDX_DIM
    cosr, sinr = cosr_ref[...], sinr_ref[...]
    kpe = (yb[:, base:base + MLA_ROPE] * cosr[:, :MLA_ROPE]
           + yb[:, base + MLA_ROPE:base + 2 * MLA_ROPE] * sinr[:, :MLA_ROPE])

    yc = seg(3, 2 * LANES)
    ng_ref[0] = yc[:, :LANES]
    diw_ref[0] = yc[:, LANES:]

    q = _dot(_rms(cq, qn_ref[...]), wuq_ref[...])
    qn = q[:, :H * MLA_NOPE]
    qp = (q[:, H * MLA_NOPE:H * MLA_QK] * cosr
          + q[:, H * MLA_QK:H * (MLA_QK + MLA_ROPE)] * sinr)
    kv = _dot(_rms(ckv, kvn_ref[...]), wukv_ref[...])
    for hh in range(H):
        lq_ref[0, hh] = jnp.concatenate(
            [qn[:, hh * MLA_NOPE:(hh + 1) * MLA_NOPE], qp[:, hh * MLA_ROPE:(hh + 1) * MLA_ROPE]],
            axis=-1).astype(lq_ref.dtype)
        lk_ref[0, hh] = jnp.concatenate(
            [kv[:, hh * MLA_NOPE:(hh + 1) * MLA_NOPE], kpe], axis=-1).astype(lk_ref.dtype)
        lv_ref[0, hh] = kv[:, H * MLA_NOPE + hh * MLA_V:H * MLA_NOPE + (hh + 1) * MLA_V].astype(lv_ref.dtype)


def _proj_call(x2d, ada_l, norm_g, w_in, tables, mla_q_norm, mla_w_uq, mla_kv_norm, mla_w_uk, mla_w_uv, B, T):
    N, D = x2d.shape
    H = GROUP_HEADS
    tm = ROW_TILE
    tpb = T // tm
    w_ext = jnp.concatenate([w_in, jnp.zeros((D, 1), w_in.dtype)], axis=1)
    w_all = w_ext[:, _proj_columns()].astype(MXU_DTYPE)
    per_head = np.arange(H * MLA_QK).reshape(H, MLA_QK)
    nope = per_head[:, :MLA_NOPE].reshape(-1)
    rope = per_head[:, MLA_NOPE:].reshape(-1)
    rope_sw = rope.reshape(-1, 2, MLA_ROPE // 2)[:, ::-1, :].reshape(-1)
    wuq = mla_w_uq[:, np.concatenate([nope, rope, rope_sw])].astype(MXU_DTYPE)
    wukv = jnp.concatenate([mla_w_uk, mla_w_uv], axis=1).astype(MXU_DTYPE)
    cos, sin, cosr, sinr = tables

    row = lambda i: (i, 0)
    const2 = lambda i: (0, 0)
    trow = lambda i: (i % tpb, 0)
    in_specs = [
        pl.BlockSpec((tm, D), row),
        pl.BlockSpec((1, N_ADA, D), lambda i: (i // tpb, 0, 0)),
        pl.BlockSpec((1, D), const2),
        pl.BlockSpec((D, PROJ_COLS), const2),
        pl.BlockSpec((tm, LANES), trow), pl.BlockSpec((tm, LANES), trow),
        pl.BlockSpec((tm, LANES), trow), pl.BlockSpec((tm, LANES), trow),
        pl.BlockSpec((1, MLA_Q_LORA), const2),
        pl.BlockSpec(wuq.shape, const2),
        pl.BlockSpec((1, MLA_KV_LORA), const2),
        pl.BlockSpec(wukv.shape, const2),
    ]
    heads = lambda d: (jax.ShapeDtypeStruct((B, H, T, d), MXU_DTYPE),
                       pl.BlockSpec((1, H, tm, d), lambda i: (i // tpb, 0, i % tpb, 0)))
    shared = lambda d, dt=MXU_DTYPE: (jax.ShapeDtypeStruct((B, T, d), dt),
                                      pl.BlockSpec((1, tm, d), lambda i: (i // tpb, i % tpb, 0)))
    outs = [heads(HEAD_DIM), heads(HEAD_DIM), heads(HEAD_DIM),
            heads(MLA_QK), heads(MLA_QK), heads(MLA_V),
            heads(HEAD_DIM),
            shared(HEAD_DIM), shared(HEAD_DIM), shared(HEAD_DIM), shared(HEAD_DIM),
            shared(HEAD_DIM), shared(HEAD_DIM),
            shared(LANES, jnp.float32),
            heads(HEAD_DIM), shared(HEAD_DIM), shared(HEAD_DIM),
            shared(DSA_IDX_HEADS * DSA_IDX_DIM), shared(DSA_IDX_DIM),
            shared(LANES, jnp.float32)]
    return pl.pallas_call(
        _proj_kernel,
        grid=(N // tm,),
        in_specs=in_specs,
        out_specs=[o[1] for o in outs],
        out_shape=[o[0] for o in outs],
        compiler_params=_params("arbitrary"),
        name="mixer_in_proj",
    )(x2d, ada_l, norm_g.reshape(1, D), w_all, cos, sin, cosr, sinr,
      mla_q_norm.reshape(1, -1), wuq, mla_kv_norm.reshape(1, -1), wukv)


def _softmax_init(m_scr, l_scr, acc_scr):
    m_scr[...] = jnp.full(m_scr.shape, NEG_INF, jnp.float32)
    l_scr[...] = jnp.zeros_like(l_scr)
    acc_scr[...] = jnp.zeros_like(acc_scr)


def _softmax_step(s, mask, v, m_scr, l_scr, acc_scr):
    sm = jnp.where(mask, s, NEG_INF)
    m_prev = m_scr[...]
    m_new = jnp.maximum(m_prev, jnp.max(sm, axis=-1, keepdims=True))
    p = jnp.where(mask, jnp.exp(sm - m_new), 0.0)
    alpha = jnp.exp(m_prev - m_new)
    l_scr[...] = alpha * l_scr[...] + jnp.sum(p, axis=-1, keepdims=True)
    acc_scr[...] = alpha * acc_scr[...] + _dot(p, v)
    m_scr[...] = m_new


def _softmax_finish(l_scr, acc_scr):
    l = l_scr[...]
    return acc_scr[...] / jnp.where(l > 0.0, l, 1.0)


def _rank_desc(x, n):
    lane = _iota(x.shape, 1)
    rank = jnp.zeros(x.shape, jnp.float32)
    for j in range(n):
        col = x[:, j:j + 1]
        rank = rank + jnp.where(col > x, 1.0, 0.0) + jnp.where((col == x) & (lane > j), 1.0, 0.0)
    return rank


def _positions(q0, k0, tq, tk):
    return q0 + _iota((tq, tk), 0), k0 + _iota((tq, tk), 1)


def _mla_kernel(q_ref, k_ref, v_ref, o_ref, m_scr, l_scr, acc_scr, *, tq, scale):
    q0 = pl.program_id(1) * tq
    for h in range(GROUP_HEADS):
        q = q_ref[0, h]
        _softmax_init(m_scr, l_scr, acc_scr)

        def body(kc, carry):
            k0 = pl.multiple_of(kc * tq, tq)
            qpos, kpos = _positions(q0, k0, tq, tq)
            s = _dot_nt(q, k_ref[0, h, pl.ds(k0, tq), :]) * scale
            _softmax_step(s, kpos <= qpos, v_ref[0, h, pl.ds(k0, tq), :], m_scr, l_scr, acc_scr)
            return carry

        lax.fori_loop(0, pl.program_id(1) + 1, body, 0)
        o_ref[0, :, h * MLA_V:(h + 1) * MLA_V] = _softmax_finish(l_scr, acc_scr)


def _att_scratch(rows, d):
    return [pltpu.VMEM((rows, 1), jnp.float32), pltpu.VMEM((rows, 1), jnp.float32),
            pltpu.VMEM((rows, d), jnp.float32)]


def _mla_call(q, k, v):
    B, H, T, _ = q.shape
    tq = ATT_TILE
    return pl.pallas_call(
        functools.partial(_mla_kernel, tq=tq, scale=MLA_QK ** -0.5),
        grid=(B, T // tq),
        in_specs=[pl.BlockSpec((1, H, tq, MLA_QK), lambda b, i: (b, 0, i, 0)),
                  pl.BlockSpec((1, H, T, MLA_QK), lambda b, i: (b, 0, 0, 0)),
                  pl.BlockSpec((1, H, T, MLA_V), lambda b, i: (b, 0, 0, 0))],
        out_specs=pl.BlockSpec((1, tq, H * MLA_V), lambda b, i: (b, i, 0)),
        out_shape=jax.ShapeDtypeStruct((B, T, H * MLA_V), jnp.float32),
        scratch_shapes=_att_scratch(tq, MLA_V),
        compiler_params=_params("arbitrary", "arbitrary"),
        name="mla_attention",
    )(q, k, v)


def _moba_kernel(q_ref, k_ref, v_ref, o_ref, kmean_scr, m_scr, l_scr, acc_scr, *, tq, nb, topk, scale):
    qi = pl.program_id(1)
    q0 = qi * tq
    T = nb * MOBA_BLOCK

    @pl.when(qi == 0)
    def _():
        avg = jnp.where(_div_pow2(_iota((nb, T), 1), MOBA_BLOCK) == _iota((nb, T), 0), 1.0 / MOBA_BLOCK, 0.0)
        for h in range(GROUP_HEADS):
            kmean_scr[h] = _dot(avg, k_ref[0, h])

    own = qi
    blk = _iota((tq, nb), 1)
    for h in range(GROUP_HEADS):
        q = q_ref[0, h]
        gate = jnp.where(blk < own, _dot_nt(q, kmean_scr[h]), NEG_INF)
        chosen = (_rank_desc(gate, nb) < topk) & (blk < own)
        selm = jnp.where(chosen, 1.0, 0.0)
        _softmax_init(m_scr, l_scr, acc_scr)

        def body(kc, carry):
            k0 = pl.multiple_of(kc * tq, tq)
            onehot = jnp.where(_iota((nb, tq), 0) == kc, 1.0, 0.0)
            allow = _dot(selm, onehot) > 0.5
            s = _dot_nt(q, k_ref[0, h, pl.ds(k0, tq), :]) * scale
            _softmax_step(s, allow, v_ref[0, h, pl.ds(k0, tq), :], m_scr, l_scr, acc_scr)
            return carry

        lax.fori_loop(0, own, body, 0)
        k0 = pl.multiple_of(q0, tq)
        qpos, kpos = _positions(q0, k0, tq, tq)
        s = _dot_nt(q, k_ref[0, h, pl.ds(k0, tq), :]) * scale
        _softmax_step(s, kpos <= qpos, v_ref[0, h, pl.ds(k0, tq), :], m_scr, l_scr, acc_scr)
        o_ref[0, :, h * HEAD_DIM:(h + 1) * HEAD_DIM] = _softmax_finish(l_scr, acc_scr)


def _moba_call(q, k, v):
    B, H, T, d = q.shape
    tq = MOBA_BLOCK
    nb = T // MOBA_BLOCK
    return pl.pallas_call(
        functools.partial(_moba_kernel, tq=tq, nb=nb, topk=min(MOBA_TOPK, nb - 1), scale=d ** -0.5),
        grid=(B, T // tq),
        in_specs=[pl.BlockSpec((1, H, tq, d), lambda b, i: (b, 0, i, 0)),
                  pl.BlockSpec((1, H, T, d), lambda b, i: (b, 0, 0, 0)),
                  pl.BlockSpec((1, H, T, d), lambda b, i: (b, 0, 0, 0))],
        out_specs=pl.BlockSpec((1, tq, H * d), lambda b, i: (b, i, 0)),
        out_shape=jax.ShapeDtypeStruct((B, T, H * d), jnp.float32),
        scratch_shapes=[pltpu.VMEM((H, nb, d), jnp.float32)] + _att_scratch(tq, d),
        compiler_params=_params("arbitrary", "arbitrary"),
        name="moba_attention",
    )(q, k, v)


def _cmp_kernel(k_ref, v_ref, pek_ref, pev_ref, kw1_ref, kw2_ref, vw1_ref, vw2_ref, ko_ref, vo_ref, *, rows):
    half = NSA_CMP_STRIDE * HEAD_DIM

    def compress(t_ref, pe_ref, w1_ref, w2_ref):
        t = t_ref[0].astype(jnp.float32)
        first = _dot(t + pe_ref[0:1, :], w1_ref[0:half, :])
        second = _dot(t + pe_ref[1:2, :], w1_ref[half:, :])
        hid = first + pltpu.roll(second, rows - 1, 0)
        return _dot(_silu(hid), w2_ref[...])

    ko_ref[0] = compress(k_ref, pek_ref, kw1_ref, kw2_ref)
    vo_ref[0] = compress(v_ref, pev_ref, vw1_ref, vw2_ref)


def _cmp_call(kc, vc, pe_k, pe_v, k_w1, k_w2, v_w1, v_w2):
    B, T, d = kc.shape
    rows = T // NSA_CMP_STRIDE
    wide = NSA_CMP_STRIDE * d
    assert NSA_CMP_LEN == 2 * NSA_CMP_STRIDE
    const2 = lambda b: (0, 0)
    blk = pl.BlockSpec((1, rows, wide), lambda b: (b, 0, 0))
    out = pl.BlockSpec((1, rows, d), lambda b: (b, 0, 0))
    return pl.pallas_call(
        functools.partial(_cmp_kernel, rows=rows),
        grid=(B,),
        in_specs=[blk, blk, pl.BlockSpec((2, wide), const2), pl.BlockSpec((2, wide), const2),
                  pl.BlockSpec(k_w1.shape, const2), pl.BlockSpec(k_w2.shape, const2),
                  pl.BlockSpec(v_w1.shape, const2), pl.BlockSpec(v_w2.shape, const2)],
        out_specs=[out, out],
        out_shape=[jax.ShapeDtypeStruct((B, rows, d), jnp.float32)] * 2,
        compiler_params=_params("arbitrary"),
        name="nsa_compress",
    )(kc.reshape(B, rows, wide), vc.reshape(B, rows, wide), pe_k.reshape(2, wide), pe_v.reshape(2, wide),
      k_w1.astype(MXU_DTYPE), k_w2.astype(MXU_DTYPE), v_w1.astype(MXU_DTYPE), v_w2.astype(MXU_DTYPE))


def _nsa_kernel(q_ref, kcmp_ref, vcmp_ref, ks_ref, vs_ref, kw_ref, vw_ref, g_ref, o_ref,
                oc_scr, os_scr, m_scr, l_scr, acc_scr, *, tq, ncp, n_sel, topn, scale):
    H, hd = GROUP_HEADS, HEAD_DIM
    qi = pl.program_id(1)
    q0 = qi * tq
    tq_col = q0 + _iota((tq, 1), 0)

    cmp_end = _iota((tq, ncp), 1) * NSA_CMP_STRIDE + (NSA_CMP_LEN - 1)
    m_c = cmp_end <= tq_col
    p_sum = jnp.zeros((tq, ncp), jnp.float32)
    for h in range(H):
        s = jnp.where(m_c, _dot_nt(q_ref[0, h], kcmp_ref[0]) * scale, NEG_INF)
        e = jnp.where(m_c, jnp.exp(s - jnp.max(s, axis=-1, keepdims=True)), 0.0)
        l = jnp.sum(e, axis=-1, keepdims=True)
        p = e / jnp.where(l > 0.0, l, 1.0)
        p_sum = p_sum + p
        oc_scr[h] = _dot(p, vcmp_ref[0])

    cmp_start = _iota((ncp, n_sel), 0) * NSA_CMP_STRIDE
    sel_start = _iota((ncp, n_sel), 1) * NSA_SEL_BLOCK
    overlap = (cmp_start < sel_start + NSA_SEL_BLOCK) & (cmp_start + NSA_CMP_LEN > sel_start)
    imp = _dot(p_sum, jnp.where(overlap, 1.0, 0.0))
    sel_id = _iota((tq, n_sel), 1)
    own = _div_pow2(tq_col, NSA_SEL_BLOCK)
    causal = sel_id <= own
    forced = causal & ((sel_id == 0) | (sel_id >= own - 1))
    imp = jnp.where(forced, NSA_FORCE_SCORE, jnp.where(causal, imp, -NSA_FORCE_SCORE))
    selm = jnp.where(_rank_desc(imp, n_sel) < topn, 1.0, 0.0)

    for h in range(H):
        q = q_ref[0, h]

        _softmax_init(m_scr, l_scr, acc_scr)

        def sel_body(kc, carry):
            k0 = pl.multiple_of(kc * tq, tq)
            qpos, kpos = _positions(q0, k0, tq, tq)
            blk_lo = _iota((n_sel, tq), 0) * NSA_SEL_BLOCK
            key = k0 + _iota((n_sel, tq), 1)
            expand = jnp.where((key >= blk_lo) & (key < blk_lo + NSA_SEL_BLOCK), 1.0, 0.0)
            allow = (_dot(selm, expand) > 0.5) & (kpos <= qpos)
            s = _dot_nt(q, ks_ref[0, pl.ds(k0, tq), :]) * scale
            _softmax_step(s, allow, vs_ref[0, pl.ds(k0, tq), :], m_scr, l_scr, acc_scr)
            return carry

        lax.fori_loop(0, qi + 1, sel_body, 0)
        os_scr[h] = _softmax_finish(l_scr, acc_scr)

        _softmax_init(m_scr, l_scr, acc_scr)

        def win_body(kc, carry):
            k0 = pl.multiple_of(kc * tq, tq)
            qpos, kpos = _positions(q0, k0, tq, tq)
            allow = (kpos <= qpos) & (kpos > qpos - NSA_WINDOW)
            s = _dot_nt(q, kw_ref[0, pl.ds(k0, tq), :]) * scale
            _softmax_step(s, allow, vw_ref[0, pl.ds(k0, tq), :], m_scr, l_scr, acc_scr)
            return carry

        first = jnp.maximum(q0 - NSA_WINDOW + 1, 0) // tq
        lax.fori_loop(first, qi + 1, win_body, 0)
        o_w = _softmax_finish(l_scr, acc_scr)

        gates = 1.0 / (1.0 + jnp.exp(-g_ref[0][:, 3 * h:3 * h + 3]))
        o_ref[0, :, h * hd:(h + 1) * hd] = (gates[:, 0:1] * oc_scr[h] + gates[:, 1:2] * os_scr[h]
                                            + gates[:, 2:3] * o_w)


def _nsa_call(q, kcmp, vcmp, ks, vs, kw, vw, gate_logits):
    B, H, T, d = q.shape
    tq = ATT_TILE
    ncp = kcmp.shape[1]
    n_sel = T // NSA_SEL_BLOCK
    full = lambda n, w: pl.BlockSpec((1, n, w), lambda b, i: (b, 0, 0))
    return pl.pallas_call(
        functools.partial(_nsa_kernel, tq=tq, ncp=ncp, n_sel=n_sel, topn=min(NSA_SEL_TOPN, n_sel),
                          scale=d ** -0.5),
        grid=(B, T // tq),
        in_specs=[pl.BlockSpec((1, H, tq, d), lambda b, i: (b, 0, i, 0)),
                  full(ncp, d), full(ncp, d), full(T, d), full(T, d), full(T, d), full(T, d),
                  pl.BlockSpec((1, tq, LANES), lambda b, i: (b, i, 0))],
        out_specs=pl.BlockSpec((1, tq, H * d), lambda b, i: (b, i, 0)),
        out_shape=jax.ShapeDtypeStruct((B, T, H * d), jnp.float32),
        scratch_shapes=[pltpu.VMEM((H, tq, d), jnp.float32), pltpu.VMEM((H, tq, d), jnp.float32)]
        + _att_scratch(tq, d),
        compiler_params=_params("arbitrary", "arbitrary"),
        name="nsa_attention",
    )(q, kcmp, vcmp, ks, vs, kw, vw, gate_logits)


def _sortable(x):
    b = int(np.float32(x).view(np.int32))
    return b ^ ((b >> 31) & 0x7FFFFFFF)


def _from_sortable(k):
    return lax.bitcast_convert_type(k ^ ((k >> 31) & 0x7FFFFFFF), jnp.float32)


def _dsa_kernel(q_ref, k_ref, v_ref, iq_ref, ik_ref, iw_ref, o_ref,
                s_scr, j_scr, m_scr, l_scr, acc_scr, *, tq, topk, idx_scale, scale, seq_bits):
    H, hd = GROUP_HEADS, HEAD_DIM
    qi = pl.program_id(1)
    q0 = qi * tq
    n_kc = qi + 1

    iq = iq_ref[0]
    iw = iw_ref[0]

    def score_body(kc, carry):
        k0 = pl.multiple_of(kc * tq, tq)
        qpos, kpos = _positions(q0, k0, tq, tq)
        ik = ik_ref[0, pl.ds(k0, tq), :]
        acc = jnp.zeros((tq, tq), jnp.float32)
        for h in range(DSA_IDX_HEADS):
            sh = _dot_nt(iq[:, h * DSA_IDX_DIM:(h + 1) * DSA_IDX_DIM], ik)
            acc = acc + iw[:, h:h + 1] * jnp.maximum(sh, 0.0)
        s_scr[:, pl.ds(k0, tq)] = jnp.where(kpos <= qpos, acc * idx_scale + 0.0, NEG_INF)
        return carry

    lax.fori_loop(0, n_kc, score_body, 0)

    def count(pred):
        def body(kc, c):
            k0 = pl.multiple_of(kc * tq, tq)
            return c + jnp.sum(jnp.where(pred(s_scr[:, pl.ds(k0, tq)], k0), 1.0, 0.0), axis=-1, keepdims=True)
        return lax.fori_loop(0, n_kc, body, jnp.zeros((tq, 1), jnp.float32))

    kf = float(topk)

    def bis_cond(c):
        lo, hi = c
        return _any(lo < hi)

    def bis_body(c):
        lo, hi = c
        mid = (lo | hi) - ((lo ^ hi) >> 1)
        thr = _from_sortable(mid)
        cnt = count(lambda s, k0: s >= thr)
        ge = cnt >= kf
        lo = jnp.where(ge, mid, lo)
        hi = jnp.where(cnt == kf, mid, jnp.where(ge, hi, mid - 1))
        return lo, hi

    lo0 = jnp.full((tq, 1), _sortable(NEG_INF), jnp.int32)
    hi0 = jnp.full((tq, 1), _sortable(np.inf), jnp.int32)
    lo, _ = lax.while_loop(bis_cond, bis_body, (lo0, hi0))
    thr = _from_sortable(lo)

    n_gt = count(lambda s, k0: s > thr)
    n_ge = count(lambda s, k0: s >= thr)
    need = kf - n_gt
    tie = n_ge > kf
    T = s_scr.shape[1]
    j_scr[...] = jnp.full((tq, 1), T, jnp.int32)

    @pl.when(_any(tie))
    def _():
        def jb(_, c):
            jlo, jhi = c
            mid = (jlo + jhi) >> 1
            cnt = count(lambda s, k0: (s == thr) & (k0 + _iota((tq, tq), 1) <= mid))
            ok = cnt >= need
            return jnp.where(ok, jlo, mid + 1), jnp.where(ok, mid, jhi)
        jlo, _ = lax.fori_loop(0, seq_bits, jb, (jnp.zeros((tq, 1), jnp.int32), jnp.full((tq, 1), T - 1, jnp.int32)))
        j_scr[...] = jnp.where(tie, jlo, T)

    jmax = j_scr[...]

    for h in range(H):
        q = q_ref[0, h]
        _softmax_init(m_scr, l_scr, acc_scr)

        def body(kc, carry):
            k0 = pl.multiple_of(kc * tq, tq)
            qpos, kpos = _positions(q0, k0, tq, tq)
            sc = s_scr[:, pl.ds(k0, tq)]
            picked = (sc > thr) | ((sc == thr) & (kpos <= jmax))
            s = _dot_nt(q, k_ref[0, pl.ds(k0, tq), :]) * scale
            _softmax_step(s, picked & (kpos <= qpos), v_ref[0, pl.ds(k0, tq), :], m_scr, l_scr, acc_scr)
            return carry

        lax.fori_loop(0, n_kc, body, 0)
        o_ref[0, :, h * hd:(h + 1) * hd] = _softmax_finish(l_scr, acc_scr)


def _dsa_call(q, k, v, iq, ik, iw):
    B, H, T, d = q.shape
    tq = ATT_TILE
    topk = min(DSA_TOPK, T // 4)
    full = lambda w: pl.BlockSpec((1, T, w), lambda b, i: (b, 0, 0))
    tile = lambda w: pl.BlockSpec((1, tq, w), lambda b, i: (b, i, 0))
    return pl.pallas_call(
        functools.partial(_dsa_kernel, tq=tq, topk=topk, idx_scale=(DSA_IDX_HEADS * DSA_IDX_DIM) ** -0.5,
                          scale=d ** -0.5, seq_bits=max(1, math.ceil(math.log2(T)))),
        grid=(B, T // tq),
        in_specs=[pl.BlockSpec((1, H, tq, d), lambda b, i: (b, 0, i, 0)),
                  full(d), full(d), tile(DSA_IDX_HEADS * DSA_IDX_DIM), full(DSA_IDX_DIM), tile(LANES)],
        out_specs=pl.BlockSpec((1, tq, H * d), lambda b, i: (b, i, 0)),
        out_shape=jax.ShapeDtypeStruct((B, T, H * d), jnp.float32),
        scratch_shapes=[pltpu.VMEM((tq, T), jnp.float32), pltpu.VMEM((tq, 1), jnp.int32)] + _att_scratch(tq, d),
        compiler_params=_params("arbitrary", "arbitrary"),
        name="dsa_attention",
    )(q, k, v, iq, ik, iw)


def _out_kernel(x_ref, ada_ref, o1_ref, o2_ref, o3_ref, o4_ref, gn_ref, w_ref, y_ref):
    a = ada_ref[0]
    gn = gn_ref[...]
    y = jnp.concatenate([_rms(o[0], gn[i:i + 1]).astype(MXU_DTYPE)
                         for i, o in enumerate((o1_ref, o2_ref, o3_ref, o4_ref))], axis=-1)
    y_ref[...] = x_ref[...] + a[5:6] * jnp.dot(y, w_ref[...], preferred_element_type=jnp.float32)


def _out_call(x2d, ada_l, groups, group_norm, w_out, B, T):
    N, D = x2d.shape
    tm = ROW_TILE
    tpb = T // tm
    grp = pl.BlockSpec((1, tm, GROUP_WIDTH), lambda i: (i // tpb, i % tpb, 0))
    return pl.pallas_call(
        _out_kernel,
        grid=(N // tm,),
        in_specs=[pl.BlockSpec((tm, D), lambda i: (i, 0)),
                  pl.BlockSpec((1, N_ADA, D), lambda i: (i // tpb, 0, 0)),
                  grp, grp, grp, grp,
                  pl.BlockSpec((N_GROUPS, GROUP_WIDTH), lambda i: (0, 0)),
                  pl.BlockSpec((MIX_WIDTH, D), lambda i: (0, 0))],
        out_specs=pl.BlockSpec((tm, D), lambda i: (i, 0)),
        out_shape=jax.ShapeDtypeStruct((N, D), jnp.float32),
        compiler_params=_params("arbitrary"),
        name="mixer_out_proj",
    )(x2d, ada_l, *groups, group_norm, w_out.astype(MXU_DTYPE))


def _rope_table(n_pos, dim):
    inv_freq = 1.0 / (ROPE_THETA ** (np.arange(0, dim, 2, dtype=np.float32) / dim))
    ang = jnp.arange(n_pos, dtype=jnp.float32)[:, None] * jnp.asarray(inv_freq, jnp.float32)[None, :]
    cos, sin = jnp.cos(ang), jnp.sin(ang)
    reps = LANES // dim
    return (jnp.tile(jnp.concatenate([cos, cos], axis=-1), (1, reps)),
            jnp.tile(jnp.concatenate([-sin, sin], axis=-1), (1, reps)))


def kernel(x, c, ada_w, ada_b, ffn1_norm, ffn1_w_gate, ffn1_w_up, ffn1_w_down, mix_norm, w_in, mla_q_norm, mla_w_uq, mla_kv_norm, mla_w_uk, mla_w_uv, nsa_pe_k, nsa_pe_v, nsa_cmp_k_w1, nsa_cmp_k_w2, nsa_cmp_v_w1, nsa_cmp_v_w2, group_norm, w_out, ffn2_norm, ffn2_w_gate, ffn2_w_up, ffn2_w_down, final_norm):
    B, T, D = x.shape
    L = ada_w.shape[0]
    assert D == D_MODEL and T % ROW_TILE == 0 and T % ATT_TILE == 0 and T % MOBA_BLOCK == 0
    assert ATT_TILE >= min(DSA_TOPK, T // 4) and ATT_TILE >= NSA_SEL_BLOCK
    tpb = T // ROW_TILE
    ada = _ada_call(c, ada_w, ada_b)
    tables = _rope_table(T, HEAD_DIM) + _rope_table(T, MLA_ROPE)
    x2d = x.reshape(B * T, D)
    for l in range(L):
        x2d = _ffn_call(x2d, ada[l], ffn1_norm[l], ffn1_w_gate[l], ffn1_w_up[l], ffn1_w_down[l], 0, tpb)
        (mq, mk, mv, lq, lk, lv, nq, nkc, nvc, nks, nvs, nkw, nvw, ngate,
         dq, dk, dv, diq, dik, diw) = _proj_call(
            x2d, ada[l], mix_norm[l], w_in[l], tables, mla_q_norm[l], mla_w_uq[l], mla_kv_norm[l],
            mla_w_uk[l], mla_w_uv[l], B, T)
        o_moba = _moba_call(mq, mk, mv)
        o_mla = _mla_call(lq, lk, lv)
        kcmp, vcmp = _cmp_call(nkc, nvc, nsa_pe_k[l], nsa_pe_v[l], nsa_cmp_k_w1[l], nsa_cmp_k_w2[l],
                               nsa_cmp_v_w1[l], nsa_cmp_v_w2[l])
        o_nsa = _nsa_call(nq, kcmp, vcmp, nks, nvs, nkw, nvw, ngate)
        o_dsa = _dsa_call(dq, dk, dv, diq, dik, diw)
        x2d = _out_call(x2d, ada[l], (o_moba, o_mla, o_nsa, o_dsa), group_norm[l], w_out[l], B, T)
        x2d = _ffn_call(x2d, ada[l], ffn2_norm[l], ffn2_w_gate[l], ffn2_w_up[l], ffn2_w_down[l], 6, tpb,
                        final_gain=final_norm if l == L - 1 else None)
    return x2d.reshape(B, T, D)
```

```python
import functools
import math

import numpy as np
import jax
import jax.numpy as jnp
from jax import lax
from jax.experimental import pallas as pl
from jax.experimental.pallas import tpu as pltpu

D_MODEL = 1024
N_GROUPS = 4
HEAD_DIM = 64
GROUP_HEADS = D_MODEL // (N_GROUPS * HEAD_DIM)
GROUP_WIDTH = GROUP_HEADS * HEAD_DIM
MIX_WIDTH = N_GROUPS * GROUP_WIDTH
D_FF = 256 * ((8 * D_MODEL + 3 * 256 - 1) // (3 * 256))
N_ADA = 9
FFN_RESIDUAL_WEIGHT = 0.5
ROPE_THETA = 10000.0
RMS_EPS = 1e-6
NEG_INF = -1e30

MOBA_BLOCK = 256
MOBA_TOPK = 3

MLA_Q_LORA = D_MODEL // 4
MLA_KV_LORA = D_MODEL // 8
MLA_NOPE = HEAD_DIM
MLA_ROPE = HEAD_DIM // 2
MLA_V = HEAD_DIM
MLA_QK = MLA_NOPE + MLA_ROPE

NSA_CMP_LEN = 32
NSA_CMP_STRIDE = 16
NSA_CMP_HIDDEN = 4 * HEAD_DIM
NSA_SEL_BLOCK = 64
NSA_SEL_TOPN = 16
NSA_WINDOW = 512
NSA_FORCE_SCORE = 1e4

DSA_TOPK = 256
DSA_IDX_HEADS = 8
DSA_IDX_DIM = 32

IN_NAMES = ("mq", "mk", "mv", "cq", "ckv", "kr", "nq", "nkc", "nvc", "nks", "nvs", "nkw", "nvw",
            "ngate", "dq", "dk", "dv", "diq", "dik", "diw")
IN_SIZES = (
    GROUP_WIDTH, GROUP_WIDTH, GROUP_WIDTH,
    MLA_Q_LORA, MLA_KV_LORA, MLA_ROPE,
    GROUP_WIDTH, HEAD_DIM, HEAD_DIM, HEAD_DIM, HEAD_DIM,
    HEAD_DIM, HEAD_DIM, 3 * GROUP_HEADS,
    GROUP_WIDTH, HEAD_DIM, HEAD_DIM,
    DSA_IDX_HEADS * DSA_IDX_DIM, DSA_IDX_DIM, DSA_IDX_HEADS,
)
N_IN = sum(IN_SIZES)

LANES = 128
MXU_DTYPE = jnp.bfloat16
VMEM_LIMIT = 56 * 1024 * 1024

ATT_TILE = 512
ROW_TILE = 512
FF_CHUNK = 256

LOG2E = math.log2(math.e)
M_INIT = -1e29
DEN_LANE = HEAD_DIM


def _params(*semantics):
    return pltpu.CompilerParams(dimension_semantics=semantics, vmem_limit_bytes=VMEM_LIMIT)


def _dot(a, b):
    return jnp.dot(a.astype(MXU_DTYPE), b.astype(MXU_DTYPE), preferred_element_type=jnp.float32)


def _dot_nt(a, b):
    return lax.dot_general(a.astype(MXU_DTYPE), b.astype(MXU_DTYPE), (((1,), (1,)), ((), ())),
                           preferred_element_type=jnp.float32)


def _rms(x, g):
    return x * lax.rsqrt(jnp.mean(x * x, axis=-1, keepdims=True) + RMS_EPS) * g


def _silu(x):
    return x * (1.0 / (1.0 + jnp.exp(-x)))


def _iota(shape, dim):
    return lax.broadcasted_iota(jnp.int32, shape, dim)


def _log2(n):
    assert n & (n - 1) == 0
    return n.bit_length() - 1


def _any(pred):
    return jnp.max(jnp.where(pred, 1.0, 0.0)) > 0.5


def _lanes(x, width):
    return x if width == LANES else jnp.tile(x, (1, width // LANES))


def _ada_kernel(c_ref, w_ref, b_ref, o_ref):
    o_ref[0] = _dot(_silu(c_ref[...]), w_ref[0]) + b_ref[0]


def _ada_call(c, ada_w, ada_b):
    L, D, _ = ada_w.shape
    B = c.shape[0]
    out = pl.pallas_call(
        _ada_kernel,
        grid=(L, N_ADA),
        in_specs=[
            pl.BlockSpec((B, D), lambda l, k: (0, 0)),
            pl.BlockSpec((1, D, D), lambda l, k: (l, 0, k)),
            pl.BlockSpec((1, 1, D), lambda l, k: (l, 0, k)),
        ],
        out_specs=pl.BlockSpec((1, B, D), lambda l, k: (l, 0, k)),
        out_shape=jax.ShapeDtypeStruct((L, B, N_ADA * D), jnp.float32),
        compiler_params=_params("arbitrary", "arbitrary"),
        name="ada_proj",
    )(c, ada_w, ada_b.reshape(L, 1, N_ADA * D))
    return out.reshape(L, B, N_ADA, D)


def _ffn_kernel(x_ref, ada_ref, gn_ref, wg_ref, wu_ref, wd_ref, *rest, k0, n_chunks, final):
    if final:
        fg_ref, o_ref, h_scr, acc_scr = rest
    else:
        o_ref, h_scr, acc_scr = rest
    a = ada_ref[0]
    h = _rms(x_ref[...], gn_ref[...]) * (1.0 + a[k0 + 1:k0 + 2]) + a[k0:k0 + 1]
    h_scr[...] = h.astype(MXU_DTYPE)
    acc_scr[...] = jnp.zeros_like(acc_scr)

    def body(j, carry):
        hh = h_scr[...]
        g = jnp.dot(hh, wg_ref[j], preferred_element_type=jnp.float32)
        u = jnp.dot(hh, wu_ref[j], preferred_element_type=jnp.float32)
        act = (_silu(g) * u).astype(MXU_DTYPE)
        acc_scr[...] += jnp.dot(act, wd_ref[j], preferred_element_type=jnp.float32)
        return carry

    lax.fori_loop(0, n_chunks, body, 0)
    y = x_ref[...] + (FFN_RESIDUAL_WEIGHT * a[k0 + 2:k0 + 3]) * acc_scr[...]
    if final:
        y = _rms(y, fg_ref[...])
    o_ref[...] = y


def _ffn_call(x2d, ada_l, norm_g, w_gate, w_up, w_down, k0, tiles_per_batch, final_gain=None):
    N, D = x2d.shape
    F = w_gate.shape[1]
    n_chunks = F // FF_CHUNK
    tm = ROW_TILE
    wg = w_gate.astype(MXU_DTYPE).reshape(D, n_chunks, FF_CHUNK).transpose(1, 0, 2)
    wu = w_up.astype(MXU_DTYPE).reshape(D, n_chunks, FF_CHUNK).transpose(1, 0, 2)
    wd = w_down.astype(MXU_DTYPE).reshape(n_chunks, FF_CHUNK, D)
    const3 = lambda i: (0, 0, 0)
    in_specs = [
        pl.BlockSpec((tm, D), lambda i: (i, 0)),
        pl.BlockSpec((1, N_ADA, D), lambda i: (i // tiles_per_batch, 0, 0)),
        pl.BlockSpec((1, D), lambda i: (0, 0)),
        pl.BlockSpec((n_chunks, D, FF_CHUNK), const3),
        pl.BlockSpec((n_chunks, D, FF_CHUNK), const3),
        pl.BlockSpec((n_chunks, FF_CHUNK, D), const3),
    ]
    args = [x2d, ada_l, norm_g.reshape(1, D), wg, wu, wd]
    final = final_gain is not None
    if final:
        in_specs.append(pl.BlockSpec((1, D), lambda i: (0, 0)))
        args.append(final_gain.reshape(1, D))
    return pl.pallas_call(
        functools.partial(_ffn_kernel, k0=k0, n_chunks=n_chunks, final=final),
        grid=(N // tm,),
        in_specs=in_specs,
        out_specs=pl.BlockSpec((tm, D), lambda i: (i, 0)),
        out_shape=jax.ShapeDtypeStruct((N, D), jnp.float32),
        scratch_shapes=[pltpu.VMEM((tm, D), MXU_DTYPE), pltpu.VMEM((tm, D), jnp.float32)],
        compiler_params=_params("arbitrary"),
        name="ffn",
    )(*args)


G_QMAIN, G_QSWAP, G_KC, G_V, G_MISC, ROW_GROUPS = 0, 12, 24, 26, 33, 41
T_MK, T_MKS, T_SW, T_SWS, T_DK, T_DKS, T_IK, T_ROWS = 0, 256, 512, 640, 768, 896, 1024, 1152


def _swap_halves(c, width):
    return c.reshape(-1, 2, width // 2)[:, ::-1, :].reshape(-1)


def _proj_indices():
    off = dict(zip(IN_NAMES, np.cumsum((0,) + IN_SIZES[:-1]).tolist()))
    size = dict(zip(IN_NAMES, IN_SIZES))
    cols = lambda name: np.arange(off[name], off[name] + size[name])
    zero = lambda n: np.full((n,), N_IN)
    hd = HEAD_DIM

    def head_groups(c):
        return np.concatenate([np.concatenate([c[i:i + hd], zero(LANES - hd)]) for i in range(0, c.size, hd)])

    q = np.concatenate([cols("mq"), cols("nq"), cols("dq")])
    row = np.concatenate([
        head_groups(q), head_groups(_swap_halves(q, hd)),
        head_groups(cols("nkc")), head_groups(_swap_halves(cols("nkc"), hd)),
        head_groups(np.concatenate([cols("mv"), cols("nvs"), cols("nvw"), cols("dv")])),
        cols("cq"), cols("ckv"), head_groups(cols("nvc")), cols("diq"),
        cols("ngate"), zero(LANES - size["ngate"]), cols("diw"), zero(LANES - size["diw"])])
    assert row.size == ROW_GROUPS * LANES
    sw = np.concatenate([cols("nks"), cols("nkw")])
    dk_main = np.concatenate([cols("dk"), cols("kr"), zero(LANES - hd - MLA_ROPE)])
    dk_swap = np.concatenate([_swap_halves(cols("dk"), hd), _swap_halves(cols("kr"), MLA_ROPE),
                              zero(LANES - hd - MLA_ROPE)])
    tr = np.concatenate([cols("mk"), _swap_halves(cols("mk"), hd), sw, _swap_halves(sw, hd),
                         dk_main, dk_swap, np.tile(cols("dik"), LANES // DSA_IDX_DIM)])
    assert tr.size == T_ROWS
    return row, tr


def _proj_kernel(x_ref, ada_ref, gn_ref, wr_ref, wt_ref, rtab_ref, ttab_ref,
                 qn_ref, wuq_ref, kvn_ref, wukt_ref, wuv_ref,
                 mq_ref, mkt_ref, mv_ref, lq_ref, lkt_ref, lv_ref,
                 nq_ref, nkc_ref, nvc_ref, nkst_ref, nvs_ref, nkwt_ref, nvw_ref, ng_ref,
                 dq_ref, dkt_ref, dv_ref, diq_ref, dikt_ref, diw_ref, *, tm, tpb, n_moba, n_sel):
    H, hd, G = GROUP_HEADS, HEAD_DIM, LANES
    t0 = (pl.program_id(0) % tpb) * tm
    a = ada_ref[0]
    h = (_rms(x_ref[...], gn_ref[...]) * (1.0 + a[4:5]) + a[3:4]).astype(MXU_DTYPE)

    def rows(g0, n):
        return jnp.dot(h, wr_ref[:, g0 * G:(g0 + n) * G], preferred_element_type=jnp.float32)

    def cols(r0, n):
        return _dot_nt(wt_ref[r0:r0 + n, :], h)

    roped_q = (rows(G_QMAIN, 3 * H) * _lanes(rtab_ref[0], 3 * H * G)
               + rows(G_QSWAP, 3 * H) * _lanes(rtab_ref[1], 3 * H * G))
    for hh in range(H):
        mq_ref[0, hh] = roped_q[:, hh * G:(hh + 1) * G].astype(mq_ref.dtype)
        nq_ref[0, hh] = roped_q[:, (H + hh) * G:(H + hh + 1) * G].astype(nq_ref.dtype)
        dq_ref[0, hh] = roped_q[:, (2 * H + hh) * G:(2 * H + hh + 1) * G].astype(dq_ref.dtype)
    kc = rows(G_KC, 2)
    nkc_ref[0] = (kc[:, :G] * rtab_ref[2] + kc[:, G:] * rtab_ref[3])[:, :hd].astype(nkc_ref.dtype)

    ones_hi = jnp.where(_iota((1, G), 1) >= DEN_LANE, 1.0, 0.0)
    v = rows(G_V, H + 3) + _lanes(ones_hi, (H + 3) * G)
    for hh in range(H):
        mv_ref[0, hh] = v[:, hh * G:(hh + 1) * G].astype(mv_ref.dtype)
    nvs_ref[0] = v[:, H * G:(H + 1) * G].astype(nvs_ref.dtype)
    nvw_ref[0] = v[:, (H + 1) * G:(H + 2) * G].astype(nvw_ref.dtype)
    dv_ref[0] = v[:, (H + 2) * G:(H + 3) * G].astype(dv_ref.dtype)

    misc = rows(G_MISC, 8)
    cq = misc[:, :MLA_Q_LORA]
    ckv = misc[:, MLA_Q_LORA:MLA_Q_LORA + MLA_KV_LORA]
    nvc_ref[0] = misc[:, 3 * G:3 * G + hd].astype(nvc_ref.dtype)
    diq_ref[0] = misc[:, 4 * G:6 * G].astype(diq_ref.dtype)
    ng_ref[0] = misc[:, 6 * G:7 * G]
    diw_ref[0] = misc[:, 7 * G:8 * G]

    tok = t0 + _iota((hd, tm), 1)
    rid = _iota((hd, tm), 0)
    oh_moba = jnp.where((rid == (tok >> _log2(MOBA_BLOCK))) & (rid < n_moba), 1.0, 0.0)
    oh_sel = jnp.where((rid == (tok >> _log2(NSA_SEL_BLOCK))) & (rid < n_sel), 1.0, 0.0)
    zeros_lo = jnp.zeros((hd, tm), jnp.float32)
    ta_c, ta_s, tb_c, tb_s = ttab_ref[0], ttab_ref[1], ttab_ref[2], ttab_ref[3]
    mkt = (cols(T_MK, H * hd) * jnp.tile(ta_c, (H * hd // G, 1))
           + cols(T_MKS, H * hd) * jnp.tile(ta_s, (H * hd // G, 1)))
    for hh in range(H):
        mkt_ref[0, hh] = jnp.concatenate([mkt[hh * hd:(hh + 1) * hd], oh_moba], axis=0).astype(mkt_ref.dtype)
    sw = cols(T_SW, G) * ta_c + cols(T_SWS, G) * ta_s
    nkst_ref[0] = jnp.concatenate([sw[:hd], oh_sel], axis=0).astype(nkst_ref.dtype)
    nkwt_ref[0] = jnp.concatenate([sw[hd:], zeros_lo], axis=0).astype(nkwt_ref.dtype)
    dkr = cols(T_DK, G) * tb_c + cols(T_DKS, G) * tb_s
    rid2 = _iota((G, tm), 0)
    dkt_ref[0] = jnp.where(rid2 < hd, dkr, 0.0).astype(dkt_ref.dtype)
    kpe_rows = jnp.where((rid2 >= MLA_NOPE) & (rid2 < MLA_QK), dkr, 0.0)
    dikt_ref[0] = cols(T_IK, G).astype(dikt_ref.dtype)

    cqn = _rms(cq, qn_ref[...]).astype(MXU_DTYPE)
    lq = (jnp.dot(cqn, wuq_ref[:, :H * G], preferred_element_type=jnp.float32) * _lanes(rtab_ref[4], H * G)
          + jnp.dot(cqn, wuq_ref[:, H * G:], preferred_element_type=jnp.float32) * _lanes(rtab_ref[5], H * G))
    ckvn = _rms(ckv, kvn_ref[...]).astype(MXU_DTYPE)
    knt = _dot_nt(wukt_ref[...], ckvn)
    lv = jnp.dot(ckvn, wuv_ref[...], preferred_element_type=jnp.float32) + _lanes(ones_hi, H * G)
    for hh in range(H):
        lq_ref[0, hh] = lq[:, hh * G:(hh + 1) * G].astype(lq_ref.dtype)
        lkt_ref[0, hh] = (knt[hh * G:(hh + 1) * G] + kpe_rows).astype(lkt_ref.dtype)
        lv_ref[0, hh] = lv[:, hh * G:(hh + 1) * G].astype(lv_ref.dtype)


def _rope_tables(T):
    def cs(dim):
        inv_freq = 1.0 / (ROPE_THETA ** (np.arange(0, dim, 2, dtype=np.float32) / dim))
        ang = jnp.arange(T, dtype=jnp.float32)[:, None] * jnp.asarray(inv_freq, jnp.float32)[None, :]
        cos, sin = jnp.cos(ang), jnp.sin(ang)
        return jnp.concatenate([cos, cos], axis=-1), jnp.concatenate([-sin, sin], axis=-1)

    c64, s64 = cs(HEAD_DIM)
    c32, s32 = cs(MLA_ROPE)
    pad = lambda t, n: jnp.concatenate([t, jnp.zeros((T, n), jnp.float32)], axis=-1)
    sc = HEAD_DIM ** -0.5 * LOG2E
    sl = MLA_QK ** -0.5 * LOG2E
    ones = jnp.ones((T, MLA_NOPE), jnp.float32)
    rest = LANES - MLA_QK
    rtab = jnp.stack([
        pad(c64 * sc, LANES - HEAD_DIM), pad(s64 * sc, LANES - HEAD_DIM),
        pad(c64, LANES - HEAD_DIM), pad(s64, LANES - HEAD_DIM),
        pad(jnp.concatenate([ones, c32], axis=-1) * sl, rest),
        pad(jnp.concatenate([0.0 * ones, s32], axis=-1) * sl, rest)])
    ttab = jnp.stack([
        jnp.concatenate([c64, c64], axis=-1).T, jnp.concatenate([s64, s64], axis=-1).T,
        pad(jnp.concatenate([c64, c32], axis=-1), rest).T, pad(jnp.concatenate([s64, s32], axis=-1), rest).T])
    return rtab, ttab


def _proj_call(x2d, ada_l, norm_g, w_in, tables, mla_q_norm, mla_w_uq, mla_kv_norm, mla_w_uk, mla_w_uv, B, T):
    N, D = x2d.shape
    H, G = GROUP_HEADS, LANES
    tm = ROW_TILE
    tpb = T // tm
    n_moba, n_sel = T // MOBA_BLOCK, T // NSA_SEL_BLOCK
    assert n_moba <= HEAD_DIM and n_sel <= HEAD_DIM
    zcol = lambda w: jnp.concatenate([w, jnp.zeros((w.shape[0], 1), w.dtype)], axis=1)
    row_idx, tr_idx = _proj_indices()
    w_ext = zcol(w_in)
    w_row = w_ext[:, row_idx].astype(MXU_DTYPE)
    w_tr = w_ext[:, tr_idx].T.astype(MXU_DTYPE)
    zq = mla_w_uq.shape[1]
    per_head = np.arange(H * MLA_QK).reshape(H, MLA_QK)
    main = np.concatenate([np.concatenate([per_head[i], np.full((G - MLA_QK,), zq)]) for i in range(H)])
    part = np.concatenate([np.concatenate([np.full((MLA_NOPE,), zq), _swap_halves(per_head[i, MLA_NOPE:], MLA_ROPE),
                                           np.full((G - MLA_QK,), zq)]) for i in range(H)])
    wuq = zcol(mla_w_uq)[:, np.concatenate([main, part])].astype(MXU_DTYPE)
    zv = mla_w_uk.shape[1]
    grp = np.concatenate([np.concatenate([np.arange(i * HEAD_DIM, (i + 1) * HEAD_DIM), np.full((G - HEAD_DIM,), zv)])
                          for i in range(H)])
    wukt = zcol(mla_w_uk)[:, grp].T.astype(MXU_DTYPE)
    wuv = zcol(mla_w_uv)[:, grp].astype(MXU_DTYPE)
    rtab, ttab = tables

    row = lambda i: (i, 0)
    const2 = lambda i: (0, 0)
    in_specs = [
        pl.BlockSpec((tm, D), row),
        pl.BlockSpec((1, N_ADA, D), lambda i: (i // tpb, 0, 0)),
        pl.BlockSpec((1, D), const2),
        pl.BlockSpec(w_row.shape, const2),
        pl.BlockSpec(w_tr.shape, const2),
        pl.BlockSpec((6, tm, G), lambda i: (0, i % tpb, 0)),
        pl.BlockSpec((4, G, tm), lambda i: (0, 0, i % tpb)),
        pl.BlockSpec((1, MLA_Q_LORA), const2),
        pl.BlockSpec(wuq.shape, const2),
        pl.BlockSpec((1, MLA_KV_LORA), const2),
        pl.BlockSpec(wukt.shape, const2),
        pl.BlockSpec(wuv.shape, const2),
    ]
    dt = MXU_DTYPE
    hq = (jax.ShapeDtypeStruct((B, H, T, G), dt), pl.BlockSpec((1, H, tm, G), lambda i: (i // tpb, 0, i % tpb, 0)))
    hkt = (jax.ShapeDtypeStruct((B, H, G, T), dt), pl.BlockSpec((1, H, G, tm), lambda i: (i // tpb, 0, 0, i % tpb)))
    srow = lambda d, t=dt: (jax.ShapeDtypeStruct((B, T, d), t), pl.BlockSpec((1, tm, d), lambda i: (i // tpb, i % tpb, 0)))
    skt = (jax.ShapeDtypeStruct((B, G, T), dt), pl.BlockSpec((1, G, tm), lambda i: (i // tpb, 0, i % tpb)))
    outs = [hq, hkt, hq,
            hq, hkt, hq,
            hq, srow(HEAD_DIM), srow(HEAD_DIM), skt, srow(G), skt, srow(G), srow(G, jnp.float32),
            hq, skt, srow(G), srow(2 * G), skt, srow(G, jnp.float32)]
    return pl.pallas_call(
        functools.partial(_proj_kernel, tm=tm, tpb=tpb, n_moba=n_moba, n_sel=n_sel),
        grid=(N // tm,),
        in_specs=in_specs,
        out_specs=[o[1] for o in outs],
        out_shape=[o[0] for o in outs],
        compiler_params=_params("arbitrary"),
        name="mixer_in_proj",
    )(x2d, ada_l, norm_g.reshape(1, D), w_row, w_tr, rtab, ttab,
      mla_q_norm.reshape(1, -1), wuq, mla_kv_norm.reshape(1, -1), wukt, wuv)


def _flash_init(m_scr, acc_scr):
    m_scr[...] = jnp.full(m_scr.shape, M_INIT, jnp.float32)
    acc_scr[...] = jnp.zeros_like(acc_scr)


def _flash_update(h, s, v, m_scr, acc_scr):
    m_prev = m_scr[h]
    m_new = jnp.maximum(m_prev, jnp.max(s, axis=-1, keepdims=True))
    p = jnp.exp2(s - _lanes(m_new, s.shape[1]))
    acc_scr[h] = jnp.exp2(m_prev - m_new) * acc_scr[h] + _dot(p, v)
    m_scr[h] = m_new


def _flash_out(h, acc_scr):
    acc = acc_scr[h]
    den = acc[:, DEN_LANE:DEN_LANE + 1]
    return acc[:, :HEAD_DIM] / jnp.where(den > 0.0, den, 1.0)


def _causal_bias(t):
    return jnp.where(_iota((t, t), 1) <= _iota((t, t), 0), 0.0, NEG_INF)


def _rank_desc(x, first, n):
    lane = _iota((1, x.shape[1]), 1)
    rank = jnp.zeros(x.shape, jnp.float32)
    for j in range(first, first + n):
        col = x[:, j:j + 1]
        later = jnp.where(lane > j, 1.0, 0.0)
        rank = rank + jnp.where(col > x, 1.0, jnp.where(col == x, later, 0.0))
    return rank


def _att_scratch(tq):
    return [pltpu.VMEM((GROUP_HEADS, tq, LANES), jnp.float32), pltpu.VMEM((GROUP_HEADS, tq, LANES), jnp.float32)]


def _att_specs(B, H, T, tq, shared_kv):
    q = pl.BlockSpec((1, H, tq, LANES), lambda b, i: (b, 0, i, 0))
    if shared_kv:
        kt = pl.BlockSpec((1, LANES, T), lambda b, i: (b, 0, 0))
        v = pl.BlockSpec((1, T, LANES), lambda b, i: (b, 0, 0))
    else:
        kt = pl.BlockSpec((1, H, LANES, T), lambda b, i: (b, 0, 0, 0))
        v = pl.BlockSpec((1, H, T, LANES), lambda b, i: (b, 0, 0, 0))
    out = pl.BlockSpec((1, tq, H * HEAD_DIM), lambda b, i: (b, i, 0))
    return q, kt, v, out


def _mla_kernel(q_ref, kt_ref, v_ref, o_ref, m_scr, acc_scr, *, tq):
    H = GROUP_HEADS
    qi = pl.program_id(1)
    qs = [q_ref[0, h] for h in range(H)]
    _flash_init(m_scr, acc_scr)

    def chunk(k0, bias):
        for h in range(H):
            s = jnp.dot(qs[h], kt_ref[0, h, :, pl.ds(k0, tq)], preferred_element_type=jnp.float32)
            if bias is not None:
                s = s + bias
            _flash_update(h, s, v_ref[0, h, pl.ds(k0, tq), :], m_scr, acc_scr)

    def body(c, carry):
        chunk(pl.multiple_of(c * tq, tq), None)
        return carry

    lax.fori_loop(0, qi, body, 0)
    chunk(pl.multiple_of(qi * tq, tq), _causal_bias(tq))
    for h in range(H):
        o_ref[0, :, h * HEAD_DIM:(h + 1) * HEAD_DIM] = _flash_out(h, acc_scr)


def _mla_call(q, kt, v):
    B, H, T, _ = q.shape
    tq = ATT_TILE
    qs, ks, vs, out = _att_specs(B, H, T, tq, False)
    return pl.pallas_call(
        functools.partial(_mla_kernel, tq=tq),
        grid=(B, T // tq),
        in_specs=[qs, ks, vs],
        out_specs=out,
        out_shape=jax.ShapeDtypeStruct((B, T, H * HEAD_DIM), jnp.float32),
        scratch_shapes=_att_scratch(tq),
        compiler_params=_params("arbitrary", "arbitrary"),
        name="mla_attention",
    )(q, kt, v)


def _moba_kernel(q_ref, kt_ref, v_ref, o_ref, kmean_scr, m_scr, acc_scr, *, tq, nb, topk):
    H, hd = GROUP_HEADS, HEAD_DIM
    qi = pl.program_id(1)
    q0 = qi * tq
    T = nb * MOBA_BLOCK

    @pl.when(qi == 0)
    def _():
        tok_blk = _iota((T, LANES), 0) >> _log2(MOBA_BLOCK)
        avg = jnp.where(tok_blk == _iota((T, LANES), 1) - hd, 1.0 / MOBA_BLOCK, 0.0)
        for h in range(H):
            kmean_scr[h] = _dot(kt_ref[0, h], avg)

    lane = _iota((tq, LANES), 1)
    blk = lane - hd
    valid = (lane >= hd) & (blk < nb)
    own = (q0 + _iota((tq, LANES), 0)) >> _log2(MOBA_BLOCK)
    past = valid & (blk < own)
    qs = []
    for h in range(H):
        q = q_ref[0, h]
        gate = jnp.where(past, _dot(q, kmean_scr[h]), NEG_INF)
        chosen = (_rank_desc(gate, hd, nb) < topk) & past
        allowed = chosen | (blk == own)
        bias = jnp.where(valid & jnp.logical_not(allowed), NEG_INF, 0.0)
        qs.append(jnp.where(lane < hd, q, bias.astype(q.dtype)))
    _flash_init(m_scr, acc_scr)

    def chunk(k0, bias):
        for h in range(H):
            s = jnp.dot(qs[h], kt_ref[0, h, :, pl.ds(k0, tq)], preferred_element_type=jnp.float32)
            if bias is not None:
                s = s + bias
            _flash_update(h, s, v_ref[0, h, pl.ds(k0, tq), :], m_scr, acc_scr)

    def body(c, carry):
        chunk(pl.multiple_of(c * tq, tq), None)
        return carry

    lax.fori_loop(0, qi, body, 0)
    chunk(pl.multiple_of(q0, tq), _causal_bias(tq))
    for h in range(H):
        o_ref[0, :, h * hd:(h + 1) * hd] = _flash_out(h, acc_scr)


def _moba_call(q, kt, v):
    B, H, T, _ = q.shape
    tq = ATT_TILE
    nb = T // MOBA_BLOCK
    qs, ks, vs, out = _att_specs(B, H, T, tq, False)
    return pl.pallas_call(
        functools.partial(_moba_kernel, tq=tq, nb=nb, topk=min(MOBA_TOPK, nb - 1)),
        grid=(B, T // tq),
        in_specs=[qs, ks, vs],
        out_specs=out,
        out_shape=jax.ShapeDtypeStruct((B, T, H * HEAD_DIM), jnp.float32),
        scratch_shapes=[pltpu.VMEM((H, LANES, LANES), jnp.float32)] + _att_scratch(tq),
        compiler_params=_params("arbitrary", "arbitrary"),
        name="moba_attention",
    )(q, kt, v)


def _cmp_kernel(k_ref, v_ref, pek_ref, pev_ref, kw1_ref, kw2_ref, vw1_ref, vw2_ref, ko_ref, vo_ref, *, rows):
    half = NSA_CMP_STRIDE * HEAD_DIM

    def compress(t_ref, pe_ref, w1_ref, w2_ref):
        t = t_ref[0].astype(jnp.float32)
        first = _dot(t + pe_ref[0:1, :], w1_ref[0:half, :])
        second = _dot(t + pe_ref[1:2, :], w1_ref[half:, :])
        hid = first + pltpu.roll(second, rows - 1, 0)
        return _dot(_silu(hid), w2_ref[...])

    ko_ref[0] = compress(k_ref, pek_ref, kw1_ref, kw2_ref)
    vo_ref[0] = compress(v_ref, pev_ref, vw1_ref, vw2_ref)


def _cmp_call(kc, vc, pe_k, pe_v, k_w1, k_w2, v_w1, v_w2):
    B, T, d = kc.shape
    rows = T // NSA_CMP_STRIDE
    wide = NSA_CMP_STRIDE * d
    assert NSA_CMP_LEN == 2 * NSA_CMP_STRIDE
    const2 = lambda b: (0, 0)
    blk = pl.BlockSpec((1, rows, wide), lambda b: (b, 0, 0))
    out = pl.BlockSpec((1, rows, d), lambda b: (b, 0, 0))
    return pl.pallas_call(
        functools.partial(_cmp_kernel, rows=rows),
        grid=(B,),
        in_specs=[blk, blk, pl.BlockSpec((2, wide), const2), pl.BlockSpec((2, wide), const2),
                  pl.BlockSpec(k_w1.shape, const2), pl.BlockSpec(k_w2.shape, const2),
                  pl.BlockSpec(v_w1.shape, const2), pl.BlockSpec(v_w2.shape, const2)],
        out_specs=[out, out],
        out_shape=[jax.ShapeDtypeStruct((B, rows, d), jnp.float32)] * 2,
        compiler_params=_params("arbitrary"),
        name="nsa_compress",
    )(kc.reshape(B, rows, wide), vc.reshape(B, rows, wide), pe_k.reshape(2, wide), pe_v.reshape(2, wide),
      k_w1.astype(MXU_DTYPE), k_w2.astype(MXU_DTYPE), v_w1.astype(MXU_DTYPE), v_w2.astype(MXU_DTYPE))


def _nsa_kernel(q_ref, kcmp_ref, vcmp_ref, kst_ref, vs_ref, kwt_ref, vw_ref, g_ref, o_ref,
                oc_scr, os_scr, m_scr, acc_scr, *, tq, ncp, n_sel, topn):
    H, hd = GROUP_HEADS, HEAD_DIM
    qi = pl.program_id(1)
    q0 = qi * tq
    tq_col = q0 + _iota((tq, 1), 0)
    qraw = [q_ref[0, h] for h in range(H)]

    cmp_end = _iota((tq, ncp), 1) * NSA_CMP_STRIDE + (NSA_CMP_LEN - 1)
    m_c = cmp_end <= tq_col
    p_sum = jnp.zeros((tq, ncp), jnp.float32)
    for h in range(H):
        s = jnp.where(m_c, _dot_nt(qraw[h][:, :hd], kcmp_ref[0]), NEG_INF)
        e = jnp.where(m_c, jnp.exp2(s - jnp.max(s, axis=-1, keepdims=True)), 0.0)
        l = jnp.sum(e, axis=-1, keepdims=True)
        p = e / jnp.where(l > 0.0, l, 1.0)
        p_sum = p_sum + p
        oc_scr[h] = _dot(p, vcmp_ref[0])

    cmp_start = _iota((ncp, LANES), 0) * NSA_CMP_STRIDE
    sel_start = (_iota((ncp, LANES), 1) - hd) * NSA_SEL_BLOCK
    overlap = ((cmp_start < sel_start + NSA_SEL_BLOCK) & (cmp_start + NSA_CMP_LEN > sel_start)
               & (_iota((ncp, LANES), 1) >= hd))
    imp = _dot(p_sum, jnp.where(overlap, 1.0, 0.0))
    lane = _iota((tq, LANES), 1)
    sel_id = lane - hd
    valid = (lane >= hd) & (sel_id < n_sel)
    own = (q0 + _iota((tq, LANES), 0)) >> _log2(NSA_SEL_BLOCK)
    causal = sel_id <= own
    forced = causal & ((sel_id == 0) | (sel_id >= own - 1))
    imp = jnp.where(forced, NSA_FORCE_SCORE, jnp.where(causal, imp, -NSA_FORCE_SCORE))
    imp = jnp.where(valid, imp, NEG_INF)
    picked = _rank_desc(imp, hd, n_sel) < topn
    bias = jnp.where(valid & jnp.logical_not(picked), NEG_INF, 0.0).astype(qraw[0].dtype)
    qsel = [jnp.where(lane < hd, qraw[h], bias) for h in range(H)]

    _flash_init(m_scr, acc_scr)

    def sel_chunk(k0, extra):
        for h in range(H):
            s = jnp.dot(qsel[h], kst_ref[0, :, pl.ds(k0, tq)], preferred_element_type=jnp.float32)
            if extra is not None:
                s = s + extra
            _flash_update(h, s, vs_ref[0, pl.ds(k0, tq), :], m_scr, acc_scr)

    def sel_body(c, carry):
        sel_chunk(pl.multiple_of(c * tq, tq), None)
        return carry

    lax.fori_loop(0, qi, sel_body, 0)
    tri = _causal_bias(tq)
    sel_chunk(pl.multiple_of(q0, tq), tri)
    for h in range(H):
        os_scr[h] = _flash_out(h, acc_scr)

    _flash_init(m_scr, acc_scr)

    def win_chunk(k0, extra):
        for h in range(H):
            s = jnp.dot(qraw[h], kwt_ref[0, :, pl.ds(k0, tq)], preferred_element_type=jnp.float32) + extra
            _flash_update(h, s, vw_ref[0, pl.ds(k0, tq), :], m_scr, acc_scr)

    first = jnp.maximum(q0 - NSA_WINDOW + 1, 0) // tq

    def win_body(c, carry):
        k0 = pl.multiple_of(c * tq, tq)
        qpos = q0 + _iota((tq, tq), 0)
        kpos = k0 + _iota((tq, tq), 1)
        win_chunk(k0, jnp.where((kpos <= qpos) & (kpos > qpos - NSA_WINDOW), 0.0, NEG_INF))
        return carry

    lax.fori_loop(first, qi + 1, win_body, 0)
    for h in range(H):
        gates = 1.0 / (1.0 + jnp.exp(-g_ref[0][:, 3 * h:3 * h + 3]))
        o_ref[0, :, h * hd:(h + 1) * hd] = (gates[:, 0:1] * oc_scr[h] + gates[:, 1:2] * os_scr[h]
                                            + gates[:, 2:3] * _flash_out(h, acc_scr))


def _nsa_call(q, kcmp, vcmp, kst, vs, kwt, vw, gate_logits):
    B, H, T, _ = q.shape
    tq = ATT_TILE
    ncp = kcmp.shape[1]
    n_sel = T // NSA_SEL_BLOCK
    qs, ks, vsp, out = _att_specs(B, H, T, tq, True)
    cmp_spec = pl.BlockSpec((1, ncp, HEAD_DIM), lambda b, i: (b, 0, 0))
    return pl.pallas_call(
        functools.partial(_nsa_kernel, tq=tq, ncp=ncp, n_sel=n_sel, topn=min(NSA_SEL_TOPN, n_sel)),
        grid=(B, T // tq),
        in_specs=[qs, cmp_spec, cmp_spec, ks, vsp, ks, vsp,
                  pl.BlockSpec((1, tq, LANES), lambda b, i: (b, i, 0))],
        out_specs=out,
        out_shape=jax.ShapeDtypeStruct((B, T, H * HEAD_DIM), jnp.float32),
        scratch_shapes=[pltpu.VMEM((H, tq, HEAD_DIM), jnp.float32), pltpu.VMEM((H, tq, HEAD_DIM), jnp.float32)]
        + _att_scratch(tq),
        compiler_params=_params("arbitrary", "arbitrary"),
        name="nsa_attention",
    )(q, kcmp, vcmp, kst, vs, kwt, vw, gate_logits)


def _sortable(x):
    b = int(np.float32(x).view(np.int32))
    return b ^ ((b >> 31) & 0x7FFFFFFF)


def _from_sortable(k):
    return lax.bitcast_convert_type(k ^ ((k >> 31) & 0x7FFFFFFF), jnp.float32)


def _dsa_kernel(q_ref, kt_ref, v_ref, iq_ref, ikt_ref, iw_ref, o_ref,
                s_scr, j_scr, m_scr, acc_scr, *, tq, topk, idx_scale, seq_bits):
    H, hd = GROUP_HEADS, HEAD_DIM
    qi = pl.program_id(1)
    q0 = qi * tq
    n_kc = qi + 1
    T = s_scr.shape[1]
    reps = tq // LANES

    lane = _iota((tq, LANES), 1)
    quarter = lane >> _log2(DSA_IDX_DIM)
    per_group = LANES // DSA_IDX_DIM
    iq = iq_ref[0]
    iqh = [jnp.where(quarter == (h % per_group), iq[:, (h // per_group) * LANES:(h // per_group + 1) * LANES],
                     jnp.zeros((), iq.dtype)) for h in range(DSA_IDX_HEADS)]
    iw = iw_ref[0]
    iwb = [jnp.broadcast_to(iw[:, h:h + 1], (tq, LANES)) for h in range(DSA_IDX_HEADS)]

    def score_body(c, carry):
        k0 = pl.multiple_of(c * tq, tq)
        ikt = ikt_ref[0, :, pl.ds(k0, tq)]
        acc = jnp.zeros((tq, tq), jnp.float32)
        for h in range(DSA_IDX_HEADS):
            sh = jnp.dot(iqh[h], ikt, preferred_element_type=jnp.float32)
            acc = acc + _lanes(iwb[h], tq) * jnp.maximum(sh, 0.0)
        qpos = q0 + _iota((tq, tq), 0)
        kpos = k0 + _iota((tq, tq), 1)
        s_scr[:, pl.ds(k0, tq)] = jnp.where(kpos <= qpos, acc * idx_scale + 0.0, NEG_INF)
        return carry

    lax.fori_loop(0, n_kc, score_body, 0)

    def count(pred):
        def body(c, part):
            k0 = pl.multiple_of(c * tq, tq)
            hit = jnp.where(pred(s_scr[:, pl.ds(k0, tq)], k0), 1.0, 0.0)
            for r in range(reps):
                part = part + hit[:, r * LANES:(r + 1) * LANES]
            return part
        part = lax.fori_loop(0, n_kc, body, jnp.zeros((tq, LANES), jnp.float32))
        return jnp.broadcast_to(jnp.sum(part, axis=-1, keepdims=True), (tq, LANES))

    kf = float(topk)

    def bis_cond(c):
        lo, hi = c
        return _any(lo < hi)

    def bis_body(c):
        lo, hi = c
        mid = (lo | hi) - ((lo ^ hi) >> 1)
        thr = _lanes(_from_sortable(mid), tq)
        cnt = count(lambda s, k0: s >= thr)
        ge = cnt >= kf
        lo = jnp.where(ge, mid, lo)
        hi = jnp.where(cnt == kf, mid, jnp.where(ge, hi, mid - 1))
        return lo, hi

    lo0 = jnp.full((tq, LANES), _sortable(NEG_INF), jnp.int32)
    hi0 = jnp.full((tq, LANES), _sortable(np.inf), jnp.int32)
    lo, _ = lax.while_loop(bis_cond, bis_body, (lo0, hi0))
    thr = _lanes(_from_sortable(lo), tq)

    n_gt = count(lambda s, k0: s > thr)
    n_ge = count(lambda s, k0: s >= thr)
    need = kf - n_gt
    tie = n_ge > kf
    j_scr[...] = jnp.full((tq, LANES), T, jnp.int32)

    @pl.when(_any(tie))
    def _():
        def jb(_, c):
            jlo, jhi = c
            mid = (jlo + jhi) >> 1
            midw = _lanes(mid, tq)
            cnt = count(lambda s, k0: (s == thr) & (k0 + _iota((tq, tq), 1) <= midw))
            ok = cnt >= need
            return jnp.where(ok, jlo, mid + 1), jnp.where(ok, mid, jhi)
        jlo, _ = lax.fori_loop(0, seq_bits, jb, (jnp.zeros((tq, LANES), jnp.int32),
                                                 jnp.full((tq, LANES), T - 1, jnp.int32)))
        j_scr[...] = jnp.where(tie, jlo, T)

    jmax = _lanes(j_scr[...], tq)

    def bias_body(c, carry):
        k0 = pl.multiple_of(c * tq, tq)
        qpos = q0 + _iota((tq, tq), 0)
        kpos = k0 + _iota((tq, tq), 1)
        sc = s_scr[:, pl.ds(k0, tq)]
        picked = (sc > thr) | ((sc == thr) & (kpos <= jmax))
        s_scr[:, pl.ds(k0, tq)] = jnp.where(picked & (kpos <= qpos), 0.0, NEG_INF)
        return carry

    lax.fori_loop(0, n_kc, bias_body, 0)

    qs = [q_ref[0, h] for h in range(H)]
    _flash_init(m_scr, acc_scr)

    def att_body(c, carry):
        k0 = pl.multiple_of(c * tq, tq)
        bias = s_scr[:, pl.ds(k0, tq)]
        for h in range(H):
            s = jnp.dot(qs[h], kt_ref[0, :, pl.ds(k0, tq)], preferred_element_type=jnp.float32) + bias
            _flash_update(h, s, v_ref[0, pl.ds(k0, tq), :], m_scr, acc_scr)
        return carry

    lax.fori_loop(0, n_kc, att_body, 0)
    for h in range(H):
        o_ref[0, :, h * hd:(h + 1) * hd] = _flash_out(h, acc_scr)


def _dsa_call(q, kt, v, iq, ikt, iw):
    B, H, T, _ = q.shape
    tq = ATT_TILE
    topk = min(DSA_TOPK, T // 4)
    assert tq >= topk
    qs, ks, vs, out = _att_specs(B, H, T, tq, True)
    return pl.pallas_call(
        functools.partial(_dsa_kernel, tq=tq, topk=topk, idx_scale=(DSA_IDX_HEADS * DSA_IDX_DIM) ** -0.5,
                          seq_bits=max(1, math.ceil(math.log2(T)))),
        grid=(B, T // tq),
        in_specs=[qs, ks, vs, pl.BlockSpec((1, tq, 2 * LANES), lambda b, i: (b, i, 0)), ks,
                  pl.BlockSpec((1, tq, LANES), lambda b, i: (b, i, 0))],
        out_specs=out,
        out_shape=jax.ShapeDtypeStruct((B, T, H * HEAD_DIM), jnp.float32),
        scratch_shapes=[pltpu.VMEM((tq, T), jnp.float32), pltpu.VMEM((tq, LANES), jnp.int32)] + _att_scratch(tq),
        compiler_params=_params("arbitrary", "arbitrary"),
        name="dsa_attention",
    )(q, kt, v, iq, ikt, iw)


def _out_kernel(x_ref, ada_ref, o1_ref, o2_ref, o3_ref, o4_ref, gn_ref, w_ref, y_ref):
    a = ada_ref[0]
    gn = gn_ref[...]
    y = jnp.concatenate([_rms(o[0], gn[i:i + 1]).astype(MXU_DTYPE)
                         for i, o in enumerate((o1_ref, o2_ref, o3_ref, o4_ref))], axis=-1)
    y_ref[...] = x_ref[...] + a[5:6] * jnp.dot(y, w_ref[...], preferred_element_type=jnp.float32)


def _out_call(x2d, ada_l, groups, group_norm, w_out, B, T):
    N, D = x2d.shape
    tm = ROW_TILE
    tpb = T // tm
    grp = pl.BlockSpec((1, tm, GROUP_WIDTH), lambda i: (i // tpb, i % tpb, 0))
    return pl.pallas_call(
        _out_kernel,
        grid=(N // tm,),
        in_specs=[pl.BlockSpec((tm, D), lambda i: (i, 0)),
                  pl.BlockSpec((1, N_ADA, D), lambda i: (i // tpb, 0, 0)),
                  grp, grp, grp, grp,
                  pl.BlockSpec((N_GROUPS, GROUP_WIDTH), lambda i: (0, 0)),
                  pl.BlockSpec((MIX_WIDTH, D), lambda i: (0, 0))],
        out_specs=pl.BlockSpec((tm, D), lambda i: (i, 0)),
        out_shape=jax.ShapeDtypeStruct((N, D), jnp.float32),
        compiler_params=_params("arbitrary"),
        name="mixer_out_proj",
    )(x2d, ada_l, *groups, group_norm, w_out.astype(MXU_DTYPE))


def _mixer_groups(x2d, ada_l, tables, mix_norm, w_in, mla_q_norm, mla_w_uq, mla_kv_norm, mla_w_uk, mla_w_uv,
                  nsa_pe_k, nsa_pe_v, nsa_cmp_k_w1, nsa_cmp_k_w2, nsa_cmp_v_w1, nsa_cmp_v_w2, B, T):
    (mq, mkt, mv, lq, lkt, lv, nq, nkc, nvc, nkst, nvs, nkwt, nvw, ngate,
     dq, dkt, dv, diq, dikt, diw) = _proj_call(
        x2d, ada_l, mix_norm, w_in, tables, mla_q_norm, mla_w_uq, mla_kv_norm, mla_w_uk, mla_w_uv, B, T)
    o_moba = _moba_call(mq, mkt, mv)
    o_mla = _mla_call(lq, lkt, lv)
    kcmp, vcmp = _cmp_call(nkc, nvc, nsa_pe_k, nsa_pe_v, nsa_cmp_k_w1, nsa_cmp_k_w2, nsa_cmp_v_w1, nsa_cmp_v_w2)
    o_nsa = _nsa_call(nq, kcmp, vcmp, nkst, nvs, nkwt, nvw, ngate)
    o_dsa = _dsa_call(dq, dkt, dv, diq, dikt, diw)
    return o_moba, o_mla, o_nsa, o_dsa


def kernel(x, c, ada_w, ada_b, ffn1_norm, ffn1_w_gate, ffn1_w_up, ffn1_w_down, mix_norm, w_in, mla_q_norm, mla_w_uq, mla_kv_norm, mla_w_uk, mla_w_uv, nsa_pe_k, nsa_pe_v, nsa_cmp_k_w1, nsa_cmp_k_w2, nsa_cmp_v_w1, nsa_cmp_v_w2, group_norm, w_out, ffn2_norm, ffn2_w_gate, ffn2_w_up, ffn2_w_down, final_norm):
    B, T, D = x.shape
    L = ada_w.shape[0]
    assert D == D_MODEL and T % ROW_TILE == 0 and T % ATT_TILE == 0
    assert ATT_TILE % MOBA_BLOCK == 0 and ATT_TILE % NSA_SEL_BLOCK == 0 and ATT_TILE >= NSA_WINDOW
    tpb = T // ROW_TILE
    ada = _ada_call(c, ada_w, ada_b)
    tables = _rope_tables(T)
    x2d = x.reshape(B * T, D)
    for l in range(L):
        x2d = _ffn_call(x2d, ada[l], ffn1_norm[l], ffn1_w_gate[l], ffn1_w_up[l], ffn1_w_down[l], 0, tpb)
        groups = _mixer_groups(x2d, ada[l], tables, mix_norm[l], w_in[l], mla_q_norm[l], mla_w_uq[l],
                               mla_kv_norm[l], mla_w_uk[l], mla_w_uv[l], nsa_pe_k[l], nsa_pe_v[l],
                               nsa_cmp_k_w1[l], nsa_cmp_k_w2[l], nsa_cmp_v_w1[l], nsa_cmp_v_w2[l], B, T)
        x2d = _out_call(x2d, ada[l], groups, group_norm[l], w_out[l], B, T)
        x2d = _ffn_call(x2d, ada[l], ffn2_norm[l], ffn2_w_gate[l], ffn2_w_up[l], ffn2_w_down[l], 6, tpb,
                        final_gain=final_norm if l == L - 1 else None)
    return x2d.reshape(B, T, D)
```

```python
import functools
import math

import numpy as np
import jax
import jax.numpy as jnp
from jax import lax
from jax.experimental import pallas as pl
from jax.experimental.pallas import tpu as pltpu

D_MODEL = 1024
N_GROUPS = 4
HEAD_DIM = 64
GROUP_HEADS = D_MODEL // (N_GROUPS * HEAD_DIM)
GROUP_WIDTH = GROUP_HEADS * HEAD_DIM
MIX_WIDTH = N_GROUPS * GROUP_WIDTH
D_FF = 256 * ((8 * D_MODEL + 3 * 256 - 1) // (3 * 256))
N_ADA = 9
FFN_RESIDUAL_WEIGHT = 0.5
ROPE_THETA = 10000.0
RMS_EPS = 1e-6
NEG_INF = -1e30

MOBA_BLOCK = 256
MOBA_TOPK = 3

MLA_Q_LORA = D_MODEL // 4
MLA_KV_LORA = D_MODEL // 8
MLA_NOPE = HEAD_DIM
MLA_ROPE = HEAD_DIM // 2
MLA_V = HEAD_DIM
MLA_QK = MLA_NOPE + MLA_ROPE

NSA_CMP_LEN = 32
NSA_CMP_STRIDE = 16
NSA_CMP_HIDDEN = 4 * HEAD_DIM
NSA_SEL_BLOCK = 64
NSA_SEL_TOPN = 16
NSA_WINDOW = 512
NSA_FORCE_SCORE = 1e4

DSA_TOPK = 256
DSA_IDX_HEADS = 8
DSA_IDX_DIM = 32

IN_NAMES = ("mq", "mk", "mv", "cq", "ckv", "kr", "nq", "nkc", "nvc", "nks", "nvs", "nkw", "nvw",
            "ngate", "dq", "dk", "dv", "diq", "dik", "diw")
IN_SIZES = (
    GROUP_WIDTH, GROUP_WIDTH, GROUP_WIDTH,
    MLA_Q_LORA, MLA_KV_LORA, MLA_ROPE,
    GROUP_WIDTH, HEAD_DIM, HEAD_DIM, HEAD_DIM, HEAD_DIM,
    HEAD_DIM, HEAD_DIM, 3 * GROUP_HEADS,
    GROUP_WIDTH, HEAD_DIM, HEAD_DIM,
    DSA_IDX_HEADS * DSA_IDX_DIM, DSA_IDX_DIM, DSA_IDX_HEADS,
)
N_IN = sum(IN_SIZES)

LANES = 128
MXU_DTYPE = jnp.bfloat16
VMEM_LIMIT = 56 * 1024 * 1024

ATT_TILE = 512
ROW_TILE = 512
FF_CHUNK = 256

LOG2E = math.log2(math.e)
M_INIT = -1e29
DEN_LANE = HEAD_DIM


def _params(*semantics):
    return pltpu.CompilerParams(dimension_semantics=semantics, vmem_limit_bytes=VMEM_LIMIT)


def _dot(a, b):
    return jnp.dot(a.astype(MXU_DTYPE), b.astype(MXU_DTYPE), preferred_element_type=jnp.float32)


def _dot_nt(a, b):
    return lax.dot_general(a.astype(MXU_DTYPE), b.astype(MXU_DTYPE), (((1,), (1,)), ((), ())),
                           preferred_element_type=jnp.float32)


def _rms(x, g):
    return x * lax.rsqrt(jnp.mean(x * x, axis=-1, keepdims=True) + RMS_EPS) * g


def _silu(x):
    return x * (1.0 / (1.0 + jnp.exp(-x)))


def _iota(shape, dim):
    return lax.broadcasted_iota(jnp.int32, shape, dim)


def _log2(n):
    assert n & (n - 1) == 0
    return n.bit_length() - 1


def _any(pred):
    return jnp.max(jnp.where(pred, 1.0, 0.0)) > 0.5


def _lanes(x, width):
    return x if width == LANES else jnp.tile(x, (1, width // LANES))


def _ada_kernel(c_ref, w_ref, b_ref, o_ref):
    o_ref[0] = _dot(_silu(c_ref[...]), w_ref[0]) + b_ref[0]


def _ada_call(c, ada_w, ada_b):
    L, D, _ = ada_w.shape
    B = c.shape[0]
    out = pl.pallas_call(
        _ada_kernel,
        grid=(L, N_ADA),
        in_specs=[
            pl.BlockSpec((B, D), lambda l, k: (0, 0)),
            pl.BlockSpec((1, D, D), lambda l, k: (l, 0, k)),
            pl.BlockSpec((1, 1, D), lambda l, k: (l, 0, k)),
        ],
        out_specs=pl.BlockSpec((1, B, D), lambda l, k: (l, 0, k)),
        out_shape=jax.ShapeDtypeStruct((L, B, N_ADA * D), jnp.float32),
        compiler_params=_params("arbitrary", "arbitrary"),
        name="ada_proj",
    )(c, ada_w, ada_b.reshape(L, 1, N_ADA * D))
    return out.reshape(L, B, N_ADA, D)


def _ffn_kernel(x_ref, ada_ref, gn_ref, wg_ref, wu_ref, wd_ref, *rest, k0, n_chunks, final):
    if final:
        fg_ref, o_ref, h_scr, acc_scr = rest
    else:
        o_ref, h_scr, acc_scr = rest
    a = ada_ref[0]
    h = _rms(x_ref[...], gn_ref[...]) * (1.0 + a[k0 + 1:k0 + 2]) + a[k0:k0 + 1]
    h_scr[...] = h.astype(MXU_DTYPE)
    acc_scr[...] = jnp.zeros_like(acc_scr)

    def body(j, carry):
        hh = h_scr[...]
        g = jnp.dot(hh, wg_ref[j], preferred_element_type=jnp.float32)
        u = jnp.dot(hh, wu_ref[j], preferred_element_type=jnp.float32)
        act = (_silu(g) * u).astype(MXU_DTYPE)
        acc_scr[...] += jnp.dot(act, wd_ref[j], preferred_element_type=jnp.float32)
        return carry

    lax.fori_loop(0, n_chunks, body, 0)
    y = x_ref[...] + (FFN_RESIDUAL_WEIGHT * a[k0 + 2:k0 + 3]) * acc_scr[...]
    if final:
        y = _rms(y, fg_ref[...])
    o_ref[...] = y


def _ffn_call(x2d, ada_l, norm_g, w_gate, w_up, w_down, k0, tiles_per_batch, final_gain=None):
    N, D = x2d.shape
    F = w_gate.shape[1]
    n_chunks = F // FF_CHUNK
    tm = ROW_TILE
    wg = w_gate.astype(MXU_DTYPE).reshape(D, n_chunks, FF_CHUNK).transpose(1, 0, 2)
    wu = w_up.astype(MXU_DTYPE).reshape(D, n_chunks, FF_CHUNK).transpose(1, 0, 2)
    wd = w_down.astype(MXU_DTYPE).reshape(n_chunks, FF_CHUNK, D)
    const3 = lambda i: (0, 0, 0)
    in_specs = [
        pl.BlockSpec((tm, D), lambda i: (i, 0)),
        pl.BlockSpec((1, N_ADA, D), lambda i: (i // tiles_per_batch, 0, 0)),
        pl.BlockSpec((1, D), lambda i: (0, 0)),
        pl.BlockSpec((n_chunks, D, FF_CHUNK), const3),
        pl.BlockSpec((n_chunks, D, FF_CHUNK), const3),
        pl.BlockSpec((n_chunks, FF_CHUNK, D), const3),
    ]
    args = [x2d, ada_l, norm_g.reshape(1, D), wg, wu, wd]
    final = final_gain is not None
    if final:
        in_specs.append(pl.BlockSpec((1, D), lambda i: (0, 0)))
        args.append(final_gain.reshape(1, D))
    return pl.pallas_call(
        functools.partial(_ffn_kernel, k0=k0, n_chunks=n_chunks, final=final),
        grid=(N // tm,),
        in_specs=in_specs,
        out_specs=pl.BlockSpec((tm, D), lambda i: (i, 0)),
        out_shape=jax.ShapeDtypeStruct((N, D), jnp.float32),
        scratch_shapes=[pltpu.VMEM((tm, D), MXU_DTYPE), pltpu.VMEM((tm, D), jnp.float32)],
        compiler_params=_params("arbitrary"),
        name="ffn",
    )(*args)


G_QMAIN, G_QSWAP, G_KC, G_V, G_MISC, ROW_GROUPS = 0, 12, 24, 26, 33, 41
T_MK, T_MKS, T_SW, T_SWS, T_DK, T_DKS, T_IK, T_ROWS = 0, 256, 512, 640, 768, 896, 1024, 1152


def _swap_halves(c, width):
    return c.reshape(-1, 2, width // 2)[:, ::-1, :].reshape(-1)


def _proj_indices():
    off = dict(zip(IN_NAMES, np.cumsum((0,) + IN_SIZES[:-1]).tolist()))
    size = dict(zip(IN_NAMES, IN_SIZES))
    cols = lambda name: np.arange(off[name], off[name] + size[name])
    zero = lambda n: np.full((n,), N_IN)
    hd = HEAD_DIM

    def head_groups(c):
        return np.concatenate([np.concatenate([c[i:i + hd], zero(LANES - hd)]) for i in range(0, c.size, hd)])

    q = np.concatenate([cols("mq"), cols("nq"), cols("dq")])
    row = np.concatenate([
        head_groups(q), head_groups(_swap_halves(q, hd)),
        head_groups(cols("nkc")), head_groups(_swap_halves(cols("nkc"), hd)),
        head_groups(np.concatenate([cols("mv"), cols("nvs"), cols("nvw"), cols("dv")])),
        cols("cq"), cols("ckv"), head_groups(cols("nvc")), cols("diq"),
        cols("ngate"), zero(LANES - size["ngate"]), cols("diw"), zero(LANES - size["diw"])])
    assert row.size == ROW_GROUPS * LANES
    sw = np.concatenate([cols("nks"), cols("nkw")])
    dk_main = np.concatenate([cols("dk"), cols("kr"), zero(LANES - hd - MLA_ROPE)])
    dk_swap = np.concatenate([_swap_halves(cols("dk"), hd), _swap_halves(cols("kr"), MLA_ROPE),
                              zero(LANES - hd - MLA_ROPE)])
    tr = np.concatenate([cols("mk"), _swap_halves(cols("mk"), hd), sw, _swap_halves(sw, hd),
                         dk_main, dk_swap, np.tile(cols("dik"), LANES // DSA_IDX_DIM)])
    assert tr.size == T_ROWS
    return row, tr


def _proj_kernel(x_ref, ada_ref, gn_ref, wr_ref, wt_ref, rtab_ref, ttab_ref,
                 qn_ref, wuq_ref, kvn_ref, wukt_ref, wuv_ref,
                 mq_ref, mkt_ref, mv_ref, lq_ref, lkt_ref, lv_ref,
                 nq_ref, nkc_ref, nvc_ref, nkst_ref, nvs_ref, nkwt_ref, nvw_ref, ng_ref,
                 dq_ref, dkt_ref, dv_ref, diq_ref, dikt_ref, diw_ref, *, tm, tpb, n_moba, n_sel):
    H, hd, G = GROUP_HEADS, HEAD_DIM, LANES
    t0 = (pl.program_id(0) % tpb) * tm
    a = ada_ref[0]
    h = (_rms(x_ref[...], gn_ref[...]) * (1.0 + a[4:5]) + a[3:4]).astype(MXU_DTYPE)

    def rows(g0, n):
        return jnp.dot(h, wr_ref[:, g0 * G:(g0 + n) * G], preferred_element_type=jnp.float32)

    def cols(r0, n):
        return _dot_nt(wt_ref[r0:r0 + n, :], h)

    roped_q = (rows(G_QMAIN, 3 * H) * _lanes(rtab_ref[0], 3 * H * G)
               + rows(G_QSWAP, 3 * H) * _lanes(rtab_ref[1], 3 * H * G))
    for hh in range(H):
        mq_ref[0, hh] = roped_q[:, hh * G:(hh + 1) * G].astype(mq_ref.dtype)
        nq_ref[0, hh] = roped_q[:, (H + hh) * G:(H + hh + 1) * G].astype(nq_ref.dtype)
        dq_ref[0, hh] = roped_q[:, (2 * H + hh) * G:(2 * H + hh + 1) * G].astype(dq_ref.dtype)
    kc = rows(G_KC, 2)
    nkc_ref[0] = (kc[:, :G] * rtab_ref[2] + kc[:, G:] * rtab_ref[3])[:, :hd].astype(nkc_ref.dtype)

    ones_hi = jnp.where(_iota((1, G), 1) >= DEN_LANE, 1.0, 0.0)
    v = rows(G_V, H + 3) + _lanes(ones_hi, (H + 3) * G)
    for hh in range(H):
        mv_ref[0, hh] = v[:, hh * G:(hh + 1) * G].astype(mv_ref.dtype)
    nvs_ref[0] = v[:, H * G:(H + 1) * G].astype(nvs_ref.dtype)
    nvw_ref[0] = v[:, (H + 1) * G:(H + 2) * G].astype(nvw_ref.dtype)
    dv_ref[0] = v[:, (H + 2) * G:(H + 3) * G].astype(dv_ref.dtype)

    misc = rows(G_MISC, 8)
    cq = misc[:, :MLA_Q_LORA]
    ckv = misc[:, MLA_Q_LORA:MLA_Q_LORA + MLA_KV_LORA]
    nvc_ref[0] = misc[:, 3 * G:3 * G + hd].astype(nvc_ref.dtype)
    diq_ref[0] = misc[:, 4 * G:6 * G].astype(diq_ref.dtype)
    ng_ref[0] = misc[:, 6 * G:7 * G]
    diw_ref[0] = misc[:, 7 * G:8 * G]

    tok = t0 + _iota((hd, tm), 1)
    rid = _iota((hd, tm), 0)
    oh_moba = jnp.where((rid & (n_moba - 1)) == (tok >> _log2(MOBA_BLOCK)), 1.0, 0.0)
    oh_sel = jnp.where((rid == (tok >> _log2(NSA_SEL_BLOCK))) & (rid < n_sel), 1.0, 0.0)
    zeros_lo = jnp.zeros((hd, tm), jnp.float32)
    ta_c, ta_s, tb_c, tb_s = ttab_ref[0], ttab_ref[1], ttab_ref[2], ttab_ref[3]
    mkt = (cols(T_MK, H * hd) * jnp.tile(ta_c, (H * hd // G, 1))
           + cols(T_MKS, H * hd) * jnp.tile(ta_s, (H * hd // G, 1)))
    for hh in range(H):
        mine = (rid >> _log2(n_moba)) == hh
        mkt_ref[0, hh] = jnp.concatenate([mkt[hh * hd:(hh + 1) * hd], jnp.where(mine, oh_moba, 0.0)],
                                         axis=0).astype(mkt_ref.dtype)
    sw = cols(T_SW, G) * ta_c + cols(T_SWS, G) * ta_s
    nkst_ref[0] = jnp.concatenate([sw[:hd], oh_sel], axis=0).astype(nkst_ref.dtype)
    nkwt_ref[0] = jnp.concatenate([sw[hd:], zeros_lo], axis=0).astype(nkwt_ref.dtype)
    dkr = cols(T_DK, G) * tb_c + cols(T_DKS, G) * tb_s
    rid2 = _iota((G, tm), 0)
    dkt_ref[0] = jnp.where(rid2 < hd, dkr, 0.0).astype(dkt_ref.dtype)
    kpe_rows = jnp.where((rid2 >= MLA_NOPE) & (rid2 < MLA_QK), dkr, 0.0)
    dikt_ref[0] = cols(T_IK, G).astype(dikt_ref.dtype)

    cqn = _rms(cq, qn_ref[...]).astype(MXU_DTYPE)
    lq = (jnp.dot(cqn, wuq_ref[:, :H * G], preferred_element_type=jnp.float32) * _lanes(rtab_ref[4], H * G)
          + jnp.dot(cqn, wuq_ref[:, H * G:], preferred_element_type=jnp.float32) * _lanes(rtab_ref[5], H * G))
    ckvn = _rms(ckv, kvn_ref[...]).astype(MXU_DTYPE)
    knt = _dot_nt(wukt_ref[...], ckvn)
    lv = jnp.dot(ckvn, wuv_ref[...], preferred_element_type=jnp.float32) + _lanes(ones_hi, H * G)
    for hh in range(H):
        lq_ref[0, hh] = lq[:, hh * G:(hh + 1) * G].astype(lq_ref.dtype)
        lkt_ref[0, hh] = (knt[hh * G:(hh + 1) * G] + kpe_rows).astype(lkt_ref.dtype)
        lv_ref[0, hh] = lv[:, hh * G:(hh + 1) * G].astype(lv_ref.dtype)


def _rope_tables(T):
    def cs(dim):
        inv_freq = 1.0 / (ROPE_THETA ** (np.arange(0, dim, 2, dtype=np.float32) / dim))
        ang = jnp.arange(T, dtype=jnp.float32)[:, None] * jnp.asarray(inv_freq, jnp.float32)[None, :]
        cos, sin = jnp.cos(ang), jnp.sin(ang)
        return jnp.concatenate([cos, cos], axis=-1), jnp.concatenate([-sin, sin], axis=-1)

    c64, s64 = cs(HEAD_DIM)
    c32, s32 = cs(MLA_ROPE)
    pad = lambda t, n: jnp.concatenate([t, jnp.zeros((T, n), jnp.float32)], axis=-1)
    sc = HEAD_DIM ** -0.5 * LOG2E
    sl = MLA_QK ** -0.5 * LOG2E
    ones = jnp.ones((T, MLA_NOPE), jnp.float32)
    rest = LANES - MLA_QK
    rtab = jnp.stack([
        pad(c64 * sc, LANES - HEAD_DIM), pad(s64 * sc, LANES - HEAD_DIM),
        pad(c64, LANES - HEAD_DIM), pad(s64, LANES - HEAD_DIM),
        pad(jnp.concatenate([ones, c32], axis=-1) * sl, rest),
        pad(jnp.concatenate([0.0 * ones, s32], axis=-1) * sl, rest)])
    ttab = jnp.stack([
        jnp.concatenate([c64, c64], axis=-1).T, jnp.concatenate([s64, s64], axis=-1).T,
        pad(jnp.concatenate([c64, c32], axis=-1), rest).T, pad(jnp.concatenate([s64, s32], axis=-1), rest).T])
    return rtab, ttab


def _proj_call(x2d, ada_l, norm_g, w_in, tables, mla_q_norm, mla_w_uq, mla_kv_norm, mla_w_uk, mla_w_uv, B, T):
    N, D = x2d.shape
    H, G = GROUP_HEADS, LANES
    tm = ROW_TILE
    tpb = T // tm
    n_moba, n_sel = T // MOBA_BLOCK, T // NSA_SEL_BLOCK
    assert H * n_moba <= LANES - HEAD_DIM and n_sel <= LANES - HEAD_DIM and n_moba & (n_moba - 1) == 0
    zcol = lambda w: jnp.concatenate([w, jnp.zeros((w.shape[0], 1), w.dtype)], axis=1)
    row_idx, tr_idx = _proj_indices()
    w_ext = zcol(w_in)
    w_row = w_ext[:, row_idx].astype(MXU_DTYPE)
    w_tr = w_ext[:, tr_idx].T.astype(MXU_DTYPE)
    zq = mla_w_uq.shape[1]
    per_head = np.arange(H * MLA_QK).reshape(H, MLA_QK)
    main = np.concatenate([np.concatenate([per_head[i], np.full((G - MLA_QK,), zq)]) for i in range(H)])
    part = np.concatenate([np.concatenate([np.full((MLA_NOPE,), zq), _swap_halves(per_head[i, MLA_NOPE:], MLA_ROPE),
                                           np.full((G - MLA_QK,), zq)]) for i in range(H)])
    wuq = zcol(mla_w_uq)[:, np.concatenate([main, part])].astype(MXU_DTYPE)
    zv = mla_w_uk.shape[1]
    grp = np.concatenate([np.concatenate([np.arange(i * HEAD_DIM, (i + 1) * HEAD_DIM), np.full((G - HEAD_DIM,), zv)])
                          for i in range(H)])
    wukt = zcol(mla_w_uk)[:, grp].T.astype(MXU_DTYPE)
    wuv = zcol(mla_w_uv)[:, grp].astype(MXU_DTYPE)
    rtab, ttab = tables

    row = lambda i: (i, 0)
    const2 = lambda i: (0, 0)
    in_specs = [
        pl.BlockSpec((tm, D), row),
        pl.BlockSpec((1, N_ADA, D), lambda i: (i // tpb, 0, 0)),
        pl.BlockSpec((1, D), const2),
        pl.BlockSpec(w_row.shape, const2),
        pl.BlockSpec(w_tr.shape, const2),
        pl.BlockSpec((6, tm, G), lambda i: (0, i % tpb, 0)),
        pl.BlockSpec((4, G, tm), lambda i: (0, 0, i % tpb)),
        pl.BlockSpec((1, MLA_Q_LORA), const2),
        pl.BlockSpec(wuq.shape, const2),
        pl.BlockSpec((1, MLA_KV_LORA), const2),
        pl.BlockSpec(wukt.shape, const2),
        pl.BlockSpec(wuv.shape, const2),
    ]
    dt = MXU_DTYPE
    hq = (jax.ShapeDtypeStruct((B, H, T, G), dt), pl.BlockSpec((1, H, tm, G), lambda i: (i // tpb, 0, i % tpb, 0)))
    hkt = (jax.ShapeDtypeStruct((B, H, G, T), dt), pl.BlockSpec((1, H, G, tm), lambda i: (i // tpb, 0, 0, i % tpb)))
    srow = lambda d, t=dt: (jax.ShapeDtypeStruct((B, T, d), t), pl.BlockSpec((1, tm, d), lambda i: (i // tpb, i % tpb, 0)))
    skt = (jax.ShapeDtypeStruct((B, G, T), dt), pl.BlockSpec((1, G, tm), lambda i: (i // tpb, 0, i % tpb)))
    outs = [hq, hkt, hq,
            hq, hkt, hq,
            hq, srow(HEAD_DIM), srow(HEAD_DIM), skt, srow(G), skt, srow(G), srow(G, jnp.float32),
            hq, skt, srow(G), srow(2 * G), skt, srow(G, jnp.float32)]
    return pl.pallas_call(
        functools.partial(_proj_kernel, tm=tm, tpb=tpb, n_moba=n_moba, n_sel=n_sel),
        grid=(N // tm,),
        in_specs=in_specs,
        out_specs=[o[1] for o in outs],
        out_shape=[o[0] for o in outs],
        compiler_params=_params("arbitrary"),
        name="mixer_in_proj",
    )(x2d, ada_l, norm_g.reshape(1, D), w_row, w_tr, rtab, ttab,
      mla_q_norm.reshape(1, -1), wuq, mla_kv_norm.reshape(1, -1), wukt, wuv)


def _flash_init(m_scr, acc_scr):
    m_scr[...] = jnp.full(m_scr.shape, M_INIT, jnp.float32)
    acc_scr[...] = jnp.zeros_like(acc_scr)


def _flash_update(h, s, v, m_scr, acc_scr):
    m_prev = m_scr[h]
    m_new = jnp.maximum(m_prev, jnp.max(s, axis=-1, keepdims=True))
    p = jnp.exp2(s - _lanes(m_new, s.shape[1]))
    acc_scr[h] = jnp.exp2(m_prev - m_new) * acc_scr[h] + _dot(p, v)
    m_scr[h] = m_new


def _flash_out(h, acc_scr):
    acc = acc_scr[h]
    den = acc[:, DEN_LANE:DEN_LANE + 1]
    return acc[:, :HEAD_DIM] / jnp.where(den > 0.0, den, 1.0)


def _causal_bias(t):
    return jnp.where(_iota((t, t), 1) <= _iota((t, t), 0), 0.0, NEG_INF)


def _rank_desc(x):
    n = x.shape[0]
    row = _iota(x.shape, 0)
    rank = jnp.zeros(x.shape, jnp.float32)
    for j in range(n):
        cand = x[j:j + 1, :]
        rank = rank + jnp.where(cand > x, 1.0, jnp.where((cand == x) & (row > j), 1.0, 0.0))
    return rank


def _att_scratch(tq):
    return [pltpu.VMEM((GROUP_HEADS, tq, LANES), jnp.float32), pltpu.VMEM((GROUP_HEADS, tq, LANES), jnp.float32)]


def _att_specs(B, H, T, tq, shared_kv):
    q = pl.BlockSpec((1, H, tq, LANES), lambda b, i: (b, 0, i, 0))
    if shared_kv:
        kt = pl.BlockSpec((1, LANES, T), lambda b, i: (b, 0, 0))
        v = pl.BlockSpec((1, T, LANES), lambda b, i: (b, 0, 0))
    else:
        kt = pl.BlockSpec((1, H, LANES, T), lambda b, i: (b, 0, 0, 0))
        v = pl.BlockSpec((1, H, T, LANES), lambda b, i: (b, 0, 0, 0))
    out = pl.BlockSpec((1, tq, H * HEAD_DIM), lambda b, i: (b, i, 0))
    return q, kt, v, out


def _mla_kernel(q_ref, kt_ref, v_ref, o_ref, m_scr, acc_scr, *, tq):
    H = GROUP_HEADS
    qi = pl.program_id(1)
    qs = [q_ref[0, h] for h in range(H)]
    _flash_init(m_scr, acc_scr)

    def chunk(k0, bias):
        for h in range(H):
            s = jnp.dot(qs[h], kt_ref[0, h, :, pl.ds(k0, tq)], preferred_element_type=jnp.float32)
            if bias is not None:
                s = s + bias
            _flash_update(h, s, v_ref[0, h, pl.ds(k0, tq), :], m_scr, acc_scr)

    def body(c, carry):
        chunk(pl.multiple_of(c * tq, tq), None)
        return carry

    lax.fori_loop(0, qi, body, 0)
    chunk(pl.multiple_of(qi * tq, tq), _causal_bias(tq))
    for h in range(H):
        o_ref[0, :, h * HEAD_DIM:(h + 1) * HEAD_DIM] = _flash_out(h, acc_scr)


def _mla_call(q, kt, v):
    B, H, T, _ = q.shape
    tq = ATT_TILE
    qs, ks, vs, out = _att_specs(B, H, T, tq, False)
    return pl.pallas_call(
        functools.partial(_mla_kernel, tq=tq),
        grid=(B, T // tq),
        in_specs=[qs, ks, vs],
        out_specs=out,
        out_shape=jax.ShapeDtypeStruct((B, T, H * HEAD_DIM), jnp.float32),
        scratch_shapes=_att_scratch(tq),
        compiler_params=_params("arbitrary", "arbitrary"),
        name="mla_attention",
    )(q, kt, v)


def _moba_kernel(q_ref, kt_ref, v_ref, o_ref, kmean_scr, m_scr, acc_scr, *, tq, nb, topk):
    H, hd = GROUP_HEADS, HEAD_DIM
    qi = pl.program_id(1)
    q0 = qi * tq
    T = nb * MOBA_BLOCK

    @pl.when(qi == 0)
    def _():
        avg = jnp.where((_iota((nb, T), 1) >> _log2(MOBA_BLOCK)) == _iota((nb, T), 0), 1.0 / MOBA_BLOCK, 0.0)
        for h in range(H):
            kmean_scr[h] = _dot_nt(avg, kt_ref[0, h])

    blk = _iota((nb, tq), 0)
    own = (q0 + _iota((nb, tq), 1)) >> _log2(MOBA_BLOCK)
    past = blk < own
    bias_rows = [jnp.zeros((hd, tq), jnp.float32)]
    for h in range(H):
        gate = jnp.where(past, _dot_nt(kmean_scr[h], q_ref[0, h]), NEG_INF)
        allowed = ((_rank_desc(gate) < topk) & past) | (blk == own)
        bias_rows.append(jnp.where(allowed, 0.0, NEG_INF))
    if H * nb < LANES - hd:
        bias_rows.append(jnp.zeros((LANES - hd - H * nb, tq), jnp.float32))
    bias = jnp.concatenate(bias_rows, axis=0).T
    lane = _iota((tq, LANES), 1)
    qs = [jnp.where(lane < hd, q_ref[0, h], bias.astype(q_ref.dtype)) for h in range(H)]
    _flash_init(m_scr, acc_scr)

    def chunk(k0, bias):
        for h in range(H):
            s = jnp.dot(qs[h], kt_ref[0, h, :, pl.ds(k0, tq)], preferred_element_type=jnp.float32)
            if bias is not None:
                s = s + bias
            _flash_update(h, s, v_ref[0, h, pl.ds(k0, tq), :], m_scr, acc_scr)

    def body(c, carry):
        chunk(pl.multiple_of(c * tq, tq), None)
        return carry

    lax.fori_loop(0, qi, body, 0)
    chunk(pl.multiple_of(q0, tq), _causal_bias(tq))
    for h in range(H):
        o_ref[0, :, h * hd:(h + 1) * hd] = _flash_out(h, acc_scr)


def _moba_call(q, kt, v):
    B, H, T, _ = q.shape
    tq = ATT_TILE
    nb = T // MOBA_BLOCK
    qs, ks, vs, out = _att_specs(B, H, T, tq, False)
    return pl.pallas_call(
        functools.partial(_moba_kernel, tq=tq, nb=nb, topk=min(MOBA_TOPK, nb - 1)),
        grid=(B, T // tq),
        in_specs=[qs, ks, vs],
        out_specs=out,
        out_shape=jax.ShapeDtypeStruct((B, T, H * HEAD_DIM), jnp.float32),
        scratch_shapes=[pltpu.VMEM((H, nb, LANES), jnp.float32)] + _att_scratch(tq),
        compiler_params=_params("arbitrary", "arbitrary"),
        name="moba_attention",
    )(q, kt, v)


def _cmp_kernel(k_ref, v_ref, pek_ref, pev_ref, kw1_ref, kw2_ref, vw1_ref, vw2_ref, ko_ref, vo_ref, *, rows):
    half = NSA_CMP_STRIDE * HEAD_DIM

    def compress(t_ref, pe_ref, w1_ref, w2_ref):
        t = t_ref[0].astype(jnp.float32)
        first = _dot(t + pe_ref[0:1, :], w1_ref[0:half, :])
        second = _dot(t + pe_ref[1:2, :], w1_ref[half:, :])
        hid = first + pltpu.roll(second, rows - 1, 0)
        return _dot(_silu(hid), w2_ref[...])

    ko_ref[0] = compress(k_ref, pek_ref, kw1_ref, kw2_ref)
    vo_ref[0] = compress(v_ref, pev_ref, vw1_ref, vw2_ref)


def _cmp_call(kc, vc, pe_k, pe_v, k_w1, k_w2, v_w1, v_w2):
    B, T, d = kc.shape
    rows = T // NSA_CMP_STRIDE
    wide = NSA_CMP_STRIDE * d
    assert NSA_CMP_LEN == 2 * NSA_CMP_STRIDE
    const2 = lambda b: (0, 0)
    blk = pl.BlockSpec((1, rows, wide), lambda b: (b, 0, 0))
    out = pl.BlockSpec((1, rows, d), lambda b: (b, 0, 0))
    return pl.pallas_call(
        functools.partial(_cmp_kernel, rows=rows),
        grid=(B,),
        in_specs=[blk, blk, pl.BlockSpec((2, wide), const2), pl.BlockSpec((2, wide), const2),
                  pl.BlockSpec(k_w1.shape, const2), pl.BlockSpec(k_w2.shape, const2),
                  pl.BlockSpec(v_w1.shape, const2), pl.BlockSpec(v_w2.shape, const2)],
        out_specs=[out, out],
        out_shape=[jax.ShapeDtypeStruct((B, rows, d), jnp.float32)] * 2,
        compiler_params=_params("arbitrary"),
        name="nsa_compress",
    )(kc.reshape(B, rows, wide), vc.reshape(B, rows, wide), pe_k.reshape(2, wide), pe_v.reshape(2, wide),
      k_w1.astype(MXU_DTYPE), k_w2.astype(MXU_DTYPE), v_w1.astype(MXU_DTYPE), v_w2.astype(MXU_DTYPE))


def _nsa_kernel(q_ref, kcmp_ref, vcmp_ref, kst_ref, vs_ref, kwt_ref, vw_ref, g_ref, o_ref,
                oc_scr, os_scr, m_scr, acc_scr, *, tq, ncp, n_sel, topn):
    H, hd = GROUP_HEADS, HEAD_DIM
    qi = pl.program_id(1)
    q0 = qi * tq
    tq_col = q0 + _iota((tq, 1), 0)
    qraw = [q_ref[0, h] for h in range(H)]

    cmp_end = _iota((tq, ncp), 1) * NSA_CMP_STRIDE + (NSA_CMP_LEN - 1)
    m_c = cmp_end <= tq_col
    p_sum = jnp.zeros((tq, ncp), jnp.float32)
    for h in range(H):
        s = jnp.where(m_c, _dot_nt(qraw[h][:, :hd], kcmp_ref[0]), NEG_INF)
        e = jnp.where(m_c, jnp.exp2(s - jnp.max(s, axis=-1, keepdims=True)), 0.0)
        l = jnp.sum(e, axis=-1, keepdims=True)
        p = e / jnp.where(l > 0.0, l, 1.0)
        p_sum = p_sum + p
        oc_scr[h] = _dot(p, vcmp_ref[0])

    cmp_start = _iota((n_sel, ncp), 1) * NSA_CMP_STRIDE
    sel_start = _iota((n_sel, ncp), 0) * NSA_SEL_BLOCK
    overlap = (cmp_start < sel_start + NSA_SEL_BLOCK) & (cmp_start + NSA_CMP_LEN > sel_start)
    imp = _dot_nt(jnp.where(overlap, 1.0, 0.0), p_sum)
    sel_id = _iota((n_sel, tq), 0)
    own = (q0 + _iota((n_sel, tq), 1)) >> _log2(NSA_SEL_BLOCK)
    causal = sel_id <= own
    forced = causal & ((sel_id == 0) | (sel_id >= own - 1))
    imp = jnp.where(forced, NSA_FORCE_SCORE, jnp.where(causal, imp, -NSA_FORCE_SCORE))
    bias_rows = [jnp.zeros((hd, tq), jnp.float32), jnp.where(_rank_desc(imp) < topn, 0.0, NEG_INF)]
    if n_sel < LANES - hd:
        bias_rows.append(jnp.zeros((LANES - hd - n_sel, tq), jnp.float32))
    bias = jnp.concatenate(bias_rows, axis=0).T.astype(qraw[0].dtype)
    lane = _iota((tq, LANES), 1)
    qsel = [jnp.where(lane < hd, qraw[h], bias) for h in range(H)]

    _flash_init(m_scr, acc_scr)

    def sel_chunk(k0, extra):
        for h in range(H):
            s = jnp.dot(qsel[h], kst_ref[0, :, pl.ds(k0, tq)], preferred_element_type=jnp.float32)
            if extra is not None:
                s = s + extra
            _flash_update(h, s, vs_ref[0, pl.ds(k0, tq), :], m_scr, acc_scr)

    def sel_body(c, carry):
        sel_chunk(pl.multiple_of(c * tq, tq), None)
        return carry

    lax.fori_loop(0, qi, sel_body, 0)
    tri = _causal_bias(tq)
    sel_chunk(pl.multiple_of(q0, tq), tri)
    for h in range(H):
        os_scr[h] = _flash_out(h, acc_scr)

    _flash_init(m_scr, acc_scr)

    def win_chunk(k0, extra):
        for h in range(H):
            s = jnp.dot(qraw[h], kwt_ref[0, :, pl.ds(k0, tq)], preferred_element_type=jnp.float32) + extra
            _flash_update(h, s, vw_ref[0, pl.ds(k0, tq), :], m_scr, acc_scr)

    first = jnp.maximum(q0 - NSA_WINDOW + 1, 0) // tq

    def win_body(c, carry):
        k0 = pl.multiple_of(c * tq, tq)
        qpos = q0 + _iota((tq, tq), 0)
        kpos = k0 + _iota((tq, tq), 1)
        win_chunk(k0, jnp.where((kpos <= qpos) & (kpos > qpos - NSA_WINDOW), 0.0, NEG_INF))
        return carry

    lax.fori_loop(first, qi + 1, win_body, 0)
    for h in range(H):
        gates = 1.0 / (1.0 + jnp.exp(-g_ref[0][:, 3 * h:3 * h + 3]))
        o_ref[0, :, h * hd:(h + 1) * hd] = (gates[:, 0:1] * oc_scr[h] + gates[:, 1:2] * os_scr[h]
                                            + gates[:, 2:3] * _flash_out(h, acc_scr))


def _nsa_call(q, kcmp, vcmp, kst, vs, kwt, vw, gate_logits):
    B, H, T, _ = q.shape
    tq = ATT_TILE
    ncp = kcmp.shape[1]
    n_sel = T // NSA_SEL_BLOCK
    qs, ks, vsp, out = _att_specs(B, H, T, tq, True)
    cmp_spec = pl.BlockSpec((1, ncp, HEAD_DIM), lambda b, i: (b, 0, 0))
    return pl.pallas_call(
        functools.partial(_nsa_kernel, tq=tq, ncp=ncp, n_sel=n_sel, topn=min(NSA_SEL_TOPN, n_sel)),
        grid=(B, T // tq),
        in_specs=[qs, cmp_spec, cmp_spec, ks, vsp, ks, vsp,
                  pl.BlockSpec((1, tq, LANES), lambda b, i: (b, i, 0))],
        out_specs=out,
        out_shape=jax.ShapeDtypeStruct((B, T, H * HEAD_DIM), jnp.float32),
        scratch_shapes=[pltpu.VMEM((H, tq, HEAD_DIM), jnp.float32), pltpu.VMEM((H, tq, HEAD_DIM), jnp.float32)]
        + _att_scratch(tq),
        compiler_params=_params("arbitrary", "arbitrary"),
        name="nsa_attention",
    )(q, kcmp, vcmp, kst, vs, kwt, vw, gate_logits)


def _sortable(x):
    b = int(np.float32(x).view(np.int32))
    return b ^ ((b >> 31) & 0x7FFFFFFF)


def _from_sortable(k):
    return lax.bitcast_convert_type(k ^ ((k >> 31) & 0x7FFFFFFF), jnp.float32)


def _to_sortable(x):
    k = lax.bitcast_convert_type(x, jnp.int32)
    return k ^ ((k >> 31) & 0x7FFFFFFF)


COUNT_ROWS = 64
VALUE_STEPS = 24


def _dsa_kernel(q_ref, kt_ref, v_ref, iq_ref, ikt_ref, iw_ref, o_ref,
                s_scr, j_scr, t_scr, m_scr, acc_scr, *, tq, topk, idx_scale, seq_bits):
    H, hd = GROUP_HEADS, HEAD_DIM
    qi = pl.program_id(1)
    q0 = qi * tq
    n_kc = qi + 1
    T = s_scr.shape[1]
    reps = tq // LANES

    lane = _iota((tq, LANES), 1)
    quarter = lane >> _log2(DSA_IDX_DIM)
    per_group = LANES // DSA_IDX_DIM
    iq = iq_ref[0]
    iqh = [jnp.where(quarter == (h % per_group), iq[:, (h // per_group) * LANES:(h // per_group + 1) * LANES],
                     jnp.zeros((), iq.dtype)) for h in range(DSA_IDX_HEADS)]
    iw = iw_ref[0]
    iwb = [jnp.broadcast_to(iw[:, h:h + 1], (tq, LANES)) for h in range(DSA_IDX_HEADS)]

    def score_body(c, top):
        k0 = pl.multiple_of(c * tq, tq)
        ikt = ikt_ref[0, :, pl.ds(k0, tq)]
        acc = jnp.zeros((tq, tq), jnp.float32)
        for h in range(DSA_IDX_HEADS):
            sh = jnp.dot(iqh[h], ikt, preferred_element_type=jnp.float32)
            acc = acc + _lanes(iwb[h], tq) * jnp.maximum(sh, 0.0)
        qpos = q0 + _iota((tq, tq), 0)
        kpos = k0 + _iota((tq, tq), 1)
        sc = jnp.where(kpos <= qpos, acc * idx_scale + 0.0, NEG_INF)
        s_scr[:, pl.ds(k0, tq)] = sc
        for r in range(reps):
            top = jnp.maximum(top, sc[:, r * LANES:(r + 1) * LANES])
        return top

    top = lax.fori_loop(0, n_kc, score_body, jnp.full((tq, LANES), NEG_INF, jnp.float32))
    row_max = jnp.broadcast_to(jnp.max(top, axis=-1, keepdims=True), (tq, LANES))
    lane_sum = jnp.ones((LANES, LANES), jnp.float32)

    def count_ge(t):
        t_scr[...] = t
        parts = []
        for rb in range(tq // COUNT_ROWS):
            rows = slice(rb * COUNT_ROWS, (rb + 1) * COUNT_ROWS)
            t_rb = t_scr[rows, :]

            def body(c, part):
                k0 = pl.multiple_of(c * tq, tq)
                sc = s_scr[rows, pl.ds(k0, tq)]
                for r in range(reps):
                    part = part + jnp.where(sc[:, r * LANES:(r + 1) * LANES] >= t_rb, 1.0, 0.0)
                return part

            parts.append(lax.fori_loop(0, n_kc, body, jnp.zeros((COUNT_ROWS, LANES), jnp.float32)))
        return _dot(jnp.concatenate(parts, axis=0), lane_sum)

    def count(pred):
        def body(c, part):
            k0 = pl.multiple_of(c * tq, tq)
            hit = jnp.where(pred(s_scr[:, pl.ds(k0, tq)], k0), 1.0, 0.0)
            for r in range(reps):
                part = part + hit[:, r * LANES:(r + 1) * LANES]
            return part
        part = lax.fori_loop(0, n_kc, body, jnp.zeros((tq, LANES), jnp.float32))
        return jnp.broadcast_to(jnp.sum(part, axis=-1, keepdims=True), (tq, LANES))

    kf = float(topk)
    floor_key = _sortable(NEG_INF)

    def bis_cond(c):
        _, lo, hi, _ = c
        return _any(lo < hi)

    def bis_body(c):
        it, lo, hi, n_lo = c
        key_mid = (lo | hi) - ((lo ^ hi) >> 1)
        val_mid = _to_sortable(0.5 * (_from_sortable(lo) + _from_sortable(hi)))
        steps = jnp.zeros((tq, LANES), jnp.int32) + it
        use_val = (val_mid > lo) & (val_mid <= hi) & (lo > floor_key) & (steps < VALUE_STEPS)
        mid = jnp.where(use_val, val_mid, key_mid)
        cnt = count_ge(_from_sortable(mid))
        ge = cnt >= kf
        lo = jnp.where(ge, mid, lo)
        n_lo = jnp.where(ge, cnt, n_lo)
        hi = jnp.where(cnt == kf, mid, jnp.where(ge, hi, mid - 1))
        return it + 1, lo, hi, n_lo

    lo0 = jnp.full((tq, LANES), floor_key, jnp.int32)
    _, lo, _, n_ge = lax.while_loop(
        bis_cond, bis_body, (jnp.int32(0), lo0, _to_sortable(row_max), count_ge(_from_sortable(lo0))))
    thr = _lanes(_from_sortable(lo), tq)

    tie = n_ge > kf
    j_scr[...] = jnp.full((tq, LANES), T, jnp.int32)

    @pl.when(_any(tie))
    def _():
        need = kf - count(lambda s, k0: s > thr)

        def jb(_, c):
            jlo, jhi = c
            mid = (jlo + jhi) >> 1
            midw = _lanes(mid, tq)
            cnt = count(lambda s, k0: (s == thr) & (k0 + _iota((tq, tq), 1) <= midw))
            ok = cnt >= need
            return jnp.where(ok, jlo, mid + 1), jnp.where(ok, mid, jhi)
        jlo, _ = lax.fori_loop(0, seq_bits, jb, (jnp.zeros((tq, LANES), jnp.int32),
                                                 jnp.full((tq, LANES), T - 1, jnp.int32)))
        j_scr[...] = jnp.where(tie, jlo, T)

    jmax = _lanes(j_scr[...], tq)

    def bias_body(c, carry):
        k0 = pl.multiple_of(c * tq, tq)
        qpos = q0 + _iota((tq, tq), 0)
        kpos = k0 + _iota((tq, tq), 1)
        sc = s_scr[:, pl.ds(k0, tq)]
        picked = (sc > thr) | ((sc == thr) & (kpos <= jmax))
        s_scr[:, pl.ds(k0, tq)] = jnp.where(picked & (kpos <= qpos), 0.0, NEG_INF)
        return carry

    lax.fori_loop(0, n_kc, bias_body, 0)

    qs = [q_ref[0, h] for h in range(H)]
    _flash_init(m_scr, acc_scr)

    def att_body(c, carry):
        k0 = pl.multiple_of(c * tq, tq)
        bias = s_scr[:, pl.ds(k0, tq)]
        for h in range(H):
            s = jnp.dot(qs[h], kt_ref[0, :, pl.ds(k0, tq)], preferred_element_type=jnp.float32) + bias
            _flash_update(h, s, v_ref[0, pl.ds(k0, tq), :], m_scr, acc_scr)
        return carry

    lax.fori_loop(0, n_kc, att_body, 0)
    for h in range(H):
        o_ref[0, :, h * hd:(h + 1) * hd] = _flash_out(h, acc_scr)


def _dsa_call(q, kt, v, iq, ikt, iw):
    B, H, T, _ = q.shape
    tq = ATT_TILE
    topk = min(DSA_TOPK, T // 4)
    assert tq >= topk
    qs, ks, vs, out = _att_specs(B, H, T, tq, True)
    return pl.pallas_call(
        functools.partial(_dsa_kernel, tq=tq, topk=topk, idx_scale=(DSA_IDX_HEADS * DSA_IDX_DIM) ** -0.5,
                          seq_bits=max(1, math.ceil(math.log2(T)))),
        grid=(B, T // tq),
        in_specs=[qs, ks, vs, pl.BlockSpec((1, tq, 2 * LANES), lambda b, i: (b, i, 0)), ks,
                  pl.BlockSpec((1, tq, LANES), lambda b, i: (b, i, 0))],
        out_specs=out,
        out_shape=jax.ShapeDtypeStruct((B, T, H * HEAD_DIM), jnp.float32),
        scratch_shapes=[pltpu.VMEM((tq, T), jnp.float32), pltpu.VMEM((tq, LANES), jnp.int32),
                        pltpu.VMEM((tq, LANES), jnp.float32)] + _att_scratch(tq),
        compiler_params=_params("arbitrary", "arbitrary"),
        name="dsa_attention",
    )(q, kt, v, iq, ikt, iw)


def _out_kernel(x_ref, ada_ref, o1_ref, o2_ref, o3_ref, o4_ref, gn_ref, w_ref, y_ref):
    a = ada_ref[0]
    gn = gn_ref[...]
    y = jnp.concatenate([_rms(o[0], gn[i:i + 1]).astype(MXU_DTYPE)
                         for i, o in enumerate((o1_ref, o2_ref, o3_ref, o4_ref))], axis=-1)
    y_ref[...] = x_ref[...] + a[5:6] * jnp.dot(y, w_ref[...], preferred_element_type=jnp.float32)


def _out_call(x2d, ada_l, groups, group_norm, w_out, B, T):
    N, D = x2d.shape
    tm = ROW_TILE
    tpb = T // tm
    grp = pl.BlockSpec((1, tm, GROUP_WIDTH), lambda i: (i // tpb, i % tpb, 0))
    return pl.pallas_call(
        _out_kernel,
        grid=(N // tm,),
        in_specs=[pl.BlockSpec((tm, D), lambda i: (i, 0)),
                  pl.BlockSpec((1, N_ADA, D), lambda i: (i // tpb, 0, 0)),
                  grp, grp, grp, grp,
                  pl.BlockSpec((N_GROUPS, GROUP_WIDTH), lambda i: (0, 0)),
                  pl.BlockSpec((MIX_WIDTH, D), lambda i: (0, 0))],
        out_specs=pl.BlockSpec((tm, D), lambda i: (i, 0)),
        out_shape=jax.ShapeDtypeStruct((N, D), jnp.float32),
        compiler_params=_params("arbitrary"),
        name="mixer_out_proj",
    )(x2d, ada_l, *groups, group_norm, w_out.astype(MXU_DTYPE))


def _mixer_groups(x2d, ada_l, tables, mix_norm, w_in, mla_q_norm, mla_w_uq, mla_kv_norm, mla_w_uk, mla_w_uv,
                  nsa_pe_k, nsa_pe_v, nsa_cmp_k_w1, nsa_cmp_k_w2, nsa_cmp_v_w1, nsa_cmp_v_w2, B, T):
    (mq, mkt, mv, lq, lkt, lv, nq, nkc, nvc, nkst, nvs, nkwt, nvw, ngate,
     dq, dkt, dv, diq, dikt, diw) = _proj_call(
        x2d, ada_l, mix_norm, w_in, tables, mla_q_norm, mla_w_uq, mla_kv_norm, mla_w_uk, mla_w_uv, B, T)
    o_moba = _moba_call(mq, mkt, mv)
    o_mla = _mla_call(lq, lkt, lv)
    kcmp, vcmp = _cmp_call(nkc, nvc, nsa_pe_k, nsa_pe_v, nsa_cmp_k_w1, nsa_cmp_k_w2, nsa_cmp_v_w1, nsa_cmp_v_w2)
    o_nsa = _nsa_call(nq, kcmp, vcmp, nkst, nvs, nkwt, nvw, ngate)
    o_dsa = _dsa_call(dq, dkt, dv, diq, dikt, diw)
    return o_moba, o_mla, o_nsa, o_dsa


def kernel(x, c, ada_w, ada_b, ffn1_norm, ffn1_w_gate, ffn1_w_up, ffn1_w_down, mix_norm, w_in, mla_q_norm, mla_w_uq, mla_kv_norm, mla_w_uk, mla_w_uv, nsa_pe_k, nsa_pe_v, nsa_cmp_k_w1, nsa_cmp_k_w2, nsa_cmp_v_w1, nsa_cmp_v_w2, group_norm, w_out, ffn2_norm, ffn2_w_gate, ffn2_w_up, ffn2_w_down, final_norm):
    B, T, D = x.shape
    L = ada_w.shape[0]
    assert D == D_MODEL and T % ROW_TILE == 0 and T % ATT_TILE == 0
    assert ATT_TILE % MOBA_BLOCK == 0 and ATT_TILE % NSA_SEL_BLOCK == 0 and ATT_TILE >= NSA_WINDOW
    tpb = T // ROW_TILE
    ada = _ada_call(c, ada_w, ada_b)
    tables = _rope_tables(T)
    x2d = x.reshape(B * T, D)
    for l in range(L):
        x2d = _ffn_call(x2d, ada[l], ffn1_norm[l], ffn1_w_gate[l], ffn1_w_up[l], ffn1_w_down[l], 0, tpb)
        groups = _mixer_groups(x2d, ada[l], tables, mix_norm[l], w_in[l], mla_q_norm[l], mla_w_uq[l],
                               mla_kv_norm[l], mla_w_uk[l], mla_w_uv[l], nsa_pe_k[l], nsa_pe_v[l],
                               nsa_cmp_k_w1[l], nsa_cmp_k_w2[l], nsa_cmp_v_w1[l], nsa_cmp_v_w2[l], B, T)
        x2d = _out_call(x2d, ada[l], groups, group_norm[l], w_out[l], B, T)
        x2d = _ffn_call(x2d, ada[l], ffn2_norm[l], ffn2_w_gate[l], ffn2_w_up[l], ffn2_w_down[l], 6, tpb,
                        final_gain=final_norm if l == L - 1 else None)
    return x2d.reshape(B, T, D)
```

```python
import functools
import math

import numpy as np
import jax
import jax.numpy as jnp
from jax import lax
from jax.experimental import pallas as pl
from jax.experimental.pallas import tpu as pltpu

D_MODEL = 1024
N_GROUPS = 4
HEAD_DIM = 64
GROUP_HEADS = D_MODEL // (N_GROUPS * HEAD_DIM)
GROUP_WIDTH = GROUP_HEADS * HEAD_DIM
MIX_WIDTH = N_GROUPS * GROUP_WIDTH
D_FF = 256 * ((8 * D_MODEL + 3 * 256 - 1) // (3 * 256))
N_ADA = 9
FFN_RESIDUAL_WEIGHT = 0.5
ROPE_THETA = 10000.0
RMS_EPS = 1e-6
NEG_INF = -1e30

MOBA_BLOCK = 256
MOBA_TOPK = 3

MLA_Q_LORA = D_MODEL // 4
MLA_KV_LORA = D_MODEL // 8
MLA_NOPE = HEAD_DIM
MLA_ROPE = HEAD_DIM // 2
MLA_V = HEAD_DIM
MLA_QK = MLA_NOPE + MLA_ROPE

NSA_CMP_LEN = 32
NSA_CMP_STRIDE = 16
NSA_CMP_HIDDEN = 4 * HEAD_DIM
NSA_SEL_BLOCK = 64
NSA_SEL_TOPN = 16
NSA_WINDOW = 512
NSA_FORCE_SCORE = 1e4

DSA_TOPK = 256
DSA_IDX_HEADS = 8
DSA_IDX_DIM = 32

IN_NAMES = ("mq", "mk", "mv", "cq", "ckv", "kr", "nq", "nkc", "nvc", "nks", "nvs", "nkw", "nvw",
            "ngate", "dq", "dk", "dv", "diq", "dik", "diw")
IN_SIZES = (
    GROUP_WIDTH, GROUP_WIDTH, GROUP_WIDTH,
    MLA_Q_LORA, MLA_KV_LORA, MLA_ROPE,
    GROUP_WIDTH, HEAD_DIM, HEAD_DIM, HEAD_DIM, HEAD_DIM,
    HEAD_DIM, HEAD_DIM, 3 * GROUP_HEADS,
    GROUP_WIDTH, HEAD_DIM, HEAD_DIM,
    DSA_IDX_HEADS * DSA_IDX_DIM, DSA_IDX_DIM, DSA_IDX_HEADS,
)
N_IN = sum(IN_SIZES)

LANES = 128
MXU_DTYPE = jnp.bfloat16
VMEM_LIMIT = 56 * 1024 * 1024

ATT_TILE = 512
ROW_TILE = 512
FF_CHUNK = 256

LOG2E = math.log2(math.e)
M_INIT = -1e29
DEN_LANE = HEAD_DIM


def _params(*semantics):
    return pltpu.CompilerParams(dimension_semantics=semantics, vmem_limit_bytes=VMEM_LIMIT)


def _dot(a, b):
    return jnp.dot(a.astype(MXU_DTYPE), b.astype(MXU_DTYPE), preferred_element_type=jnp.float32)


def _dot_nt(a, b):
    return lax.dot_general(a.astype(MXU_DTYPE), b.astype(MXU_DTYPE), (((1,), (1,)), ((), ())),
                           preferred_element_type=jnp.float32)


def _rms(x, g):
    return x * lax.rsqrt(jnp.mean(x * x, axis=-1, keepdims=True) + RMS_EPS) * g


def _silu(x):
    return x * (1.0 / (1.0 + jnp.exp(-x)))


def _iota(shape, dim):
    return lax.broadcasted_iota(jnp.int32, shape, dim)


def _log2(n):
    assert n & (n - 1) == 0
    return n.bit_length() - 1


def _any(pred):
    return jnp.max(jnp.where(pred, 1.0, 0.0)) > 0.5


def _lanes(x, width):
    return x if width == LANES else jnp.tile(x, (1, width // LANES))


def _ada_kernel(c_ref, w_ref, b_ref, o_ref):
    o_ref[0] = _dot(_silu(c_ref[...]), w_ref[0]) + b_ref[0]


def _ada_call(c, ada_w, ada_b):
    L, D, _ = ada_w.shape
    B = c.shape[0]
    out = pl.pallas_call(
        _ada_kernel,
        grid=(L, N_ADA),
        in_specs=[
            pl.BlockSpec((B, D), lambda l, k: (0, 0)),
            pl.BlockSpec((1, D, D), lambda l, k: (l, 0, k)),
            pl.BlockSpec((1, 1, D), lambda l, k: (l, 0, k)),
        ],
        out_specs=pl.BlockSpec((1, B, D), lambda l, k: (l, 0, k)),
        out_shape=jax.ShapeDtypeStruct((L, B, N_ADA * D), jnp.float32),
        compiler_params=_params("arbitrary", "arbitrary"),
        name="ada_proj",
    )(c, ada_w, ada_b.reshape(L, 1, N_ADA * D))
    return out.reshape(L, B, N_ADA, D)


def _ffn_kernel(x_ref, ada_ref, gn_ref, wg_ref, wu_ref, wd_ref, *rest, k0, n_chunks, final):
    if final:
        fg_ref, o_ref, h_scr, acc_scr = rest
    else:
        o_ref, h_scr, acc_scr = rest
    a = ada_ref[0]
    h = _rms(x_ref[...], gn_ref[...]) * (1.0 + a[k0 + 1:k0 + 2]) + a[k0:k0 + 1]
    h_scr[...] = h.astype(MXU_DTYPE)
    acc_scr[...] = jnp.zeros_like(acc_scr)

    def body(j, carry):
        hh = h_scr[...]
        g = jnp.dot(hh, wg_ref[j], preferred_element_type=jnp.float32)
        u = jnp.dot(hh, wu_ref[j], preferred_element_type=jnp.float32)
        act = (_silu(g) * u).astype(MXU_DTYPE)
        acc_scr[...] += jnp.dot(act, wd_ref[j], preferred_element_type=jnp.float32)
        return carry

    lax.fori_loop(0, n_chunks, body, 0)
    y = x_ref[...] + (FFN_RESIDUAL_WEIGHT * a[k0 + 2:k0 + 3]) * acc_scr[...]
    if final:
        y = _rms(y, fg_ref[...])
    o_ref[...] = y


def _ffn_call(x2d, ada_l, norm_g, w_gate, w_up, w_down, k0, tiles_per_batch, final_gain=None):
    N, D = x2d.shape
    F = w_gate.shape[1]
    n_chunks = F // FF_CHUNK
    tm = ROW_TILE
    wg = w_gate.astype(MXU_DTYPE).reshape(D, n_chunks, FF_CHUNK).transpose(1, 0, 2)
    wu = w_up.astype(MXU_DTYPE).reshape(D, n_chunks, FF_CHUNK).transpose(1, 0, 2)
    wd = w_down.astype(MXU_DTYPE).reshape(n_chunks, FF_CHUNK, D)
    const3 = lambda i: (0, 0, 0)
    in_specs = [
        pl.BlockSpec((tm, D), lambda i: (i, 0)),
        pl.BlockSpec((1, N_ADA, D), lambda i: (i // tiles_per_batch, 0, 0)),
        pl.BlockSpec((1, D), lambda i: (0, 0)),
        pl.BlockSpec((n_chunks, D, FF_CHUNK), const3),
        pl.BlockSpec((n_chunks, D, FF_CHUNK), const3),
        pl.BlockSpec((n_chunks, FF_CHUNK, D), const3),
    ]
    args = [x2d, ada_l, norm_g.reshape(1, D), wg, wu, wd]
    final = final_gain is not None
    if final:
        in_specs.append(pl.BlockSpec((1, D), lambda i: (0, 0)))
        args.append(final_gain.reshape(1, D))
    return pl.pallas_call(
        functools.partial(_ffn_kernel, k0=k0, n_chunks=n_chunks, final=final),
        grid=(N // tm,),
        in_specs=in_specs,
        out_specs=pl.BlockSpec((tm, D), lambda i: (i, 0)),
        out_shape=jax.ShapeDtypeStruct((N, D), jnp.float32),
        scratch_shapes=[pltpu.VMEM((tm, D), MXU_DTYPE), pltpu.VMEM((tm, D), jnp.float32)],
        compiler_params=_params("arbitrary"),
        name="ffn",
    )(*args)


G_QMAIN, G_QSWAP, G_KC, G_V, G_MISC, ROW_GROUPS = 0, 12, 24, 26, 33, 41
T_MK, T_MKS, T_SW, T_SWS, T_DK, T_DKS, T_IK, T_ROWS = 0, 256, 512, 640, 768, 896, 1024, 1152


def _swap_halves(c, width):
    return c.reshape(-1, 2, width // 2)[:, ::-1, :].reshape(-1)


def _proj_indices():
    off = dict(zip(IN_NAMES, np.cumsum((0,) + IN_SIZES[:-1]).tolist()))
    size = dict(zip(IN_NAMES, IN_SIZES))
    cols = lambda name: np.arange(off[name], off[name] + size[name])
    zero = lambda n: np.full((n,), N_IN)
    hd = HEAD_DIM

    def head_groups(c):
        return np.concatenate([np.concatenate([c[i:i + hd], zero(LANES - hd)]) for i in range(0, c.size, hd)])

    q = np.concatenate([cols("mq"), cols("nq"), cols("dq")])
    row = np.concatenate([
        head_groups(q), head_groups(_swap_halves(q, hd)),
        head_groups(cols("nkc")), head_groups(_swap_halves(cols("nkc"), hd)),
        head_groups(np.concatenate([cols("mv"), cols("nvs"), cols("nvw"), cols("dv")])),
        cols("cq"), cols("ckv"), head_groups(cols("nvc")), cols("diq"),
        cols("ngate"), zero(LANES - size["ngate"]), cols("diw"), zero(LANES - size["diw"])])
    assert row.size == ROW_GROUPS * LANES
    sw = np.concatenate([cols("nks"), cols("nkw")])
    dk_main = np.concatenate([cols("dk"), cols("kr"), zero(LANES - hd - MLA_ROPE)])
    dk_swap = np.concatenate([_swap_halves(cols("dk"), hd), _swap_halves(cols("kr"), MLA_ROPE),
                              zero(LANES - hd - MLA_ROPE)])
    tr = np.concatenate([cols("mk"), _swap_halves(cols("mk"), hd), sw, _swap_halves(sw, hd),
                         dk_main, dk_swap, np.tile(cols("dik"), LANES // DSA_IDX_DIM)])
    assert tr.size == T_ROWS
    return row, tr


def _proj_kernel(x_ref, ada_ref, gn_ref, wr_ref, wt_ref, rtab_ref, ttab_ref,
                 qn_ref, wuq_ref, kvn_ref, wukt_ref, wuv_ref,
                 mq_ref, mkt_ref, mv_ref, lq_ref, lkt_ref, lv_ref,
                 nq_ref, nkc_ref, nvc_ref, nkst_ref, nvs_ref, nkwt_ref, nvw_ref, ng_ref,
                 dq_ref, dkt_ref, dv_ref, diq_ref, dikt_ref, diw_ref, *, tm, tpb, n_moba, n_sel):
    H, hd, G = GROUP_HEADS, HEAD_DIM, LANES
    t0 = (pl.program_id(0) % tpb) * tm
    a = ada_ref[0]
    h = (_rms(x_ref[...], gn_ref[...]) * (1.0 + a[4:5]) + a[3:4]).astype(MXU_DTYPE)

    def rows(g0, n):
        return jnp.dot(h, wr_ref[:, g0 * G:(g0 + n) * G], preferred_element_type=jnp.float32)

    def cols(r0, n):
        return _dot_nt(wt_ref[r0:r0 + n, :], h)

    roped_q = (rows(G_QMAIN, 3 * H) * _lanes(rtab_ref[0], 3 * H * G)
               + rows(G_QSWAP, 3 * H) * _lanes(rtab_ref[1], 3 * H * G))
    for hh in range(H):
        mq_ref[0, hh] = roped_q[:, hh * G:(hh + 1) * G].astype(mq_ref.dtype)
        nq_ref[0, hh] = roped_q[:, (H + hh) * G:(H + hh + 1) * G].astype(nq_ref.dtype)
        dq_ref[0, hh] = roped_q[:, (2 * H + hh) * G:(2 * H + hh + 1) * G].astype(dq_ref.dtype)
    kc = rows(G_KC, 2)
    nkc_ref[0] = (kc[:, :G] * rtab_ref[2] + kc[:, G:] * rtab_ref[3])[:, :hd].astype(nkc_ref.dtype)

    ones_hi = jnp.where(_iota((1, G), 1) >= DEN_LANE, 1.0, 0.0)
    v = rows(G_V, H + 3) + _lanes(ones_hi, (H + 3) * G)
    for hh in range(H):
        mv_ref[0, hh] = v[:, hh * G:(hh + 1) * G].astype(mv_ref.dtype)
    nvs_ref[0] = v[:, H * G:(H + 1) * G].astype(nvs_ref.dtype)
    nvw_ref[0] = v[:, (H + 1) * G:(H + 2) * G].astype(nvw_ref.dtype)
    dv_ref[0] = v[:, (H + 2) * G:(H + 3) * G].astype(dv_ref.dtype)

    misc = rows(G_MISC, 8)
    cq = misc[:, :MLA_Q_LORA]
    ckv = misc[:, MLA_Q_LORA:MLA_Q_LORA + MLA_KV_LORA]
    nvc_ref[0] = misc[:, 3 * G:3 * G + hd].astype(nvc_ref.dtype)
    diq_ref[0] = misc[:, 4 * G:6 * G].astype(diq_ref.dtype)
    ng_ref[0] = misc[:, 6 * G:7 * G]
    diw_ref[0] = misc[:, 7 * G:8 * G]

    tok = t0 + _iota((hd, tm), 1)
    rid = _iota((hd, tm), 0)
    oh_moba = jnp.where((rid & (n_moba - 1)) == (tok >> _log2(MOBA_BLOCK)), 1.0, 0.0)
    oh_sel = jnp.where((rid == (tok >> _log2(NSA_SEL_BLOCK))) & (rid < n_sel), 1.0, 0.0)
    zeros_lo = jnp.zeros((hd, tm), jnp.float32)
    ta_c, ta_s, tb_c, tb_s = ttab_ref[0], ttab_ref[1], ttab_ref[2], ttab_ref[3]
    mkt = (cols(T_MK, H * hd) * jnp.tile(ta_c, (H * hd // G, 1))
           + cols(T_MKS, H * hd) * jnp.tile(ta_s, (H * hd // G, 1)))
    for hh in range(H):
        mine = (rid >> _log2(n_moba)) == hh
        mkt_ref[0, hh] = jnp.concatenate([mkt[hh * hd:(hh + 1) * hd], jnp.where(mine, oh_moba, 0.0)],
                                         axis=0).astype(mkt_ref.dtype)
    sw = cols(T_SW, G) * ta_c + cols(T_SWS, G) * ta_s
    nkst_ref[0] = jnp.concatenate([sw[:hd], oh_sel], axis=0).astype(nkst_ref.dtype)
    nkwt_ref[0] = jnp.concatenate([sw[hd:], zeros_lo], axis=0).astype(nkwt_ref.dtype)
    dkr = cols(T_DK, G) * tb_c + cols(T_DKS, G) * tb_s
    rid2 = _iota((G, tm), 0)
    dkt_ref[0] = jnp.where(rid2 < hd, dkr, 0.0).astype(dkt_ref.dtype)
    kpe_rows = jnp.where((rid2 >= MLA_NOPE) & (rid2 < MLA_QK), dkr, 0.0)
    dikt_ref[0] = cols(T_IK, G).astype(dikt_ref.dtype)

    cqn = _rms(cq, qn_ref[...]).astype(MXU_DTYPE)
    lq = (jnp.dot(cqn, wuq_ref[:, :H * G], preferred_element_type=jnp.float32) * _lanes(rtab_ref[4], H * G)
          + jnp.dot(cqn, wuq_ref[:, H * G:], preferred_element_type=jnp.float32) * _lanes(rtab_ref[5], H * G))
    ckvn = _rms(ckv, kvn_ref[...]).astype(MXU_DTYPE)
    knt = _dot_nt(wukt_ref[...], ckvn)
    lv = jnp.dot(ckvn, wuv_ref[...], preferred_element_type=jnp.float32) + _lanes(ones_hi, H * G)
    for hh in range(H):
        lq_ref[0, hh] = lq[:, hh * G:(hh + 1) * G].astype(lq_ref.dtype)
        lkt_ref[0, hh] = (knt[hh * G:(hh + 1) * G] + kpe_rows).astype(lkt_ref.dtype)
        lv_ref[0, hh] = lv[:, hh * G:(hh + 1) * G].astype(lv_ref.dtype)


def _rope_tables(T):
    def cs(dim):
        inv_freq = 1.0 / (ROPE_THETA ** (np.arange(0, dim, 2, dtype=np.float32) / dim))
        ang = jnp.arange(T, dtype=jnp.float32)[:, None] * jnp.asarray(inv_freq, jnp.float32)[None, :]
        cos, sin = jnp.cos(ang), jnp.sin(ang)
        return jnp.concatenate([cos, cos], axis=-1), jnp.concatenate([-sin, sin], axis=-1)

    c64, s64 = cs(HEAD_DIM)
    c32, s32 = cs(MLA_ROPE)
    pad = lambda t, n: jnp.concatenate([t, jnp.zeros((T, n), jnp.float32)], axis=-1)
    sc = HEAD_DIM ** -0.5 * LOG2E
    sl = MLA_QK ** -0.5 * LOG2E
    ones = jnp.ones((T, MLA_NOPE), jnp.float32)
    rest = LANES - MLA_QK
    rtab = jnp.stack([
        pad(c64 * sc, LANES - HEAD_DIM), pad(s64 * sc, LANES - HEAD_DIM),
        pad(c64, LANES - HEAD_DIM), pad(s64, LANES - HEAD_DIM),
        pad(jnp.concatenate([ones, c32], axis=-1) * sl, rest),
        pad(jnp.concatenate([0.0 * ones, s32], axis=-1) * sl, rest)])
    ttab = jnp.stack([
        jnp.concatenate([c64, c64], axis=-1).T, jnp.concatenate([s64, s64], axis=-1).T,
        pad(jnp.concatenate([c64, c32], axis=-1), rest).T, pad(jnp.concatenate([s64, s32], axis=-1), rest).T])
    return rtab, ttab


def _proj_call(x2d, ada_l, norm_g, w_in, tables, mla_q_norm, mla_w_uq, mla_kv_norm, mla_w_uk, mla_w_uv, B, T):
    N, D = x2d.shape
    H, G = GROUP_HEADS, LANES
    tm = ROW_TILE
    tpb = T // tm
    n_moba, n_sel = T // MOBA_BLOCK, T // NSA_SEL_BLOCK
    assert H * n_moba <= LANES - HEAD_DIM and n_sel <= LANES - HEAD_DIM and n_moba & (n_moba - 1) == 0
    zcol = lambda w: jnp.concatenate([w, jnp.zeros((w.shape[0], 1), w.dtype)], axis=1)
    row_idx, tr_idx = _proj_indices()
    w_ext = zcol(w_in)
    w_row = w_ext[:, row_idx].astype(MXU_DTYPE)
    w_tr = w_ext[:, tr_idx].T.astype(MXU_DTYPE)
    zq = mla_w_uq.shape[1]
    per_head = np.arange(H * MLA_QK).reshape(H, MLA_QK)
    main = np.concatenate([np.concatenate([per_head[i], np.full((G - MLA_QK,), zq)]) for i in range(H)])
    part = np.concatenate([np.concatenate([np.full((MLA_NOPE,), zq), _swap_halves(per_head[i, MLA_NOPE:], MLA_ROPE),
                                           np.full((G - MLA_QK,), zq)]) for i in range(H)])
    wuq = zcol(mla_w_uq)[:, np.concatenate([main, part])].astype(MXU_DTYPE)
    zv = mla_w_uk.shape[1]
    grp = np.concatenate([np.concatenate([np.arange(i * HEAD_DIM, (i + 1) * HEAD_DIM), np.full((G - HEAD_DIM,), zv)])
                          for i in range(H)])
    wukt = zcol(mla_w_uk)[:, grp].T.astype(MXU_DTYPE)
    wuv = zcol(mla_w_uv)[:, grp].astype(MXU_DTYPE)
    rtab, ttab = tables

    row = lambda i: (i, 0)
    const2 = lambda i: (0, 0)
    in_specs = [
        pl.BlockSpec((tm, D), row),
        pl.BlockSpec((1, N_ADA, D), lambda i: (i // tpb, 0, 0)),
        pl.BlockSpec((1, D), const2),
        pl.BlockSpec(w_row.shape, const2),
        pl.BlockSpec(w_tr.shape, const2),
        pl.BlockSpec((6, tm, G), lambda i: (0, i % tpb, 0)),
        pl.BlockSpec((4, G, tm), lambda i: (0, 0, i % tpb)),
        pl.BlockSpec((1, MLA_Q_LORA), const2),
        pl.BlockSpec(wuq.shape, const2),
        pl.BlockSpec((1, MLA_KV_LORA), const2),
        pl.BlockSpec(wukt.shape, const2),
        pl.BlockSpec(wuv.shape, const2),
    ]
    dt = MXU_DTYPE
    hq = (jax.ShapeDtypeStruct((B, H, T, G), dt), pl.BlockSpec((1, H, tm, G), lambda i: (i // tpb, 0, i % tpb, 0)))
    hkt = (jax.ShapeDtypeStruct((B, H, G, T), dt), pl.BlockSpec((1, H, G, tm), lambda i: (i // tpb, 0, 0, i % tpb)))
    srow = lambda d, t=dt: (jax.ShapeDtypeStruct((B, T, d), t), pl.BlockSpec((1, tm, d), lambda i: (i // tpb, i % tpb, 0)))
    skt = (jax.ShapeDtypeStruct((B, G, T), dt), pl.BlockSpec((1, G, tm), lambda i: (i // tpb, 0, i % tpb)))
    outs = [hq, hkt, hq,
            hq, hkt, hq,
            hq, srow(HEAD_DIM), srow(HEAD_DIM), skt, srow(G), skt, srow(G), srow(G, jnp.float32),
            hq, skt, srow(G), srow(2 * G), skt, srow(G, jnp.float32)]
    return pl.pallas_call(
        functools.partial(_proj_kernel, tm=tm, tpb=tpb, n_moba=n_moba, n_sel=n_sel),
        grid=(N // tm,),
        in_specs=in_specs,
        out_specs=[o[1] for o in outs],
        out_shape=[o[0] for o in outs],
        compiler_params=_params("arbitrary"),
        name="mixer_in_proj",
    )(x2d, ada_l, norm_g.reshape(1, D), w_row, w_tr, rtab, ttab,
      mla_q_norm.reshape(1, -1), wuq, mla_kv_norm.reshape(1, -1), wukt, wuv)


def _flash_init(m_scr, acc_scr):
    m_scr[...] = jnp.full(m_scr.shape, M_INIT, jnp.float32)
    acc_scr[...] = jnp.zeros_like(acc_scr)


def _flash_update(h, s, v, m_scr, acc_scr):
    m_prev = m_scr[h]
    m_new = jnp.maximum(m_prev, jnp.max(s, axis=-1, keepdims=True))
    p = jnp.exp2(s - _lanes(m_new, s.shape[1]))
    acc_scr[h] = jnp.exp2(m_prev - m_new) * acc_scr[h] + _dot(p, v)
    m_scr[h] = m_new


def _flash_out(h, acc_scr):
    acc = acc_scr[h]
    den = acc[:, DEN_LANE:DEN_LANE + 1]
    return acc[:, :HEAD_DIM] / jnp.where(den > 0.0, den, 1.0)


def _causal_bias(t):
    return jnp.where(_iota((t, t), 1) <= _iota((t, t), 0), 0.0, NEG_INF)


def _rank_desc(x):
    n = x.shape[0]
    row = _iota(x.shape, 0)
    rank = jnp.zeros(x.shape, jnp.float32)
    for j in range(n):
        cand = x[j:j + 1, :]
        rank = rank + jnp.where(cand > x, 1.0, jnp.where((cand == x) & (row > j), 1.0, 0.0))
    return rank


def _att_scratch(tq):
    return [pltpu.VMEM((GROUP_HEADS, tq, LANES), jnp.float32), pltpu.VMEM((GROUP_HEADS, tq, LANES), jnp.float32)]


def _att_specs(B, H, T, tq, shared_kv):
    q = pl.BlockSpec((1, H, tq, LANES), lambda b, i: (b, 0, i, 0))
    if shared_kv:
        kt = pl.BlockSpec((1, LANES, T), lambda b, i: (b, 0, 0))
        v = pl.BlockSpec((1, T, LANES), lambda b, i: (b, 0, 0))
    else:
        kt = pl.BlockSpec((1, H, LANES, T), lambda b, i: (b, 0, 0, 0))
        v = pl.BlockSpec((1, H, T, LANES), lambda b, i: (b, 0, 0, 0))
    out = pl.BlockSpec((1, tq, H * HEAD_DIM), lambda b, i: (b, i, 0))
    return q, kt, v, out


def _mla_kernel(q_ref, kt_ref, v_ref, o_ref, m_scr, acc_scr, *, tq):
    H = GROUP_HEADS
    qi = pl.program_id(1)
    qs = [q_ref[0, h] for h in range(H)]
    _flash_init(m_scr, acc_scr)

    def chunk(k0, bias):
        for h in range(H):
            s = jnp.dot(qs[h], kt_ref[0, h, :, pl.ds(k0, tq)], preferred_element_type=jnp.float32)
            if bias is not None:
                s = s + bias
            _flash_update(h, s, v_ref[0, h, pl.ds(k0, tq), :], m_scr, acc_scr)

    def body(c, carry):
        chunk(pl.multiple_of(c * tq, tq), None)
        return carry

    lax.fori_loop(0, qi, body, 0)
    chunk(pl.multiple_of(qi * tq, tq), _causal_bias(tq))
    for h in range(H):
        o_ref[0, :, h * HEAD_DIM:(h + 1) * HEAD_DIM] = _flash_out(h, acc_scr)


def _mla_call(q, kt, v):
    B, H, T, _ = q.shape
    tq = ATT_TILE
    qs, ks, vs, out = _att_specs(B, H, T, tq, False)
    return pl.pallas_call(
        functools.partial(_mla_kernel, tq=tq),
        grid=(B, T // tq),
        in_specs=[qs, ks, vs],
        out_specs=out,
        out_shape=jax.ShapeDtypeStruct((B, T, H * HEAD_DIM), jnp.float32),
        scratch_shapes=_att_scratch(tq),
        compiler_params=_params("arbitrary", "arbitrary"),
        name="mla_attention",
    )(q, kt, v)


def _moba_kernel(q_ref, kt_ref, v_ref, o_ref, kmean_scr, m_scr, acc_scr, *, tq, nb, topk):
    H, hd = GROUP_HEADS, HEAD_DIM
    qi = pl.program_id(1)
    q0 = qi * tq
    T = nb * MOBA_BLOCK

    @pl.when(qi == 0)
    def _():
        avg = jnp.where((_iota((nb, T), 1) >> _log2(MOBA_BLOCK)) == _iota((nb, T), 0), 1.0 / MOBA_BLOCK, 0.0)
        for h in range(H):
            kmean_scr[h] = _dot_nt(avg, kt_ref[0, h])

    blk = _iota((nb, tq), 0)
    own = (q0 + _iota((nb, tq), 1)) >> _log2(MOBA_BLOCK)
    past = blk < own
    bias_rows = [jnp.zeros((hd, tq), jnp.float32)]
    for h in range(H):
        gate = jnp.where(past, _dot_nt(kmean_scr[h], q_ref[0, h]), NEG_INF)
        allowed = ((_rank_desc(gate) < topk) & past) | (blk == own)
        bias_rows.append(jnp.where(allowed, 0.0, NEG_INF))
    if H * nb < LANES - hd:
        bias_rows.append(jnp.zeros((LANES - hd - H * nb, tq), jnp.float32))
    bias = jnp.concatenate(bias_rows, axis=0).T
    lane = _iota((tq, LANES), 1)
    qs = [jnp.where(lane < hd, q_ref[0, h], bias.astype(q_ref.dtype)) for h in range(H)]
    _flash_init(m_scr, acc_scr)

    def chunk(k0, bias):
        for h in range(H):
            s = jnp.dot(qs[h], kt_ref[0, h, :, pl.ds(k0, tq)], preferred_element_type=jnp.float32)
            if bias is not None:
                s = s + bias
            _flash_update(h, s, v_ref[0, h, pl.ds(k0, tq), :], m_scr, acc_scr)

    def body(c, carry):
        chunk(pl.multiple_of(c * tq, tq), None)
        return carry

    lax.fori_loop(0, qi, body, 0)
    chunk(pl.multiple_of(q0, tq), _causal_bias(tq))
    for h in range(H):
        o_ref[0, :, h * hd:(h + 1) * hd] = _flash_out(h, acc_scr)


def _moba_call(q, kt, v):
    B, H, T, _ = q.shape
    tq = ATT_TILE
    nb = T // MOBA_BLOCK
    qs, ks, vs, out = _att_specs(B, H, T, tq, False)
    return pl.pallas_call(
        functools.partial(_moba_kernel, tq=tq, nb=nb, topk=min(MOBA_TOPK, nb - 1)),
        grid=(B, T // tq),
        in_specs=[qs, ks, vs],
        out_specs=out,
        out_shape=jax.ShapeDtypeStruct((B, T, H * HEAD_DIM), jnp.float32),
        scratch_shapes=[pltpu.VMEM((H, nb, LANES), jnp.float32)] + _att_scratch(tq),
        compiler_params=_params("arbitrary", "arbitrary"),
        name="moba_attention",
    )(q, kt, v)


def _cmp_kernel(k_ref, v_ref, pek_ref, pev_ref, kw1_ref, kw2_ref, vw1_ref, vw2_ref, ko_ref, vo_ref, *, rows):
    half = NSA_CMP_STRIDE * HEAD_DIM

    def compress(t_ref, pe_ref, w1_ref, w2_ref):
        t = t_ref[0].astype(jnp.float32)
        first = _dot(t + pe_ref[0:1, :], w1_ref[0:half, :])
        second = _dot(t + pe_ref[1:2, :], w1_ref[half:, :])
        hid = first + pltpu.roll(second, rows - 1, 0)
        return _dot(_silu(hid), w2_ref[...])

    ko_ref[0] = compress(k_ref, pek_ref, kw1_ref, kw2_ref)
    vo_ref[0] = compress(v_ref, pev_ref, vw1_ref, vw2_ref)


def _cmp_call(kc, vc, pe_k, pe_v, k_w1, k_w2, v_w1, v_w2):
    B, T, d = kc.shape
    rows = T // NSA_CMP_STRIDE
    wide = NSA_CMP_STRIDE * d
    assert NSA_CMP_LEN == 2 * NSA_CMP_STRIDE
    const2 = lambda b: (0, 0)
    blk = pl.BlockSpec((1, rows, wide), lambda b: (b, 0, 0))
    out = pl.BlockSpec((1, rows, d), lambda b: (b, 0, 0))
    return pl.pallas_call(
        functools.partial(_cmp_kernel, rows=rows),
        grid=(B,),
        in_specs=[blk, blk, pl.BlockSpec((2, wide), const2), pl.BlockSpec((2, wide), const2),
                  pl.BlockSpec(k_w1.shape, const2), pl.BlockSpec(k_w2.shape, const2),
                  pl.BlockSpec(v_w1.shape, const2), pl.BlockSpec(v_w2.shape, const2)],
        out_specs=[out, out],
        out_shape=[jax.ShapeDtypeStruct((B, rows, d), jnp.float32)] * 2,
        compiler_params=_params("arbitrary"),
        name="nsa_compress",
    )(kc.reshape(B, rows, wide), vc.reshape(B, rows, wide), pe_k.reshape(2, wide), pe_v.reshape(2, wide),
      k_w1.astype(MXU_DTYPE), k_w2.astype(MXU_DTYPE), v_w1.astype(MXU_DTYPE), v_w2.astype(MXU_DTYPE))


def _nsa_kernel(q_ref, kcmp_ref, vcmp_ref, kst_ref, vs_ref, kwt_ref, vw_ref, g_ref, o_ref,
                oc_scr, os_scr, m_scr, acc_scr, *, tq, ncp, n_sel, topn):
    H, hd = GROUP_HEADS, HEAD_DIM
    qi = pl.program_id(1)
    q0 = qi * tq
    tq_col = q0 + _iota((tq, 1), 0)
    qraw = [q_ref[0, h] for h in range(H)]

    cmp_end = _iota((tq, ncp), 1) * NSA_CMP_STRIDE + (NSA_CMP_LEN - 1)
    m_c = cmp_end <= tq_col
    p_sum = jnp.zeros((tq, ncp), jnp.float32)
    for h in range(H):
        s = jnp.where(m_c, _dot_nt(qraw[h][:, :hd], kcmp_ref[0]), NEG_INF)
        e = jnp.where(m_c, jnp.exp2(s - jnp.max(s, axis=-1, keepdims=True)), 0.0)
        l = jnp.sum(e, axis=-1, keepdims=True)
        p = e / jnp.where(l > 0.0, l, 1.0)
        p_sum = p_sum + p
        oc_scr[h] = _dot(p, vcmp_ref[0])

    cmp_start = _iota((n_sel, ncp), 1) * NSA_CMP_STRIDE
    sel_start = _iota((n_sel, ncp), 0) * NSA_SEL_BLOCK
    overlap = (cmp_start < sel_start + NSA_SEL_BLOCK) & (cmp_start + NSA_CMP_LEN > sel_start)
    imp = _dot_nt(jnp.where(overlap, 1.0, 0.0), p_sum)
    sel_id = _iota((n_sel, tq), 0)
    own = (q0 + _iota((n_sel, tq), 1)) >> _log2(NSA_SEL_BLOCK)
    causal = sel_id <= own
    forced = causal & ((sel_id == 0) | (sel_id >= own - 1))
    imp = jnp.where(forced, NSA_FORCE_SCORE, jnp.where(causal, imp, -NSA_FORCE_SCORE))
    bias_rows = [jnp.zeros((hd, tq), jnp.float32), jnp.where(_rank_desc(imp) < topn, 0.0, NEG_INF)]
    if n_sel < LANES - hd:
        bias_rows.append(jnp.zeros((LANES - hd - n_sel, tq), jnp.float32))
    bias = jnp.concatenate(bias_rows, axis=0).T.astype(qraw[0].dtype)
    lane = _iota((tq, LANES), 1)
    qsel = [jnp.where(lane < hd, qraw[h], bias) for h in range(H)]

    _flash_init(m_scr, acc_scr)

    def sel_chunk(k0, extra):
        for h in range(H):
            s = jnp.dot(qsel[h], kst_ref[0, :, pl.ds(k0, tq)], preferred_element_type=jnp.float32)
            if extra is not None:
                s = s + extra
            _flash_update(h, s, vs_ref[0, pl.ds(k0, tq), :], m_scr, acc_scr)

    def sel_body(c, carry):
        sel_chunk(pl.multiple_of(c * tq, tq), None)
        return carry

    lax.fori_loop(0, qi, sel_body, 0)
    tri = _causal_bias(tq)
    sel_chunk(pl.multiple_of(q0, tq), tri)
    for h in range(H):
        os_scr[h] = _flash_out(h, acc_scr)

    _flash_init(m_scr, acc_scr)

    def win_chunk(k0, extra):
        for h in range(H):
            s = jnp.dot(qraw[h], kwt_ref[0, :, pl.ds(k0, tq)], preferred_element_type=jnp.float32) + extra
            _flash_update(h, s, vw_ref[0, pl.ds(k0, tq), :], m_scr, acc_scr)

    first = jnp.maximum(q0 - NSA_WINDOW + 1, 0) // tq

    def win_body(c, carry):
        k0 = pl.multiple_of(c * tq, tq)
        qpos = q0 + _iota((tq, tq), 0)
        kpos = k0 + _iota((tq, tq), 1)
        win_chunk(k0, jnp.where((kpos <= qpos) & (kpos > qpos - NSA_WINDOW), 0.0, NEG_INF))
        return carry

    lax.fori_loop(first, qi + 1, win_body, 0)
    for h in range(H):
        gates = 1.0 / (1.0 + jnp.exp(-g_ref[0][:, 3 * h:3 * h + 3]))
        o_ref[0, :, h * hd:(h + 1) * hd] = (gates[:, 0:1] * oc_scr[h] + gates[:, 1:2] * os_scr[h]
                                            + gates[:, 2:3] * _flash_out(h, acc_scr))


def _nsa_call(q, kcmp, vcmp, kst, vs, kwt, vw, gate_logits):
    B, H, T, _ = q.shape
    tq = ATT_TILE
    ncp = kcmp.shape[1]
    n_sel = T // NSA_SEL_BLOCK
    qs, ks, vsp, out = _att_specs(B, H, T, tq, True)
    cmp_spec = pl.BlockSpec((1, ncp, HEAD_DIM), lambda b, i: (b, 0, 0))
    return pl.pallas_call(
        functools.partial(_nsa_kernel, tq=tq, ncp=ncp, n_sel=n_sel, topn=min(NSA_SEL_TOPN, n_sel)),
        grid=(B, T // tq),
        in_specs=[qs, cmp_spec, cmp_spec, ks, vsp, ks, vsp,
                  pl.BlockSpec((1, tq, LANES), lambda b, i: (b, i, 0))],
        out_specs=out,
        out_shape=jax.ShapeDtypeStruct((B, T, H * HEAD_DIM), jnp.float32),
        scratch_shapes=[pltpu.VMEM((H, tq, HEAD_DIM), jnp.float32), pltpu.VMEM((H, tq, HEAD_DIM), jnp.float32)]
        + _att_scratch(tq),
        compiler_params=_params("arbitrary", "arbitrary"),
        name="nsa_attention",
    )(q, kcmp, vcmp, kst, vs, kwt, vw, gate_logits)


def _sortable(x):
    b = int(np.float32(x).view(np.int32))
    return b ^ ((b >> 31) & 0x7FFFFFFF)


def _from_sortable(k):
    return lax.bitcast_convert_type(k ^ ((k >> 31) & 0x7FFFFFFF), jnp.float32)


def _to_sortable(x):
    k = lax.bitcast_convert_type(x, jnp.int32)
    return k ^ ((k >> 31) & 0x7FFFFFFF)


COUNT_ROWS = 64
VALUE_STEPS = 24


def _dsa_kernel(q_ref, kt_ref, v_ref, iq_ref, ikt_ref, iw_ref, o_ref,
                s_scr, j_scr, t_scr, m_scr, acc_scr, *, tq, topk, idx_scale, seq_bits):
    H, hd = GROUP_HEADS, HEAD_DIM
    qi = pl.program_id(1)
    q0 = qi * tq
    n_kc = qi + 1
    T = s_scr.shape[1]
    reps = tq // LANES

    lane = _iota((tq, LANES), 1)
    quarter = lane >> _log2(DSA_IDX_DIM)
    per_group = LANES // DSA_IDX_DIM
    iq = iq_ref[0]
    iqh = [jnp.where(quarter == (h % per_group), iq[:, (h // per_group) * LANES:(h // per_group + 1) * LANES],
                     jnp.zeros((), iq.dtype)) for h in range(DSA_IDX_HEADS)]
    iw = iw_ref[0]
    iwb = [jnp.broadcast_to(iw[:, h:h + 1], (tq, LANES)) for h in range(DSA_IDX_HEADS)]

    def score_body(c, ends):
        top, bot = ends
        k0 = pl.multiple_of(c * tq, tq)
        ikt = ikt_ref[0, :, pl.ds(k0, tq)]
        acc = jnp.zeros((tq, tq), jnp.float32)
        for h in range(DSA_IDX_HEADS):
            sh = jnp.dot(iqh[h], ikt, preferred_element_type=jnp.float32)
            acc = acc + _lanes(iwb[h], tq) * jnp.maximum(sh, 0.0)
        qpos = q0 + _iota((tq, tq), 0)
        kpos = k0 + _iota((tq, tq), 1)
        val = acc * idx_scale + 0.0
        sc = jnp.where(kpos <= qpos, val, NEG_INF)
        s_scr[:, pl.ds(k0, tq)] = sc
        low = jnp.where(kpos <= qpos, val, np.inf)
        for r in range(reps):
            top = jnp.maximum(top, sc[:, r * LANES:(r + 1) * LANES])
            bot = jnp.minimum(bot, low[:, r * LANES:(r + 1) * LANES])
        return top, bot

    top, bot = lax.fori_loop(0, n_kc, score_body, (jnp.full((tq, LANES), NEG_INF, jnp.float32),
                                                   jnp.full((tq, LANES), np.inf, jnp.float32)))
    row_max = jnp.broadcast_to(jnp.max(top, axis=-1, keepdims=True), (tq, LANES))
    row_min = jnp.broadcast_to(jnp.min(bot, axis=-1, keepdims=True), (tq, LANES))
    lane_sum = jnp.ones((LANES, LANES), jnp.float32)

    def count_ge(t, strict=False):
        above = (lambda a, b: a > b) if strict else (lambda a, b: a >= b)
        t_scr[...] = t
        parts = []
        for rb in range(tq // COUNT_ROWS):
            rows = slice(rb * COUNT_ROWS, (rb + 1) * COUNT_ROWS)
            t_rb = t_scr[rows, :]

            def body(c, part):
                k0 = pl.multiple_of(c * tq, tq)
                sc = s_scr[rows, pl.ds(k0, tq)]
                for r in range(reps):
                    part = part + jnp.where(above(sc[:, r * LANES:(r + 1) * LANES], t_rb), 1.0, 0.0)
                return part

            parts.append(lax.fori_loop(0, n_kc, body, jnp.zeros((COUNT_ROWS, LANES), jnp.float32)))
        return _dot(jnp.concatenate(parts, axis=0), lane_sum)

    def count(pred):
        def body(c, part):
            k0 = pl.multiple_of(c * tq, tq)
            hit = jnp.where(pred(s_scr[:, pl.ds(k0, tq)], k0), 1.0, 0.0)
            for r in range(reps):
                part = part + hit[:, r * LANES:(r + 1) * LANES]
            return part
        part = lax.fori_loop(0, n_kc, body, jnp.zeros((tq, LANES), jnp.float32))
        return jnp.broadcast_to(jnp.sum(part, axis=-1, keepdims=True), (tq, LANES))

    kf = float(topk)
    floor_key = _sortable(NEG_INF)

    def bis_cond(c):
        _, lo, hi, _ = c
        return _any(lo < hi)

    def bis_body(c):
        it, lo, hi, n_lo = c
        key_mid = (lo | hi) - ((lo ^ hi) >> 1)
        val_mid = _to_sortable(0.5 * (_from_sortable(lo) + _from_sortable(hi)))
        steps = jnp.zeros((tq, LANES), jnp.int32) + it
        use_val = (val_mid > lo) & (val_mid <= hi) & (lo > floor_key) & (steps < VALUE_STEPS)
        mid = jnp.where(use_val, val_mid, key_mid)
        cnt = count_ge(_from_sortable(mid))
        ge = cnt >= kf
        lo = jnp.where(ge, mid, lo)
        n_lo = jnp.where(ge, cnt, n_lo)
        hi = jnp.where(cnt == kf, mid, jnp.where(ge, hi, mid - 1))
        return it + 1, lo, hi, n_lo

    zeros = jnp.zeros((tq, LANES), jnp.float32)
    n_pos = count_ge(zeros, strict=True)
    n_nonneg = count_ge(zeros)
    n_causal = q0 + _iota((tq, LANES), 0) + 1
    lo_neg = jnp.where(n_causal >= topk, _to_sortable(row_min), floor_key)
    n_neg = count_ge(_from_sortable(lo_neg))
    is_pos = n_pos >= kf
    is_zero = n_nonneg >= kf
    pick = lambda p, z, n: jnp.where(is_pos, p, jnp.where(is_zero, z, n))
    lo0 = pick(_sortable(np.float32(1e-45)), _sortable(0.0), lo_neg)
    hi0 = pick(_to_sortable(row_max), _sortable(0.0), _sortable(-0.0) - 1)
    _, lo, _, n_ge = lax.while_loop(bis_cond, bis_body, (jnp.int32(0), lo0, hi0, pick(n_pos, n_nonneg, n_neg)))
    thr = _lanes(_from_sortable(lo), tq)

    tie = n_ge > kf
    j_scr[...] = jnp.full((tq, LANES), T, jnp.int32)

    @pl.when(_any(tie))
    def _():
        need = kf - count(lambda s, k0: s > thr)

        def jb(_, c):
            jlo, jhi = c
            mid = (jlo + jhi) >> 1
            midw = _lanes(mid, tq)
            cnt = count(lambda s, k0: (s == thr) & (k0 + _iota((tq, tq), 1) <= midw))
            ok = cnt >= need
            return jnp.where(ok, jlo, mid + 1), jnp.where(ok, mid, jhi)
        jlo, _ = lax.fori_loop(0, seq_bits, jb, (jnp.zeros((tq, LANES), jnp.int32),
                                                 jnp.full((tq, LANES), T - 1, jnp.int32)))
        j_scr[...] = jnp.where(tie, jlo, T)

    jmax = _lanes(j_scr[...], tq)

    def bias_body(c, carry):
        k0 = pl.multiple_of(c * tq, tq)
        qpos = q0 + _iota((tq, tq), 0)
        kpos = k0 + _iota((tq, tq), 1)
        sc = s_scr[:, pl.ds(k0, tq)]
        picked = (sc > thr) | ((sc == thr) & (kpos <= jmax))
        s_scr[:, pl.ds(k0, tq)] = jnp.where(picked & (kpos <= qpos), 0.0, NEG_INF)
        return carry

    lax.fori_loop(0, n_kc, bias_body, 0)

    qs = [q_ref[0, h] for h in range(H)]
    _flash_init(m_scr, acc_scr)

    def att_body(c, carry):
        k0 = pl.multiple_of(c * tq, tq)
        bias = s_scr[:, pl.ds(k0, tq)]
        for h in range(H):
            s = jnp.dot(qs[h], kt_ref[0, :, pl.ds(k0, tq)], preferred_element_type=jnp.float32) + bias
            _flash_update(h, s, v_ref[0, pl.ds(k0, tq), :], m_scr, acc_scr)
        return carry

    lax.fori_loop(0, n_kc, att_body, 0)
    for h in range(H):
        o_ref[0, :, h * hd:(h + 1) * hd] = _flash_out(h, acc_scr)


def _dsa_call(q, kt, v, iq, ikt, iw):
    B, H, T, _ = q.shape
    tq = ATT_TILE
    topk = min(DSA_TOPK, T // 4)
    assert tq >= topk
    qs, ks, vs, out = _att_specs(B, H, T, tq, True)
    return pl.pallas_call(
        functools.partial(_dsa_kernel, tq=tq, topk=topk, idx_scale=(DSA_IDX_HEADS * DSA_IDX_DIM) ** -0.5,
                          seq_bits=max(1, math.ceil(math.log2(T)))),
        grid=(B, T // tq),
        in_specs=[qs, ks, vs, pl.BlockSpec((1, tq, 2 * LANES), lambda b, i: (b, i, 0)), ks,
                  pl.BlockSpec((1, tq, LANES), lambda b, i: (b, i, 0))],
        out_specs=out,
        out_shape=jax.ShapeDtypeStruct((B, T, H * HEAD_DIM), jnp.float32),
        scratch_shapes=[pltpu.VMEM((tq, T), jnp.float32), pltpu.VMEM((tq, LANES), jnp.int32),
                        pltpu.VMEM((tq, LANES), jnp.float32)] + _att_scratch(tq),
        compiler_params=_params("arbitrary", "arbitrary"),
        name="dsa_attention",
    )(q, kt, v, iq, ikt, iw)


def _out_kernel(x_ref, ada_ref, o1_ref, o2_ref, o3_ref, o4_ref, gn_ref, w_ref, y_ref):
    a = ada_ref[0]
    gn = gn_ref[...]
    y = jnp.concatenate([_rms(o[0], gn[i:i + 1]).astype(MXU_DTYPE)
                         for i, o in enumerate((o1_ref, o2_ref, o3_ref, o4_ref))], axis=-1)
    y_ref[...] = x_ref[...] + a[5:6] * jnp.dot(y, w_ref[...], preferred_element_type=jnp.float32)


def _out_call(x2d, ada_l, groups, group_norm, w_out, B, T):
    N, D = x2d.shape
    tm = ROW_TILE
    tpb = T // tm
    grp = pl.BlockSpec((1, tm, GROUP_WIDTH), lambda i: (i // tpb, i % tpb, 0))
    return pl.pallas_call(
        _out_kernel,
        grid=(N // tm,),
        in_specs=[pl.BlockSpec((tm, D), lambda i: (i, 0)),
                  pl.BlockSpec((1, N_ADA, D), lambda i: (i // tpb, 0, 0)),
                  grp, grp, grp, grp,
                  pl.BlockSpec((N_GROUPS, GROUP_WIDTH), lambda i: (0, 0)),
                  pl.BlockSpec((MIX_WIDTH, D), lambda i: (0, 0))],
        out_specs=pl.BlockSpec((tm, D), lambda i: (i, 0)),
        out_shape=jax.ShapeDtypeStruct((N, D), jnp.float32),
        compiler_params=_params("arbitrary"),
        name="mixer_out_proj",
    )(x2d, ada_l, *groups, group_norm, w_out.astype(MXU_DTYPE))


def _mixer_groups(x2d, ada_l, tables, mix_norm, w_in, mla_q_norm, mla_w_uq, mla_kv_norm, mla_w_uk, mla_w_uv,
                  nsa_pe_k, nsa_pe_v, nsa_cmp_k_w1, nsa_cmp_k_w2, nsa_cmp_v_w1, nsa_cmp_v_w2, B, T):
    (mq, mkt, mv, lq, lkt, lv, nq, nkc, nvc, nkst, nvs, nkwt, nvw, ngate,
     dq, dkt, dv, diq, dikt, diw) = _proj_call(
        x2d, ada_l, mix_norm, w_in, tables, mla_q_norm, mla_w_uq, mla_kv_norm, mla_w_uk, mla_w_uv, B, T)
    o_moba = _moba_call(mq, mkt, mv)
    o_mla = _mla_call(lq, lkt, lv)
    kcmp, vcmp = _cmp_call(nkc, nvc, nsa_pe_k, nsa_pe_v, nsa_cmp_k_w1, nsa_cmp_k_w2, nsa_cmp_v_w1, nsa_cmp_v_w2)
    o_nsa = _nsa_call(nq, kcmp, vcmp, nkst, nvs, nkwt, nvw, ngate)
    o_dsa = _dsa_call(dq, dkt, dv, diq, dikt, diw)
    return o_moba, o_mla, o_nsa, o_dsa


def kernel(x, c, ada_w, ada_b, ffn1_norm, ffn1_w_gate, ffn1_w_up, ffn1_w_down, mix_norm, w_in, mla_q_norm, mla_w_uq, mla_kv_norm, mla_w_uk, mla_w_uv, nsa_pe_k, nsa_pe_v, nsa_cmp_k_w1, nsa_cmp_k_w2, nsa_cmp_v_w1, nsa_cmp_v_w2, group_norm, w_out, ffn2_norm, ffn2_w_gate, ffn2_w_up, ffn2_w_down, final_norm):
    B, T, D = x.shape
    L = ada_w.shape[0]
    assert D == D_MODEL and T % ROW_TILE == 0 and T % ATT_TILE == 0
    assert ATT_TILE % MOBA_BLOCK == 0 and ATT_TILE % NSA_SEL_BLOCK == 0 and ATT_TILE >= NSA_WINDOW
    tpb = T // ROW_TILE
    ada = _ada_call(c, ada_w, ada_b)
    tables = _rope_tables(T)
    x2d = x.reshape(B * T, D)
    for l in range(L):
        x2d = _ffn_call(x2d, ada[l], ffn1_norm[l], ffn1_w_gate[l], ffn1_w_up[l], ffn1_w_down[l], 0, tpb)
        groups = _mixer_groups(x2d, ada[l], tables, mix_norm[l], w_in[l], mla_q_norm[l], mla_w_uq[l],
                               mla_kv_norm[l], mla_w_uk[l], mla_w_uv[l], nsa_pe_k[l], nsa_pe_v[l],
                               nsa_cmp_k_w1[l], nsa_cmp_k_w2[l], nsa_cmp_v_w1[l], nsa_cmp_v_w2[l], B, T)
        x2d = _out_call(x2d, ada[l], groups, group_norm[l], w_out[l], B, T)
        x2d = _ffn_call(x2d, ada[l], ffn2_norm[l], ffn2_w_gate[l], ffn2_w_up[l], ffn2_w_down[l], 6, tpb,
                        final_gain=final_norm if l == L - 1 else None)
    return x2d.reshape(B, T, D)
```

```python
import functools
import math

import numpy as np
import jax
import jax.numpy as jnp
from jax import lax
from jax.experimental import pallas as pl
from jax.experimental.pallas import tpu as pltpu

D_MODEL = 1024
N_GROUPS = 4
HEAD_DIM = 64
GROUP_HEADS = D_MODEL // (N_GROUPS * HEAD_DIM)
GROUP_WIDTH = GROUP_HEADS * HEAD_DIM
MIX_WIDTH = N_GROUPS * GROUP_WIDTH
D_FF = 256 * ((8 * D_MODEL + 3 * 256 - 1) // (3 * 256))
N_ADA = 9
FFN_RESIDUAL_WEIGHT = 0.5
ROPE_THETA = 10000.0
RMS_EPS = 1e-6
NEG_INF = -1e30

MOBA_BLOCK = 256
MOBA_TOPK = 3

MLA_Q_LORA = D_MODEL // 4
MLA_KV_LORA = D_MODEL // 8
MLA_NOPE = HEAD_DIM
MLA_ROPE = HEAD_DIM // 2
MLA_V = HEAD_DIM
MLA_QK = MLA_NOPE + MLA_ROPE

NSA_CMP_LEN = 32
NSA_CMP_STRIDE = 16
NSA_CMP_HIDDEN = 4 * HEAD_DIM
NSA_SEL_BLOCK = 64
NSA_SEL_TOPN = 16
NSA_WINDOW = 512
NSA_FORCE_SCORE = 1e4

DSA_TOPK = 256
DSA_IDX_HEADS = 8
DSA_IDX_DIM = 32

IN_NAMES = ("mq", "mk", "mv", "cq", "ckv", "kr", "nq", "nkc", "nvc", "nks", "nvs", "nkw", "nvw",
            "ngate", "dq", "dk", "dv", "diq", "dik", "diw")
IN_SIZES = (
    GROUP_WIDTH, GROUP_WIDTH, GROUP_WIDTH,
    MLA_Q_LORA, MLA_KV_LORA, MLA_ROPE,
    GROUP_WIDTH, HEAD_DIM, HEAD_DIM, HEAD_DIM, HEAD_DIM,
    HEAD_DIM, HEAD_DIM, 3 * GROUP_HEADS,
    GROUP_WIDTH, HEAD_DIM, HEAD_DIM,
    DSA_IDX_HEADS * DSA_IDX_DIM, DSA_IDX_DIM, DSA_IDX_HEADS,
)
N_IN = sum(IN_SIZES)

LANES = 128
MXU_DTYPE = jnp.bfloat16
VMEM_LIMIT = 56 * 1024 * 1024

ATT_TILE = 512
ROW_TILE = 512
FF_CHUNK = 1408

LOG2E = math.log2(math.e)
M_INIT = -1e29
DEN_LANE = HEAD_DIM


def _params(*semantics):
    return pltpu.CompilerParams(dimension_semantics=semantics, vmem_limit_bytes=VMEM_LIMIT)


def _dot(a, b):
    return jnp.dot(a.astype(MXU_DTYPE), b.astype(MXU_DTYPE), preferred_element_type=jnp.float32)


def _dot_nt(a, b):
    return lax.dot_general(a.astype(MXU_DTYPE), b.astype(MXU_DTYPE), (((1,), (1,)), ((), ())),
                           preferred_element_type=jnp.float32)


def _rms(x, g):
    return x * lax.rsqrt(jnp.mean(x * x, axis=-1, keepdims=True) + RMS_EPS) * g


def _silu(x):
    return x * (1.0 / (1.0 + jnp.exp(-x)))


def _iota(shape, dim):
    return lax.broadcasted_iota(jnp.int32, shape, dim)


def _log2(n):
    assert n & (n - 1) == 0
    return n.bit_length() - 1


def _any(pred):
    return jnp.max(jnp.where(pred, 1.0, 0.0)) > 0.5


def _lanes(x, width):
    return x if width == LANES else jnp.tile(x, (1, width // LANES))


def _ada_kernel(c_ref, w_ref, b_ref, o_ref):
    o_ref[0] = _dot(_silu(c_ref[...]), w_ref[0]) + b_ref[0]


def _ada_call(c, ada_w, ada_b):
    L, D, _ = ada_w.shape
    B = c.shape[0]
    out = pl.pallas_call(
        _ada_kernel,
        grid=(L, N_ADA),
        in_specs=[
            pl.BlockSpec((B, D), lambda l, k: (0, 0)),
            pl.BlockSpec((1, D, D), lambda l, k: (l, 0, k)),
            pl.BlockSpec((1, 1, D), lambda l, k: (l, 0, k)),
        ],
        out_specs=pl.BlockSpec((1, B, D), lambda l, k: (l, 0, k)),
        out_shape=jax.ShapeDtypeStruct((L, B, N_ADA * D), jnp.float32),
        compiler_params=_params("arbitrary", "arbitrary"),
        name="ada_proj",
    )(c, ada_w, ada_b.reshape(L, 1, N_ADA * D))
    return out.reshape(L, B, N_ADA, D)


def _ffn_kernel(x_ref, ada_ref, gn_ref, wg_ref, wu_ref, wd_ref, *rest, k0, n_chunks, final):
    if final:
        fg_ref, o_ref = rest
    else:
        (o_ref,) = rest
    a = ada_ref[0]
    h = (_rms(x_ref[...], gn_ref[...]) * (1.0 + a[k0 + 1:k0 + 2]) + a[k0:k0 + 1]).astype(MXU_DTYPE)
    acc = None
    for j in range(n_chunks):
        g = jnp.dot(h, wg_ref[j], preferred_element_type=jnp.float32)
        u = jnp.dot(h, wu_ref[j], preferred_element_type=jnp.float32)
        part = jnp.dot((_silu(g) * u).astype(MXU_DTYPE), wd_ref[j], preferred_element_type=jnp.float32)
        acc = part if acc is None else acc + part
    y = x_ref[...] + (FFN_RESIDUAL_WEIGHT * a[k0 + 2:k0 + 3]) * acc
    if final:
        y = _rms(y, fg_ref[...])
    o_ref[...] = y


def _ffn_call(x2d, ada_l, norm_g, w_gate, w_up, w_down, k0, tiles_per_batch, final_gain=None):
    N, D = x2d.shape
    F = w_gate.shape[1]
    n_chunks = F // FF_CHUNK
    tm = ROW_TILE
    wg = w_gate.astype(MXU_DTYPE).reshape(D, n_chunks, FF_CHUNK).transpose(1, 0, 2)
    wu = w_up.astype(MXU_DTYPE).reshape(D, n_chunks, FF_CHUNK).transpose(1, 0, 2)
    wd = w_down.astype(MXU_DTYPE).reshape(n_chunks, FF_CHUNK, D)
    const3 = lambda i: (0, 0, 0)
    in_specs = [
        pl.BlockSpec((tm, D), lambda i: (i, 0)),
        pl.BlockSpec((1, N_ADA, D), lambda i: (i // tiles_per_batch, 0, 0)),
        pl.BlockSpec((1, D), lambda i: (0, 0)),
        pl.BlockSpec((n_chunks, D, FF_CHUNK), const3),
        pl.BlockSpec((n_chunks, D, FF_CHUNK), const3),
        pl.BlockSpec((n_chunks, FF_CHUNK, D), const3),
    ]
    args = [x2d, ada_l, norm_g.reshape(1, D), wg, wu, wd]
    final = final_gain is not None
    if final:
        in_specs.append(pl.BlockSpec((1, D), lambda i: (0, 0)))
        args.append(final_gain.reshape(1, D))
    return pl.pallas_call(
        functools.partial(_ffn_kernel, k0=k0, n_chunks=n_chunks, final=final),
        grid=(N // tm,),
        in_specs=in_specs,
        out_specs=pl.BlockSpec((tm, D), lambda i: (i, 0)),
        out_shape=jax.ShapeDtypeStruct((N, D), jnp.float32),
        compiler_params=_params("arbitrary"),
        name="ffn",
    )(*args)


G_QMAIN, G_QSWAP, G_KC, G_V, G_MISC, ROW_GROUPS = 0, 12, 24, 26, 33, 41
T_MK, T_MKS, T_SW, T_SWS, T_DK, T_DKS, T_IK, T_ROWS = 0, 256, 512, 640, 768, 896, 1024, 1152


def _swap_halves(c, width):
    return c.reshape(-1, 2, width // 2)[:, ::-1, :].reshape(-1)


def _proj_indices():
    off = dict(zip(IN_NAMES, np.cumsum((0,) + IN_SIZES[:-1]).tolist()))
    size = dict(zip(IN_NAMES, IN_SIZES))
    cols = lambda name: np.arange(off[name], off[name] + size[name])
    zero = lambda n: np.full((n,), N_IN)
    hd = HEAD_DIM

    def head_groups(c):
        return np.concatenate([np.concatenate([c[i:i + hd], zero(LANES - hd)]) for i in range(0, c.size, hd)])

    q = np.concatenate([cols("mq"), cols("nq"), cols("dq")])
    row = np.concatenate([
        head_groups(q), head_groups(_swap_halves(q, hd)),
        head_groups(cols("nkc")), head_groups(_swap_halves(cols("nkc"), hd)),
        head_groups(np.concatenate([cols("mv"), cols("nvs"), cols("nvw"), cols("dv")])),
        cols("cq"), cols("ckv"), head_groups(cols("nvc")), cols("diq"),
        cols("ngate"), zero(LANES - size["ngate"]), cols("diw"), zero(LANES - size["diw"])])
    assert row.size == ROW_GROUPS * LANES
    sw = np.concatenate([cols("nks"), cols("nkw")])
    dk_main = np.concatenate([cols("dk"), cols("kr"), zero(LANES - hd - MLA_ROPE)])
    dk_swap = np.concatenate([_swap_halves(cols("dk"), hd), _swap_halves(cols("kr"), MLA_ROPE),
                              zero(LANES - hd - MLA_ROPE)])
    tr = np.concatenate([cols("mk"), _swap_halves(cols("mk"), hd), sw, _swap_halves(sw, hd),
                         dk_main, dk_swap, np.tile(cols("dik"), LANES // DSA_IDX_DIM)])
    assert tr.size == T_ROWS
    return row, tr


def _proj_kernel(x_ref, ada_ref, gn_ref, wr_ref, wt_ref, rtab_ref, ttab_ref,
                 qn_ref, wuq_ref, kvn_ref, wukt_ref, wuv_ref,
                 mq_ref, mkt_ref, mv_ref, lq_ref, lkt_ref, lv_ref,
                 nq_ref, nkc_ref, nvc_ref, nkst_ref, nvs_ref, nkwt_ref, nvw_ref, ng_ref,
                 dq_ref, dkt_ref, dv_ref, diq_ref, dikt_ref, diw_ref, *, tm, tpb, n_moba, n_sel):
    H, hd, G = GROUP_HEADS, HEAD_DIM, LANES
    t0 = (pl.program_id(0) % tpb) * tm
    a = ada_ref[0]
    h = (_rms(x_ref[...], gn_ref[...]) * (1.0 + a[4:5]) + a[3:4]).astype(MXU_DTYPE)

    def rows(g0, n):
        return jnp.dot(h, wr_ref[:, g0 * G:(g0 + n) * G], preferred_element_type=jnp.float32)

    def cols(r0, n):
        return _dot_nt(wt_ref[r0:r0 + n, :], h)

    roped_q = (rows(G_QMAIN, 3 * H) * _lanes(rtab_ref[0], 3 * H * G)
               + rows(G_QSWAP, 3 * H) * _lanes(rtab_ref[1], 3 * H * G))
    for hh in range(H):
        mq_ref[0, hh] = roped_q[:, hh * G:(hh + 1) * G].astype(mq_ref.dtype)
        nq_ref[0, hh] = roped_q[:, (H + hh) * G:(H + hh + 1) * G].astype(nq_ref.dtype)
        dq_ref[0, hh] = roped_q[:, (2 * H + hh) * G:(2 * H + hh + 1) * G].astype(dq_ref.dtype)
    kc = rows(G_KC, 2)
    nkc_ref[0] = (kc[:, :G] * rtab_ref[2] + kc[:, G:] * rtab_ref[3])[:, :hd].astype(nkc_ref.dtype)

    ones_hi = jnp.where(_iota((1, G), 1) >= DEN_LANE, 1.0, 0.0)
    v = rows(G_V, H + 3) + _lanes(ones_hi, (H + 3) * G)
    for hh in range(H):
        mv_ref[0, hh] = v[:, hh * G:(hh + 1) * G].astype(mv_ref.dtype)
    nvs_ref[0] = v[:, H * G:(H + 1) * G].astype(nvs_ref.dtype)
    nvw_ref[0] = v[:, (H + 1) * G:(H + 2) * G].astype(nvw_ref.dtype)
    dv_ref[0] = v[:, (H + 2) * G:(H + 3) * G].astype(dv_ref.dtype)

    misc = rows(G_MISC, 8)
    cq = misc[:, :MLA_Q_LORA]
    ckv = misc[:, MLA_Q_LORA:MLA_Q_LORA + MLA_KV_LORA]
    nvc_ref[0] = misc[:, 3 * G:3 * G + hd].astype(nvc_ref.dtype)
    diq_ref[0] = misc[:, 4 * G:6 * G].astype(diq_ref.dtype)
    ng_ref[0] = misc[:, 6 * G:7 * G]
    diw_ref[0] = misc[:, 7 * G:8 * G]

    tok = t0 + _iota((hd, tm), 1)
    rid = _iota((hd, tm), 0)
    oh_moba = jnp.where((rid & (n_moba - 1)) == (tok >> _log2(MOBA_BLOCK)), 1.0, 0.0)
    oh_sel = jnp.where((rid == (tok >> _log2(NSA_SEL_BLOCK))) & (rid < n_sel), 1.0, 0.0)
    zeros_lo = jnp.zeros((hd, tm), jnp.float32)
    ta_c, ta_s, tb_c, tb_s = ttab_ref[0], ttab_ref[1], ttab_ref[2], ttab_ref[3]
    mkt = (cols(T_MK, H * hd) * jnp.tile(ta_c, (H * hd // G, 1))
           + cols(T_MKS, H * hd) * jnp.tile(ta_s, (H * hd // G, 1)))
    for hh in range(H):
        mine = (rid >> _log2(n_moba)) == hh
        mkt_ref[0, hh] = jnp.concatenate([mkt[hh * hd:(hh + 1) * hd], jnp.where(mine, oh_moba, 0.0)],
                                         axis=0).astype(mkt_ref.dtype)
    sw = cols(T_SW, G) * ta_c + cols(T_SWS, G) * ta_s
    nkst_ref[0] = jnp.concatenate([sw[:hd], oh_sel], axis=0).astype(nkst_ref.dtype)
    nkwt_ref[0] = jnp.concatenate([sw[hd:], zeros_lo], axis=0).astype(nkwt_ref.dtype)
    dkr = cols(T_DK, G) * tb_c + cols(T_DKS, G) * tb_s
    rid2 = _iota((G, tm), 0)
    dkt_ref[0] = jnp.where(rid2 < hd, dkr, 0.0).astype(dkt_ref.dtype)
    kpe_rows = jnp.where((rid2 >= MLA_NOPE) & (rid2 < MLA_QK), dkr, 0.0)
    dikt_ref[0] = cols(T_IK, G).astype(dikt_ref.dtype)

    cqn = _rms(cq, qn_ref[...]).astype(MXU_DTYPE)
    lq = (jnp.dot(cqn, wuq_ref[:, :H * G], preferred_element_type=jnp.float32) * _lanes(rtab_ref[4], H * G)
          + jnp.dot(cqn, wuq_ref[:, H * G:], preferred_element_type=jnp.float32) * _lanes(rtab_ref[5], H * G))
    ckvn = _rms(ckv, kvn_ref[...]).astype(MXU_DTYPE)
    knt = _dot_nt(wukt_ref[...], ckvn)
    lv = jnp.dot(ckvn, wuv_ref[...], preferred_element_type=jnp.float32) + _lanes(ones_hi, H * G)
    for hh in range(H):
        lq_ref[0, hh] = lq[:, hh * G:(hh + 1) * G].astype(lq_ref.dtype)
        lkt_ref[0, hh] = (knt[hh * G:(hh + 1) * G] + kpe_rows).astype(lkt_ref.dtype)
        lv_ref[0, hh] = lv[:, hh * G:(hh + 1) * G].astype(lv_ref.dtype)


def _rope_tables(T):
    def cs(dim):
        inv_freq = 1.0 / (ROPE_THETA ** (np.arange(0, dim, 2, dtype=np.float32) / dim))
        ang = jnp.arange(T, dtype=jnp.float32)[:, None] * jnp.asarray(inv_freq, jnp.float32)[None, :]
        cos, sin = jnp.cos(ang), jnp.sin(ang)
        return jnp.concatenate([cos, cos], axis=-1), jnp.concatenate([-sin, sin], axis=-1)

    c64, s64 = cs(HEAD_DIM)
    c32, s32 = cs(MLA_ROPE)
    pad = lambda t, n: jnp.concatenate([t, jnp.zeros((T, n), jnp.float32)], axis=-1)
    sc = HEAD_DIM ** -0.5 * LOG2E
    sl = MLA_QK ** -0.5 * LOG2E
    ones = jnp.ones((T, MLA_NOPE), jnp.float32)
    rest = LANES - MLA_QK
    rtab = jnp.stack([
        pad(c64 * sc, LANES - HEAD_DIM), pad(s64 * sc, LANES - HEAD_DIM),
        pad(c64, LANES - HEAD_DIM), pad(s64, LANES - HEAD_DIM),
        pad(jnp.concatenate([ones, c32], axis=-1) * sl, rest),
        pad(jnp.concatenate([0.0 * ones, s32], axis=-1) * sl, rest)])
    ttab = jnp.stack([
        jnp.concatenate([c64, c64], axis=-1).T, jnp.concatenate([s64, s64], axis=-1).T,
        pad(jnp.concatenate([c64, c32], axis=-1), rest).T, pad(jnp.concatenate([s64, s32], axis=-1), rest).T])
    return rtab, ttab


def _proj_call(x2d, ada_l, norm_g, w_in, tables, mla_q_norm, mla_w_uq, mla_kv_norm, mla_w_uk, mla_w_uv, B, T):
    N, D = x2d.shape
    H, G = GROUP_HEADS, LANES
    tm = ROW_TILE
    tpb = T // tm
    n_moba, n_sel = T // MOBA_BLOCK, T // NSA_SEL_BLOCK
    assert H * n_moba <= LANES - HEAD_DIM and n_sel <= LANES - HEAD_DIM and n_moba & (n_moba - 1) == 0
    zcol = lambda w: jnp.concatenate([w, jnp.zeros((w.shape[0], 1), w.dtype)], axis=1)
    row_idx, tr_idx = _proj_indices()
    w_ext = zcol(w_in)
    w_row = w_ext[:, row_idx].astype(MXU_DTYPE)
    w_tr = w_ext[:, tr_idx].T.astype(MXU_DTYPE)
    zq = mla_w_uq.shape[1]
    per_head = np.arange(H * MLA_QK).reshape(H, MLA_QK)
    main = np.concatenate([np.concatenate([per_head[i], np.full((G - MLA_QK,), zq)]) for i in range(H)])
    part = np.concatenate([np.concatenate([np.full((MLA_NOPE,), zq), _swap_halves(per_head[i, MLA_NOPE:], MLA_ROPE),
                                           np.full((G - MLA_QK,), zq)]) for i in range(H)])
    wuq = zcol(mla_w_uq)[:, np.concatenate([main, part])].astype(MXU_DTYPE)
    zv = mla_w_uk.shape[1]
    grp = np.concatenate([np.concatenate([np.arange(i * HEAD_DIM, (i + 1) * HEAD_DIM), np.full((G - HEAD_DIM,), zv)])
                          for i in range(H)])
    wukt = zcol(mla_w_uk)[:, grp].T.astype(MXU_DTYPE)
    wuv = zcol(mla_w_uv)[:, grp].astype(MXU_DTYPE)
    rtab, ttab = tables

    row = lambda i: (i, 0)
    const2 = lambda i: (0, 0)
    in_specs = [
        pl.BlockSpec((tm, D), row),
        pl.BlockSpec((1, N_ADA, D), lambda i: (i // tpb, 0, 0)),
        pl.BlockSpec((1, D), const2),
        pl.BlockSpec(w_row.shape, const2),
        pl.BlockSpec(w_tr.shape, const2),
        pl.BlockSpec((6, tm, G), lambda i: (0, i % tpb, 0)),
        pl.BlockSpec((4, G, tm), lambda i: (0, 0, i % tpb)),
        pl.BlockSpec((1, MLA_Q_LORA), const2),
        pl.BlockSpec(wuq.shape, const2),
        pl.BlockSpec((1, MLA_KV_LORA), const2),
        pl.BlockSpec(wukt.shape, const2),
        pl.BlockSpec(wuv.shape, const2),
    ]
    dt = MXU_DTYPE
    hq = (jax.ShapeDtypeStruct((B, H, T, G), dt), pl.BlockSpec((1, H, tm, G), lambda i: (i // tpb, 0, i % tpb, 0)))
    hkt = (jax.ShapeDtypeStruct((B, H, G, T), dt), pl.BlockSpec((1, H, G, tm), lambda i: (i // tpb, 0, 0, i % tpb)))
    srow = lambda d, t=dt: (jax.ShapeDtypeStruct((B, T, d), t), pl.BlockSpec((1, tm, d), lambda i: (i // tpb, i % tpb, 0)))
    skt = (jax.ShapeDtypeStruct((B, G, T), dt), pl.BlockSpec((1, G, tm), lambda i: (i // tpb, 0, i % tpb)))
    outs = [hq, hkt, hq,
            hq, hkt, hq,
            hq, srow(HEAD_DIM), srow(HEAD_DIM), skt, srow(G), skt, srow(G), srow(G, jnp.float32),
            hq, skt, srow(G), srow(2 * G), skt, srow(G, jnp.float32)]
    return pl.pallas_call(
        functools.partial(_proj_kernel, tm=tm, tpb=tpb, n_moba=n_moba, n_sel=n_sel),
        grid=(N // tm,),
        in_specs=in_specs,
        out_specs=[o[1] for o in outs],
        out_shape=[o[0] for o in outs],
        compiler_params=_params("arbitrary"),
        name="mixer_in_proj",
    )(x2d, ada_l, norm_g.reshape(1, D), w_row, w_tr, rtab, ttab,
      mla_q_norm.reshape(1, -1), wuq, mla_kv_norm.reshape(1, -1), wukt, wuv)


def _flash_init(m_scr, acc_scr):
    m_scr[...] = jnp.full(m_scr.shape, M_INIT, jnp.float32)
    acc_scr[...] = jnp.zeros_like(acc_scr)


def _flash_update(h, s, v, m_scr, acc_scr):
    m_prev = m_scr[h]
    m_new = jnp.maximum(m_prev, jnp.max(s, axis=-1, keepdims=True))
    p = jnp.exp2(s - _lanes(m_new, s.shape[1]))
    acc_scr[h] = jnp.exp2(m_prev - m_new) * acc_scr[h] + _dot(p, v)
    m_scr[h] = m_new


def _flash_out(h, acc_scr):
    acc = acc_scr[h]
    den = acc[:, DEN_LANE:DEN_LANE + 1]
    return acc[:, :HEAD_DIM] / jnp.where(den > 0.0, den, 1.0)


def _causal_bias(t):
    return jnp.where(_iota((t, t), 1) <= _iota((t, t), 0), 0.0, NEG_INF)


def _rank_desc(x):
    n = x.shape[0]
    row = _iota(x.shape, 0)
    rank = jnp.zeros(x.shape, jnp.float32)
    for j in range(n):
        cand = x[j:j + 1, :]
        rank = rank + jnp.where(cand > x, 1.0, jnp.where((cand == x) & (row > j), 1.0, 0.0))
    return rank


def _att_scratch(tq):
    return [pltpu.VMEM((GROUP_HEADS, tq, LANES), jnp.float32), pltpu.VMEM((GROUP_HEADS, tq, LANES), jnp.float32)]


def _att_specs(B, H, T, tq, shared_kv):
    q = pl.BlockSpec((1, H, tq, LANES), lambda b, i: (b, 0, i, 0))
    if shared_kv:
        kt = pl.BlockSpec((1, LANES, T), lambda b, i: (b, 0, 0))
        v = pl.BlockSpec((1, T, LANES), lambda b, i: (b, 0, 0))
    else:
        kt = pl.BlockSpec((1, H, LANES, T), lambda b, i: (b, 0, 0, 0))
        v = pl.BlockSpec((1, H, T, LANES), lambda b, i: (b, 0, 0, 0))
    out = pl.BlockSpec((1, tq, H * HEAD_DIM), lambda b, i: (b, i, 0))
    return q, kt, v, out


def _mla_kernel(q_ref, kt_ref, v_ref, o_ref, m_scr, acc_scr, *, tq):
    H = GROUP_HEADS
    qi = pl.program_id(1)
    qs = [q_ref[0, h] for h in range(H)]
    _flash_init(m_scr, acc_scr)

    def chunk(k0, bias):
        for h in range(H):
            s = jnp.dot(qs[h], kt_ref[0, h, :, pl.ds(k0, tq)], preferred_element_type=jnp.float32)
            if bias is not None:
                s = s + bias
            _flash_update(h, s, v_ref[0, h, pl.ds(k0, tq), :], m_scr, acc_scr)

    def body(c, carry):
        chunk(pl.multiple_of(c * tq, tq), None)
        return carry

    lax.fori_loop(0, qi, body, 0)
    chunk(pl.multiple_of(qi * tq, tq), _causal_bias(tq))
    for h in range(H):
        o_ref[0, :, h * HEAD_DIM:(h + 1) * HEAD_DIM] = _flash_out(h, acc_scr)


def _mla_call(q, kt, v):
    B, H, T, _ = q.shape
    tq = ATT_TILE
    qs, ks, vs, out = _att_specs(B, H, T, tq, False)
    return pl.pallas_call(
        functools.partial(_mla_kernel, tq=tq),
        grid=(B, T // tq),
        in_specs=[qs, ks, vs],
        out_specs=out,
        out_shape=jax.ShapeDtypeStruct((B, T, H * HEAD_DIM), jnp.float32),
        scratch_shapes=_att_scratch(tq),
        compiler_params=_params("arbitrary", "arbitrary"),
        name="mla_attention",
    )(q, kt, v)


def _moba_kernel(q_ref, kt_ref, v_ref, o_ref, kmean_scr, m_scr, acc_scr, *, tq, nb, topk):
    H, hd = GROUP_HEADS, HEAD_DIM
    qi = pl.program_id(1)
    q0 = qi * tq
    T = nb * MOBA_BLOCK

    @pl.when(qi == 0)
    def _():
        avg = jnp.where((_iota((nb, T), 1) >> _log2(MOBA_BLOCK)) == _iota((nb, T), 0), 1.0 / MOBA_BLOCK, 0.0)
        for h in range(H):
            kmean_scr[h] = _dot_nt(avg, kt_ref[0, h])

    blk = _iota((nb, tq), 0)
    own = (q0 + _iota((nb, tq), 1)) >> _log2(MOBA_BLOCK)
    past = blk < own
    bias_rows = [jnp.zeros((hd, tq), jnp.float32)]
    for h in range(H):
        gate = jnp.where(past, _dot_nt(kmean_scr[h], q_ref[0, h]), NEG_INF)
        allowed = ((_rank_desc(gate) < topk) & past) | (blk == own)
        bias_rows.append(jnp.where(allowed, 0.0, NEG_INF))
    if H * nb < LANES - hd:
        bias_rows.append(jnp.zeros((LANES - hd - H * nb, tq), jnp.float32))
    bias = jnp.concatenate(bias_rows, axis=0).T
    lane = _iota((tq, LANES), 1)
    qs = [jnp.where(lane < hd, q_ref[0, h], bias.astype(q_ref.dtype)) for h in range(H)]
    _flash_init(m_scr, acc_scr)

    def chunk(k0, bias):
        for h in range(H):
            s = jnp.dot(qs[h], kt_ref[0, h, :, pl.ds(k0, tq)], preferred_element_type=jnp.float32)
            if bias is not None:
                s = s + bias
            _flash_update(h, s, v_ref[0, h, pl.ds(k0, tq), :], m_scr, acc_scr)

    def body(c, carry):
        chunk(pl.multiple_of(c * tq, tq), None)
        return carry

    lax.fori_loop(0, qi, body, 0)
    chunk(pl.multiple_of(q0, tq), _causal_bias(tq))
    for h in range(H):
        o_ref[0, :, h * hd:(h + 1) * hd] = _flash_out(h, acc_scr)


def _moba_call(q, kt, v):
    B, H, T, _ = q.shape
    tq = ATT_TILE
    nb = T // MOBA_BLOCK
    qs, ks, vs, out = _att_specs(B, H, T, tq, False)
    return pl.pallas_call(
        functools.partial(_moba_kernel, tq=tq, nb=nb, topk=min(MOBA_TOPK, nb - 1)),
        grid=(B, T // tq),
        in_specs=[qs, ks, vs],
        out_specs=out,
        out_shape=jax.ShapeDtypeStruct((B, T, H * HEAD_DIM), jnp.float32),
        scratch_shapes=[pltpu.VMEM((H, nb, LANES), jnp.float32)] + _att_scratch(tq),
        compiler_params=_params("arbitrary", "arbitrary"),
        name="moba_attention",
    )(q, kt, v)


def _cmp_kernel(k_ref, v_ref, pek_ref, pev_ref, kw1_ref, kw2_ref, vw1_ref, vw2_ref, ko_ref, vo_ref, *, rows):
    half = NSA_CMP_STRIDE * HEAD_DIM

    def compress(t_ref, pe_ref, w1_ref, w2_ref):
        t = t_ref[0].astype(jnp.float32)
        first = _dot(t + pe_ref[0:1, :], w1_ref[0:half, :])
        second = _dot(t + pe_ref[1:2, :], w1_ref[half:, :])
        hid = first + pltpu.roll(second, rows - 1, 0)
        return _dot(_silu(hid), w2_ref[...])

    ko_ref[0] = compress(k_ref, pek_ref, kw1_ref, kw2_ref)
    vo_ref[0] = compress(v_ref, pev_ref, vw1_ref, vw2_ref)


def _cmp_call(kc, vc, pe_k, pe_v, k_w1, k_w2, v_w1, v_w2):
    B, T, d = kc.shape
    rows = T // NSA_CMP_STRIDE
    wide = NSA_CMP_STRIDE * d
    assert NSA_CMP_LEN == 2 * NSA_CMP_STRIDE
    const2 = lambda b: (0, 0)
    blk = pl.BlockSpec((1, rows, wide), lambda b: (b, 0, 0))
    out = pl.BlockSpec((1, rows, d), lambda b: (b, 0, 0))
    return pl.pallas_call(
        functools.partial(_cmp_kernel, rows=rows),
        grid=(B,),
        in_specs=[blk, blk, pl.BlockSpec((2, wide), const2), pl.BlockSpec((2, wide), const2),
                  pl.BlockSpec(k_w1.shape, const2), pl.BlockSpec(k_w2.shape, const2),
                  pl.BlockSpec(v_w1.shape, const2), pl.BlockSpec(v_w2.shape, const2)],
        out_specs=[out, out],
        out_shape=[jax.ShapeDtypeStruct((B, rows, d), jnp.float32)] * 2,
        compiler_params=_params("arbitrary"),
        name="nsa_compress",
    )(kc.reshape(B, rows, wide), vc.reshape(B, rows, wide), pe_k.reshape(2, wide), pe_v.reshape(2, wide),
      k_w1.astype(MXU_DTYPE), k_w2.astype(MXU_DTYPE), v_w1.astype(MXU_DTYPE), v_w2.astype(MXU_DTYPE))


def _nsa_kernel(q_ref, kcmp_ref, vcmp_ref, kst_ref, vs_ref, kwt_ref, vw_ref, g_ref, o_ref,
                oc_scr, os_scr, m_scr, acc_scr, *, tq, ncp, n_sel, topn):
    H, hd = GROUP_HEADS, HEAD_DIM
    qi = pl.program_id(1)
    q0 = qi * tq
    tq_col = q0 + _iota((tq, 1), 0)
    qraw = [q_ref[0, h] for h in range(H)]

    cmp_end = _iota((tq, ncp), 1) * NSA_CMP_STRIDE + (NSA_CMP_LEN - 1)
    m_c = cmp_end <= tq_col
    p_sum = jnp.zeros((tq, ncp), jnp.float32)
    for h in range(H):
        s = jnp.where(m_c, _dot_nt(qraw[h][:, :hd], kcmp_ref[0]), NEG_INF)
        e = jnp.where(m_c, jnp.exp2(s - jnp.max(s, axis=-1, keepdims=True)), 0.0)
        l = jnp.sum(e, axis=-1, keepdims=True)
        p = e / jnp.where(l > 0.0, l, 1.0)
        p_sum = p_sum + p
        oc_scr[h] = _dot(p, vcmp_ref[0])

    cmp_start = _iota((n_sel, ncp), 1) * NSA_CMP_STRIDE
    sel_start = _iota((n_sel, ncp), 0) * NSA_SEL_BLOCK
    overlap = (cmp_start < sel_start + NSA_SEL_BLOCK) & (cmp_start + NSA_CMP_LEN > sel_start)
    imp = _dot_nt(jnp.where(overlap, 1.0, 0.0), p_sum)
    sel_id = _iota((n_sel, tq), 0)
    own = (q0 + _iota((n_sel, tq), 1)) >> _log2(NSA_SEL_BLOCK)
    causal = sel_id <= own
    forced = causal & ((sel_id == 0) | (sel_id >= own - 1))
    imp = jnp.where(forced, NSA_FORCE_SCORE, jnp.where(causal, imp, -NSA_FORCE_SCORE))
    bias_rows = [jnp.zeros((hd, tq), jnp.float32), jnp.where(_rank_desc(imp) < topn, 0.0, NEG_INF)]
    if n_sel < LANES - hd:
        bias_rows.append(jnp.zeros((LANES - hd - n_sel, tq), jnp.float32))
    bias = jnp.concatenate(bias_rows, axis=0).T.astype(qraw[0].dtype)
    lane = _iota((tq, LANES), 1)
    qsel = [jnp.where(lane < hd, qraw[h], bias) for h in range(H)]

    _flash_init(m_scr, acc_scr)

    def sel_chunk(k0, extra):
        for h in range(H):
            s = jnp.dot(qsel[h], kst_ref[0, :, pl.ds(k0, tq)], preferred_element_type=jnp.float32)
            if extra is not None:
                s = s + extra
            _flash_update(h, s, vs_ref[0, pl.ds(k0, tq), :], m_scr, acc_scr)

    def sel_body(c, carry):
        sel_chunk(pl.multiple_of(c * tq, tq), None)
        return carry

    lax.fori_loop(0, qi, sel_body, 0)
    tri = _causal_bias(tq)
    sel_chunk(pl.multiple_of(q0, tq), tri)
    for h in range(H):
        os_scr[h] = _flash_out(h, acc_scr)

    _flash_init(m_scr, acc_scr)

    def win_chunk(k0, extra):
        for h in range(H):
            s = jnp.dot(qraw[h], kwt_ref[0, :, pl.ds(k0, tq)], preferred_element_type=jnp.float32) + extra
            _flash_update(h, s, vw_ref[0, pl.ds(k0, tq), :], m_scr, acc_scr)

    first = jnp.maximum(q0 - NSA_WINDOW + 1, 0) // tq

    def win_body(c, carry):
        k0 = pl.multiple_of(c * tq, tq)
        qpos = q0 + _iota((tq, tq), 0)
        kpos = k0 + _iota((tq, tq), 1)
        win_chunk(k0, jnp.where((kpos <= qpos) & (kpos > qpos - NSA_WINDOW), 0.0, NEG_INF))
        return carry

    lax.fori_loop(first, qi + 1, win_body, 0)
    for h in range(H):
        gates = 1.0 / (1.0 + jnp.exp(-g_ref[0][:, 3 * h:3 * h + 3]))
        o_ref[0, :, h * hd:(h + 1) * hd] = (gates[:, 0:1] * oc_scr[h] + gates[:, 1:2] * os_scr[h]
                                            + gates[:, 2:3] * _flash_out(h, acc_scr))


def _nsa_call(q, kcmp, vcmp, kst, vs, kwt, vw, gate_logits):
    B, H, T, _ = q.shape
    tq = ATT_TILE
    ncp = kcmp.shape[1]
    n_sel = T // NSA_SEL_BLOCK
    qs, ks, vsp, out = _att_specs(B, H, T, tq, True)
    cmp_spec = pl.BlockSpec((1, ncp, HEAD_DIM), lambda b, i: (b, 0, 0))
    return pl.pallas_call(
        functools.partial(_nsa_kernel, tq=tq, ncp=ncp, n_sel=n_sel, topn=min(NSA_SEL_TOPN, n_sel)),
        grid=(B, T // tq),
        in_specs=[qs, cmp_spec, cmp_spec, ks, vsp, ks, vsp,
                  pl.BlockSpec((1, tq, LANES), lambda b, i: (b, i, 0))],
        out_specs=out,
        out_shape=jax.ShapeDtypeStruct((B, T, H * HEAD_DIM), jnp.float32),
        scratch_shapes=[pltpu.VMEM((H, tq, HEAD_DIM), jnp.float32), pltpu.VMEM((H, tq, HEAD_DIM), jnp.float32)]
        + _att_scratch(tq),
        compiler_params=_params("arbitrary", "arbitrary"),
        name="nsa_attention",
    )(q, kcmp, vcmp, kst, vs, kwt, vw, gate_logits)


def _sortable(x):
    b = int(np.float32(x).view(np.int32))
    return b ^ ((b >> 31) & 0x7FFFFFFF)


def _from_sortable(k):
    return lax.bitcast_convert_type(k ^ ((k >> 31) & 0x7FFFFFFF), jnp.float32)


def _to_sortable(x):
    k = lax.bitcast_convert_type(x, jnp.int32)
    return k ^ ((k >> 31) & 0x7FFFFFFF)


COUNT_ROWS = 64
VALUE_STEPS = 24


def _dsa_kernel(q_ref, kt_ref, v_ref, iq_ref, ikt_ref, iw_ref, o_ref,
                s_scr, t_scr, m_scr, acc_scr, *, tq, topk, idx_scale):
    H, hd = GROUP_HEADS, HEAD_DIM
    qi = pl.program_id(1)
    q0 = qi * tq
    n_kc = qi + 1
    T = s_scr.shape[1]
    reps = tq // LANES

    lane = _iota((tq, LANES), 1)
    quarter = lane >> _log2(DSA_IDX_DIM)
    per_group = LANES // DSA_IDX_DIM
    iq = iq_ref[0]
    iqh = [jnp.where(quarter == (h % per_group), iq[:, (h // per_group) * LANES:(h // per_group + 1) * LANES],
                     jnp.zeros((), iq.dtype)) for h in range(DSA_IDX_HEADS)]
    iw = iw_ref[0]
    iwb = [jnp.broadcast_to(iw[:, h:h + 1], (tq, LANES)) for h in range(DSA_IDX_HEADS)]

    def score_body(c, ends):
        top, bot = ends
        k0 = pl.multiple_of(c * tq, tq)
        ikt = ikt_ref[0, :, pl.ds(k0, tq)]
        acc = jnp.zeros((tq, tq), jnp.float32)
        for h in range(DSA_IDX_HEADS):
            sh = jnp.dot(iqh[h], ikt, preferred_element_type=jnp.float32)
            acc = acc + _lanes(iwb[h], tq) * jnp.maximum(sh, 0.0)
        qpos = q0 + _iota((tq, tq), 0)
        kpos = k0 + _iota((tq, tq), 1)
        val = acc * idx_scale + 0.0
        sc = jnp.where(kpos <= qpos, val, NEG_INF)
        s_scr[:, pl.ds(k0, tq)] = sc
        low = jnp.where(kpos <= qpos, val, np.inf)
        for r in range(reps):
            top = jnp.maximum(top, sc[:, r * LANES:(r + 1) * LANES])
            bot = jnp.minimum(bot, low[:, r * LANES:(r + 1) * LANES])
        return top, bot

    top, bot = lax.fori_loop(0, n_kc, score_body, (jnp.full((tq, LANES), NEG_INF, jnp.float32),
                                                   jnp.full((tq, LANES), np.inf, jnp.float32)))
    row_max = jnp.broadcast_to(jnp.max(top, axis=-1, keepdims=True), (tq, LANES))
    row_min = jnp.broadcast_to(jnp.min(bot, axis=-1, keepdims=True), (tq, LANES))
    lane_sum = jnp.ones((LANES, LANES), jnp.float32)

    def count_ge(t, strict=False):
        above = (lambda a, b: a > b) if strict else (lambda a, b: a >= b)
        t_scr[...] = t
        parts = []
        for rb in range(tq // COUNT_ROWS):
            rows = slice(rb * COUNT_ROWS, (rb + 1) * COUNT_ROWS)
            t_rb = t_scr[rows, :]

            def body(c, part):
                k0 = pl.multiple_of(c * tq, tq)
                sc = s_scr[rows, pl.ds(k0, tq)]
                for r in range(reps):
                    part = part + jnp.where(above(sc[:, r * LANES:(r + 1) * LANES], t_rb), 1.0, 0.0)
                return part

            parts.append(lax.fori_loop(0, n_kc, body, jnp.zeros((COUNT_ROWS, LANES), jnp.float32)))
        return _dot(jnp.concatenate(parts, axis=0), lane_sum)

    kf = float(topk)
    floor_key = _sortable(NEG_INF)

    def bis_cond(c):
        _, lo, hi, _ = c
        return _any(lo < hi)

    def bis_body(c):
        it, lo, hi, n_lo = c
        key_mid = (lo | hi) - ((lo ^ hi) >> 1)
        val_mid = _to_sortable(0.5 * (_from_sortable(lo) + _from_sortable(hi)))
        steps = jnp.zeros((tq, LANES), jnp.int32) + it
        use_val = (val_mid > lo) & (val_mid <= hi) & (lo > floor_key) & (steps < VALUE_STEPS)
        mid = jnp.where(use_val, val_mid, key_mid)
        cnt = count_ge(_from_sortable(mid))
        ge = cnt >= kf
        lo = jnp.where(ge, mid, lo)
        n_lo = jnp.where(ge, cnt, n_lo)
        hi = jnp.where(cnt == kf, mid, jnp.where(ge, hi, mid - 1))
        return it + 1, lo, hi, n_lo

    zeros = jnp.zeros((tq, LANES), jnp.float32)
    n_pos = count_ge(zeros, strict=True)
    n_nonneg = count_ge(zeros)
    n_causal = q0 + _iota((tq, LANES), 0) + 1
    lo_neg = jnp.where(n_causal >= topk, _to_sortable(row_min), floor_key)
    n_neg = count_ge(_from_sortable(lo_neg))
    is_pos = n_pos >= kf
    is_zero = n_nonneg >= kf
    pick = lambda p, z, n: jnp.where(is_pos, p, jnp.where(is_zero, z, n))
    lo0 = pick(_sortable(np.float32(1e-45)), _sortable(0.0), lo_neg)
    hi0 = pick(_to_sortable(row_max), _sortable(0.0), _sortable(-0.0) - 1)
    _, lo, _, n_ge = lax.while_loop(bis_cond, bis_body, (jnp.int32(0), lo0, hi0, pick(n_pos, n_nonneg, n_neg)))
    thr = _lanes(_from_sortable(lo), tq)

    any_tie = _any(n_ge > kf)

    def causal_at(k0):
        return (k0 + _iota((tq, tq), 1)) <= (q0 + _iota((tq, tq), 0))

    @pl.when(jnp.logical_not(any_tie))
    def _():
        def body(c, carry):
            k0 = pl.multiple_of(c * tq, tq)
            sc = s_scr[:, pl.ds(k0, tq)]
            s_scr[:, pl.ds(k0, tq)] = jnp.where((sc >= thr) & causal_at(k0), 0.0, NEG_INF)
            return carry
        lax.fori_loop(0, n_kc, body, 0)

    @pl.when(any_tie)
    def _():
        need = _lanes(kf - count_ge(_from_sortable(lo), strict=True), tq)
        prefix = jnp.where(_iota((tq, tq), 0) <= _iota((tq, tq), 1), 1.0, 0.0)
        ones = jnp.ones((tq, LANES), jnp.float32)

        def body(c, seen):
            k0 = pl.multiple_of(c * tq, tq)
            sc = s_scr[:, pl.ds(k0, tq)]
            eq = jnp.where(sc == thr, 1.0, 0.0)
            rank_eq = _dot(eq, prefix) + _lanes(seen, tq)
            picked = (sc > thr) | ((sc == thr) & (rank_eq <= need))
            s_scr[:, pl.ds(k0, tq)] = jnp.where(picked & causal_at(k0), 0.0, NEG_INF)
            return seen + _dot(eq, ones)
        lax.fori_loop(0, n_kc, body, jnp.zeros((tq, LANES), jnp.float32))

    qs = [q_ref[0, h] for h in range(H)]
    _flash_init(m_scr, acc_scr)

    def att_body(c, carry):
        k0 = pl.multiple_of(c * tq, tq)
        bias = s_scr[:, pl.ds(k0, tq)]
        for h in range(H):
            s = jnp.dot(qs[h], kt_ref[0, :, pl.ds(k0, tq)], preferred_element_type=jnp.float32) + bias
            _flash_update(h, s, v_ref[0, pl.ds(k0, tq), :], m_scr, acc_scr)
        return carry

    lax.fori_loop(0, n_kc, att_body, 0)
    for h in range(H):
        o_ref[0, :, h * hd:(h + 1) * hd] = _flash_out(h, acc_scr)


def _dsa_call(q, kt, v, iq, ikt, iw):
    B, H, T, _ = q.shape
    tq = ATT_TILE
    topk = min(DSA_TOPK, T // 4)
    assert tq >= topk
    qs, ks, vs, out = _att_specs(B, H, T, tq, True)
    return pl.pallas_call(
        functools.partial(_dsa_kernel, tq=tq, topk=topk, idx_scale=(DSA_IDX_HEADS * DSA_IDX_DIM) ** -0.5),
        grid=(B, T // tq),
        in_specs=[qs, ks, vs, pl.BlockSpec((1, tq, 2 * LANES), lambda b, i: (b, i, 0)), ks,
                  pl.BlockSpec((1, tq, LANES), lambda b, i: (b, i, 0))],
        out_specs=out,
        out_shape=jax.ShapeDtypeStruct((B, T, H * HEAD_DIM), jnp.float32),
        scratch_shapes=[pltpu.VMEM((tq, T), jnp.float32), pltpu.VMEM((tq, LANES), jnp.float32)] + _att_scratch(tq),
        compiler_params=_params("arbitrary", "arbitrary"),
        name="dsa_attention",
    )(q, kt, v, iq, ikt, iw)


def _out_kernel(x_ref, ada_ref, o1_ref, o2_ref, o3_ref, o4_ref, gn_ref, w_ref, y_ref):
    a = ada_ref[0]
    gn = gn_ref[...]
    y = jnp.concatenate([_rms(o[0], gn[i:i + 1]).astype(MXU_DTYPE)
                         for i, o in enumerate((o1_ref, o2_ref, o3_ref, o4_ref))], axis=-1)
    y_ref[...] = x_ref[...] + a[5:6] * jnp.dot(y, w_ref[...], preferred_element_type=jnp.float32)


def _out_call(x2d, ada_l, groups, group_norm, w_out, B, T):
    N, D = x2d.shape
    tm = ROW_TILE
    tpb = T // tm
    grp = pl.BlockSpec((1, tm, GROUP_WIDTH), lambda i: (i // tpb, i % tpb, 0))
    return pl.pallas_call(
        _out_kernel,
        grid=(N // tm,),
        in_specs=[pl.BlockSpec((tm, D), lambda i: (i, 0)),
                  pl.BlockSpec((1, N_ADA, D), lambda i: (i // tpb, 0, 0)),
                  grp, grp, grp, grp,
                  pl.BlockSpec((N_GROUPS, GROUP_WIDTH), lambda i: (0, 0)),
                  pl.BlockSpec((MIX_WIDTH, D), lambda i: (0, 0))],
        out_specs=pl.BlockSpec((tm, D), lambda i: (i, 0)),
        out_shape=jax.ShapeDtypeStruct((N, D), jnp.float32),
        compiler_params=_params("arbitrary"),
        name="mixer_out_proj",
    )(x2d, ada_l, *groups, group_norm, w_out.astype(MXU_DTYPE))


def _mixer_groups(x2d, ada_l, tables, mix_norm, w_in, mla_q_norm, mla_w_uq, mla_kv_norm, mla_w_uk, mla_w_uv,
                  nsa_pe_k, nsa_pe_v, nsa_cmp_k_w1, nsa_cmp_k_w2, nsa_cmp_v_w1, nsa_cmp_v_w2, B, T):
    (mq, mkt, mv, lq, lkt, lv, nq, nkc, nvc, nkst, nvs, nkwt, nvw, ngate,
     dq, dkt, dv, diq, dikt, diw) = _proj_call(
        x2d, ada_l, mix_norm, w_in, tables, mla_q_norm, mla_w_uq, mla_kv_norm, mla_w_uk, mla_w_uv, B, T)
    o_moba = _moba_call(mq, mkt, mv)
    o_mla = _mla_call(lq, lkt, lv)
    kcmp, vcmp = _cmp_call(nkc, nvc, nsa_pe_k, nsa_pe_v, nsa_cmp_k_w1, nsa_cmp_k_w2, nsa_cmp_v_w1, nsa_cmp_v_w2)
    o_nsa = _nsa_call(nq, kcmp, vcmp, nkst, nvs, nkwt, nvw, ngate)
    o_dsa = _dsa_call(dq, dkt, dv, diq, dikt, diw)
    return o_moba, o_mla, o_nsa, o_dsa


def kernel(x, c, ada_w, ada_b, ffn1_norm, ffn1_w_gate, ffn1_w_up, ffn1_w_down, mix_norm, w_in, mla_q_norm, mla_w_uq, mla_kv_norm, mla_w_uk, mla_w_uv, nsa_pe_k, nsa_pe_v, nsa_cmp_k_w1, nsa_cmp_k_w2, nsa_cmp_v_w1, nsa_cmp_v_w2, group_norm, w_out, ffn2_norm, ffn2_w_gate, ffn2_w_up, ffn2_w_down, final_norm):
    B, T, D = x.shape
    L = ada_w.shape[0]
    assert D == D_MODEL and T % ROW_TILE == 0 and T % ATT_TILE == 0
    assert ATT_TILE % MOBA_BLOCK == 0 and ATT_TILE % NSA_SEL_BLOCK == 0 and ATT_TILE >= NSA_WINDOW
    tpb = T // ROW_TILE
    ada = _ada_call(c, ada_w, ada_b)
    tables = _rope_tables(T)
    x2d = x.reshape(B * T, D)
    for l in range(L):
        x2d = _ffn_call(x2d, ada[l], ffn1_norm[l], ffn1_w_gate[l], ffn1_w_up[l], ffn1_w_down[l], 0, tpb)
        groups = _mixer_groups(x2d, ada[l], tables, mix_norm[l], w_in[l], mla_q_norm[l], mla_w_uq[l],
                               mla_kv_norm[l], mla_w_uk[l], mla_w_uv[l], nsa_pe_k[l], nsa_pe_v[l],
                               nsa_cmp_k_w1[l], nsa_cmp_k_w2[l], nsa_cmp_v_w1[l], nsa_cmp_v_w2[l], B, T)
        x2d = _out_call(x2d, ada[l], groups, group_norm[l], w_out[l], B, T)
        x2d = _ffn_call(x2d, ada[l], ffn2_norm[l], ffn2_w_gate[l], ffn2_w_up[l], ffn2_w_down[l], 6, tpb,
                        final_gain=final_norm if l == L - 1 else None)
    return x2d.reshape(B, T, D)
```

```python
import functools
import math

import numpy as np
import jax
import jax.numpy as jnp
from jax import lax
from jax.experimental import pallas as pl
from jax.experimental.pallas import tpu as pltpu

D_MODEL = 1024
N_GROUPS = 4
HEAD_DIM = 64
GROUP_HEADS = D_MODEL // (N_GROUPS * HEAD_DIM)
GROUP_WIDTH = GROUP_HEADS * HEAD_DIM
MIX_WIDTH = N_GROUPS * GROUP_WIDTH
D_FF = 256 * ((8 * D_MODEL + 3 * 256 - 1) // (3 * 256))
N_ADA = 9
FFN_RESIDUAL_WEIGHT = 0.5
ROPE_THETA = 10000.0
RMS_EPS = 1e-6
NEG_INF = -1e30

MOBA_BLOCK = 256
MOBA_TOPK = 3

MLA_Q_LORA = D_MODEL // 4
MLA_KV_LORA = D_MODEL // 8
MLA_NOPE = HEAD_DIM
MLA_ROPE = HEAD_DIM // 2
MLA_V = HEAD_DIM
MLA_QK = MLA_NOPE + MLA_ROPE

NSA_CMP_LEN = 32
NSA_CMP_STRIDE = 16
NSA_CMP_HIDDEN = 4 * HEAD_DIM
NSA_SEL_BLOCK = 64
NSA_SEL_TOPN = 16
NSA_WINDOW = 512
NSA_FORCE_SCORE = 1e4

DSA_TOPK = 256
DSA_IDX_HEADS = 8
DSA_IDX_DIM = 32

IN_NAMES = ("mq", "mk", "mv", "cq", "ckv", "kr", "nq", "nkc", "nvc", "nks", "nvs", "nkw", "nvw",
            "ngate", "dq", "dk", "dv", "diq", "dik", "diw")
IN_SIZES = (
    GROUP_WIDTH, GROUP_WIDTH, GROUP_WIDTH,
    MLA_Q_LORA, MLA_KV_LORA, MLA_ROPE,
    GROUP_WIDTH, HEAD_DIM, HEAD_DIM, HEAD_DIM, HEAD_DIM,
    HEAD_DIM, HEAD_DIM, 3 * GROUP_HEADS,
    GROUP_WIDTH, HEAD_DIM, HEAD_DIM,
    DSA_IDX_HEADS * DSA_IDX_DIM, DSA_IDX_DIM, DSA_IDX_HEADS,
)
N_IN = sum(IN_SIZES)

LANES = 128
MXU_DTYPE = jnp.bfloat16
VMEM_LIMIT = 56 * 1024 * 1024

ATT_TILE = 512
ROW_TILE = 512
MXU_TILE = 256
FF_CHUNK = 6 * MXU_TILE

LOG2E = math.log2(math.e)
M_INIT = -1e29
DEN_LANE = HEAD_DIM


def _params(*semantics):
    return pltpu.CompilerParams(dimension_semantics=semantics, vmem_limit_bytes=VMEM_LIMIT)


def _dot(a, b):
    return jnp.dot(a.astype(MXU_DTYPE), b.astype(MXU_DTYPE), preferred_element_type=jnp.float32)


def _dot_nt(a, b):
    return lax.dot_general(a.astype(MXU_DTYPE), b.astype(MXU_DTYPE), (((1,), (1,)), ((), ())),
                           preferred_element_type=jnp.float32)


def _rms(x, g):
    return x * lax.rsqrt(jnp.mean(x * x, axis=-1, keepdims=True) + RMS_EPS) * g


def _silu(x):
    return x * (1.0 / (1.0 + jnp.exp(-x)))


def _iota(shape, dim):
    return lax.broadcasted_iota(jnp.int32, shape, dim)


def _log2(n):
    assert n & (n - 1) == 0
    return n.bit_length() - 1


def _any(pred):
    return jnp.max(jnp.where(pred, 1.0, 0.0)) > 0.5


def _lanes(x, width):
    return x if width == LANES else jnp.tile(x, (1, width // LANES))


def _ada_kernel(c_ref, w_ref, b_ref, o_ref):
    o_ref[0] = _dot(_silu(c_ref[...]), w_ref[0]) + b_ref[0]


def _ada_call(c, ada_w, ada_b):
    L, D, _ = ada_w.shape
    B = c.shape[0]
    out = pl.pallas_call(
        _ada_kernel,
        grid=(L, N_ADA),
        in_specs=[
            pl.BlockSpec((B, D), lambda l, k: (0, 0)),
            pl.BlockSpec((1, D, D), lambda l, k: (l, 0, k)),
            pl.BlockSpec((1, 1, D), lambda l, k: (l, 0, k)),
        ],
        out_specs=pl.BlockSpec((1, B, D), lambda l, k: (l, 0, k)),
        out_shape=jax.ShapeDtypeStruct((L, B, N_ADA * D), jnp.float32),
        compiler_params=_params("arbitrary", "arbitrary"),
        name="ada_proj",
    )(c, ada_w, ada_b.reshape(L, 1, N_ADA * D))
    return out.reshape(L, B, N_ADA, D)


def _ffn_chunks(F):
    bounds = list(range(0, F, FF_CHUNK)) + [F]
    return list(zip(bounds[:-1], bounds[1:]))


def _ffn_kernel(x_ref, ada_ref, gn_ref, wg_ref, wu_ref, wd_ref, *rest, k0, final):
    if final:
        fg_ref, o_ref = rest
    else:
        (o_ref,) = rest
    a = ada_ref[0]
    h = (_rms(x_ref[...], gn_ref[...]) * (1.0 + a[k0 + 1:k0 + 2]) + a[k0:k0 + 1]).astype(MXU_DTYPE)
    acc = None
    for lo, hi in _ffn_chunks(wg_ref.shape[1]):
        g = jnp.dot(h, wg_ref[:, lo:hi], preferred_element_type=jnp.float32)
        u = jnp.dot(h, wu_ref[:, lo:hi], preferred_element_type=jnp.float32)
        part = jnp.dot((_silu(g) * u).astype(MXU_DTYPE), wd_ref[lo:hi, :], preferred_element_type=jnp.float32)
        acc = part if acc is None else acc + part
    y = x_ref[...] + (FFN_RESIDUAL_WEIGHT * a[k0 + 2:k0 + 3]) * acc
    if final:
        y = _rms(y, fg_ref[...])
    o_ref[...] = y


def _ffn_call(x2d, ada_l, norm_g, w_gate, w_up, w_down, k0, tiles_per_batch, final_gain=None):
    N, D = x2d.shape
    F = w_gate.shape[1]
    tm = ROW_TILE
    in_specs = [
        pl.BlockSpec((tm, D), lambda i: (i, 0)),
        pl.BlockSpec((1, N_ADA, D), lambda i: (i // tiles_per_batch, 0, 0)),
        pl.BlockSpec((1, D), lambda i: (0, 0)),
        pl.BlockSpec((D, F), lambda i: (0, 0)),
        pl.BlockSpec((D, F), lambda i: (0, 0)),
        pl.BlockSpec((F, D), lambda i: (0, 0)),
    ]
    args = [x2d, ada_l, norm_g.reshape(1, D), w_gate.astype(MXU_DTYPE), w_up.astype(MXU_DTYPE),
            w_down.astype(MXU_DTYPE)]
    final = final_gain is not None
    if final:
        in_specs.append(pl.BlockSpec((1, D), lambda i: (0, 0)))
        args.append(final_gain.reshape(1, D))
    return pl.pallas_call(
        functools.partial(_ffn_kernel, k0=k0, final=final),
        grid=(N // tm,),
        in_specs=in_specs,
        out_specs=pl.BlockSpec((tm, D), lambda i: (i, 0)),
        out_shape=jax.ShapeDtypeStruct((N, D), jnp.float32),
        compiler_params=_params("arbitrary"),
        name="ffn",
    )(*args)


G_QMAIN, G_QSWAP, G_KC, G_V, G_MISC, ROW_GROUPS = 0, 12, 24, 26, 33, 41
T_MK, T_MKS, T_SW, T_SWS, T_DK, T_DKS, T_IK, T_ROWS = 0, 256, 512, 640, 768, 896, 1024, 1152


def _swap_halves(c, width):
    return c.reshape(-1, 2, width // 2)[:, ::-1, :].reshape(-1)


def _proj_indices():
    off = dict(zip(IN_NAMES, np.cumsum((0,) + IN_SIZES[:-1]).tolist()))
    size = dict(zip(IN_NAMES, IN_SIZES))
    cols = lambda name: np.arange(off[name], off[name] + size[name])
    zero = lambda n: np.full((n,), N_IN)
    hd = HEAD_DIM

    def head_groups(c):
        return np.concatenate([np.concatenate([c[i:i + hd], zero(LANES - hd)]) for i in range(0, c.size, hd)])

    q = np.concatenate([cols("mq"), cols("nq"), cols("dq")])
    row = np.concatenate([
        head_groups(q), head_groups(_swap_halves(q, hd)),
        head_groups(cols("nkc")), head_groups(_swap_halves(cols("nkc"), hd)),
        head_groups(np.concatenate([cols("mv"), cols("nvs"), cols("nvw"), cols("dv")])),
        cols("cq"), cols("ckv"), head_groups(cols("nvc")), cols("diq"),
        cols("ngate"), zero(LANES - size["ngate"]), cols("diw"), zero(LANES - size["diw"])])
    assert row.size == ROW_GROUPS * LANES
    sw = np.concatenate([cols("nks"), cols("nkw")])
    dk_main = np.concatenate([cols("dk"), cols("kr"), zero(LANES - hd - MLA_ROPE)])
    dk_swap = np.concatenate([_swap_halves(cols("dk"), hd), _swap_halves(cols("kr"), MLA_ROPE),
                              zero(LANES - hd - MLA_ROPE)])
    tr = np.concatenate([cols("mk"), _swap_halves(cols("mk"), hd), sw, _swap_halves(sw, hd),
                         dk_main, dk_swap, np.tile(cols("dik"), LANES // DSA_IDX_DIM)])
    assert tr.size == T_ROWS
    return row, tr


def _proj_kernel(x_ref, ada_ref, gn_ref, wr_ref, wt_ref, rtab_ref, ttab_ref,
                 qn_ref, wuq_ref, kvn_ref, wukt_ref, wuv_ref,
                 mq_ref, mkt_ref, mv_ref, lq_ref, lkt_ref, lv_ref,
                 nq_ref, nkc_ref, nvc_ref, nkst_ref, nvs_ref, nkwt_ref, nvw_ref, ng_ref,
                 dq_ref, dkt_ref, dv_ref, diq_ref, dikt_ref, diw_ref, *, tm, tpb, n_moba, n_sel):
    H, hd, G = GROUP_HEADS, HEAD_DIM, LANES
    t0 = (pl.program_id(0) % tpb) * tm
    a = ada_ref[0]
    h = (_rms(x_ref[...], gn_ref[...]) * (1.0 + a[4:5]) + a[3:4]).astype(MXU_DTYPE)

    def rows(g0, n):
        return jnp.dot(h, wr_ref[:, g0 * G:(g0 + n) * G], preferred_element_type=jnp.float32)

    def cols(r0, n):
        return _dot_nt(wt_ref[r0:r0 + n, :], h)

    roped_q = (rows(G_QMAIN, 3 * H) * _lanes(rtab_ref[0], 3 * H * G)
               + rows(G_QSWAP, 3 * H) * _lanes(rtab_ref[1], 3 * H * G))
    for hh in range(H):
        mq_ref[0, hh] = roped_q[:, hh * G:(hh + 1) * G].astype(mq_ref.dtype)
        nq_ref[0, hh] = roped_q[:, (H + hh) * G:(H + hh + 1) * G].astype(nq_ref.dtype)
        dq_ref[0, hh] = roped_q[:, (2 * H + hh) * G:(2 * H + hh + 1) * G].astype(dq_ref.dtype)
    kc = rows(G_KC, 2)
    nkc_ref[0] = (kc[:, :G] * rtab_ref[2] + kc[:, G:] * rtab_ref[3])[:, :hd].astype(nkc_ref.dtype)

    ones_hi = jnp.where(_iota((1, G), 1) >= DEN_LANE, 1.0, 0.0)
    v = rows(G_V, H + 3) + _lanes(ones_hi, (H + 3) * G)
    for hh in range(H):
        mv_ref[0, hh] = v[:, hh * G:(hh + 1) * G].astype(mv_ref.dtype)
    nvs_ref[0] = v[:, H * G:(H + 1) * G].astype(nvs_ref.dtype)
    nvw_ref[0] = v[:, (H + 1) * G:(H + 2) * G].astype(nvw_ref.dtype)
    dv_ref[0] = v[:, (H + 2) * G:(H + 3) * G].astype(dv_ref.dtype)

    misc = rows(G_MISC, 8)
    cq = misc[:, :MLA_Q_LORA]
    ckv = misc[:, MLA_Q_LORA:MLA_Q_LORA + MLA_KV_LORA]
    nvc_ref[0] = misc[:, 3 * G:3 * G + hd].astype(nvc_ref.dtype)
    diq_ref[0] = misc[:, 4 * G:6 * G].astype(diq_ref.dtype)
    ng_ref[0] = misc[:, 6 * G:7 * G]
    diw_ref[0] = misc[:, 7 * G:8 * G]

    tok = t0 + _iota((hd, tm), 1)
    rid = _iota((hd, tm), 0)
    oh_moba = jnp.where((rid & (n_moba - 1)) == (tok >> _log2(MOBA_BLOCK)), 1.0, 0.0)
    oh_sel = jnp.where((rid == (tok >> _log2(NSA_SEL_BLOCK))) & (rid < n_sel), 1.0, 0.0)
    zeros_lo = jnp.zeros((hd, tm), jnp.float32)
    ta_c, ta_s, tb_c, tb_s = ttab_ref[0], ttab_ref[1], ttab_ref[2], ttab_ref[3]
    mkt = (cols(T_MK, H * hd) * jnp.tile(ta_c, (H * hd // G, 1))
           + cols(T_MKS, H * hd) * jnp.tile(ta_s, (H * hd // G, 1)))
    for hh in range(H):
        mine = (rid >> _log2(n_moba)) == hh
        mkt_ref[0, hh] = jnp.concatenate([mkt[hh * hd:(hh + 1) * hd], jnp.where(mine, oh_moba, 0.0)],
                                         axis=0).astype(mkt_ref.dtype)
    sw = cols(T_SW, G) * ta_c + cols(T_SWS, G) * ta_s
    nkst_ref[0] = jnp.concatenate([sw[:hd], oh_sel], axis=0).astype(nkst_ref.dtype)
    nkwt_ref[0] = jnp.concatenate([sw[hd:], zeros_lo], axis=0).astype(nkwt_ref.dtype)
    dkr = cols(T_DK, G) * tb_c + cols(T_DKS, G) * tb_s
    rid2 = _iota((G, tm), 0)
    dkt_ref[0] = jnp.where(rid2 < hd, dkr, 0.0).astype(dkt_ref.dtype)
    kpe_rows = jnp.where((rid2 >= MLA_NOPE) & (rid2 < MLA_QK), dkr, 0.0)
    dikt_ref[0] = cols(T_IK, G).astype(dikt_ref.dtype)

    cqn = _rms(cq, qn_ref[...]).astype(MXU_DTYPE)
    lq = (jnp.dot(cqn, wuq_ref[:, :H * G], preferred_element_type=jnp.float32) * _lanes(rtab_ref[4], H * G)
          + jnp.dot(cqn, wuq_ref[:, H * G:], preferred_element_type=jnp.float32) * _lanes(rtab_ref[5], H * G))
    ckvn = _rms(ckv, kvn_ref[...]).astype(MXU_DTYPE)
    knt = _dot_nt(wukt_ref[...], ckvn)
    lv = jnp.dot(ckvn, wuv_ref[...], preferred_element_type=jnp.float32) + _lanes(ones_hi, H * G)
    for hh in range(H):
        lq_ref[0, hh] = lq[:, hh * G:(hh + 1) * G].astype(lq_ref.dtype)
        lkt_ref[0, hh] = (knt[hh * G:(hh + 1) * G] + kpe_rows).astype(lkt_ref.dtype)
        lv_ref[0, hh] = lv[:, hh * G:(hh + 1) * G].astype(lv_ref.dtype)


def _rope_tables(T):
    def cs(dim):
        inv_freq = 1.0 / (ROPE_THETA ** (np.arange(0, dim, 2, dtype=np.float32) / dim))
        ang = jnp.arange(T, dtype=jnp.float32)[:, None] * jnp.asarray(inv_freq, jnp.float32)[None, :]
        cos, sin = jnp.cos(ang), jnp.sin(ang)
        return jnp.concatenate([cos, cos], axis=-1), jnp.concatenate([-sin, sin], axis=-1)

    c64, s64 = cs(HEAD_DIM)
    c32, s32 = cs(MLA_ROPE)
    pad = lambda t, n: jnp.concatenate([t, jnp.zeros((T, n), jnp.float32)], axis=-1)
    sc = HEAD_DIM ** -0.5 * LOG2E
    sl = MLA_QK ** -0.5 * LOG2E
    ones = jnp.ones((T, MLA_NOPE), jnp.float32)
    rest = LANES - MLA_QK
    rtab = jnp.stack([
        pad(c64 * sc, LANES - HEAD_DIM), pad(s64 * sc, LANES - HEAD_DIM),
        pad(c64, LANES - HEAD_DIM), pad(s64, LANES - HEAD_DIM),
        pad(jnp.concatenate([ones, c32], axis=-1) * sl, rest),
        pad(jnp.concatenate([0.0 * ones, s32], axis=-1) * sl, rest)])
    ttab = jnp.stack([
        jnp.concatenate([c64, c64], axis=-1).T, jnp.concatenate([s64, s64], axis=-1).T,
        pad(jnp.concatenate([c64, c32], axis=-1), rest).T, pad(jnp.concatenate([s64, s32], axis=-1), rest).T])
    return rtab, ttab


def _proj_call(x2d, ada_l, norm_g, w_in, tables, mla_q_norm, mla_w_uq, mla_kv_norm, mla_w_uk, mla_w_uv, B, T):
    N, D = x2d.shape
    H, G = GROUP_HEADS, LANES
    tm = ROW_TILE
    tpb = T // tm
    n_moba, n_sel = T // MOBA_BLOCK, T // NSA_SEL_BLOCK
    assert H * n_moba <= LANES - HEAD_DIM and n_sel <= LANES - HEAD_DIM and n_moba & (n_moba - 1) == 0
    zcol = lambda w: jnp.concatenate([w, jnp.zeros((w.shape[0], 1), w.dtype)], axis=1)
    row_idx, tr_idx = _proj_indices()
    w_ext = zcol(w_in)
    w_row = w_ext[:, row_idx].astype(MXU_DTYPE)
    w_tr = w_ext[:, tr_idx].T.astype(MXU_DTYPE)
    zq = mla_w_uq.shape[1]
    per_head = np.arange(H * MLA_QK).reshape(H, MLA_QK)
    main = np.concatenate([np.concatenate([per_head[i], np.full((G - MLA_QK,), zq)]) for i in range(H)])
    part = np.concatenate([np.concatenate([np.full((MLA_NOPE,), zq), _swap_halves(per_head[i, MLA_NOPE:], MLA_ROPE),
                                           np.full((G - MLA_QK,), zq)]) for i in range(H)])
    wuq = zcol(mla_w_uq)[:, np.concatenate([main, part])].astype(MXU_DTYPE)
    zv = mla_w_uk.shape[1]
    grp = np.concatenate([np.concatenate([np.arange(i * HEAD_DIM, (i + 1) * HEAD_DIM), np.full((G - HEAD_DIM,), zv)])
                          for i in range(H)])
    wukt = zcol(mla_w_uk)[:, grp].T.astype(MXU_DTYPE)
    wuv = zcol(mla_w_uv)[:, grp].astype(MXU_DTYPE)
    rtab, ttab = tables

    row = lambda i: (i, 0)
    const2 = lambda i: (0, 0)
    in_specs = [
        pl.BlockSpec((tm, D), row),
        pl.BlockSpec((1, N_ADA, D), lambda i: (i // tpb, 0, 0)),
        pl.BlockSpec((1, D), const2),
        pl.BlockSpec(w_row.shape, const2),
        pl.BlockSpec(w_tr.shape, const2),
        pl.BlockSpec((6, tm, G), lambda i: (0, i % tpb, 0)),
        pl.BlockSpec((4, G, tm), lambda i: (0, 0, i % tpb)),
        pl.BlockSpec((1, MLA_Q_LORA), const2),
        pl.BlockSpec(wuq.shape, const2),
        pl.BlockSpec((1, MLA_KV_LORA), const2),
        pl.BlockSpec(wukt.shape, const2),
        pl.BlockSpec(wuv.shape, const2),
    ]
    dt = MXU_DTYPE
    hq = (jax.ShapeDtypeStruct((B, H, T, G), dt), pl.BlockSpec((1, H, tm, G), lambda i: (i // tpb, 0, i % tpb, 0)))
    hkt = (jax.ShapeDtypeStruct((B, H, G, T), dt), pl.BlockSpec((1, H, G, tm), lambda i: (i // tpb, 0, 0, i % tpb)))
    srow = lambda d, t=dt: (jax.ShapeDtypeStruct((B, T, d), t), pl.BlockSpec((1, tm, d), lambda i: (i // tpb, i % tpb, 0)))
    skt = (jax.ShapeDtypeStruct((B, G, T), dt), pl.BlockSpec((1, G, tm), lambda i: (i // tpb, 0, i % tpb)))
    outs = [hq, hkt, hq,
            hq, hkt, hq,
            hq, srow(HEAD_DIM), srow(HEAD_DIM), skt, srow(G), skt, srow(G), srow(G, jnp.float32),
            hq, skt, srow(G), srow(2 * G), skt, srow(G, jnp.float32)]
    return pl.pallas_call(
        functools.partial(_proj_kernel, tm=tm, tpb=tpb, n_moba=n_moba, n_sel=n_sel),
        grid=(N // tm,),
        in_specs=in_specs,
        out_specs=[o[1] for o in outs],
        out_shape=[o[0] for o in outs],
        compiler_params=_params("arbitrary"),
        name="mixer_in_proj",
    )(x2d, ada_l, norm_g.reshape(1, D), w_row, w_tr, rtab, ttab,
      mla_q_norm.reshape(1, -1), wuq, mla_kv_norm.reshape(1, -1), wukt, wuv)


def _flash_init(m_scr, acc_scr):
    m_scr[...] = jnp.full(m_scr.shape, M_INIT, jnp.float32)
    acc_scr[...] = jnp.zeros_like(acc_scr)


def _flash_update(h, s, v, m_scr, acc_scr):
    m_prev = m_scr[h]
    m_new = jnp.maximum(m_prev, jnp.max(s, axis=-1, keepdims=True))
    p = jnp.exp2(s - _lanes(m_new, s.shape[1]))
    acc_scr[h] = jnp.exp2(m_prev - m_new) * acc_scr[h] + _dot(p, v)
    m_scr[h] = m_new


def _flash_out(h, acc_scr):
    acc = acc_scr[h]
    den = acc[:, DEN_LANE:DEN_LANE + 1]
    return acc[:, :HEAD_DIM] / jnp.where(den > 0.0, den, 1.0)


def _causal_bias(t):
    return jnp.where(_iota((t, t), 1) <= _iota((t, t), 0), 0.0, NEG_INF)


def _rank_desc(x):
    n = x.shape[0]
    row = _iota(x.shape, 0)
    rank = jnp.zeros(x.shape, jnp.float32)
    for j in range(n):
        cand = x[j:j + 1, :]
        rank = rank + jnp.where(cand > x, 1.0, jnp.where((cand == x) & (row > j), 1.0, 0.0))
    return rank


def _att_scratch(tq):
    return [pltpu.VMEM((GROUP_HEADS, tq, LANES), jnp.float32), pltpu.VMEM((GROUP_HEADS, tq, LANES), jnp.float32)]


def _att_specs(B, H, T, tq, shared_kv):
    q = pl.BlockSpec((1, H, tq, LANES), lambda b, i: (b, 0, i, 0))
    if shared_kv:
        kt = pl.BlockSpec((1, LANES, T), lambda b, i: (b, 0, 0))
        v = pl.BlockSpec((1, T, LANES), lambda b, i: (b, 0, 0))
    else:
        kt = pl.BlockSpec((1, H, LANES, T), lambda b, i: (b, 0, 0, 0))
        v = pl.BlockSpec((1, H, T, LANES), lambda b, i: (b, 0, 0, 0))
    out = pl.BlockSpec((1, tq, H * HEAD_DIM), lambda b, i: (b, i, 0))
    return q, kt, v, out


def _mla_kernel(q_ref, kt_ref, v_ref, o_ref, m_scr, acc_scr, *, tq):
    H = GROUP_HEADS
    qi = pl.program_id(1)
    qs = [q_ref[0, h] for h in range(H)]
    _flash_init(m_scr, acc_scr)

    def chunk(k0, bias):
        for h in range(H):
            s = jnp.dot(qs[h], kt_ref[0, h, :, pl.ds(k0, tq)], preferred_element_type=jnp.float32)
            if bias is not None:
                s = s + bias
            _flash_update(h, s, v_ref[0, h, pl.ds(k0, tq), :], m_scr, acc_scr)

    def body(c, carry):
        chunk(pl.multiple_of(c * tq, tq), None)
        return carry

    lax.fori_loop(0, qi, body, 0)
    chunk(pl.multiple_of(qi * tq, tq), _causal_bias(tq))
    for h in range(H):
        o_ref[0, :, h * HEAD_DIM:(h + 1) * HEAD_DIM] = _flash_out(h, acc_scr)


def _mla_call(q, kt, v):
    B, H, T, _ = q.shape
    tq = ATT_TILE
    qs, ks, vs, out = _att_specs(B, H, T, tq, False)
    return pl.pallas_call(
        functools.partial(_mla_kernel, tq=tq),
        grid=(B, T // tq),
        in_specs=[qs, ks, vs],
        out_specs=out,
        out_shape=jax.ShapeDtypeStruct((B, T, H * HEAD_DIM), jnp.float32),
        scratch_shapes=_att_scratch(tq),
        compiler_params=_params("arbitrary", "arbitrary"),
        name="mla_attention",
    )(q, kt, v)


def _moba_kernel(q_ref, kt_ref, v_ref, o_ref, kmean_scr, m_scr, acc_scr, *, tq, nb, topk):
    H, hd = GROUP_HEADS, HEAD_DIM
    qi = pl.program_id(1)
    q0 = qi * tq
    T = nb * MOBA_BLOCK

    @pl.when(qi == 0)
    def _():
        avg = jnp.where((_iota((nb, T), 1) >> _log2(MOBA_BLOCK)) == _iota((nb, T), 0), 1.0 / MOBA_BLOCK, 0.0)
        for h in range(H):
            kmean_scr[h] = _dot_nt(avg, kt_ref[0, h])

    blk = _iota((nb, tq), 0)
    own = (q0 + _iota((nb, tq), 1)) >> _log2(MOBA_BLOCK)
    past = blk < own
    bias_rows = [jnp.zeros((hd, tq), jnp.float32)]
    for h in range(H):
        gate = jnp.where(past, _dot_nt(kmean_scr[h], q_ref[0, h]), NEG_INF)
        allowed = ((_rank_desc(gate) < topk) & past) | (blk == own)
        bias_rows.append(jnp.where(allowed, 0.0, NEG_INF))
    if H * nb < LANES - hd:
        bias_rows.append(jnp.zeros((LANES - hd - H * nb, tq), jnp.float32))
    bias = jnp.concatenate(bias_rows, axis=0).T
    lane = _iota((tq, LANES), 1)
    qs = [jnp.where(lane < hd, q_ref[0, h], bias.astype(q_ref.dtype)) for h in range(H)]
    _flash_init(m_scr, acc_scr)

    def chunk(k0, bias):
        for h in range(H):
            s = jnp.dot(qs[h], kt_ref[0, h, :, pl.ds(k0, tq)], preferred_element_type=jnp.float32)
            if bias is not None:
                s = s + bias
            _flash_update(h, s, v_ref[0, h, pl.ds(k0, tq), :], m_scr, acc_scr)

    def body(c, carry):
        chunk(pl.multiple_of(c * tq, tq), None)
        return carry

    lax.fori_loop(0, qi, body, 0)
    chunk(pl.multiple_of(q0, tq), _causal_bias(tq))
    for h in range(H):
        o_ref[0, :, h * hd:(h + 1) * hd] = _flash_out(h, acc_scr)


def _moba_call(q, kt, v):
    B, H, T, _ = q.shape
    tq = ATT_TILE
    nb = T // MOBA_BLOCK
    qs, ks, vs, out = _att_specs(B, H, T, tq, False)
    return pl.pallas_call(
        functools.partial(_moba_kernel, tq=tq, nb=nb, topk=min(MOBA_TOPK, nb - 1)),
        grid=(B, T // tq),
        in_specs=[qs, ks, vs],
        out_specs=out,
        out_shape=jax.ShapeDtypeStruct((B, T, H * HEAD_DIM), jnp.float32),
        scratch_shapes=[pltpu.VMEM((H, nb, LANES), jnp.float32)] + _att_scratch(tq),
        compiler_params=_params("arbitrary", "arbitrary"),
        name="moba_attention",
    )(q, kt, v)


def _cmp_kernel(k_ref, v_ref, pek_ref, pev_ref, kw1_ref, kw2_ref, vw1_ref, vw2_ref, ko_ref, vo_ref, *, rows):
    half = NSA_CMP_STRIDE * HEAD_DIM

    def compress(t_ref, pe_ref, w1_ref, w2_ref):
        t = t_ref[0].astype(jnp.float32)
        first = _dot(t + pe_ref[0:1, :], w1_ref[0:half, :])
        second = _dot(t + pe_ref[1:2, :], w1_ref[half:, :])
        hid = first + pltpu.roll(second, rows - 1, 0)
        return _dot(_silu(hid), w2_ref[...])

    ko_ref[0] = compress(k_ref, pek_ref, kw1_ref, kw2_ref)
    vo_ref[0] = compress(v_ref, pev_ref, vw1_ref, vw2_ref)


def _cmp_call(kc, vc, pe_k, pe_v, k_w1, k_w2, v_w1, v_w2):
    B, T, d = kc.shape
    rows = T // NSA_CMP_STRIDE
    wide = NSA_CMP_STRIDE * d
    assert NSA_CMP_LEN == 2 * NSA_CMP_STRIDE
    const2 = lambda b: (0, 0)
    blk = pl.BlockSpec((1, rows, wide), lambda b: (b, 0, 0))
    out = pl.BlockSpec((1, rows, d), lambda b: (b, 0, 0))
    return pl.pallas_call(
        functools.partial(_cmp_kernel, rows=rows),
        grid=(B,),
        in_specs=[blk, blk, pl.BlockSpec((2, wide), const2), pl.BlockSpec((2, wide), const2),
                  pl.BlockSpec(k_w1.shape, const2), pl.BlockSpec(k_w2.shape, const2),
                  pl.BlockSpec(v_w1.shape, const2), pl.BlockSpec(v_w2.shape, const2)],
        out_specs=[out, out],
        out_shape=[jax.ShapeDtypeStruct((B, rows, d), jnp.float32)] * 2,
        compiler_params=_params("arbitrary"),
        name="nsa_compress",
    )(kc.reshape(B, rows, wide), vc.reshape(B, rows, wide), pe_k.reshape(2, wide), pe_v.reshape(2, wide),
      k_w1.astype(MXU_DTYPE), k_w2.astype(MXU_DTYPE), v_w1.astype(MXU_DTYPE), v_w2.astype(MXU_DTYPE))


def _nsa_kernel(q_ref, kcmp_ref, vcmp_ref, kst_ref, vs_ref, kwt_ref, vw_ref, g_ref, o_ref,
                oc_scr, os_scr, m_scr, acc_scr, *, tq, ncp, n_sel, topn):
    H, hd = GROUP_HEADS, HEAD_DIM
    qi = pl.program_id(1)
    q0 = qi * tq
    tq_col = q0 + _iota((tq, 1), 0)
    qraw = [q_ref[0, h] for h in range(H)]

    cmp_end = _iota((tq, ncp), 1) * NSA_CMP_STRIDE + (NSA_CMP_LEN - 1)
    m_c = cmp_end <= tq_col
    p_sum = jnp.zeros((tq, ncp), jnp.float32)
    for h in range(H):
        s = jnp.where(m_c, _dot_nt(qraw[h][:, :hd], kcmp_ref[0]), NEG_INF)
        e = jnp.where(m_c, jnp.exp2(s - jnp.max(s, axis=-1, keepdims=True)), 0.0)
        l = jnp.sum(e, axis=-1, keepdims=True)
        p = e / jnp.where(l > 0.0, l, 1.0)
        p_sum = p_sum + p
        oc_scr[h] = _dot(p, vcmp_ref[0])

    cmp_start = _iota((n_sel, ncp), 1) * NSA_CMP_STRIDE
    sel_start = _iota((n_sel, ncp), 0) * NSA_SEL_BLOCK
    overlap = (cmp_start < sel_start + NSA_SEL_BLOCK) & (cmp_start + NSA_CMP_LEN > sel_start)
    imp = _dot_nt(jnp.where(overlap, 1.0, 0.0), p_sum)
    sel_id = _iota((n_sel, tq), 0)
    own = (q0 + _iota((n_sel, tq), 1)) >> _log2(NSA_SEL_BLOCK)
    causal = sel_id <= own
    forced = causal & ((sel_id == 0) | (sel_id >= own - 1))
    imp = jnp.where(forced, NSA_FORCE_SCORE, jnp.where(causal, imp, -NSA_FORCE_SCORE))
    bias_rows = [jnp.zeros((hd, tq), jnp.float32), jnp.where(_rank_desc(imp) < topn, 0.0, NEG_INF)]
    if n_sel < LANES - hd:
        bias_rows.append(jnp.zeros((LANES - hd - n_sel, tq), jnp.float32))
    bias = jnp.concatenate(bias_rows, axis=0).T.astype(qraw[0].dtype)
    lane = _iota((tq, LANES), 1)
    qsel = [jnp.where(lane < hd, qraw[h], bias) for h in range(H)]

    _flash_init(m_scr, acc_scr)

    def sel_chunk(k0, extra):
        for h in range(H):
            s = jnp.dot(qsel[h], kst_ref[0, :, pl.ds(k0, tq)], preferred_element_type=jnp.float32)
            if extra is not None:
                s = s + extra
            _flash_update(h, s, vs_ref[0, pl.ds(k0, tq), :], m_scr, acc_scr)

    def sel_body(c, carry):
        sel_chunk(pl.multiple_of(c * tq, tq), None)
        return carry

    lax.fori_loop(0, qi, sel_body, 0)
    tri = _causal_bias(tq)
    sel_chunk(pl.multiple_of(q0, tq), tri)
    for h in range(H):
        os_scr[h] = _flash_out(h, acc_scr)

    _flash_init(m_scr, acc_scr)

    def win_chunk(k0, extra):
        for h in range(H):
            s = jnp.dot(qraw[h], kwt_ref[0, :, pl.ds(k0, tq)], preferred_element_type=jnp.float32) + extra
            _flash_update(h, s, vw_ref[0, pl.ds(k0, tq), :], m_scr, acc_scr)

    first = jnp.maximum(q0 - NSA_WINDOW + 1, 0) // tq

    def win_body(c, carry):
        k0 = pl.multiple_of(c * tq, tq)
        qpos = q0 + _iota((tq, tq), 0)
        kpos = k0 + _iota((tq, tq), 1)
        win_chunk(k0, jnp.where((kpos <= qpos) & (kpos > qpos - NSA_WINDOW), 0.0, NEG_INF))
        return carry

    lax.fori_loop(first, qi + 1, win_body, 0)
    for h in range(H):
        gates = 1.0 / (1.0 + jnp.exp(-g_ref[0][:, 3 * h:3 * h + 3]))
        o_ref[0, :, h * hd:(h + 1) * hd] = (gates[:, 0:1] * oc_scr[h] + gates[:, 1:2] * os_scr[h]
                                            + gates[:, 2:3] * _flash_out(h, acc_scr))


def _nsa_call(q, kcmp, vcmp, kst, vs, kwt, vw, gate_logits):
    B, H, T, _ = q.shape
    tq = ATT_TILE
    ncp = kcmp.shape[1]
    n_sel = T // NSA_SEL_BLOCK
    qs, ks, vsp, out = _att_specs(B, H, T, tq, True)
    cmp_spec = pl.BlockSpec((1, ncp, HEAD_DIM), lambda b, i: (b, 0, 0))
    return pl.pallas_call(
        functools.partial(_nsa_kernel, tq=tq, ncp=ncp, n_sel=n_sel, topn=min(NSA_SEL_TOPN, n_sel)),
        grid=(B, T // tq),
        in_specs=[qs, cmp_spec, cmp_spec, ks, vsp, ks, vsp,
                  pl.BlockSpec((1, tq, LANES), lambda b, i: (b, i, 0))],
        out_specs=out,
        out_shape=jax.ShapeDtypeStruct((B, T, H * HEAD_DIM), jnp.float32),
        scratch_shapes=[pltpu.VMEM((H, tq, HEAD_DIM), jnp.float32), pltpu.VMEM((H, tq, HEAD_DIM), jnp.float32)]
        + _att_scratch(tq),
        compiler_params=_params("arbitrary", "arbitrary"),
        name="nsa_attention",
    )(q, kcmp, vcmp, kst, vs, kwt, vw, gate_logits)


def _sortable(x):
    b = int(np.float32(x).view(np.int32))
    return b ^ ((b >> 31) & 0x7FFFFFFF)


def _from_sortable(k):
    return lax.bitcast_convert_type(k ^ ((k >> 31) & 0x7FFFFFFF), jnp.float32)


def _to_sortable(x):
    k = lax.bitcast_convert_type(x, jnp.int32)
    return k ^ ((k >> 31) & 0x7FFFFFFF)


COUNT_ROWS = 64
VALUE_STEPS = 24


def _dsa_kernel(q_ref, kt_ref, v_ref, iq_ref, ikt_ref, iw_ref, o_ref,
                s_scr, t_scr, m_scr, acc_scr, *, tq, topk, idx_scale):
    H, hd = GROUP_HEADS, HEAD_DIM
    qi = pl.program_id(1)
    q0 = qi * tq
    n_kc = qi + 1
    T = s_scr.shape[1]
    reps = tq // LANES

    lane = _iota((tq, LANES), 1)
    quarter = lane >> _log2(DSA_IDX_DIM)
    per_group = LANES // DSA_IDX_DIM
    iq = iq_ref[0]
    iqh = [jnp.where(quarter == (h % per_group), iq[:, (h // per_group) * LANES:(h // per_group + 1) * LANES],
                     jnp.zeros((), iq.dtype)) for h in range(DSA_IDX_HEADS)]
    iw = iw_ref[0]
    iwb = [jnp.broadcast_to(iw[:, h:h + 1], (tq, LANES)) for h in range(DSA_IDX_HEADS)]

    def score_body(c, ends):
        top, bot = ends
        k0 = pl.multiple_of(c * tq, tq)
        ikt = ikt_ref[0, :, pl.ds(k0, tq)]
        acc = jnp.zeros((tq, tq), jnp.float32)
        for h in range(DSA_IDX_HEADS):
            sh = jnp.dot(iqh[h], ikt, preferred_element_type=jnp.float32)
            acc = acc + _lanes(iwb[h], tq) * jnp.maximum(sh, 0.0)
        qpos = q0 + _iota((tq, tq), 0)
        kpos = k0 + _iota((tq, tq), 1)
        val = acc * idx_scale + 0.0
        sc = jnp.where(kpos <= qpos, val, NEG_INF)
        s_scr[:, pl.ds(k0, tq)] = sc
        low = jnp.where(kpos <= qpos, val, np.inf)
        for r in range(reps):
            top = jnp.maximum(top, sc[:, r * LANES:(r + 1) * LANES])
            bot = jnp.minimum(bot, low[:, r * LANES:(r + 1) * LANES])
        return top, bot

    top, bot = lax.fori_loop(0, n_kc, score_body, (jnp.full((tq, LANES), NEG_INF, jnp.float32),
                                                   jnp.full((tq, LANES), np.inf, jnp.float32)))
    groups = tq // LANES

    def spread(c):
        return jnp.concatenate([jnp.broadcast_to(c[g:g + 1, :], (LANES, LANES)).T for g in range(groups)], axis=0)

    def gather(x):
        return jnp.concatenate([x[g * LANES:(g + 1) * LANES].T[0:1] for g in range(groups)], axis=0)

    def totals(part):
        ones = jnp.ones((8, LANES), jnp.float32)
        return jnp.concatenate([_dot_nt(ones, part[g * LANES:(g + 1) * LANES])[0:1] for g in range(groups)], axis=0)

    row_max = gather(jnp.broadcast_to(jnp.max(top, axis=-1, keepdims=True), (tq, LANES)))
    row_min = gather(jnp.broadcast_to(jnp.min(bot, axis=-1, keepdims=True), (tq, LANES)))

    def count_ge(t, strict=False):
        above = (lambda a, b: a > b) if strict else (lambda a, b: a >= b)
        t_scr[...] = spread(t)
        parts = []
        for rb in range(tq // COUNT_ROWS):
            rows = slice(rb * COUNT_ROWS, (rb + 1) * COUNT_ROWS)
            t_rb = t_scr[rows, :]

            def body(c, part):
                k0 = pl.multiple_of(c * tq, tq)
                sc = s_scr[rows, pl.ds(k0, tq)]
                for r in range(reps):
                    part = part + jnp.where(above(sc[:, r * LANES:(r + 1) * LANES], t_rb), 1.0, 0.0)
                return part

            parts.append(lax.fori_loop(0, n_kc, body, jnp.zeros((COUNT_ROWS, LANES), jnp.float32)))
        return totals(jnp.concatenate(parts, axis=0))

    kf = float(topk)
    floor_key = _sortable(NEG_INF)

    def bis_cond(c):
        _, lo, hi, _ = c
        return _any(lo < hi)

    def bis_body(c):
        it, lo, hi, n_lo = c
        key_mid = (lo | hi) - ((lo ^ hi) >> 1)
        val_mid = _to_sortable(0.5 * (_from_sortable(lo) + _from_sortable(hi)))
        steps = jnp.zeros(lo.shape, jnp.int32) + it
        use_val = (val_mid > lo) & (val_mid <= hi) & (lo > floor_key) & (steps < VALUE_STEPS)
        mid = jnp.where(use_val, val_mid, key_mid)
        cnt = count_ge(_from_sortable(mid))
        ge = cnt >= kf
        lo = jnp.where(ge, mid, lo)
        n_lo = jnp.where(ge, cnt, n_lo)
        hi = jnp.where(cnt == kf, mid, jnp.where(ge, hi, mid - 1))
        return it + 1, lo, hi, n_lo

    zeros = jnp.zeros((groups, LANES), jnp.float32)
    n_pos = count_ge(zeros, strict=True)
    n_nonneg = count_ge(zeros)
    n_causal = q0 + _iota((groups, LANES), 0) * LANES + _iota((groups, LANES), 1) + 1
    lo_neg = jnp.where(n_causal >= topk, _to_sortable(row_min), floor_key)
    n_neg = count_ge(_from_sortable(lo_neg))
    is_pos = n_pos >= kf
    is_zero = n_nonneg >= kf
    pick = lambda p, z, n: jnp.where(is_pos, p, jnp.where(is_zero, z, n))
    lo0 = pick(_sortable(np.float32(1e-45)), _sortable(0.0), lo_neg)
    hi0 = pick(_to_sortable(row_max), _sortable(0.0), _sortable(-0.0) - 1)
    _, lo, _, n_ge = lax.while_loop(bis_cond, bis_body, (jnp.int32(0), lo0, hi0, pick(n_pos, n_nonneg, n_neg)))
    thr = _lanes(spread(_from_sortable(lo)), tq)

    any_tie = _any(n_ge > kf)

    def causal_at(k0):
        return (k0 + _iota((tq, tq), 1)) <= (q0 + _iota((tq, tq), 0))

    @pl.when(jnp.logical_not(any_tie))
    def _():
        def body(c, carry):
            k0 = pl.multiple_of(c * tq, tq)
            sc = s_scr[:, pl.ds(k0, tq)]
            s_scr[:, pl.ds(k0, tq)] = jnp.where((sc >= thr) & causal_at(k0), 0.0, NEG_INF)
            return carry
        lax.fori_loop(0, n_kc, body, 0)

    @pl.when(any_tie)
    def _():
        need = _lanes(spread(kf - count_ge(_from_sortable(lo), strict=True)), tq)
        prefix = jnp.where(_iota((tq, tq), 0) <= _iota((tq, tq), 1), 1.0, 0.0)
        ones = jnp.ones((tq, LANES), jnp.float32)

        def body(c, seen):
            k0 = pl.multiple_of(c * tq, tq)
            sc = s_scr[:, pl.ds(k0, tq)]
            eq = jnp.where(sc == thr, 1.0, 0.0)
            rank_eq = _dot(eq, prefix) + _lanes(seen, tq)
            picked = (sc > thr) | ((sc == thr) & (rank_eq <= need))
            s_scr[:, pl.ds(k0, tq)] = jnp.where(picked & causal_at(k0), 0.0, NEG_INF)
            return seen + _dot(eq, ones)
        lax.fori_loop(0, n_kc, body, jnp.zeros((tq, LANES), jnp.float32))

    qs = [q_ref[0, h] for h in range(H)]
    _flash_init(m_scr, acc_scr)

    def att_body(c, carry):
        k0 = pl.multiple_of(c * tq, tq)
        bias = s_scr[:, pl.ds(k0, tq)]
        for h in range(H):
            s = jnp.dot(qs[h], kt_ref[0, :, pl.ds(k0, tq)], preferred_element_type=jnp.float32) + bias
            _flash_update(h, s, v_ref[0, pl.ds(k0, tq), :], m_scr, acc_scr)
        return carry

    lax.fori_loop(0, n_kc, att_body, 0)
    for h in range(H):
        o_ref[0, :, h * hd:(h + 1) * hd] = _flash_out(h, acc_scr)


def _dsa_call(q, kt, v, iq, ikt, iw):
    B, H, T, _ = q.shape
    tq = ATT_TILE
    topk = min(DSA_TOPK, T // 4)
    assert tq >= topk
    qs, ks, vs, out = _att_specs(B, H, T, tq, True)
    return pl.pallas_call(
        functools.partial(_dsa_kernel, tq=tq, topk=topk, idx_scale=(DSA_IDX_HEADS * DSA_IDX_DIM) ** -0.5),
        grid=(B, T // tq),
        in_specs=[qs, ks, vs, pl.BlockSpec((1, tq, 2 * LANES), lambda b, i: (b, i, 0)), ks,
                  pl.BlockSpec((1, tq, LANES), lambda b, i: (b, i, 0))],
        out_specs=out,
        out_shape=jax.ShapeDtypeStruct((B, T, H * HEAD_DIM), jnp.float32),
        scratch_shapes=[pltpu.VMEM((tq, T), jnp.float32), pltpu.VMEM((tq, LANES), jnp.float32)] + _att_scratch(tq),
        compiler_params=_params("arbitrary", "arbitrary"),
        name="dsa_attention",
    )(q, kt, v, iq, ikt, iw)


def _out_kernel(x_ref, ada_ref, o1_ref, o2_ref, o3_ref, o4_ref, gn_ref, w_ref, y_ref):
    a = ada_ref[0]
    gn = gn_ref[...]
    y = jnp.concatenate([_rms(o[0], gn[i:i + 1]).astype(MXU_DTYPE)
                         for i, o in enumerate((o1_ref, o2_ref, o3_ref, o4_ref))], axis=-1)
    y_ref[...] = x_ref[...] + a[5:6] * jnp.dot(y, w_ref[...], preferred_element_type=jnp.float32)


def _out_call(x2d, ada_l, groups, group_norm, w_out, B, T):
    N, D = x2d.shape
    tm = ROW_TILE
    tpb = T // tm
    grp = pl.BlockSpec((1, tm, GROUP_WIDTH), lambda i: (i // tpb, i % tpb, 0))
    return pl.pallas_call(
        _out_kernel,
        grid=(N // tm,),
        in_specs=[pl.BlockSpec((tm, D), lambda i: (i, 0)),
                  pl.BlockSpec((1, N_ADA, D), lambda i: (i // tpb, 0, 0)),
                  grp, grp, grp, grp,
                  pl.BlockSpec((N_GROUPS, GROUP_WIDTH), lambda i: (0, 0)),
                  pl.BlockSpec((MIX_WIDTH, D), lambda i: (0, 0))],
        out_specs=pl.BlockSpec((tm, D), lambda i: (i, 0)),
        out_shape=jax.ShapeDtypeStruct((N, D), jnp.float32),
        compiler_params=_params("arbitrary"),
        name="mixer_out_proj",
    )(x2d, ada_l, *groups, group_norm, w_out.astype(MXU_DTYPE))


def _mixer_groups(x2d, ada_l, tables, mix_norm, w_in, mla_q_norm, mla_w_uq, mla_kv_norm, mla_w_uk, mla_w_uv,
                  nsa_pe_k, nsa_pe_v, nsa_cmp_k_w1, nsa_cmp_k_w2, nsa_cmp_v_w1, nsa_cmp_v_w2, B, T):
    (mq, mkt, mv, lq, lkt, lv, nq, nkc, nvc, nkst, nvs, nkwt, nvw, ngate,
     dq, dkt, dv, diq, dikt, diw) = _proj_call(
        x2d, ada_l, mix_norm, w_in, tables, mla_q_norm, mla_w_uq, mla_kv_norm, mla_w_uk, mla_w_uv, B, T)
    o_moba = _moba_call(mq, mkt, mv)
    o_mla = _mla_call(lq, lkt, lv)
    kcmp, vcmp = _cmp_call(nkc, nvc, nsa_pe_k, nsa_pe_v, nsa_cmp_k_w1, nsa_cmp_k_w2, nsa_cmp_v_w1, nsa_cmp_v_w2)
    o_nsa = _nsa_call(nq, kcmp, vcmp, nkst, nvs, nkwt, nvw, ngate)
    o_dsa = _dsa_call(dq, dkt, dv, diq, dikt, diw)
    return o_moba, o_mla, o_nsa, o_dsa


def kernel(x, c, ada_w, ada_b, ffn1_norm, ffn1_w_gate, ffn1_w_up, ffn1_w_down, mix_norm, w_in, mla_q_norm, mla_w_uq, mla_kv_norm, mla_w_uk, mla_w_uv, nsa_pe_k, nsa_pe_v, nsa_cmp_k_w1, nsa_cmp_k_w2, nsa_cmp_v_w1, nsa_cmp_v_w2, group_norm, w_out, ffn2_norm, ffn2_w_gate, ffn2_w_up, ffn2_w_down, final_norm):
    B, T, D = x.shape
    L = ada_w.shape[0]
    assert D == D_MODEL and T % ROW_TILE == 0 and T % ATT_TILE == 0
    assert ATT_TILE % MOBA_BLOCK == 0 and ATT_TILE % NSA_SEL_BLOCK == 0 and ATT_TILE >= NSA_WINDOW
    tpb = T // ROW_TILE
    ada = _ada_call(c, ada_w, ada_b)
    tables = _rope_tables(T)
    x2d = x.reshape(B * T, D)
    for l in range(L):
        x2d = _ffn_call(x2d, ada[l], ffn1_norm[l], ffn1_w_gate[l], ffn1_w_up[l], ffn1_w_down[l], 0, tpb)
        groups = _mixer_groups(x2d, ada[l], tables, mix_norm[l], w_in[l], mla_q_norm[l], mla_w_uq[l],
                               mla_kv_norm[l], mla_w_uk[l], mla_w_uv[l], nsa_pe_k[l], nsa_pe_v[l],
                               nsa_cmp_k_w1[l], nsa_cmp_k_w2[l], nsa_cmp_v_w1[l], nsa_cmp_v_w2[l], B, T)
        x2d = _out_call(x2d, ada[l], groups, group_norm[l], w_out[l], B, T)
        x2d = _ffn_call(x2d, ada[l], ffn2_norm[l], ffn2_w_gate[l], ffn2_w_up[l], ffn2_w_down[l], 6, tpb,
                        final_gain=final_norm if l == L - 1 else None)
    return x2d.reshape(B, T, D)
```

```python
import functools
import math

import numpy as np
import jax
import jax.numpy as jnp
from jax import lax
from jax.experimental import pallas as pl
from jax.experimental.pallas import tpu as pltpu

D_MODEL = 1024
N_GROUPS = 4
HEAD_DIM = 64
GROUP_HEADS = D_MODEL // (N_GROUPS * HEAD_DIM)
GROUP_WIDTH = GROUP_HEADS * HEAD_DIM
MIX_WIDTH = N_GROUPS * GROUP_WIDTH
D_FF = 256 * ((8 * D_MODEL + 3 * 256 - 1) // (3 * 256))
N_ADA = 9
FFN_RESIDUAL_WEIGHT = 0.5
ROPE_THETA = 10000.0
RMS_EPS = 1e-6
NEG_INF = -1e30

MOBA_BLOCK = 256
MOBA_TOPK = 3

MLA_Q_LORA = D_MODEL // 4
MLA_KV_LORA = D_MODEL // 8
MLA_NOPE = HEAD_DIM
MLA_ROPE = HEAD_DIM // 2
MLA_V = HEAD_DIM
MLA_QK = MLA_NOPE + MLA_ROPE

NSA_CMP_LEN = 32
NSA_CMP_STRIDE = 16
NSA_CMP_HIDDEN = 4 * HEAD_DIM
NSA_SEL_BLOCK = 64
NSA_SEL_TOPN = 16
NSA_WINDOW = 512
NSA_FORCE_SCORE = 1e4

DSA_TOPK = 256
DSA_IDX_HEADS = 8
DSA_IDX_DIM = 32

IN_NAMES = ("mq", "mk", "mv", "cq", "ckv", "kr", "nq", "nkc", "nvc", "nks", "nvs", "nkw", "nvw",
            "ngate", "dq", "dk", "dv", "diq", "dik", "diw")
IN_SIZES = (
    GROUP_WIDTH, GROUP_WIDTH, GROUP_WIDTH,
    MLA_Q_LORA, MLA_KV_LORA, MLA_ROPE,
    GROUP_WIDTH, HEAD_DIM, HEAD_DIM, HEAD_DIM, HEAD_DIM,
    HEAD_DIM, HEAD_DIM, 3 * GROUP_HEADS,
    GROUP_WIDTH, HEAD_DIM, HEAD_DIM,
    DSA_IDX_HEADS * DSA_IDX_DIM, DSA_IDX_DIM, DSA_IDX_HEADS,
)
N_IN = sum(IN_SIZES)

LANES = 128
MXU_DTYPE = jnp.bfloat16
VMEM_LIMIT = 56 * 1024 * 1024

ATT_TILE = 512
ROW_TILE = 512
MXU_TILE = 256
FF_CHUNK = 6 * MXU_TILE

LOG2E = math.log2(math.e)
M_INIT = -1e29
DEN_LANE = HEAD_DIM


def _params(*semantics):
    return pltpu.CompilerParams(dimension_semantics=semantics, vmem_limit_bytes=VMEM_LIMIT)


def _dot(a, b):
    return jnp.dot(a.astype(MXU_DTYPE), b.astype(MXU_DTYPE), preferred_element_type=jnp.float32)


def _dot_nt(a, b):
    return lax.dot_general(a.astype(MXU_DTYPE), b.astype(MXU_DTYPE), (((1,), (1,)), ((), ())),
                           preferred_element_type=jnp.float32)


def _rms(x, g):
    return x * lax.rsqrt(jnp.mean(x * x, axis=-1, keepdims=True) + RMS_EPS) * g


def _silu(x):
    return x * (1.0 / (1.0 + jnp.exp(-x)))


def _iota(shape, dim):
    return lax.broadcasted_iota(jnp.int32, shape, dim)


def _log2(n):
    assert n & (n - 1) == 0
    return n.bit_length() - 1


def _any(pred):
    return jnp.max(jnp.where(pred, 1.0, 0.0)) > 0.5


def _lanes(x, width):
    return x if width == LANES else jnp.tile(x, (1, width // LANES))


def _ada_kernel(c_ref, w_ref, b_ref, o_ref):
    o_ref[0] = _dot(_silu(c_ref[...]), w_ref[0]) + b_ref[0]


def _ada_call(c, ada_w, ada_b):
    L, D, _ = ada_w.shape
    B = c.shape[0]
    out = pl.pallas_call(
        _ada_kernel,
        grid=(L, N_ADA),
        in_specs=[
            pl.BlockSpec((B, D), lambda l, k: (0, 0)),
            pl.BlockSpec((1, D, D), lambda l, k: (l, 0, k)),
            pl.BlockSpec((1, 1, D), lambda l, k: (l, 0, k)),
        ],
        out_specs=pl.BlockSpec((1, B, D), lambda l, k: (l, 0, k)),
        out_shape=jax.ShapeDtypeStruct((L, B, N_ADA * D), jnp.float32),
        compiler_params=_params("arbitrary", "arbitrary"),
        name="ada_proj",
    )(c, ada_w, ada_b.reshape(L, 1, N_ADA * D))
    return out.reshape(L, B, N_ADA, D)


def _ffn_chunks(F):
    bounds = list(range(0, F, FF_CHUNK)) + [F]
    return list(zip(bounds[:-1], bounds[1:]))


def _ffn_kernel(x_ref, ada_ref, gn_ref, wg_ref, wu_ref, wd_ref, *rest, k0, final):
    if final:
        fg_ref, o_ref = rest
    else:
        (o_ref,) = rest
    a = ada_ref[0]
    h = (_rms(x_ref[...], gn_ref[...]) * (1.0 + a[k0 + 1:k0 + 2]) + a[k0:k0 + 1]).astype(MXU_DTYPE)
    acc = None
    for lo, hi in _ffn_chunks(wg_ref.shape[1]):
        g = jnp.dot(h, wg_ref[:, lo:hi], preferred_element_type=jnp.float32)
        u = jnp.dot(h, wu_ref[:, lo:hi], preferred_element_type=jnp.float32)
        part = jnp.dot((_silu(g) * u).astype(MXU_DTYPE), wd_ref[lo:hi, :], preferred_element_type=jnp.float32)
        acc = part if acc is None else acc + part
    y = x_ref[...] + (FFN_RESIDUAL_WEIGHT * a[k0 + 2:k0 + 3]) * acc
    if final:
        y = _rms(y, fg_ref[...])
    o_ref[...] = y


def _ffn_call(x2d, ada_l, norm_g, w_gate, w_up, w_down, k0, tiles_per_batch, final_gain=None):
    N, D = x2d.shape
    F = w_gate.shape[1]
    tm = ROW_TILE
    in_specs = [
        pl.BlockSpec((tm, D), lambda i: (i, 0)),
        pl.BlockSpec((1, N_ADA, D), lambda i: (i // tiles_per_batch, 0, 0)),
        pl.BlockSpec((1, D), lambda i: (0, 0)),
        pl.BlockSpec((D, F), lambda i: (0, 0)),
        pl.BlockSpec((D, F), lambda i: (0, 0)),
        pl.BlockSpec((F, D), lambda i: (0, 0)),
    ]
    args = [x2d, ada_l, norm_g.reshape(1, D), w_gate.astype(MXU_DTYPE), w_up.astype(MXU_DTYPE),
            w_down.astype(MXU_DTYPE)]
    final = final_gain is not None
    if final:
        in_specs.append(pl.BlockSpec((1, D), lambda i: (0, 0)))
        args.append(final_gain.reshape(1, D))
    return pl.pallas_call(
        functools.partial(_ffn_kernel, k0=k0, final=final),
        grid=(N // tm,),
        in_specs=in_specs,
        out_specs=pl.BlockSpec((tm, D), lambda i: (i, 0)),
        out_shape=jax.ShapeDtypeStruct((N, D), jnp.float32),
        compiler_params=_params("arbitrary"),
        name="ffn",
    )(*args)


G_QMAIN, G_QSWAP, G_KC, G_V, G_MISC, ROW_GROUPS = 0, 12, 24, 26, 33, 41
T_MK, T_MKS, T_SW, T_SWS, T_DK, T_DKS, T_IK, T_ROWS = 0, 256, 512, 640, 768, 896, 1024, 1152


def _swap_halves(c, width):
    return c.reshape(-1, 2, width // 2)[:, ::-1, :].reshape(-1)


def _proj_indices():
    off = dict(zip(IN_NAMES, np.cumsum((0,) + IN_SIZES[:-1]).tolist()))
    size = dict(zip(IN_NAMES, IN_SIZES))
    cols = lambda name: np.arange(off[name], off[name] + size[name])
    zero = lambda n: np.full((n,), N_IN)
    hd = HEAD_DIM

    def head_groups(c):
        return np.concatenate([np.concatenate([c[i:i + hd], zero(LANES - hd)]) for i in range(0, c.size, hd)])

    q = np.concatenate([cols("mq"), cols("nq"), cols("dq")])
    row = np.concatenate([
        head_groups(q), head_groups(_swap_halves(q, hd)),
        head_groups(cols("nkc")), head_groups(_swap_halves(cols("nkc"), hd)),
        head_groups(np.concatenate([cols("mv"), cols("nvs"), cols("nvw"), cols("dv")])),
        cols("cq"), cols("ckv"), head_groups(cols("nvc")), cols("diq"),
        cols("ngate"), zero(LANES - size["ngate"]), cols("diw"), zero(LANES - size["diw"])])
    assert row.size == ROW_GROUPS * LANES
    sw = np.concatenate([cols("nks"), cols("nkw")])
    dk_main = np.concatenate([cols("dk"), cols("kr"), zero(LANES - hd - MLA_ROPE)])
    dk_swap = np.concatenate([_swap_halves(cols("dk"), hd), _swap_halves(cols("kr"), MLA_ROPE),
                              zero(LANES - hd - MLA_ROPE)])
    tr = np.concatenate([cols("mk"), _swap_halves(cols("mk"), hd), sw, _swap_halves(sw, hd),
                         dk_main, dk_swap, np.tile(cols("dik"), LANES // DSA_IDX_DIM)])
    assert tr.size == T_ROWS
    return row, tr


def _proj_kernel(x_ref, ada_ref, gn_ref, wr_ref, wt_ref, rtab_ref, ttab_ref,
                 qn_ref, wuq_ref, kvn_ref, wukt_ref, wuv_ref,
                 mq_ref, mkt_ref, mv_ref, lq_ref, lkt_ref, lv_ref,
                 nq_ref, nkc_ref, nvc_ref, nkst_ref, nvs_ref, nkwt_ref, nvw_ref, ng_ref,
                 dq_ref, dkt_ref, dv_ref, diq_ref, dikt_ref, diw_ref, *, tm, tpb, n_moba, n_sel):
    H, hd, G = GROUP_HEADS, HEAD_DIM, LANES
    t0 = (pl.program_id(0) % tpb) * tm
    a = ada_ref[0]
    h = (_rms(x_ref[...], gn_ref[...]) * (1.0 + a[4:5]) + a[3:4]).astype(MXU_DTYPE)

    def rows(g0, n):
        return jnp.dot(h, wr_ref[:, g0 * G:(g0 + n) * G], preferred_element_type=jnp.float32)

    def cols(r0, n):
        return _dot_nt(wt_ref[r0:r0 + n, :], h)

    roped_q = (rows(G_QMAIN, 3 * H) * _lanes(rtab_ref[0], 3 * H * G)
               + rows(G_QSWAP, 3 * H) * _lanes(rtab_ref[1], 3 * H * G))
    for hh in range(H):
        mq_ref[0, hh] = roped_q[:, hh * G:(hh + 1) * G].astype(mq_ref.dtype)
        nq_ref[0, hh] = roped_q[:, (H + hh) * G:(H + hh + 1) * G].astype(nq_ref.dtype)
        dq_ref[0, hh] = roped_q[:, (2 * H + hh) * G:(2 * H + hh + 1) * G].astype(dq_ref.dtype)
    kc = rows(G_KC, 2)
    nkc_ref[0] = (kc[:, :G] * rtab_ref[2] + kc[:, G:] * rtab_ref[3])[:, :hd].astype(nkc_ref.dtype)

    ones_hi = jnp.where(_iota((1, G), 1) >= DEN_LANE, 1.0, 0.0)
    v = rows(G_V, H + 3) + _lanes(ones_hi, (H + 3) * G)
    for hh in range(H):
        mv_ref[0, hh] = v[:, hh * G:(hh + 1) * G].astype(mv_ref.dtype)
    nvs_ref[0] = v[:, H * G:(H + 1) * G].astype(nvs_ref.dtype)
    nvw_ref[0] = v[:, (H + 1) * G:(H + 2) * G].astype(nvw_ref.dtype)
    dv_ref[0] = v[:, (H + 2) * G:(H + 3) * G].astype(dv_ref.dtype)

    misc = rows(G_MISC, 8)
    cq = misc[:, :MLA_Q_LORA]
    ckv = misc[:, MLA_Q_LORA:MLA_Q_LORA + MLA_KV_LORA]
    nvc_ref[0] = misc[:, 3 * G:3 * G + hd].astype(nvc_ref.dtype)
    diq_ref[0] = misc[:, 4 * G:6 * G].astype(diq_ref.dtype)
    ng_ref[0] = misc[:, 6 * G:7 * G]
    diw_ref[0] = misc[:, 7 * G:8 * G]

    tok = t0 + _iota((hd, tm), 1)
    rid = _iota((hd, tm), 0)
    oh_moba = jnp.where((rid & (n_moba - 1)) == (tok >> _log2(MOBA_BLOCK)), 1.0, 0.0)
    oh_sel = jnp.where((rid == (tok >> _log2(NSA_SEL_BLOCK))) & (rid < n_sel), 1.0, 0.0)
    zeros_lo = jnp.zeros((hd, tm), jnp.float32)
    ta_c, ta_s, tb_c, tb_s = ttab_ref[0], ttab_ref[1], ttab_ref[2], ttab_ref[3]
    mkt = (cols(T_MK, H * hd) * jnp.tile(ta_c, (H * hd // G, 1))
           + cols(T_MKS, H * hd) * jnp.tile(ta_s, (H * hd // G, 1)))
    for hh in range(H):
        mine = (rid >> _log2(n_moba)) == hh
        mkt_ref[0, hh] = jnp.concatenate([mkt[hh * hd:(hh + 1) * hd], jnp.where(mine, oh_moba, 0.0)],
                                         axis=0).astype(mkt_ref.dtype)
    sw = cols(T_SW, G) * ta_c + cols(T_SWS, G) * ta_s
    nkst_ref[0] = jnp.concatenate([sw[:hd], oh_sel], axis=0).astype(nkst_ref.dtype)
    nkwt_ref[0] = jnp.concatenate([sw[hd:], zeros_lo], axis=0).astype(nkwt_ref.dtype)
    dkr = cols(T_DK, G) * tb_c + cols(T_DKS, G) * tb_s
    rid2 = _iota((G, tm), 0)
    dkt_ref[0] = jnp.where(rid2 < hd, dkr, 0.0).astype(dkt_ref.dtype)
    kpe_rows = jnp.where((rid2 >= MLA_NOPE) & (rid2 < MLA_QK), dkr, 0.0)
    dikt_ref[0] = cols(T_IK, G).astype(dikt_ref.dtype)

    cqn = _rms(cq, qn_ref[...]).astype(MXU_DTYPE)
    lq = (jnp.dot(cqn, wuq_ref[:, :H * G], preferred_element_type=jnp.float32) * _lanes(rtab_ref[4], H * G)
          + jnp.dot(cqn, wuq_ref[:, H * G:], preferred_element_type=jnp.float32) * _lanes(rtab_ref[5], H * G))
    ckvn = _rms(ckv, kvn_ref[...]).astype(MXU_DTYPE)
    knt = _dot_nt(wukt_ref[...], ckvn)
    lv = jnp.dot(ckvn, wuv_ref[...], preferred_element_type=jnp.float32) + _lanes(ones_hi, H * G)
    for hh in range(H):
        lq_ref[0, hh] = lq[:, hh * G:(hh + 1) * G].astype(lq_ref.dtype)
        lkt_ref[0, hh] = (knt[hh * G:(hh + 1) * G] + kpe_rows).astype(lkt_ref.dtype)
        lv_ref[0, hh] = lv[:, hh * G:(hh + 1) * G].astype(lv_ref.dtype)


def _rope_tables(T):
    def cs(dim):
        inv_freq = 1.0 / (ROPE_THETA ** (np.arange(0, dim, 2, dtype=np.float32) / dim))
        ang = jnp.arange(T, dtype=jnp.float32)[:, None] * jnp.asarray(inv_freq, jnp.float32)[None, :]
        cos, sin = jnp.cos(ang), jnp.sin(ang)
        return jnp.concatenate([cos, cos], axis=-1), jnp.concatenate([-sin, sin], axis=-1)

    c64, s64 = cs(HEAD_DIM)
    c32, s32 = cs(MLA_ROPE)
    pad = lambda t, n: jnp.concatenate([t, jnp.zeros((T, n), jnp.float32)], axis=-1)
    sc = HEAD_DIM ** -0.5 * LOG2E
    sl = MLA_QK ** -0.5 * LOG2E
    ones = jnp.ones((T, MLA_NOPE), jnp.float32)
    rest = LANES - MLA_QK
    rtab = jnp.stack([
        pad(c64 * sc, LANES - HEAD_DIM), pad(s64 * sc, LANES - HEAD_DIM),
        pad(c64, LANES - HEAD_DIM), pad(s64, LANES - HEAD_DIM),
        pad(jnp.concatenate([ones, c32], axis=-1) * sl, rest),
        pad(jnp.concatenate([0.0 * ones, s32], axis=-1) * sl, rest)])
    ttab = jnp.stack([
        jnp.concatenate([c64, c64], axis=-1).T, jnp.concatenate([s64, s64], axis=-1).T,
        pad(jnp.concatenate([c64, c32], axis=-1), rest).T, pad(jnp.concatenate([s64, s32], axis=-1), rest).T])
    return rtab, ttab


def _proj_call(x2d, ada_l, norm_g, w_in, tables, mla_q_norm, mla_w_uq, mla_kv_norm, mla_w_uk, mla_w_uv, B, T):
    N, D = x2d.shape
    H, G = GROUP_HEADS, LANES
    tm = ROW_TILE
    tpb = T // tm
    n_moba, n_sel = T // MOBA_BLOCK, T // NSA_SEL_BLOCK
    assert H * n_moba <= LANES - HEAD_DIM and n_sel <= LANES - HEAD_DIM and n_moba & (n_moba - 1) == 0
    zcol = lambda w: jnp.concatenate([w, jnp.zeros((w.shape[0], 1), w.dtype)], axis=1)
    row_idx, tr_idx = _proj_indices()
    w_ext = zcol(w_in)
    w_row = w_ext[:, row_idx].astype(MXU_DTYPE)
    w_tr = w_ext[:, tr_idx].T.astype(MXU_DTYPE)
    zq = mla_w_uq.shape[1]
    per_head = np.arange(H * MLA_QK).reshape(H, MLA_QK)
    main = np.concatenate([np.concatenate([per_head[i], np.full((G - MLA_QK,), zq)]) for i in range(H)])
    part = np.concatenate([np.concatenate([np.full((MLA_NOPE,), zq), _swap_halves(per_head[i, MLA_NOPE:], MLA_ROPE),
                                           np.full((G - MLA_QK,), zq)]) for i in range(H)])
    wuq = zcol(mla_w_uq)[:, np.concatenate([main, part])].astype(MXU_DTYPE)
    zv = mla_w_uk.shape[1]
    grp = np.concatenate([np.concatenate([np.arange(i * HEAD_DIM, (i + 1) * HEAD_DIM), np.full((G - HEAD_DIM,), zv)])
                          for i in range(H)])
    wukt = zcol(mla_w_uk)[:, grp].T.astype(MXU_DTYPE)
    wuv = zcol(mla_w_uv)[:, grp].astype(MXU_DTYPE)
    rtab, ttab = tables

    row = lambda i: (i, 0)
    const2 = lambda i: (0, 0)
    in_specs = [
        pl.BlockSpec((tm, D), row),
        pl.BlockSpec((1, N_ADA, D), lambda i: (i // tpb, 0, 0)),
        pl.BlockSpec((1, D), const2),
        pl.BlockSpec(w_row.shape, const2),
        pl.BlockSpec(w_tr.shape, const2),
        pl.BlockSpec((6, tm, G), lambda i: (0, i % tpb, 0)),
        pl.BlockSpec((4, G, tm), lambda i: (0, 0, i % tpb)),
        pl.BlockSpec((1, MLA_Q_LORA), const2),
        pl.BlockSpec(wuq.shape, const2),
        pl.BlockSpec((1, MLA_KV_LORA), const2),
        pl.BlockSpec(wukt.shape, const2),
        pl.BlockSpec(wuv.shape, const2),
    ]
    dt = MXU_DTYPE
    hq = (jax.ShapeDtypeStruct((B, H, T, G), dt), pl.BlockSpec((1, H, tm, G), lambda i: (i // tpb, 0, i % tpb, 0)))
    hkt = (jax.ShapeDtypeStruct((B, H, G, T), dt), pl.BlockSpec((1, H, G, tm), lambda i: (i // tpb, 0, 0, i % tpb)))
    srow = lambda d, t=dt: (jax.ShapeDtypeStruct((B, T, d), t), pl.BlockSpec((1, tm, d), lambda i: (i // tpb, i % tpb, 0)))
    skt = (jax.ShapeDtypeStruct((B, G, T), dt), pl.BlockSpec((1, G, tm), lambda i: (i // tpb, 0, i % tpb)))
    outs = [hq, hkt, hq,
            hq, hkt, hq,
            hq, srow(HEAD_DIM), srow(HEAD_DIM), skt, srow(G), skt, srow(G), srow(G, jnp.float32),
            hq, skt, srow(G), srow(2 * G), skt, srow(G, jnp.float32)]
    return pl.pallas_call(
        functools.partial(_proj_kernel, tm=tm, tpb=tpb, n_moba=n_moba, n_sel=n_sel),
        grid=(N // tm,),
        in_specs=in_specs,
        out_specs=[o[1] for o in outs],
        out_shape=[o[0] for o in outs],
        compiler_params=_params("arbitrary"),
        name="mixer_in_proj",
    )(x2d, ada_l, norm_g.reshape(1, D), w_row, w_tr, rtab, ttab,
      mla_q_norm.reshape(1, -1), wuq, mla_kv_norm.reshape(1, -1), wukt, wuv)


def _flash_init(m_scr, acc_scr):
    m_scr[...] = jnp.full(m_scr.shape, M_INIT, jnp.float32)
    acc_scr[...] = jnp.zeros_like(acc_scr)


def _flash_update(h, s, v, m_scr, acc_scr):
    m_prev = m_scr[h]
    m_new = jnp.maximum(m_prev, jnp.max(s, axis=-1, keepdims=True))
    p = jnp.exp2(s - _lanes(m_new, s.shape[1]))
    acc_scr[h] = jnp.exp2(m_prev - m_new) * acc_scr[h] + _dot(p, v)
    m_scr[h] = m_new


def _flash_out(h, acc_scr):
    acc = acc_scr[h]
    den = acc[:, DEN_LANE:DEN_LANE + 1]
    return acc[:, :HEAD_DIM] / jnp.where(den > 0.0, den, 1.0)


def _causal_bias(t):
    return jnp.where(_iota((t, t), 1) <= _iota((t, t), 0), 0.0, NEG_INF)


def _rank_desc(x):
    n = x.shape[0]
    row = _iota(x.shape, 0)
    rank = jnp.zeros(x.shape, jnp.float32)
    for j in range(n):
        cand = x[j:j + 1, :]
        rank = rank + jnp.where(cand > x, 1.0, jnp.where((cand == x) & (row > j), 1.0, 0.0))
    return rank


def _att_scratch(tq):
    return [pltpu.VMEM((GROUP_HEADS, tq, LANES), jnp.float32), pltpu.VMEM((GROUP_HEADS, tq, LANES), jnp.float32)]


def _att_specs(B, H, T, tq, shared_kv):
    q = pl.BlockSpec((1, H, tq, LANES), lambda b, i: (b, 0, i, 0))
    if shared_kv:
        kt = pl.BlockSpec((1, LANES, T), lambda b, i: (b, 0, 0))
        v = pl.BlockSpec((1, T, LANES), lambda b, i: (b, 0, 0))
    else:
        kt = pl.BlockSpec((1, H, LANES, T), lambda b, i: (b, 0, 0, 0))
        v = pl.BlockSpec((1, H, T, LANES), lambda b, i: (b, 0, 0, 0))
    out = pl.BlockSpec((1, tq, H * HEAD_DIM), lambda b, i: (b, i, 0))
    return q, kt, v, out


def _mla_kernel(q_ref, kt_ref, v_ref, o_ref, m_scr, acc_scr, *, tq):
    H = GROUP_HEADS
    qi = pl.program_id(1)
    qs = [q_ref[0, h] for h in range(H)]
    _flash_init(m_scr, acc_scr)

    def chunk(k0, bias):
        for h in range(H):
            s = jnp.dot(qs[h], kt_ref[0, h, :, pl.ds(k0, tq)], preferred_element_type=jnp.float32)
            if bias is not None:
                s = s + bias
            _flash_update(h, s, v_ref[0, h, pl.ds(k0, tq), :], m_scr, acc_scr)

    def body(c, carry):
        chunk(pl.multiple_of(c * tq, tq), None)
        return carry

    lax.fori_loop(0, qi, body, 0)
    chunk(pl.multiple_of(qi * tq, tq), _causal_bias(tq))
    for h in range(H):
        o_ref[0, :, h * HEAD_DIM:(h + 1) * HEAD_DIM] = _flash_out(h, acc_scr)


def _mla_call(q, kt, v):
    B, H, T, _ = q.shape
    tq = ATT_TILE
    qs, ks, vs, out = _att_specs(B, H, T, tq, False)
    return pl.pallas_call(
        functools.partial(_mla_kernel, tq=tq),
        grid=(B, T // tq),
        in_specs=[qs, ks, vs],
        out_specs=out,
        out_shape=jax.ShapeDtypeStruct((B, T, H * HEAD_DIM), jnp.float32),
        scratch_shapes=_att_scratch(tq),
        compiler_params=_params("arbitrary", "arbitrary"),
        name="mla_attention",
    )(q, kt, v)


def _moba_kernel(q_ref, kt_ref, v_ref, o_ref, kmean_scr, m_scr, acc_scr, *, tq, nb, topk):
    H, hd = GROUP_HEADS, HEAD_DIM
    qi = pl.program_id(1)
    q0 = qi * tq
    T = nb * MOBA_BLOCK

    @pl.when(qi == 0)
    def _():
        avg = jnp.where((_iota((nb, T), 1) >> _log2(MOBA_BLOCK)) == _iota((nb, T), 0), 1.0 / MOBA_BLOCK, 0.0)
        for h in range(H):
            kmean_scr[h] = _dot_nt(avg, kt_ref[0, h])

    blk = _iota((nb, tq), 0)
    own = (q0 + _iota((nb, tq), 1)) >> _log2(MOBA_BLOCK)
    past = blk < own
    bias_rows = [jnp.zeros((hd, tq), jnp.float32)]
    for h in range(H):
        gate = jnp.where(past, _dot_nt(kmean_scr[h], q_ref[0, h]), NEG_INF)
        allowed = ((_rank_desc(gate) < topk) & past) | (blk == own)
        bias_rows.append(jnp.where(allowed, 0.0, NEG_INF))
    if H * nb < LANES - hd:
        bias_rows.append(jnp.zeros((LANES - hd - H * nb, tq), jnp.float32))
    bias = jnp.concatenate(bias_rows, axis=0).T
    lane = _iota((tq, LANES), 1)
    qs = [jnp.where(lane < hd, q_ref[0, h], bias.astype(q_ref.dtype)) for h in range(H)]
    _flash_init(m_scr, acc_scr)

    def chunk(k0, bias):
        for h in range(H):
            s = jnp.dot(qs[h], kt_ref[0, h, :, pl.ds(k0, tq)], preferred_element_type=jnp.float32)
            if bias is not None:
                s = s + bias
            _flash_update(h, s, v_ref[0, h, pl.ds(k0, tq), :], m_scr, acc_scr)

    def body(c, carry):
        chunk(pl.multiple_of(c * tq, tq), None)
        return carry

    lax.fori_loop(0, qi, body, 0)
    chunk(pl.multiple_of(q0, tq), _causal_bias(tq))
    for h in range(H):
        o_ref[0, :, h * hd:(h + 1) * hd] = _flash_out(h, acc_scr)


def _moba_call(q, kt, v):
    B, H, T, _ = q.shape
    tq = ATT_TILE
    nb = T // MOBA_BLOCK
    qs, ks, vs, out = _att_specs(B, H, T, tq, False)
    return pl.pallas_call(
        functools.partial(_moba_kernel, tq=tq, nb=nb, topk=min(MOBA_TOPK, nb - 1)),
        grid=(B, T // tq),
        in_specs=[qs, ks, vs],
        out_specs=out,
        out_shape=jax.ShapeDtypeStruct((B, T, H * HEAD_DIM), jnp.float32),
        scratch_shapes=[pltpu.VMEM((H, nb, LANES), jnp.float32)] + _att_scratch(tq),
        compiler_params=_params("arbitrary", "arbitrary"),
        name="moba_attention",
    )(q, kt, v)


def _cmp_kernel(k_ref, v_ref, pek_ref, pev_ref, kw1_ref, kw2_ref, vw1_ref, vw2_ref, ko_ref, vo_ref, *, rows):
    half = NSA_CMP_STRIDE * HEAD_DIM

    def compress(t_ref, pe_ref, w1_ref, w2_ref):
        t = t_ref[0].astype(jnp.float32)
        first = _dot(t + pe_ref[0:1, :], w1_ref[0:half, :])
        second = _dot(t + pe_ref[1:2, :], w1_ref[half:, :])
        hid = first + pltpu.roll(second, rows - 1, 0)
        return _dot(_silu(hid), w2_ref[...])

    ko_ref[0] = compress(k_ref, pek_ref, kw1_ref, kw2_ref)
    vo_ref[0] = compress(v_ref, pev_ref, vw1_ref, vw2_ref)


def _cmp_call(kc, vc, pe_k, pe_v, k_w1, k_w2, v_w1, v_w2):
    B, T, d = kc.shape
    rows = T // NSA_CMP_STRIDE
    wide = NSA_CMP_STRIDE * d
    assert NSA_CMP_LEN == 2 * NSA_CMP_STRIDE
    const2 = lambda b: (0, 0)
    blk = pl.BlockSpec((1, rows, wide), lambda b: (b, 0, 0))
    out = pl.BlockSpec((1, rows, d), lambda b: (b, 0, 0))
    return pl.pallas_call(
        functools.partial(_cmp_kernel, rows=rows),
        grid=(B,),
        in_specs=[blk, blk, pl.BlockSpec((2, wide), const2), pl.BlockSpec((2, wide), const2),
                  pl.BlockSpec(k_w1.shape, const2), pl.BlockSpec(k_w2.shape, const2),
                  pl.BlockSpec(v_w1.shape, const2), pl.BlockSpec(v_w2.shape, const2)],
        out_specs=[out, out],
        out_shape=[jax.ShapeDtypeStruct((B, rows, d), jnp.float32)] * 2,
        compiler_params=_params("arbitrary"),
        name="nsa_compress",
    )(kc.reshape(B, rows, wide), vc.reshape(B, rows, wide), pe_k.reshape(2, wide), pe_v.reshape(2, wide),
      k_w1.astype(MXU_DTYPE), k_w2.astype(MXU_DTYPE), v_w1.astype(MXU_DTYPE), v_w2.astype(MXU_DTYPE))


def _nsa_kernel(q_ref, kcmp_ref, vcmp_ref, kst_ref, vs_ref, kwt_ref, vw_ref, g_ref, o_ref,
                oc_scr, os_scr, m_scr, acc_scr, *, tq, ncp, n_sel, topn):
    H, hd = GROUP_HEADS, HEAD_DIM
    qi = pl.program_id(1)
    q0 = qi * tq
    tq_col = q0 + _iota((tq, 1), 0)
    qraw = [q_ref[0, h] for h in range(H)]

    cmp_end = _iota((tq, ncp), 1) * NSA_CMP_STRIDE + (NSA_CMP_LEN - 1)
    m_c = cmp_end <= tq_col
    p_sum = jnp.zeros((tq, ncp), jnp.float32)
    for h in range(H):
        s = jnp.where(m_c, _dot_nt(qraw[h][:, :hd], kcmp_ref[0]), NEG_INF)
        e = jnp.where(m_c, jnp.exp2(s - jnp.max(s, axis=-1, keepdims=True)), 0.0)
        l = jnp.sum(e, axis=-1, keepdims=True)
        p = e / jnp.where(l > 0.0, l, 1.0)
        p_sum = p_sum + p
        oc_scr[h] = _dot(p, vcmp_ref[0])

    cmp_start = _iota((n_sel, ncp), 1) * NSA_CMP_STRIDE
    sel_start = _iota((n_sel, ncp), 0) * NSA_SEL_BLOCK
    overlap = (cmp_start < sel_start + NSA_SEL_BLOCK) & (cmp_start + NSA_CMP_LEN > sel_start)
    imp = _dot_nt(jnp.where(overlap, 1.0, 0.0), p_sum)
    sel_id = _iota((n_sel, tq), 0)
    own = (q0 + _iota((n_sel, tq), 1)) >> _log2(NSA_SEL_BLOCK)
    causal = sel_id <= own
    forced = causal & ((sel_id == 0) | (sel_id >= own - 1))
    imp = jnp.where(forced, NSA_FORCE_SCORE, jnp.where(causal, imp, -NSA_FORCE_SCORE))
    bias_rows = [jnp.zeros((hd, tq), jnp.float32), jnp.where(_rank_desc(imp) < topn, 0.0, NEG_INF)]
    if n_sel < LANES - hd:
        bias_rows.append(jnp.zeros((LANES - hd - n_sel, tq), jnp.float32))
    bias = jnp.concatenate(bias_rows, axis=0).T.astype(qraw[0].dtype)
    lane = _iota((tq, LANES), 1)
    qsel = [jnp.where(lane < hd, qraw[h], bias) for h in range(H)]

    _flash_init(m_scr, acc_scr)

    def sel_chunk(k0, extra):
        for h in range(H):
            s = jnp.dot(qsel[h], kst_ref[0, :, pl.ds(k0, tq)], preferred_element_type=jnp.float32)
            if extra is not None:
                s = s + extra
            _flash_update(h, s, vs_ref[0, pl.ds(k0, tq), :], m_scr, acc_scr)

    def sel_body(c, carry):
        sel_chunk(pl.multiple_of(c * tq, tq), None)
        return carry

    lax.fori_loop(0, qi, sel_body, 0)
    tri = _causal_bias(tq)
    sel_chunk(pl.multiple_of(q0, tq), tri)
    for h in range(H):
        os_scr[h] = _flash_out(h, acc_scr)

    _flash_init(m_scr, acc_scr)

    def win_chunk(k0, extra):
        for h in range(H):
            s = jnp.dot(qraw[h], kwt_ref[0, :, pl.ds(k0, tq)], preferred_element_type=jnp.float32) + extra
            _flash_update(h, s, vw_ref[0, pl.ds(k0, tq), :], m_scr, acc_scr)

    first = jnp.maximum(q0 - NSA_WINDOW + 1, 0) // tq

    def win_body(c, carry):
        k0 = pl.multiple_of(c * tq, tq)
        qpos = q0 + _iota((tq, tq), 0)
        kpos = k0 + _iota((tq, tq), 1)
        win_chunk(k0, jnp.where((kpos <= qpos) & (kpos > qpos - NSA_WINDOW), 0.0, NEG_INF))
        return carry

    lax.fori_loop(first, qi + 1, win_body, 0)
    for h in range(H):
        gates = 1.0 / (1.0 + jnp.exp(-g_ref[0][:, 3 * h:3 * h + 3]))
        o_ref[0, :, h * hd:(h + 1) * hd] = (gates[:, 0:1] * oc_scr[h] + gates[:, 1:2] * os_scr[h]
                                            + gates[:, 2:3] * _flash_out(h, acc_scr))


def _nsa_call(q, kcmp, vcmp, kst, vs, kwt, vw, gate_logits):
    B, H, T, _ = q.shape
    tq = ATT_TILE
    ncp = kcmp.shape[1]
    n_sel = T // NSA_SEL_BLOCK
    qs, ks, vsp, out = _att_specs(B, H, T, tq, True)
    cmp_spec = pl.BlockSpec((1, ncp, HEAD_DIM), lambda b, i: (b, 0, 0))
    return pl.pallas_call(
        functools.partial(_nsa_kernel, tq=tq, ncp=ncp, n_sel=n_sel, topn=min(NSA_SEL_TOPN, n_sel)),
        grid=(B, T // tq),
        in_specs=[qs, cmp_spec, cmp_spec, ks, vsp, ks, vsp,
                  pl.BlockSpec((1, tq, LANES), lambda b, i: (b, i, 0))],
        out_specs=out,
        out_shape=jax.ShapeDtypeStruct((B, T, H * HEAD_DIM), jnp.float32),
        scratch_shapes=[pltpu.VMEM((H, tq, HEAD_DIM), jnp.float32), pltpu.VMEM((H, tq, HEAD_DIM), jnp.float32)]
        + _att_scratch(tq),
        compiler_params=_params("arbitrary", "arbitrary"),
        name="nsa_attention",
    )(q, kcmp, vcmp, kst, vs, kwt, vw, gate_logits)


def _sortable(x):
    b = int(np.float32(x).view(np.int32))
    return b ^ ((b >> 31) & 0x7FFFFFFF)


def _from_sortable(k):
    return lax.bitcast_convert_type(k ^ ((k >> 31) & 0x7FFFFFFF), jnp.float32)


def _to_sortable(x):
    k = lax.bitcast_convert_type(x, jnp.int32)
    return k ^ ((k >> 31) & 0x7FFFFFFF)


COUNT_ROWS = 64
VALUE_STEPS = 24


def _dsa_kernel(q_ref, kt_ref, v_ref, iq_ref, ikt_ref, iw_ref, o_ref,
                s_scr, t_scr, m_scr, acc_scr, *, tq, topk, idx_scale):
    H, hd = GROUP_HEADS, HEAD_DIM
    qi = pl.program_id(1)
    q0 = qi * tq
    n_kc = qi + 1
    T = s_scr.shape[1]
    reps = tq // LANES

    lane = _iota((tq, LANES), 1)
    quarter = lane >> _log2(DSA_IDX_DIM)
    per_group = LANES // DSA_IDX_DIM
    iq = iq_ref[0]
    iqh = [jnp.where(quarter == (h % per_group), iq[:, (h // per_group) * LANES:(h // per_group + 1) * LANES],
                     jnp.zeros((), iq.dtype)) for h in range(DSA_IDX_HEADS)]
    iw = iw_ref[0]
    iwb = [jnp.broadcast_to(iw[:, h:h + 1], (tq, LANES)) for h in range(DSA_IDX_HEADS)]

    def score_body(c, ends):
        top, bot = ends
        k0 = pl.multiple_of(c * tq, tq)
        ikt = ikt_ref[0, :, pl.ds(k0, tq)]
        acc = jnp.zeros((tq, tq), jnp.float32)
        for h in range(DSA_IDX_HEADS):
            sh = jnp.dot(iqh[h], ikt, preferred_element_type=jnp.float32)
            acc = acc + _lanes(iwb[h], tq) * jnp.maximum(sh, 0.0)
        qpos = q0 + _iota((tq, tq), 0)
        kpos = k0 + _iota((tq, tq), 1)
        val = acc * idx_scale + 0.0
        sc = jnp.where(kpos <= qpos, val, NEG_INF)
        s_scr[:, pl.ds(k0, tq)] = sc
        low = jnp.where(kpos <= qpos, val, np.inf)
        for r in range(reps):
            top = jnp.maximum(top, sc[:, r * LANES:(r + 1) * LANES])
            bot = jnp.minimum(bot, low[:, r * LANES:(r + 1) * LANES])
        return top, bot

    top, bot = lax.fori_loop(0, n_kc, score_body, (jnp.full((tq, LANES), NEG_INF, jnp.float32),
                                                   jnp.full((tq, LANES), np.inf, jnp.float32)))
    groups = tq // LANES

    def spread(c):
        return jnp.concatenate([jnp.broadcast_to(c[g:g + 1, :], (LANES, LANES)).T for g in range(groups)], axis=0)

    def gather(x):
        return jnp.concatenate([x[g * LANES:(g + 1) * LANES].T[0:1] for g in range(groups)], axis=0)

    def totals(part):
        ones = jnp.ones((8, LANES), jnp.float32)
        return jnp.concatenate([_dot_nt(ones, part[g * LANES:(g + 1) * LANES])[0:1] for g in range(groups)], axis=0)

    row_max = gather(jnp.broadcast_to(jnp.max(top, axis=-1, keepdims=True), (tq, LANES)))
    row_min = gather(jnp.broadcast_to(jnp.min(bot, axis=-1, keepdims=True), (tq, LANES)))

    def count_ge(t, strict=False):
        above = (lambda a, b: a > b) if strict else (lambda a, b: a >= b)
        t_scr[...] = spread(t)
        blocks = [slice(rb * COUNT_ROWS, (rb + 1) * COUNT_ROWS) for rb in range(tq // COUNT_ROWS)]

        def body(c, parts):
            k0 = pl.multiple_of(c * tq, tq)
            out = []
            for rows, part in zip(blocks, parts):
                t_rb = t_scr[rows, :]
                sc = s_scr[rows, pl.ds(k0, tq)]
                for r in range(reps):
                    part = part + jnp.where(above(sc[:, r * LANES:(r + 1) * LANES], t_rb), 1.0, 0.0)
                out.append(part)
            return tuple(out)

        parts = lax.fori_loop(0, n_kc, body, tuple(jnp.zeros((COUNT_ROWS, LANES), jnp.float32) for _ in blocks))
        return totals(jnp.concatenate(parts, axis=0))

    kf = float(topk)
    floor_key = _sortable(NEG_INF)

    def bis_cond(c):
        return c[-1]

    def bis_body(c):
        it, lo, hi, n_lo, _ = c
        active = _any(lo < hi)
        key_mid = (lo | hi) - ((lo ^ hi) >> 1)
        val_mid = _to_sortable(0.5 * (_from_sortable(lo) + _from_sortable(hi)))
        steps = jnp.zeros(lo.shape, jnp.int32) + it
        use_val = (val_mid > lo) & (val_mid <= hi) & (lo > floor_key) & (steps < VALUE_STEPS)
        mid = jnp.where(use_val, val_mid, key_mid)
        cnt = count_ge(_from_sortable(mid))
        ge = cnt >= kf
        lo = jnp.where(ge, mid, lo)
        n_lo = jnp.where(ge, cnt, n_lo)
        hi = jnp.where(cnt == kf, mid, jnp.where(ge, hi, mid - 1))
        return it + 1, lo, hi, n_lo, active

    zeros = jnp.zeros((groups, LANES), jnp.float32)
    n_pos = count_ge(zeros, strict=True)
    n_nonneg = count_ge(zeros)
    n_causal = q0 + _iota((groups, LANES), 0) * LANES + _iota((groups, LANES), 1) + 1
    lo_neg = jnp.where(n_causal >= topk, _to_sortable(row_min), floor_key)
    n_neg = count_ge(_from_sortable(lo_neg))
    is_pos = n_pos >= kf
    is_zero = n_nonneg >= kf
    pick = lambda p, z, n: jnp.where(is_pos, p, jnp.where(is_zero, z, n))
    lo0 = pick(_sortable(np.float32(1e-45)), _sortable(0.0), lo_neg)
    hi0 = pick(_to_sortable(row_max), _sortable(0.0), _sortable(-0.0) - 1)
    _, lo, _, n_ge, _ = lax.while_loop(
        bis_cond, bis_body, (jnp.int32(0), lo0, hi0, pick(n_pos, n_nonneg, n_neg), _any(lo0 < hi0)))
    thr = _lanes(spread(_from_sortable(lo)), tq)

    any_tie = _any(n_ge > kf)

    def causal_at(k0):
        return (k0 + _iota((tq, tq), 1)) <= (q0 + _iota((tq, tq), 0))

    @pl.when(jnp.logical_not(any_tie))
    def _():
        def body(c, carry):
            k0 = pl.multiple_of(c * tq, tq)
            sc = s_scr[:, pl.ds(k0, tq)]
            s_scr[:, pl.ds(k0, tq)] = jnp.where((sc >= thr) & causal_at(k0), 0.0, NEG_INF)
            return carry
        lax.fori_loop(0, n_kc, body, 0)

    @pl.when(any_tie)
    def _():
        need = _lanes(spread(kf - count_ge(_from_sortable(lo), strict=True)), tq)
        prefix = jnp.where(_iota((tq, tq), 0) <= _iota((tq, tq), 1), 1.0, 0.0)
        ones = jnp.ones((tq, LANES), jnp.float32)

        def body(c, seen):
            k0 = pl.multiple_of(c * tq, tq)
            sc = s_scr[:, pl.ds(k0, tq)]
            eq = jnp.where(sc == thr, 1.0, 0.0)
            rank_eq = _dot(eq, prefix) + _lanes(seen, tq)
            picked = (sc > thr) | ((sc == thr) & (rank_eq <= need))
            s_scr[:, pl.ds(k0, tq)] = jnp.where(picked & causal_at(k0), 0.0, NEG_INF)
            return seen + _dot(eq, ones)
        lax.fori_loop(0, n_kc, body, jnp.zeros((tq, LANES), jnp.float32))

    qs = [q_ref[0, h] for h in range(H)]
    _flash_init(m_scr, acc_scr)

    def att_body(c, carry):
        k0 = pl.multiple_of(c * tq, tq)
        bias = s_scr[:, pl.ds(k0, tq)]
        for h in range(H):
            s = jnp.dot(qs[h], kt_ref[0, :, pl.ds(k0, tq)], preferred_element_type=jnp.float32) + bias
            _flash_update(h, s, v_ref[0, pl.ds(k0, tq), :], m_scr, acc_scr)
        return carry

    lax.fori_loop(0, n_kc, att_body, 0)
    for h in range(H):
        o_ref[0, :, h * hd:(h + 1) * hd] = _flash_out(h, acc_scr)


def _dsa_call(q, kt, v, iq, ikt, iw):
    B, H, T, _ = q.shape
    tq = ATT_TILE
    topk = min(DSA_TOPK, T // 4)
    assert tq >= topk
    qs, ks, vs, out = _att_specs(B, H, T, tq, True)
    return pl.pallas_call(
        functools.partial(_dsa_kernel, tq=tq, topk=topk, idx_scale=(DSA_IDX_HEADS * DSA_IDX_DIM) ** -0.5),
        grid=(B, T // tq),
        in_specs=[qs, ks, vs, pl.BlockSpec((1, tq, 2 * LANES), lambda b, i: (b, i, 0)), ks,
                  pl.BlockSpec((1, tq, LANES), lambda b, i: (b, i, 0))],
        out_specs=out,
        out_shape=jax.ShapeDtypeStruct((B, T, H * HEAD_DIM), jnp.float32),
        scratch_shapes=[pltpu.VMEM((tq, T), jnp.float32), pltpu.VMEM((tq, LANES), jnp.float32)] + _att_scratch(tq),
        compiler_params=_params("arbitrary", "arbitrary"),
        name="dsa_attention",
    )(q, kt, v, iq, ikt, iw)


def _out_kernel(x_ref, ada_ref, o1_ref, o2_ref, o3_ref, o4_ref, gn_ref, w_ref, y_ref):
    a = ada_ref[0]
    gn = gn_ref[...]
    y = jnp.concatenate([_rms(o[0], gn[i:i + 1]).astype(MXU_DTYPE)
                         for i, o in enumerate((o1_ref, o2_ref, o3_ref, o4_ref))], axis=-1)
    y_ref[...] = x_ref[...] + a[5:6] * jnp.dot(y, w_ref[...], preferred_element_type=jnp.float32)


def _out_call(x2d, ada_l, groups, group_norm, w_out, B, T):
    N, D = x2d.shape
    tm = ROW_TILE
    tpb = T // tm
    grp = pl.BlockSpec((1, tm, GROUP_WIDTH), lambda i: (i // tpb, i % tpb, 0))
    return pl.pallas_call(
        _out_kernel,
        grid=(N // tm,),
        in_specs=[pl.BlockSpec((tm, D), lambda i: (i, 0)),
                  pl.BlockSpec((1, N_ADA, D), lambda i: (i // tpb, 0, 0)),
                  grp, grp, grp, grp,
                  pl.BlockSpec((N_GROUPS, GROUP_WIDTH), lambda i: (0, 0)),
                  pl.BlockSpec((MIX_WIDTH, D), lambda i: (0, 0))],
        out_specs=pl.BlockSpec((tm, D), lambda i: (i, 0)),
        out_shape=jax.ShapeDtypeStruct((N, D), jnp.float32),
        compiler_params=_params("arbitrary"),
        name="mixer_out_proj",
    )(x2d, ada_l, *groups, group_norm, w_out.astype(MXU_DTYPE))


def _mixer_groups(x2d, ada_l, tables, mix_norm, w_in, mla_q_norm, mla_w_uq, mla_kv_norm, mla_w_uk, mla_w_uv,
                  nsa_pe_k, nsa_pe_v, nsa_cmp_k_w1, nsa_cmp_k_w2, nsa_cmp_v_w1, nsa_cmp_v_w2, B, T):
    (mq, mkt, mv, lq, lkt, lv, nq, nkc, nvc, nkst, nvs, nkwt, nvw, ngate,
     dq, dkt, dv, diq, dikt, diw) = _proj_call(
        x2d, ada_l, mix_norm, w_in, tables, mla_q_norm, mla_w_uq, mla_kv_norm, mla_w_uk, mla_w_uv, B, T)
    o_moba = _moba_call(mq, mkt, mv)
    o_mla = _mla_call(lq, lkt, lv)
    kcmp, vcmp = _cmp_call(nkc, nvc, nsa_pe_k, nsa_pe_v, nsa_cmp_k_w1, nsa_cmp_k_w2, nsa_cmp_v_w1, nsa_cmp_v_w2)
    o_nsa = _nsa_call(nq, kcmp, vcmp, nkst, nvs, nkwt, nvw, ngate)
    o_dsa = _dsa_call(dq, dkt, dv, diq, dikt, diw)
    return o_moba, o_mla, o_nsa, o_dsa


def kernel(x, c, ada_w, ada_b, ffn1_norm, ffn1_w_gate, ffn1_w_up, ffn1_w_down, mix_norm, w_in, mla_q_norm, mla_w_uq, mla_kv_norm, mla_w_uk, mla_w_uv, nsa_pe_k, nsa_pe_v, nsa_cmp_k_w1, nsa_cmp_k_w2, nsa_cmp_v_w1, nsa_cmp_v_w2, group_norm, w_out, ffn2_norm, ffn2_w_gate, ffn2_w_up, ffn2_w_down, final_norm):
    B, T, D = x.shape
    L = ada_w.shape[0]
    assert D == D_MODEL and T % ROW_TILE == 0 and T % ATT_TILE == 0
    assert ATT_TILE % MOBA_BLOCK == 0 and ATT_TILE % NSA_SEL_BLOCK == 0 and ATT_TILE >= NSA_WINDOW
    tpb = T // ROW_TILE
    ada = _ada_call(c, ada_w, ada_b)
    tables = _rope_tables(T)
    x2d = x.reshape(B * T, D)
    for l in range(L):
        x2d = _ffn_call(x2d, ada[l], ffn1_norm[l], ffn1_w_gate[l], ffn1_w_up[l], ffn1_w_down[l], 0, tpb)
        groups = _mixer_groups(x2d, ada[l], tables, mix_norm[l], w_in[l], mla_q_norm[l], mla_w_uq[l],
                               mla_kv_norm[l], mla_w_uk[l], mla_w_uv[l], nsa_pe_k[l], nsa_pe_v[l],
                               nsa_cmp_k_w1[l], nsa_cmp_k_w2[l], nsa_cmp_v_w1[l], nsa_cmp_v_w2[l], B, T)
        x2d = _out_call(x2d, ada[l], groups, group_norm[l], w_out[l], B, T)
        x2d = _ffn_call(x2d, ada[l], ffn2_norm[l], ffn2_w_gate[l], ffn2_w_up[l], ffn2_w_down[l], 6, tpb,
                        final_gain=final_norm if l == L - 1 else None)
    return x2d.reshape(B, T, D)
```

```python
import functools
import math

import numpy as np
import jax
import jax.numpy as jnp
from jax import lax
from jax.experimental import pallas as pl
from jax.experimental.pallas import tpu as pltpu

D_MODEL = 1024
N_GROUPS = 4
HEAD_DIM = 64
GROUP_HEADS = D_MODEL // (N_GROUPS * HEAD_DIM)
GROUP_WIDTH = GROUP_HEADS * HEAD_DIM
MIX_WIDTH = N_GROUPS * GROUP_WIDTH
D_FF = 256 * ((8 * D_MODEL + 3 * 256 - 1) // (3 * 256))
N_ADA = 9
FFN_RESIDUAL_WEIGHT = 0.5
ROPE_THETA = 10000.0
RMS_EPS = 1e-6
NEG_INF = -1e30

MOBA_BLOCK = 256
MOBA_TOPK = 3

MLA_Q_LORA = D_MODEL // 4
MLA_KV_LORA = D_MODEL // 8
MLA_NOPE = HEAD_DIM
MLA_ROPE = HEAD_DIM // 2
MLA_V = HEAD_DIM
MLA_QK = MLA_NOPE + MLA_ROPE

NSA_CMP_LEN = 32
NSA_CMP_STRIDE = 16
NSA_CMP_HIDDEN = 4 * HEAD_DIM
NSA_SEL_BLOCK = 64
NSA_SEL_TOPN = 16
NSA_WINDOW = 512
NSA_FORCE_SCORE = 1e4

DSA_TOPK = 256
DSA_IDX_HEADS = 8
DSA_IDX_DIM = 32

IN_NAMES = ("mq", "mk", "mv", "cq", "ckv", "kr", "nq", "nkc", "nvc", "nks", "nvs", "nkw", "nvw",
            "ngate", "dq", "dk", "dv", "diq", "dik", "diw")
IN_SIZES = (
    GROUP_WIDTH, GROUP_WIDTH, GROUP_WIDTH,
    MLA_Q_LORA, MLA_KV_LORA, MLA_ROPE,
    GROUP_WIDTH, HEAD_DIM, HEAD_DIM, HEAD_DIM, HEAD_DIM,
    HEAD_DIM, HEAD_DIM, 3 * GROUP_HEADS,
    GROUP_WIDTH, HEAD_DIM, HEAD_DIM,
    DSA_IDX_HEADS * DSA_IDX_DIM, DSA_IDX_DIM, DSA_IDX_HEADS,
)
N_IN = sum(IN_SIZES)

LANES = 128
MXU_DTYPE = jnp.bfloat16
VMEM_LIMIT = 56 * 1024 * 1024

ATT_TILE = 512
ROW_TILE = 512
MXU_TILE = 256
FF_CHUNK = 6 * MXU_TILE

LOG2E = math.log2(math.e)
M_INIT = -1e29
DEN_LANE = HEAD_DIM


def _params(*semantics):
    return pltpu.CompilerParams(dimension_semantics=semantics, vmem_limit_bytes=VMEM_LIMIT)


def _dot(a, b):
    return jnp.dot(a.astype(MXU_DTYPE), b.astype(MXU_DTYPE), preferred_element_type=jnp.float32)


def _dot_nt(a, b):
    return lax.dot_general(a.astype(MXU_DTYPE), b.astype(MXU_DTYPE), (((1,), (1,)), ((), ())),
                           preferred_element_type=jnp.float32)


def _rms(x, g):
    return x * lax.rsqrt(jnp.mean(x * x, axis=-1, keepdims=True) + RMS_EPS) * g


def _silu(x):
    return x * (1.0 / (1.0 + jnp.exp(-x)))


def _iota(shape, dim):
    return lax.broadcasted_iota(jnp.int32, shape, dim)


def _log2(n):
    assert n & (n - 1) == 0
    return n.bit_length() - 1


def _any(pred):
    return jnp.max(jnp.where(pred, 1.0, 0.0)) > 0.5


def _lanes(x, width):
    return x if width == LANES else jnp.tile(x, (1, width // LANES))


def _ada_kernel(c_ref, w_ref, b_ref, o_ref):
    o_ref[0] = _dot(_silu(c_ref[...]), w_ref[0]) + b_ref[0]


def _ada_call(c, ada_w, ada_b):
    L, D, _ = ada_w.shape
    B = c.shape[0]
    out = pl.pallas_call(
        _ada_kernel,
        grid=(L, N_ADA),
        in_specs=[
            pl.BlockSpec((B, D), lambda l, k: (0, 0)),
            pl.BlockSpec((1, D, D), lambda l, k: (l, 0, k)),
            pl.BlockSpec((1, 1, D), lambda l, k: (l, 0, k)),
        ],
        out_specs=pl.BlockSpec((1, B, D), lambda l, k: (l, 0, k)),
        out_shape=jax.ShapeDtypeStruct((L, B, N_ADA * D), jnp.float32),
        compiler_params=_params("arbitrary", "arbitrary"),
        name="ada_proj",
    )(c, ada_w, ada_b.reshape(L, 1, N_ADA * D))
    return out.reshape(L, B, N_ADA, D)


def _ffn_chunks(F):
    bounds = list(range(0, F, FF_CHUNK)) + [F]
    return list(zip(bounds[:-1], bounds[1:]))


def _ffn_kernel(x_ref, ada_ref, gn_ref, wg_ref, wu_ref, wd_ref, *rest, k0, final):
    if final:
        fg_ref, o_ref = rest
    else:
        (o_ref,) = rest
    a = ada_ref[0]
    h = (_rms(x_ref[...], gn_ref[...]) * (1.0 + a[k0 + 1:k0 + 2]) + a[k0:k0 + 1]).astype(MXU_DTYPE)
    acc = None
    for lo, hi in _ffn_chunks(wg_ref.shape[1]):
        g = jnp.dot(h, wg_ref[:, lo:hi], preferred_element_type=jnp.float32)
        u = jnp.dot(h, wu_ref[:, lo:hi], preferred_element_type=jnp.float32)
        part = jnp.dot((_silu(g) * u).astype(MXU_DTYPE), wd_ref[lo:hi, :], preferred_element_type=jnp.float32)
        acc = part if acc is None else acc + part
    y = x_ref[...] + (FFN_RESIDUAL_WEIGHT * a[k0 + 2:k0 + 3]) * acc
    if final:
        y = _rms(y, fg_ref[...])
    o_ref[...] = y


def _ffn_call(x2d, ada_l, norm_g, w_gate, w_up, w_down, k0, tiles_per_batch, final_gain=None):
    N, D = x2d.shape
    F = w_gate.shape[1]
    tm = ROW_TILE
    in_specs = [
        pl.BlockSpec((tm, D), lambda i: (i, 0)),
        pl.BlockSpec((1, N_ADA, D), lambda i: (i // tiles_per_batch, 0, 0)),
        pl.BlockSpec((1, D), lambda i: (0, 0)),
        pl.BlockSpec((D, F), lambda i: (0, 0)),
        pl.BlockSpec((D, F), lambda i: (0, 0)),
        pl.BlockSpec((F, D), lambda i: (0, 0)),
    ]
    args = [x2d, ada_l, norm_g.reshape(1, D), w_gate.astype(MXU_DTYPE), w_up.astype(MXU_DTYPE),
            w_down.astype(MXU_DTYPE)]
    final = final_gain is not None
    if final:
        in_specs.append(pl.BlockSpec((1, D), lambda i: (0, 0)))
        args.append(final_gain.reshape(1, D))
    return pl.pallas_call(
        functools.partial(_ffn_kernel, k0=k0, final=final),
        grid=(N // tm,),
        in_specs=in_specs,
        out_specs=pl.BlockSpec((tm, D), lambda i: (i, 0)),
        out_shape=jax.ShapeDtypeStruct((N, D), jnp.float32),
        compiler_params=_params("arbitrary"),
        name="ffn",
    )(*args)


G_QMAIN, G_QSWAP, G_KC, G_V, G_MISC, ROW_GROUPS = 0, 12, 24, 26, 33, 41
T_MK, T_MKS, T_SW, T_SWS, T_DK, T_DKS, T_IK, T_ROWS = 0, 256, 512, 640, 768, 896, 1024, 1152


def _swap_halves(c, width):
    return c.reshape(-1, 2, width // 2)[:, ::-1, :].reshape(-1)


def _proj_indices():
    off = dict(zip(IN_NAMES, np.cumsum((0,) + IN_SIZES[:-1]).tolist()))
    size = dict(zip(IN_NAMES, IN_SIZES))
    cols = lambda name: np.arange(off[name], off[name] + size[name])
    zero = lambda n: np.full((n,), N_IN)
    hd = HEAD_DIM

    def head_groups(c):
        return np.concatenate([np.concatenate([c[i:i + hd], zero(LANES - hd)]) for i in range(0, c.size, hd)])

    q = np.concatenate([cols("mq"), cols("nq"), cols("dq")])
    row = np.concatenate([
        head_groups(q), head_groups(_swap_halves(q, hd)),
        head_groups(cols("nkc")), head_groups(_swap_halves(cols("nkc"), hd)),
        head_groups(np.concatenate([cols("mv"), cols("nvs"), cols("nvw"), cols("dv")])),
        cols("cq"), cols("ckv"), head_groups(cols("nvc")), cols("diq"),
        cols("ngate"), zero(LANES - size["ngate"]), cols("diw"), zero(LANES - size["diw"])])
    assert row.size == ROW_GROUPS * LANES
    sw = np.concatenate([cols("nks"), cols("nkw")])
    dk_main = np.concatenate([cols("dk"), cols("kr"), zero(LANES - hd - MLA_ROPE)])
    dk_swap = np.concatenate([_swap_halves(cols("dk"), hd), _swap_halves(cols("kr"), MLA_ROPE),
                              zero(LANES - hd - MLA_ROPE)])
    tr = np.concatenate([cols("mk"), _swap_halves(cols("mk"), hd), sw, _swap_halves(sw, hd),
                         dk_main, dk_swap, np.tile(cols("dik"), LANES // DSA_IDX_DIM)])
    assert tr.size == T_ROWS
    return row, tr


def _proj_kernel(x_ref, ada_ref, gn_ref, wr_ref, wt_ref, rtab_ref, ttab_ref,
                 qn_ref, wuq_ref, kvn_ref, wukt_ref, wuv_ref,
                 mq_ref, mkt_ref, mv_ref, lq_ref, lkt_ref, lv_ref,
                 nq_ref, nkc_ref, nvc_ref, nkst_ref, nvs_ref, nkwt_ref, nvw_ref, ng_ref,
                 dq_ref, dkt_ref, dv_ref, diq_ref, dikt_ref, diw_ref, *, tm, tpb, n_moba, n_sel):
    H, hd, G = GROUP_HEADS, HEAD_DIM, LANES
    t0 = (pl.program_id(0) % tpb) * tm
    a = ada_ref[0]
    h = (_rms(x_ref[...], gn_ref[...]) * (1.0 + a[4:5]) + a[3:4]).astype(MXU_DTYPE)

    def rows(g0, n):
        return jnp.dot(h, wr_ref[:, g0 * G:(g0 + n) * G], preferred_element_type=jnp.float32)

    def cols(r0, n):
        return _dot_nt(wt_ref[r0:r0 + n, :], h)

    roped_q = (rows(G_QMAIN, 3 * H) * _lanes(rtab_ref[0], 3 * H * G)
               + rows(G_QSWAP, 3 * H) * _lanes(rtab_ref[1], 3 * H * G))
    for hh in range(H):
        mq_ref[0, hh] = roped_q[:, hh * G:(hh + 1) * G].astype(mq_ref.dtype)
        nq_ref[0, hh] = roped_q[:, (H + hh) * G:(H + hh + 1) * G].astype(nq_ref.dtype)
        dq_ref[0, hh] = roped_q[:, (2 * H + hh) * G:(2 * H + hh + 1) * G].astype(dq_ref.dtype)
    kc = rows(G_KC, 2)
    nkc_ref[0] = (kc[:, :G] * rtab_ref[2] + kc[:, G:] * rtab_ref[3])[:, :hd].astype(nkc_ref.dtype)

    ones_hi = jnp.where(_iota((1, G), 1) >= DEN_LANE, 1.0, 0.0)
    v = rows(G_V, H + 3) + _lanes(ones_hi, (H + 3) * G)
    for hh in range(H):
        mv_ref[0, hh] = v[:, hh * G:(hh + 1) * G].astype(mv_ref.dtype)
    nvs_ref[0] = v[:, H * G:(H + 1) * G].astype(nvs_ref.dtype)
    nvw_ref[0] = v[:, (H + 1) * G:(H + 2) * G].astype(nvw_ref.dtype)
    dv_ref[0] = v[:, (H + 2) * G:(H + 3) * G].astype(dv_ref.dtype)

    misc = rows(G_MISC, 8)
    cq = misc[:, :MLA_Q_LORA]
    ckv = misc[:, MLA_Q_LORA:MLA_Q_LORA + MLA_KV_LORA]
    nvc_ref[0] = misc[:, 3 * G:3 * G + hd].astype(nvc_ref.dtype)
    diq_ref[0] = misc[:, 4 * G:6 * G].astype(diq_ref.dtype)
    ng_ref[0] = misc[:, 6 * G:7 * G]
    diw_ref[0] = misc[:, 7 * G:8 * G]

    tok = t0 + _iota((hd, tm), 1)
    rid = _iota((hd, tm), 0)
    oh_moba = jnp.where((rid & (n_moba - 1)) == (tok >> _log2(MOBA_BLOCK)), 1.0, 0.0)
    oh_sel = jnp.where((rid == (tok >> _log2(NSA_SEL_BLOCK))) & (rid < n_sel), 1.0, 0.0)
    zeros_lo = jnp.zeros((hd, tm), jnp.float32)
    ta_c, ta_s, tb_c, tb_s = ttab_ref[0], ttab_ref[1], ttab_ref[2], ttab_ref[3]
    mkt = (cols(T_MK, H * hd) * jnp.tile(ta_c, (H * hd // G, 1))
           + cols(T_MKS, H * hd) * jnp.tile(ta_s, (H * hd // G, 1)))
    for hh in range(H):
        mine = (rid >> _log2(n_moba)) == hh
        mkt_ref[0, hh] = jnp.concatenate([mkt[hh * hd:(hh + 1) * hd], jnp.where(mine, oh_moba, 0.0)],
                                         axis=0).astype(mkt_ref.dtype)
    sw = cols(T_SW, G) * ta_c + cols(T_SWS, G) * ta_s
    nkst_ref[0] = jnp.concatenate([sw[:hd], oh_sel], axis=0).astype(nkst_ref.dtype)
    nkwt_ref[0] = jnp.concatenate([sw[hd:], zeros_lo], axis=0).astype(nkwt_ref.dtype)
    dkr = cols(T_DK, G) * tb_c + cols(T_DKS, G) * tb_s
    rid2 = _iota((G, tm), 0)
    dkt_ref[0] = jnp.where(rid2 < hd, dkr, 0.0).astype(dkt_ref.dtype)
    kpe_rows = jnp.where((rid2 >= MLA_NOPE) & (rid2 < MLA_QK), dkr, 0.0)
    dikt_ref[0] = cols(T_IK, G).astype(dikt_ref.dtype)

    cqn = _rms(cq, qn_ref[...]).astype(MXU_DTYPE)
    lq = (jnp.dot(cqn, wuq_ref[:, :H * G], preferred_element_type=jnp.float32) * _lanes(rtab_ref[4], H * G)
          + jnp.dot(cqn, wuq_ref[:, H * G:], preferred_element_type=jnp.float32) * _lanes(rtab_ref[5], H * G))
    ckvn = _rms(ckv, kvn_ref[...]).astype(MXU_DTYPE)
    knt = _dot_nt(wukt_ref[...], ckvn)
    lv = jnp.dot(ckvn, wuv_ref[...], preferred_element_type=jnp.float32) + _lanes(ones_hi, H * G)
    for hh in range(H):
        lq_ref[0, hh] = lq[:, hh * G:(hh + 1) * G].astype(lq_ref.dtype)
        lkt_ref[0, hh] = (knt[hh * G:(hh + 1) * G] + kpe_rows).astype(lkt_ref.dtype)
        lv_ref[0, hh] = lv[:, hh * G:(hh + 1) * G].astype(lv_ref.dtype)


def _rope_tables(T):
    def cs(dim):
        inv_freq = 1.0 / (ROPE_THETA ** (np.arange(0, dim, 2, dtype=np.float32) / dim))
        ang = jnp.arange(T, dtype=jnp.float32)[:, None] * jnp.asarray(inv_freq, jnp.float32)[None, :]
        cos, sin = jnp.cos(ang), jnp.sin(ang)
        return jnp.concatenate([cos, cos], axis=-1), jnp.concatenate([-sin, sin], axis=-1)

    c64, s64 = cs(HEAD_DIM)
    c32, s32 = cs(MLA_ROPE)
    pad = lambda t, n: jnp.concatenate([t, jnp.zeros((T, n), jnp.float32)], axis=-1)
    sc = HEAD_DIM ** -0.5 * LOG2E
    sl = MLA_QK ** -0.5 * LOG2E
    ones = jnp.ones((T, MLA_NOPE), jnp.float32)
    rest = LANES - MLA_QK
    rtab = jnp.stack([
        pad(c64 * sc, LANES - HEAD_DIM), pad(s64 * sc, LANES - HEAD_DIM),
        pad(c64, LANES - HEAD_DIM), pad(s64, LANES - HEAD_DIM),
        pad(jnp.concatenate([ones, c32], axis=-1) * sl, rest),
        pad(jnp.concatenate([0.0 * ones, s32], axis=-1) * sl, rest)])
    ttab = jnp.stack([
        jnp.concatenate([c64, c64], axis=-1).T, jnp.concatenate([s64, s64], axis=-1).T,
        pad(jnp.concatenate([c64, c32], axis=-1), rest).T, pad(jnp.concatenate([s64, s32], axis=-1), rest).T])
    return rtab, ttab


def _proj_call(x2d, ada_l, norm_g, w_in, tables, mla_q_norm, mla_w_uq, mla_kv_norm, mla_w_uk, mla_w_uv, B, T):
    N, D = x2d.shape
    H, G = GROUP_HEADS, LANES
    tm = ROW_TILE
    tpb = T // tm
    n_moba, n_sel = T // MOBA_BLOCK, T // NSA_SEL_BLOCK
    assert H * n_moba <= LANES - HEAD_DIM and n_sel <= LANES - HEAD_DIM and n_moba & (n_moba - 1) == 0
    zcol = lambda w: jnp.concatenate([w, jnp.zeros((w.shape[0], 1), w.dtype)], axis=1)
    row_idx, tr_idx = _proj_indices()
    w_ext = zcol(w_in)
    w_row = w_ext[:, row_idx].astype(MXU_DTYPE)
    w_tr = w_ext[:, tr_idx].T.astype(MXU_DTYPE)
    zq = mla_w_uq.shape[1]
    per_head = np.arange(H * MLA_QK).reshape(H, MLA_QK)
    main = np.concatenate([np.concatenate([per_head[i], np.full((G - MLA_QK,), zq)]) for i in range(H)])
    part = np.concatenate([np.concatenate([np.full((MLA_NOPE,), zq), _swap_halves(per_head[i, MLA_NOPE:], MLA_ROPE),
                                           np.full((G - MLA_QK,), zq)]) for i in range(H)])
    wuq = zcol(mla_w_uq)[:, np.concatenate([main, part])].astype(MXU_DTYPE)
    zv = mla_w_uk.shape[1]
    grp = np.concatenate([np.concatenate([np.arange(i * HEAD_DIM, (i + 1) * HEAD_DIM), np.full((G - HEAD_DIM,), zv)])
                          for i in range(H)])
    wukt = zcol(mla_w_uk)[:, grp].T.astype(MXU_DTYPE)
    wuv = zcol(mla_w_uv)[:, grp].astype(MXU_DTYPE)
    rtab, ttab = tables

    row = lambda i: (i, 0)
    const2 = lambda i: (0, 0)
    in_specs = [
        pl.BlockSpec((tm, D), row),
        pl.BlockSpec((1, N_ADA, D), lambda i: (i // tpb, 0, 0)),
        pl.BlockSpec((1, D), const2),
        pl.BlockSpec(w_row.shape, const2),
        pl.BlockSpec(w_tr.shape, const2),
        pl.BlockSpec((6, tm, G), lambda i: (0, i % tpb, 0)),
        pl.BlockSpec((4, G, tm), lambda i: (0, 0, i % tpb)),
        pl.BlockSpec((1, MLA_Q_LORA), const2),
        pl.BlockSpec(wuq.shape, const2),
        pl.BlockSpec((1, MLA_KV_LORA), const2),
        pl.BlockSpec(wukt.shape, const2),
        pl.BlockSpec(wuv.shape, const2),
    ]
    dt = MXU_DTYPE
    hq = (jax.ShapeDtypeStruct((B, H, T, G), dt), pl.BlockSpec((1, H, tm, G), lambda i: (i // tpb, 0, i % tpb, 0)))
    hkt = (jax.ShapeDtypeStruct((B, H, G, T), dt), pl.BlockSpec((1, H, G, tm), lambda i: (i // tpb, 0, 0, i % tpb)))
    srow = lambda d, t=dt: (jax.ShapeDtypeStruct((B, T, d), t), pl.BlockSpec((1, tm, d), lambda i: (i // tpb, i % tpb, 0)))
    skt = (jax.ShapeDtypeStruct((B, G, T), dt), pl.BlockSpec((1, G, tm), lambda i: (i // tpb, 0, i % tpb)))
    outs = [hq, hkt, hq,
            hq, hkt, hq,
            hq, srow(HEAD_DIM), srow(HEAD_DIM), skt, srow(G), skt, srow(G), srow(G, jnp.float32),
            hq, skt, srow(G), srow(2 * G), skt, srow(G, jnp.float32)]
    return pl.pallas_call(
        functools.partial(_proj_kernel, tm=tm, tpb=tpb, n_moba=n_moba, n_sel=n_sel),
        grid=(N // tm,),
        in_specs=in_specs,
        out_specs=[o[1] for o in outs],
        out_shape=[o[0] for o in outs],
        compiler_params=_params("arbitrary"),
        name="mixer_in_proj",
    )(x2d, ada_l, norm_g.reshape(1, D), w_row, w_tr, rtab, ttab,
      mla_q_norm.reshape(1, -1), wuq, mla_kv_norm.reshape(1, -1), wukt, wuv)


def _flash_init(m_scr, acc_scr):
    m_scr[...] = jnp.full(m_scr.shape, M_INIT, jnp.float32)
    acc_scr[...] = jnp.zeros_like(acc_scr)


def _flash_update(h, s, v, m_scr, acc_scr):
    m_prev = m_scr[h]
    m_new = jnp.maximum(m_prev, jnp.max(s, axis=-1, keepdims=True))
    p = jnp.exp2(s - _lanes(m_new, s.shape[1]))
    acc_scr[h] = jnp.exp2(m_prev - m_new) * acc_scr[h] + _dot(p, v)
    m_scr[h] = m_new


def _flash_out(h, acc_scr):
    acc = acc_scr[h]
    den = acc[:, DEN_LANE:DEN_LANE + 1]
    return acc[:, :HEAD_DIM] / jnp.where(den > 0.0, den, 1.0)


def _causal_bias(t):
    return jnp.where(_iota((t, t), 1) <= _iota((t, t), 0), 0.0, NEG_INF)


def _rank_desc(x):
    n = x.shape[0]
    row = _iota(x.shape, 0)
    rank = jnp.zeros(x.shape, jnp.float32)
    for j in range(n):
        cand = x[j:j + 1, :]
        rank = rank + jnp.where(cand > x, 1.0, jnp.where((cand == x) & (row > j), 1.0, 0.0))
    return rank


def _att_scratch(tq, stacked=False):
    shape = (1, GROUP_HEADS * tq, LANES) if stacked else (GROUP_HEADS, tq, LANES)
    return [pltpu.VMEM(shape, jnp.float32), pltpu.VMEM(shape, jnp.float32)]


def _att_specs(B, H, T, tq, shared_kv):
    q = pl.BlockSpec((1, H, tq, LANES), lambda b, i: (b, 0, i, 0))
    if shared_kv:
        kt = pl.BlockSpec((1, LANES, T), lambda b, i: (b, 0, 0))
        v = pl.BlockSpec((1, T, LANES), lambda b, i: (b, 0, 0))
    else:
        kt = pl.BlockSpec((1, H, LANES, T), lambda b, i: (b, 0, 0, 0))
        v = pl.BlockSpec((1, H, T, LANES), lambda b, i: (b, 0, 0, 0))
    out = pl.BlockSpec((1, tq, H * HEAD_DIM), lambda b, i: (b, i, 0))
    return q, kt, v, out


def _mla_kernel(q_ref, kt_ref, v_ref, o_ref, m_scr, acc_scr, *, tq):
    H = GROUP_HEADS
    qi = pl.program_id(1)
    qs = [q_ref[0, h] for h in range(H)]
    _flash_init(m_scr, acc_scr)

    def chunk(k0, bias):
        for h in range(H):
            s = jnp.dot(qs[h], kt_ref[0, h, :, pl.ds(k0, tq)], preferred_element_type=jnp.float32)
            if bias is not None:
                s = s + bias
            _flash_update(h, s, v_ref[0, h, pl.ds(k0, tq), :], m_scr, acc_scr)

    def body(c, carry):
        chunk(pl.multiple_of(c * tq, tq), None)
        return carry

    lax.fori_loop(0, qi, body, 0)
    chunk(pl.multiple_of(qi * tq, tq), _causal_bias(tq))
    for h in range(H):
        o_ref[0, :, h * HEAD_DIM:(h + 1) * HEAD_DIM] = _flash_out(h, acc_scr)


def _mla_call(q, kt, v):
    B, H, T, _ = q.shape
    tq = ATT_TILE
    qs, ks, vs, out = _att_specs(B, H, T, tq, False)
    return pl.pallas_call(
        functools.partial(_mla_kernel, tq=tq),
        grid=(B, T // tq),
        in_specs=[qs, ks, vs],
        out_specs=out,
        out_shape=jax.ShapeDtypeStruct((B, T, H * HEAD_DIM), jnp.float32),
        scratch_shapes=_att_scratch(tq),
        compiler_params=_params("arbitrary", "arbitrary"),
        name="mla_attention",
    )(q, kt, v)


def _moba_kernel(q_ref, kt_ref, v_ref, o_ref, kmean_scr, m_scr, acc_scr, *, tq, nb, topk):
    H, hd = GROUP_HEADS, HEAD_DIM
    qi = pl.program_id(1)
    q0 = qi * tq
    T = nb * MOBA_BLOCK

    @pl.when(qi == 0)
    def _():
        avg = jnp.where((_iota((nb, T), 1) >> _log2(MOBA_BLOCK)) == _iota((nb, T), 0), 1.0 / MOBA_BLOCK, 0.0)
        for h in range(H):
            kmean_scr[h] = _dot_nt(avg, kt_ref[0, h])

    blk = _iota((nb, tq), 0)
    own = (q0 + _iota((nb, tq), 1)) >> _log2(MOBA_BLOCK)
    past = blk < own
    bias_rows = [jnp.zeros((hd, tq), jnp.float32)]
    for h in range(H):
        gate = jnp.where(past, _dot_nt(kmean_scr[h], q_ref[0, h]), NEG_INF)
        allowed = ((_rank_desc(gate) < topk) & past) | (blk == own)
        bias_rows.append(jnp.where(allowed, 0.0, NEG_INF))
    if H * nb < LANES - hd:
        bias_rows.append(jnp.zeros((LANES - hd - H * nb, tq), jnp.float32))
    bias = jnp.concatenate(bias_rows, axis=0).T
    lane = _iota((tq, LANES), 1)
    qs = [jnp.where(lane < hd, q_ref[0, h], bias.astype(q_ref.dtype)) for h in range(H)]
    _flash_init(m_scr, acc_scr)

    def chunk(k0, bias):
        for h in range(H):
            s = jnp.dot(qs[h], kt_ref[0, h, :, pl.ds(k0, tq)], preferred_element_type=jnp.float32)
            if bias is not None:
                s = s + bias
            _flash_update(h, s, v_ref[0, h, pl.ds(k0, tq), :], m_scr, acc_scr)

    def body(c, carry):
        chunk(pl.multiple_of(c * tq, tq), None)
        return carry

    lax.fori_loop(0, qi, body, 0)
    chunk(pl.multiple_of(q0, tq), _causal_bias(tq))
    for h in range(H):
        o_ref[0, :, h * hd:(h + 1) * hd] = _flash_out(h, acc_scr)


def _moba_call(q, kt, v):
    B, H, T, _ = q.shape
    tq = ATT_TILE
    nb = T // MOBA_BLOCK
    qs, ks, vs, out = _att_specs(B, H, T, tq, False)
    return pl.pallas_call(
        functools.partial(_moba_kernel, tq=tq, nb=nb, topk=min(MOBA_TOPK, nb - 1)),
        grid=(B, T // tq),
        in_specs=[qs, ks, vs],
        out_specs=out,
        out_shape=jax.ShapeDtypeStruct((B, T, H * HEAD_DIM), jnp.float32),
        scratch_shapes=[pltpu.VMEM((H, nb, LANES), jnp.float32)] + _att_scratch(tq),
        compiler_params=_params("arbitrary", "arbitrary"),
        name="moba_attention",
    )(q, kt, v)


def _cmp_kernel(k_ref, v_ref, pek_ref, pev_ref, kw1_ref, kw2_ref, vw1_ref, vw2_ref, ko_ref, vo_ref, *, rows):
    half = NSA_CMP_STRIDE * HEAD_DIM

    def compress(t_ref, pe_ref, w1_ref, w2_ref):
        t = t_ref[0].astype(jnp.float32)
        first = _dot(t + pe_ref[0:1, :], w1_ref[0:half, :])
        second = _dot(t + pe_ref[1:2, :], w1_ref[half:, :])
        hid = first + pltpu.roll(second, rows - 1, 0)
        return _dot(_silu(hid), w2_ref[...])

    ko_ref[0] = compress(k_ref, pek_ref, kw1_ref, kw2_ref)
    vo_ref[0] = compress(v_ref, pev_ref, vw1_ref, vw2_ref)


def _cmp_call(kc, vc, pe_k, pe_v, k_w1, k_w2, v_w1, v_w2):
    B, T, d = kc.shape
    rows = T // NSA_CMP_STRIDE
    wide = NSA_CMP_STRIDE * d
    assert NSA_CMP_LEN == 2 * NSA_CMP_STRIDE
    const2 = lambda b: (0, 0)
    blk = pl.BlockSpec((1, rows, wide), lambda b: (b, 0, 0))
    out = pl.BlockSpec((1, rows, d), lambda b: (b, 0, 0))
    return pl.pallas_call(
        functools.partial(_cmp_kernel, rows=rows),
        grid=(B,),
        in_specs=[blk, blk, pl.BlockSpec((2, wide), const2), pl.BlockSpec((2, wide), const2),
                  pl.BlockSpec(k_w1.shape, const2), pl.BlockSpec(k_w2.shape, const2),
                  pl.BlockSpec(v_w1.shape, const2), pl.BlockSpec(v_w2.shape, const2)],
        out_specs=[out, out],
        out_shape=[jax.ShapeDtypeStruct((B, rows, d), jnp.float32)] * 2,
        compiler_params=_params("arbitrary"),
        name="nsa_compress",
    )(kc.reshape(B, rows, wide), vc.reshape(B, rows, wide), pe_k.reshape(2, wide), pe_v.reshape(2, wide),
      k_w1.astype(MXU_DTYPE), k_w2.astype(MXU_DTYPE), v_w1.astype(MXU_DTYPE), v_w2.astype(MXU_DTYPE))


def _nsa_kernel(q_ref, kcmp_ref, vcmp_ref, kst_ref, vs_ref, kwt_ref, vw_ref, g_ref, o_ref,
                oc_scr, os_scr, m_scr, acc_scr, *, tq, ncp, n_sel, topn):
    H, hd = GROUP_HEADS, HEAD_DIM
    qi = pl.program_id(1)
    q0 = qi * tq
    tq_col = q0 + _iota((tq, 1), 0)
    qraw = [q_ref[0, h] for h in range(H)]

    cmp_end = _iota((tq, ncp), 1) * NSA_CMP_STRIDE + (NSA_CMP_LEN - 1)
    m_c = cmp_end <= tq_col
    p_sum = jnp.zeros((tq, ncp), jnp.float32)
    for h in range(H):
        s = jnp.where(m_c, _dot_nt(qraw[h][:, :hd], kcmp_ref[0]), NEG_INF)
        e = jnp.where(m_c, jnp.exp2(s - jnp.max(s, axis=-1, keepdims=True)), 0.0)
        l = jnp.sum(e, axis=-1, keepdims=True)
        p = e / jnp.where(l > 0.0, l, 1.0)
        p_sum = p_sum + p
        oc_scr[h] = _dot(p, vcmp_ref[0])

    cmp_start = _iota((n_sel, ncp), 1) * NSA_CMP_STRIDE
    sel_start = _iota((n_sel, ncp), 0) * NSA_SEL_BLOCK
    overlap = (cmp_start < sel_start + NSA_SEL_BLOCK) & (cmp_start + NSA_CMP_LEN > sel_start)
    imp = _dot_nt(jnp.where(overlap, 1.0, 0.0), p_sum)
    sel_id = _iota((n_sel, tq), 0)
    own = (q0 + _iota((n_sel, tq), 1)) >> _log2(NSA_SEL_BLOCK)
    causal = sel_id <= own
    forced = causal & ((sel_id == 0) | (sel_id >= own - 1))
    imp = jnp.where(forced, NSA_FORCE_SCORE, jnp.where(causal, imp, -NSA_FORCE_SCORE))
    bias_rows = [jnp.zeros((hd, tq), jnp.float32), jnp.where(_rank_desc(imp) < topn, 0.0, NEG_INF)]
    if n_sel < LANES - hd:
        bias_rows.append(jnp.zeros((LANES - hd - n_sel, tq), jnp.float32))
    bias = jnp.concatenate(bias_rows, axis=0).T.astype(qraw[0].dtype)
    lane = _iota((tq, LANES), 1)
    qsel = [jnp.where(lane < hd, qraw[h], bias) for h in range(H)]

    def stack(per_head):
        return jnp.concatenate(per_head, axis=0)

    def tall(bias):
        return jnp.tile(bias, (H, 1))

    q_sel, q_raw = stack(qsel), stack(qraw)
    _flash_init(m_scr, acc_scr)

    def sel_chunk(k0, extra):
        s = jnp.dot(q_sel, kst_ref[0, :, pl.ds(k0, tq)], preferred_element_type=jnp.float32)
        if extra is not None:
            s = s + tall(extra)
        _flash_update(0, s, vs_ref[0, pl.ds(k0, tq), :], m_scr, acc_scr)

    def sel_body(c, carry):
        sel_chunk(pl.multiple_of(c * tq, tq), None)
        return carry

    lax.fori_loop(0, qi, sel_body, 0)
    sel_chunk(pl.multiple_of(q0, tq), _causal_bias(tq))
    o_sel = _flash_out(0, acc_scr)
    for h in range(H):
        os_scr[h] = o_sel[h * tq:(h + 1) * tq]

    _flash_init(m_scr, acc_scr)
    first = jnp.maximum(q0 - NSA_WINDOW + 1, 0) // tq

    def win_body(c, carry):
        k0 = pl.multiple_of(c * tq, tq)
        qpos = q0 + _iota((tq, tq), 0)
        kpos = k0 + _iota((tq, tq), 1)
        band = jnp.where((kpos <= qpos) & (kpos > qpos - NSA_WINDOW), 0.0, NEG_INF)
        s = jnp.dot(q_raw, kwt_ref[0, :, pl.ds(k0, tq)], preferred_element_type=jnp.float32) + tall(band)
        _flash_update(0, s, vw_ref[0, pl.ds(k0, tq), :], m_scr, acc_scr)
        return carry

    lax.fori_loop(first, qi + 1, win_body, 0)
    o_win = _flash_out(0, acc_scr)
    for h in range(H):
        gates = 1.0 / (1.0 + jnp.exp(-g_ref[0][:, 3 * h:3 * h + 3]))
        o_ref[0, :, h * hd:(h + 1) * hd] = (gates[:, 0:1] * oc_scr[h] + gates[:, 1:2] * os_scr[h]
                                            + gates[:, 2:3] * o_win[h * tq:(h + 1) * tq])


def _nsa_call(q, kcmp, vcmp, kst, vs, kwt, vw, gate_logits):
    B, H, T, _ = q.shape
    tq = ATT_TILE
    ncp = kcmp.shape[1]
    n_sel = T // NSA_SEL_BLOCK
    qs, ks, vsp, out = _att_specs(B, H, T, tq, True)
    cmp_spec = pl.BlockSpec((1, ncp, HEAD_DIM), lambda b, i: (b, 0, 0))
    return pl.pallas_call(
        functools.partial(_nsa_kernel, tq=tq, ncp=ncp, n_sel=n_sel, topn=min(NSA_SEL_TOPN, n_sel)),
        grid=(B, T // tq),
        in_specs=[qs, cmp_spec, cmp_spec, ks, vsp, ks, vsp,
                  pl.BlockSpec((1, tq, LANES), lambda b, i: (b, i, 0))],
        out_specs=out,
        out_shape=jax.ShapeDtypeStruct((B, T, H * HEAD_DIM), jnp.float32),
        scratch_shapes=[pltpu.VMEM((H, tq, HEAD_DIM), jnp.float32), pltpu.VMEM((H, tq, HEAD_DIM), jnp.float32)]
        + _att_scratch(tq, stacked=True),
        compiler_params=_params("arbitrary", "arbitrary"),
        name="nsa_attention",
    )(q, kcmp, vcmp, kst, vs, kwt, vw, gate_logits)


def _sortable(x):
    b = int(np.float32(x).view(np.int32))
    return b ^ ((b >> 31) & 0x7FFFFFFF)


def _from_sortable(k):
    return lax.bitcast_convert_type(k ^ ((k >> 31) & 0x7FFFFFFF), jnp.float32)


def _to_sortable(x):
    k = lax.bitcast_convert_type(x, jnp.int32)
    return k ^ ((k >> 31) & 0x7FFFFFFF)


COUNT_ROWS = 64
VALUE_STEPS = 24


def _dsa_kernel(q_ref, kt_ref, v_ref, iq_ref, ikt_ref, iw_ref, o_ref,
                s_scr, t_scr, m_scr, acc_scr, *, tq, topk, idx_scale):
    H, hd = GROUP_HEADS, HEAD_DIM
    qi = pl.program_id(1)
    q0 = qi * tq
    n_kc = qi + 1
    T = s_scr.shape[1]
    reps = tq // LANES

    lane = _iota((tq, LANES), 1)
    quarter = lane >> _log2(DSA_IDX_DIM)
    per_group = LANES // DSA_IDX_DIM
    iq = iq_ref[0]
    iq_all = jnp.concatenate(
        [jnp.where(quarter == (h % per_group), iq[:, (h // per_group) * LANES:(h // per_group + 1) * LANES],
                   jnp.zeros((), iq.dtype)) for h in range(DSA_IDX_HEADS)], axis=0)
    iw = iw_ref[0]
    iwb = [jnp.broadcast_to(iw[:, h:h + 1], (tq, LANES)) for h in range(DSA_IDX_HEADS)]

    def score_body(c, ends):
        top, bot = ends
        k0 = pl.multiple_of(c * tq, tq)
        ikt = ikt_ref[0, :, pl.ds(k0, tq)]
        sh = jnp.dot(iq_all, ikt, preferred_element_type=jnp.float32)
        acc = jnp.zeros((tq, tq), jnp.float32)
        for h in range(DSA_IDX_HEADS):
            acc = acc + _lanes(iwb[h], tq) * jnp.maximum(sh[h * tq:(h + 1) * tq], 0.0)
        qpos = q0 + _iota((tq, tq), 0)
        kpos = k0 + _iota((tq, tq), 1)
        val = acc * idx_scale + 0.0
        sc = jnp.where(kpos <= qpos, val, NEG_INF)
        s_scr[:, pl.ds(k0, tq)] = sc
        low = jnp.where(kpos <= qpos, val, np.inf)
        for r in range(reps):
            top = jnp.maximum(top, sc[:, r * LANES:(r + 1) * LANES])
            bot = jnp.minimum(bot, low[:, r * LANES:(r + 1) * LANES])
        return top, bot

    top, bot = lax.fori_loop(0, n_kc, score_body, (jnp.full((tq, LANES), NEG_INF, jnp.float32),
                                                   jnp.full((tq, LANES), np.inf, jnp.float32)))
    groups = tq // LANES

    def spread(c):
        return jnp.concatenate([jnp.broadcast_to(c[g:g + 1, :], (LANES, LANES)).T for g in range(groups)], axis=0)

    def gather(x):
        return jnp.concatenate([x[g * LANES:(g + 1) * LANES].T[0:1] for g in range(groups)], axis=0)

    def totals(part):
        ones = jnp.ones((8, LANES), jnp.float32)
        return jnp.concatenate([_dot_nt(ones, part[g * LANES:(g + 1) * LANES])[0:1] for g in range(groups)], axis=0)

    row_max = gather(jnp.broadcast_to(jnp.max(top, axis=-1, keepdims=True), (tq, LANES)))
    row_min = gather(jnp.broadcast_to(jnp.min(bot, axis=-1, keepdims=True), (tq, LANES)))

    def count_ge(t, strict=False):
        above = (lambda a, b: a > b) if strict else (lambda a, b: a >= b)
        t_scr[...] = spread(t)
        blocks = [slice(rb * COUNT_ROWS, (rb + 1) * COUNT_ROWS) for rb in range(tq // COUNT_ROWS)]

        def body(c, parts):
            k0 = pl.multiple_of(c * tq, tq)
            out = []
            for rows, part in zip(blocks, parts):
                t_rb = t_scr[rows, :]
                sc = s_scr[rows, pl.ds(k0, tq)]
                for r in range(reps):
                    part = part + jnp.where(above(sc[:, r * LANES:(r + 1) * LANES], t_rb), 1.0, 0.0)
                out.append(part)
            return tuple(out)

        parts = lax.fori_loop(0, n_kc, body, tuple(jnp.zeros((COUNT_ROWS, LANES), jnp.float32) for _ in blocks))
        return totals(jnp.concatenate(parts, axis=0))

    kf = float(topk)
    floor_key = _sortable(NEG_INF)

    def bis_cond(c):
        return c[-1]

    def bis_body(c):
        it, lo, hi, n_lo, _ = c
        active = _any(lo < hi)
        key_mid = (lo | hi) - ((lo ^ hi) >> 1)
        val_mid = _to_sortable(0.5 * (_from_sortable(lo) + _from_sortable(hi)))
        steps = jnp.zeros(lo.shape, jnp.int32) + it
        use_val = (val_mid > lo) & (val_mid <= hi) & (lo > floor_key) & (steps < VALUE_STEPS)
        mid = jnp.where(use_val, val_mid, key_mid)
        cnt = count_ge(_from_sortable(mid))
        ge = cnt >= kf
        lo = jnp.where(ge, mid, lo)
        n_lo = jnp.where(ge, cnt, n_lo)
        hi = jnp.where(cnt == kf, mid, jnp.where(ge, hi, mid - 1))
        return it + 1, lo, hi, n_lo, active

    zeros = jnp.zeros((groups, LANES), jnp.float32)
    n_pos = count_ge(zeros, strict=True)
    n_nonneg = count_ge(zeros)
    n_causal = q0 + _iota((groups, LANES), 0) * LANES + _iota((groups, LANES), 1) + 1
    lo_neg = jnp.where(n_causal >= topk, _to_sortable(row_min), floor_key)
    n_neg = count_ge(_from_sortable(lo_neg))
    is_pos = n_pos >= kf
    is_zero = n_nonneg >= kf
    pick = lambda p, z, n: jnp.where(is_pos, p, jnp.where(is_zero, z, n))
    lo0 = pick(_sortable(np.float32(1e-45)), _sortable(0.0), lo_neg)
    hi0 = pick(_to_sortable(row_max), _sortable(0.0), _sortable(-0.0) - 1)
    _, lo, _, n_ge, _ = lax.while_loop(
        bis_cond, bis_body, (jnp.int32(0), lo0, hi0, pick(n_pos, n_nonneg, n_neg), _any(lo0 < hi0)))
    thr = _lanes(spread(_from_sortable(lo)), tq)

    any_tie = _any(n_ge > kf)

    def causal_at(k0):
        return (k0 + _iota((tq, tq), 1)) <= (q0 + _iota((tq, tq), 0))

    @pl.when(jnp.logical_not(any_tie))
    def _():
        def body(c, carry):
            k0 = pl.multiple_of(c * tq, tq)
            sc = s_scr[:, pl.ds(k0, tq)]
            s_scr[:, pl.ds(k0, tq)] = jnp.where((sc >= thr) & causal_at(k0), 0.0, NEG_INF)
            return carry
        lax.fori_loop(0, n_kc, body, 0)

    @pl.when(any_tie)
    def _():
        need = _lanes(spread(kf - count_ge(_from_sortable(lo), strict=True)), tq)
        prefix = jnp.where(_iota((tq, tq), 0) <= _iota((tq, tq), 1), 1.0, 0.0)
        ones = jnp.ones((tq, LANES), jnp.float32)

        def body(c, seen):
            k0 = pl.multiple_of(c * tq, tq)
            sc = s_scr[:, pl.ds(k0, tq)]
            eq = jnp.where(sc == thr, 1.0, 0.0)
            rank_eq = _dot(eq, prefix) + _lanes(seen, tq)
            picked = (sc > thr) | ((sc == thr) & (rank_eq <= need))
            s_scr[:, pl.ds(k0, tq)] = jnp.where(picked & causal_at(k0), 0.0, NEG_INF)
            return seen + _dot(eq, ones)
        lax.fori_loop(0, n_kc, body, jnp.zeros((tq, LANES), jnp.float32))

    q_all = jnp.concatenate([q_ref[0, h] for h in range(H)], axis=0)
    _flash_init(m_scr, acc_scr)

    def att_body(c, carry):
        k0 = pl.multiple_of(c * tq, tq)
        bias = s_scr[:, pl.ds(k0, tq)]
        s = jnp.dot(q_all, kt_ref[0, :, pl.ds(k0, tq)], preferred_element_type=jnp.float32)
        _flash_update(0, s + jnp.tile(bias, (H, 1)), v_ref[0, pl.ds(k0, tq), :], m_scr, acc_scr)
        return carry

    lax.fori_loop(0, n_kc, att_body, 0)
    out = _flash_out(0, acc_scr)
    for h in range(H):
        o_ref[0, :, h * hd:(h + 1) * hd] = out[h * tq:(h + 1) * tq]


def _dsa_call(q, kt, v, iq, ikt, iw):
    B, H, T, _ = q.shape
    tq = ATT_TILE
    topk = min(DSA_TOPK, T // 4)
    assert tq >= topk
    qs, ks, vs, out = _att_specs(B, H, T, tq, True)
    return pl.pallas_call(
        functools.partial(_dsa_kernel, tq=tq, topk=topk, idx_scale=(DSA_IDX_HEADS * DSA_IDX_DIM) ** -0.5),
        grid=(B, T // tq),
        in_specs=[qs, ks, vs, pl.BlockSpec((1, tq, 2 * LANES), lambda b, i: (b, i, 0)), ks,
                  pl.BlockSpec((1, tq, LANES), lambda b, i: (b, i, 0))],
        out_specs=out,
        out_shape=jax.ShapeDtypeStruct((B, T, H * HEAD_DIM), jnp.float32),
        scratch_shapes=[pltpu.VMEM((tq, T), jnp.float32), pltpu.VMEM((tq, LANES), jnp.float32)]
        + _att_scratch(tq, stacked=True),
        compiler_params=_params("arbitrary", "arbitrary"),
        name="dsa_attention",
    )(q, kt, v, iq, ikt, iw)


def _out_kernel(x_ref, ada_ref, o1_ref, o2_ref, o3_ref, o4_ref, gn_ref, w_ref, y_ref):
    a = ada_ref[0]
    gn = gn_ref[...]
    y = jnp.concatenate([_rms(o[0], gn[i:i + 1]).astype(MXU_DTYPE)
                         for i, o in enumerate((o1_ref, o2_ref, o3_ref, o4_ref))], axis=-1)
    y_ref[...] = x_ref[...] + a[5:6] * jnp.dot(y, w_ref[...], preferred_element_type=jnp.float32)


def _out_call(x2d, ada_l, groups, group_norm, w_out, B, T):
    N, D = x2d.shape
    tm = ROW_TILE
    tpb = T // tm
    grp = pl.BlockSpec((1, tm, GROUP_WIDTH), lambda i: (i // tpb, i % tpb, 0))
    return pl.pallas_call(
        _out_kernel,
        grid=(N // tm,),
        in_specs=[pl.BlockSpec((tm, D), lambda i: (i, 0)),
                  pl.BlockSpec((1, N_ADA, D), lambda i: (i // tpb, 0, 0)),
                  grp, grp, grp, grp,
                  pl.BlockSpec((N_GROUPS, GROUP_WIDTH), lambda i: (0, 0)),
                  pl.BlockSpec((MIX_WIDTH, D), lambda i: (0, 0))],
        out_specs=pl.BlockSpec((tm, D), lambda i: (i, 0)),
        out_shape=jax.ShapeDtypeStruct((N, D), jnp.float32),
        compiler_params=_params("arbitrary"),
        name="mixer_out_proj",
    )(x2d, ada_l, *groups, group_norm, w_out.astype(MXU_DTYPE))


def _mixer_groups(x2d, ada_l, tables, mix_norm, w_in, mla_q_norm, mla_w_uq, mla_kv_norm, mla_w_uk, mla_w_uv,
                  nsa_pe_k, nsa_pe_v, nsa_cmp_k_w1, nsa_cmp_k_w2, nsa_cmp_v_w1, nsa_cmp_v_w2, B, T):
    (mq, mkt, mv, lq, lkt, lv, nq, nkc, nvc, nkst, nvs, nkwt, nvw, ngate,
     dq, dkt, dv, diq, dikt, diw) = _proj_call(
        x2d, ada_l, mix_norm, w_in, tables, mla_q_norm, mla_w_uq, mla_kv_norm, mla_w_uk, mla_w_uv, B, T)
    o_moba = _moba_call(mq, mkt, mv)
    o_mla = _mla_call(lq, lkt, lv)
    kcmp, vcmp = _cmp_call(nkc, nvc, nsa_pe_k, nsa_pe_v, nsa_cmp_k_w1, nsa_cmp_k_w2, nsa_cmp_v_w1, nsa_cmp_v_w2)
    o_nsa = _nsa_call(nq, kcmp, vcmp, nkst, nvs, nkwt, nvw, ngate)
    o_dsa = _dsa_call(dq, dkt, dv, diq, dikt, diw)
    return o_moba, o_mla, o_nsa, o_dsa


def kernel(x, c, ada_w, ada_b, ffn1_norm, ffn1_w_gate, ffn1_w_up, ffn1_w_down, mix_norm, w_in, mla_q_norm, mla_w_uq, mla_kv_norm, mla_w_uk, mla_w_uv, nsa_pe_k, nsa_pe_v, nsa_cmp_k_w1, nsa_cmp_k_w2, nsa_cmp_v_w1, nsa_cmp_v_w2, group_norm, w_out, ffn2_norm, ffn2_w_gate, ffn2_w_up, ffn2_w_down, final_norm):
    B, T, D = x.shape
    L = ada_w.shape[0]
    assert D == D_MODEL and T % ROW_TILE == 0 and T % ATT_TILE == 0
    assert ATT_TILE % MOBA_BLOCK == 0 and ATT_TILE % NSA_SEL_BLOCK == 0 and ATT_TILE >= NSA_WINDOW
    tpb = T // ROW_TILE
    ada = _ada_call(c, ada_w, ada_b)
    tables = _rope_tables(T)
    x2d = x.reshape(B * T, D)
    for l in range(L):
        x2d = _ffn_call(x2d, ada[l], ffn1_norm[l], ffn1_w_gate[l], ffn1_w_up[l], ffn1_w_down[l], 0, tpb)
        groups = _mixer_groups(x2d, ada[l], tables, mix_norm[l], w_in[l], mla_q_norm[l], mla_w_uq[l],
                               mla_kv_norm[l], mla_w_uk[l], mla_w_uv[l], nsa_pe_k[l], nsa_pe_v[l],
                               nsa_cmp_k_w1[l], nsa_cmp_k_w2[l], nsa_cmp_v_w1[l], nsa_cmp_v_w2[l], B, T)
        x2d = _out_call(x2d, ada[l], groups, group_norm[l], w_out[l], B, T)
        x2d = _ffn_call(x2d, ada[l], ffn2_norm[l], ffn2_w_gate[l], ffn2_w_up[l], ffn2_w_down[l], 6, tpb,
                        final_gain=final_norm if l == L - 1 else None)
    return x2d.reshape(B, T, D)
```

```python
import functools
import math

import numpy as np
import jax
import jax.numpy as jnp
from jax import lax
from jax.experimental import pallas as pl
from jax.experimental.pallas import tpu as pltpu

D_MODEL = 1024
N_GROUPS = 4
HEAD_DIM = 64
GROUP_HEADS = D_MODEL // (N_GROUPS * HEAD_DIM)
GROUP_WIDTH = GROUP_HEADS * HEAD_DIM
MIX_WIDTH = N_GROUPS * GROUP_WIDTH
D_FF = 256 * ((8 * D_MODEL + 3 * 256 - 1) // (3 * 256))
N_ADA = 9
FFN_RESIDUAL_WEIGHT = 0.5
ROPE_THETA = 10000.0
RMS_EPS = 1e-6
NEG_INF = -1e30

MOBA_BLOCK = 256
MOBA_TOPK = 3

MLA_Q_LORA = D_MODEL // 4
MLA_KV_LORA = D_MODEL // 8
MLA_NOPE = HEAD_DIM
MLA_ROPE = HEAD_DIM // 2
MLA_V = HEAD_DIM
MLA_QK = MLA_NOPE + MLA_ROPE

NSA_CMP_LEN = 32
NSA_CMP_STRIDE = 16
NSA_CMP_HIDDEN = 4 * HEAD_DIM
NSA_SEL_BLOCK = 64
NSA_SEL_TOPN = 16
NSA_WINDOW = 512
NSA_FORCE_SCORE = 1e4

DSA_TOPK = 256
DSA_IDX_HEADS = 8
DSA_IDX_DIM = 32

IN_NAMES = ("mq", "mk", "mv", "cq", "ckv", "kr", "nq", "nkc", "nvc", "nks", "nvs", "nkw", "nvw",
            "ngate", "dq", "dk", "dv", "diq", "dik", "diw")
IN_SIZES = (
    GROUP_WIDTH, GROUP_WIDTH, GROUP_WIDTH,
    MLA_Q_LORA, MLA_KV_LORA, MLA_ROPE,
    GROUP_WIDTH, HEAD_DIM, HEAD_DIM, HEAD_DIM, HEAD_DIM,
    HEAD_DIM, HEAD_DIM, 3 * GROUP_HEADS,
    GROUP_WIDTH, HEAD_DIM, HEAD_DIM,
    DSA_IDX_HEADS * DSA_IDX_DIM, DSA_IDX_DIM, DSA_IDX_HEADS,
)
N_IN = sum(IN_SIZES)

LANES = 128
MXU_DTYPE = jnp.bfloat16
VMEM_LIMIT = 56 * 1024 * 1024

ATT_TILE = 512
ROW_TILE = 512
MXU_TILE = 256
FF_CHUNK = 6 * MXU_TILE

LOG2E = math.log2(math.e)
M_INIT = -1e29
DEN_LANE = HEAD_DIM


def _params(*semantics):
    return pltpu.CompilerParams(dimension_semantics=semantics, vmem_limit_bytes=VMEM_LIMIT)


def _dot(a, b):
    return jnp.dot(a.astype(MXU_DTYPE), b.astype(MXU_DTYPE), preferred_element_type=jnp.float32)


def _dot_nt(a, b):
    return lax.dot_general(a.astype(MXU_DTYPE), b.astype(MXU_DTYPE), (((1,), (1,)), ((), ())),
                           preferred_element_type=jnp.float32)


def _rms(x, g):
    return x * lax.rsqrt(jnp.mean(x * x, axis=-1, keepdims=True) + RMS_EPS) * g


def _silu(x):
    return x * (1.0 / (1.0 + jnp.exp(-x)))


def _iota(shape, dim):
    return lax.broadcasted_iota(jnp.int32, shape, dim)


def _log2(n):
    assert n & (n - 1) == 0
    return n.bit_length() - 1


def _any(pred):
    return jnp.max(jnp.where(pred, 1.0, 0.0)) > 0.5


def _lanes(x, width):
    return x if width == LANES else jnp.tile(x, (1, width // LANES))


def _ada_kernel(c_ref, w_ref, b_ref, o_ref):
    o_ref[0] = _dot(_silu(c_ref[...]), w_ref[0]) + b_ref[0]


def _ada_call(c, ada_w, ada_b):
    L, D, _ = ada_w.shape
    B = c.shape[0]
    out = pl.pallas_call(
        _ada_kernel,
        grid=(L, N_ADA),
        in_specs=[
            pl.BlockSpec((B, D), lambda l, k: (0, 0)),
            pl.BlockSpec((1, D, D), lambda l, k: (l, 0, k)),
            pl.BlockSpec((1, 1, D), lambda l, k: (l, 0, k)),
        ],
        out_specs=pl.BlockSpec((1, B, D), lambda l, k: (l, 0, k)),
        out_shape=jax.ShapeDtypeStruct((L, B, N_ADA * D), jnp.float32),
        compiler_params=_params("arbitrary", "arbitrary"),
        name="ada_proj",
    )(c, ada_w, ada_b.reshape(L, 1, N_ADA * D))
    return out.reshape(L, B, N_ADA, D)


def _ffn_chunks(F):
    bounds = list(range(0, F, FF_CHUNK)) + [F]
    return list(zip(bounds[:-1], bounds[1:]))


def _ffn_kernel(x_ref, ada_ref, gn_ref, wg_ref, wu_ref, wd_ref, *rest, k0, final):
    if final:
        fg_ref, o_ref = rest
    else:
        (o_ref,) = rest
    a = ada_ref[0]
    h = (_rms(x_ref[...], gn_ref[...]) * (1.0 + a[k0 + 1:k0 + 2]) + a[k0:k0 + 1]).astype(MXU_DTYPE)
    acc = None
    for lo, hi in _ffn_chunks(wg_ref.shape[1]):
        g = jnp.dot(h, wg_ref[:, lo:hi], preferred_element_type=jnp.float32)
        u = jnp.dot(h, wu_ref[:, lo:hi], preferred_element_type=jnp.float32)
        part = jnp.dot((_silu(g) * u).astype(MXU_DTYPE), wd_ref[lo:hi, :], preferred_element_type=jnp.float32)
        acc = part if acc is None else acc + part
    y = x_ref[...] + (FFN_RESIDUAL_WEIGHT * a[k0 + 2:k0 + 3]) * acc
    if final:
        y = _rms(y, fg_ref[...])
    o_ref[...] = y


def _ffn_call(x2d, ada_l, norm_g, w_gate, w_up, w_down, k0, tiles_per_batch, final_gain=None):
    N, D = x2d.shape
    F = w_gate.shape[1]
    tm = ROW_TILE
    in_specs = [
        pl.BlockSpec((tm, D), lambda i: (i, 0)),
        pl.BlockSpec((1, N_ADA, D), lambda i: (i // tiles_per_batch, 0, 0)),
        pl.BlockSpec((1, D), lambda i: (0, 0)),
        pl.BlockSpec((D, F), lambda i: (0, 0)),
        pl.BlockSpec((D, F), lambda i: (0, 0)),
        pl.BlockSpec((F, D), lambda i: (0, 0)),
    ]
    args = [x2d, ada_l, norm_g.reshape(1, D), w_gate.astype(MXU_DTYPE), w_up.astype(MXU_DTYPE),
            w_down.astype(MXU_DTYPE)]
    final = final_gain is not None
    if final:
        in_specs.append(pl.BlockSpec((1, D), lambda i: (0, 0)))
        args.append(final_gain.reshape(1, D))
    return pl.pallas_call(
        functools.partial(_ffn_kernel, k0=k0, final=final),
        grid=(N // tm,),
        in_specs=in_specs,
        out_specs=pl.BlockSpec((tm, D), lambda i: (i, 0)),
        out_shape=jax.ShapeDtypeStruct((N, D), jnp.float32),
        compiler_params=_params("arbitrary"),
        name="ffn",
    )(*args)


G_QMAIN, G_QSWAP, G_KC, G_V, G_MISC, ROW_GROUPS = 0, 6, 12, 14, 18, 25
T_MK, T_MKS, T_SW, T_SWS, T_DK, T_DKS, T_IK, T_ROWS = 0, 256, 512, 640, 768, 896, 1024, 1152


def _swap_halves(c, width):
    return c.reshape(-1, 2, width // 2)[:, ::-1, :].reshape(-1)


def _proj_indices():
    off = dict(zip(IN_NAMES, np.cumsum((0,) + IN_SIZES[:-1]).tolist()))
    size = dict(zip(IN_NAMES, IN_SIZES))
    cols = lambda name: np.arange(off[name], off[name] + size[name])
    zero = lambda n: np.full((n,), N_IN)
    hd = HEAD_DIM

    def head_groups(c):
        return np.concatenate([np.concatenate([c[i:i + hd], zero(LANES - hd)]) for i in range(0, c.size, hd)])

    q = np.concatenate([cols("mq"), cols("nq"), cols("dq")])
    row = np.concatenate([
        q, _swap_halves(q, hd),
        head_groups(cols("nkc")), head_groups(_swap_halves(cols("nkc"), hd)),
        cols("mv"), cols("nvs"), cols("nvw"), cols("dv"), cols("nvc"),
        cols("cq"), cols("ckv"), cols("diq"),
        cols("ngate"), zero(LANES - size["ngate"]), cols("diw"), zero(LANES - size["diw"])])
    assert row.size == ROW_GROUPS * LANES
    sw = np.concatenate([cols("nks"), cols("nkw")])
    dk_main = np.concatenate([cols("dk"), cols("kr"), zero(LANES - hd - MLA_ROPE)])
    dk_swap = np.concatenate([_swap_halves(cols("dk"), hd), _swap_halves(cols("kr"), MLA_ROPE),
                              zero(LANES - hd - MLA_ROPE)])
    tr = np.concatenate([cols("mk"), _swap_halves(cols("mk"), hd), sw, _swap_halves(sw, hd),
                         dk_main, dk_swap, np.tile(cols("dik"), LANES // DSA_IDX_DIM)])
    assert tr.size == T_ROWS
    return row, tr


def _proj_kernel(x_ref, ada_ref, gn_ref, wr_ref, wt_ref, rtab_ref, ttab_ref,
                 qn_ref, wuq_ref, kvn_ref, wukt_ref, wuv_ref,
                 mq_ref, mkt_ref, mv_ref, lq_ref, lkt_ref, lv_ref,
                 nq_ref, nkc_ref, nvc_ref, nkst_ref, nvs_ref, nkwt_ref, nvw_ref, ng_ref,
                 dq_ref, dkt_ref, dv_ref, diq_ref, dikt_ref, diw_ref, *, tm, tpb, n_moba, n_sel):
    H, hd, G = GROUP_HEADS, HEAD_DIM, LANES
    t0 = (pl.program_id(0) % tpb) * tm
    a = ada_ref[0]
    h = (_rms(x_ref[...], gn_ref[...]) * (1.0 + a[4:5]) + a[3:4]).astype(MXU_DTYPE)

    def rows(g0, n):
        return jnp.dot(h, wr_ref[:, g0 * G:(g0 + n) * G], preferred_element_type=jnp.float32)

    def cols(r0, n):
        return _dot_nt(wt_ref[r0:r0 + n, :], h)

    low_half = _iota((tm, G), 1) < hd

    def split(pair, fill):
        return (jnp.where(low_half, pair, fill), jnp.where(low_half, pltpu.roll(pair, hd, 1), fill))

    n_pairs = 3 * H // 2
    roped_q = (rows(G_QMAIN, n_pairs) * _lanes(rtab_ref[0], n_pairs * G)
               + rows(G_QSWAP, n_pairs) * _lanes(rtab_ref[1], n_pairs * G))
    for i, ref in enumerate((mq_ref, nq_ref, dq_ref)):
        for p in range(H // 2):
            g = i * (H // 2) + p
            even, odd = split(roped_q[:, g * G:(g + 1) * G], 0.0)
            ref[0, 2 * p] = even.astype(ref.dtype)
            ref[0, 2 * p + 1] = odd.astype(ref.dtype)
    kc = rows(G_KC, 2)
    nkc_ref[0] = (kc[:, :G] * rtab_ref[2] + kc[:, G:] * rtab_ref[3])[:, :hd].astype(nkc_ref.dtype)

    v = rows(G_V, H // 2 + 2)
    for p in range(H // 2):
        even, odd = split(v[:, p * G:(p + 1) * G], 1.0)
        mv_ref[0, 2 * p] = even.astype(mv_ref.dtype)
        mv_ref[0, 2 * p + 1] = odd.astype(mv_ref.dtype)
    nvs, nvw = split(v[:, (H // 2) * G:(H // 2 + 1) * G], 1.0)
    nvs_ref[0] = nvs.astype(nvs_ref.dtype)
    nvw_ref[0] = nvw.astype(nvw_ref.dtype)
    dv, nvc = split(v[:, (H // 2 + 1) * G:(H // 2 + 2) * G], 1.0)
    dv_ref[0] = dv.astype(dv_ref.dtype)
    nvc_ref[0] = nvc[:, :hd].astype(nvc_ref.dtype)
    ones_hi = jnp.where(_iota((1, G), 1) >= DEN_LANE, 1.0, 0.0)

    misc = rows(G_MISC, 7)
    cq = misc[:, :MLA_Q_LORA]
    ckv = misc[:, MLA_Q_LORA:MLA_Q_LORA + MLA_KV_LORA]
    diq_ref[0] = misc[:, 3 * G:5 * G].astype(diq_ref.dtype)
    ng_ref[0] = misc[:, 5 * G:6 * G]
    diw_ref[0] = misc[:, 6 * G:7 * G]

    tok = t0 + _iota((hd, tm), 1)
    rid = _iota((hd, tm), 0)
    oh_moba = jnp.where((rid & (n_moba - 1)) == (tok >> _log2(MOBA_BLOCK)), 1.0, 0.0)
    oh_sel = jnp.where((rid == (tok >> _log2(NSA_SEL_BLOCK))) & (rid < n_sel), 1.0, 0.0)
    zeros_lo = jnp.zeros((hd, tm), jnp.float32)
    ta_c, ta_s, tb_c, tb_s = ttab_ref[0], ttab_ref[1], ttab_ref[2], ttab_ref[3]
    mkt = (cols(T_MK, H * hd) * jnp.tile(ta_c, (H * hd // G, 1))
           + cols(T_MKS, H * hd) * jnp.tile(ta_s, (H * hd // G, 1)))
    for hh in range(H):
        mine = (rid >> _log2(n_moba)) == hh
        mkt_ref[0, hh] = jnp.concatenate([mkt[hh * hd:(hh + 1) * hd], jnp.where(mine, oh_moba, 0.0)],
                                         axis=0).astype(mkt_ref.dtype)
    sw = cols(T_SW, G) * ta_c + cols(T_SWS, G) * ta_s
    nkst_ref[0] = jnp.concatenate([sw[:hd], oh_sel], axis=0).astype(nkst_ref.dtype)
    nkwt_ref[0] = jnp.concatenate([sw[hd:], zeros_lo], axis=0).astype(nkwt_ref.dtype)
    dkr = cols(T_DK, G) * tb_c + cols(T_DKS, G) * tb_s
    rid2 = _iota((G, tm), 0)
    dkt_ref[0] = jnp.where(rid2 < hd, dkr, 0.0).astype(dkt_ref.dtype)
    kpe_rows = jnp.where((rid2 >= MLA_NOPE) & (rid2 < MLA_QK), dkr, 0.0)
    dikt_ref[0] = cols(T_IK, G).astype(dikt_ref.dtype)

    cqn = _rms(cq, qn_ref[...]).astype(MXU_DTYPE)
    lq = (jnp.dot(cqn, wuq_ref[:, :H * G], preferred_element_type=jnp.float32) * _lanes(rtab_ref[4], H * G)
          + jnp.dot(cqn, wuq_ref[:, H * G:], preferred_element_type=jnp.float32) * _lanes(rtab_ref[5], H * G))
    ckvn = _rms(ckv, kvn_ref[...]).astype(MXU_DTYPE)
    knt = _dot_nt(wukt_ref[...], ckvn)
    lv = jnp.dot(ckvn, wuv_ref[...], preferred_element_type=jnp.float32) + _lanes(ones_hi, H * G)
    for hh in range(H):
        lq_ref[0, hh] = lq[:, hh * G:(hh + 1) * G].astype(lq_ref.dtype)
        lkt_ref[0, hh] = (knt[hh * G:(hh + 1) * G] + kpe_rows).astype(lkt_ref.dtype)
        lv_ref[0, hh] = lv[:, hh * G:(hh + 1) * G].astype(lv_ref.dtype)


def _rope_tables(T):
    def cs(dim):
        inv_freq = 1.0 / (ROPE_THETA ** (np.arange(0, dim, 2, dtype=np.float32) / dim))
        ang = jnp.arange(T, dtype=jnp.float32)[:, None] * jnp.asarray(inv_freq, jnp.float32)[None, :]
        cos, sin = jnp.cos(ang), jnp.sin(ang)
        return jnp.concatenate([cos, cos], axis=-1), jnp.concatenate([-sin, sin], axis=-1)

    c64, s64 = cs(HEAD_DIM)
    c32, s32 = cs(MLA_ROPE)
    pad = lambda t, n: jnp.concatenate([t, jnp.zeros((T, n), jnp.float32)], axis=-1)
    sc = HEAD_DIM ** -0.5 * LOG2E
    sl = MLA_QK ** -0.5 * LOG2E
    ones = jnp.ones((T, MLA_NOPE), jnp.float32)
    rest = LANES - MLA_QK
    rtab = jnp.stack([
        jnp.tile(c64 * sc, (1, LANES // HEAD_DIM)), jnp.tile(s64 * sc, (1, LANES // HEAD_DIM)),
        pad(c64, LANES - HEAD_DIM), pad(s64, LANES - HEAD_DIM),
        pad(jnp.concatenate([ones, c32], axis=-1) * sl, rest),
        pad(jnp.concatenate([0.0 * ones, s32], axis=-1) * sl, rest)])
    ttab = jnp.stack([
        jnp.concatenate([c64, c64], axis=-1).T, jnp.concatenate([s64, s64], axis=-1).T,
        pad(jnp.concatenate([c64, c32], axis=-1), rest).T, pad(jnp.concatenate([s64, s32], axis=-1), rest).T])
    return rtab, ttab


def _proj_call(x2d, ada_l, norm_g, w_in, tables, mla_q_norm, mla_w_uq, mla_kv_norm, mla_w_uk, mla_w_uv, B, T):
    N, D = x2d.shape
    H, G = GROUP_HEADS, LANES
    tm = ROW_TILE
    tpb = T // tm
    n_moba, n_sel = T // MOBA_BLOCK, T // NSA_SEL_BLOCK
    assert H * n_moba <= LANES - HEAD_DIM and n_sel <= LANES - HEAD_DIM and n_moba & (n_moba - 1) == 0
    zcol = lambda w: jnp.concatenate([w, jnp.zeros((w.shape[0], 1), w.dtype)], axis=1)
    row_idx, tr_idx = _proj_indices()
    w_ext = zcol(w_in)
    w_row = w_ext[:, row_idx].astype(MXU_DTYPE)
    w_tr = w_ext[:, tr_idx].T.astype(MXU_DTYPE)
    zq = mla_w_uq.shape[1]
    per_head = np.arange(H * MLA_QK).reshape(H, MLA_QK)
    main = np.concatenate([np.concatenate([per_head[i], np.full((G - MLA_QK,), zq)]) for i in range(H)])
    part = np.concatenate([np.concatenate([np.full((MLA_NOPE,), zq), _swap_halves(per_head[i, MLA_NOPE:], MLA_ROPE),
                                           np.full((G - MLA_QK,), zq)]) for i in range(H)])
    wuq = zcol(mla_w_uq)[:, np.concatenate([main, part])].astype(MXU_DTYPE)
    zv = mla_w_uk.shape[1]
    grp = np.concatenate([np.concatenate([np.arange(i * HEAD_DIM, (i + 1) * HEAD_DIM), np.full((G - HEAD_DIM,), zv)])
                          for i in range(H)])
    wukt = zcol(mla_w_uk)[:, grp].T.astype(MXU_DTYPE)
    wuv = zcol(mla_w_uv)[:, grp].astype(MXU_DTYPE)
    rtab, ttab = tables

    row = lambda i: (i, 0)
    const2 = lambda i: (0, 0)
    in_specs = [
        pl.BlockSpec((tm, D), row),
        pl.BlockSpec((1, N_ADA, D), lambda i: (i // tpb, 0, 0)),
        pl.BlockSpec((1, D), const2),
        pl.BlockSpec(w_row.shape, const2),
        pl.BlockSpec(w_tr.shape, const2),
        pl.BlockSpec((6, tm, G), lambda i: (0, i % tpb, 0)),
        pl.BlockSpec((4, G, tm), lambda i: (0, 0, i % tpb)),
        pl.BlockSpec((1, MLA_Q_LORA), const2),
        pl.BlockSpec(wuq.shape, const2),
        pl.BlockSpec((1, MLA_KV_LORA), const2),
        pl.BlockSpec(wukt.shape, const2),
        pl.BlockSpec(wuv.shape, const2),
    ]
    dt = MXU_DTYPE
    hq = (jax.ShapeDtypeStruct((B, H, T, G), dt), pl.BlockSpec((1, H, tm, G), lambda i: (i // tpb, 0, i % tpb, 0)))
    hkt = (jax.ShapeDtypeStruct((B, H, G, T), dt), pl.BlockSpec((1, H, G, tm), lambda i: (i // tpb, 0, 0, i % tpb)))
    srow = lambda d, t=dt: (jax.ShapeDtypeStruct((B, T, d), t), pl.BlockSpec((1, tm, d), lambda i: (i // tpb, i % tpb, 0)))
    skt = (jax.ShapeDtypeStruct((B, G, T), dt), pl.BlockSpec((1, G, tm), lambda i: (i // tpb, 0, i % tpb)))
    outs = [hq, hkt, hq,
            hq, hkt, hq,
            hq, srow(HEAD_DIM), srow(HEAD_DIM), skt, srow(G), skt, srow(G), srow(G, jnp.float32),
            hq, skt, srow(G), srow(2 * G), skt, srow(G, jnp.float32)]
    return pl.pallas_call(
        functools.partial(_proj_kernel, tm=tm, tpb=tpb, n_moba=n_moba, n_sel=n_sel),
        grid=(N // tm,),
        in_specs=in_specs,
        out_specs=[o[1] for o in outs],
        out_shape=[o[0] for o in outs],
        compiler_params=_params("arbitrary"),
        name="mixer_in_proj",
    )(x2d, ada_l, norm_g.reshape(1, D), w_row, w_tr, rtab, ttab,
      mla_q_norm.reshape(1, -1), wuq, mla_kv_norm.reshape(1, -1), wukt, wuv)


def _flash_init(m_scr, acc_scr):
    m_scr[...] = jnp.full(m_scr.shape, M_INIT, jnp.float32)
    acc_scr[...] = jnp.zeros_like(acc_scr)


def _flash_update(h, s, v, m_scr, acc_scr):
    m_prev = m_scr[h]
    m_new = jnp.maximum(m_prev, jnp.max(s, axis=-1, keepdims=True))
    p = jnp.exp2(s - _lanes(m_new, s.shape[1]))
    acc_scr[h] = jnp.exp2(m_prev - m_new) * acc_scr[h] + _dot(p, v)
    m_scr[h] = m_new


def _flash_out(h, acc_scr):
    acc = acc_scr[h]
    den = acc[:, DEN_LANE:DEN_LANE + 1]
    return acc[:, :HEAD_DIM] / jnp.where(den > 0.0, den, 1.0)


def _causal_bias(t):
    return jnp.where(_iota((t, t), 1) <= _iota((t, t), 0), 0.0, NEG_INF)


def _rank_desc(x):
    n = x.shape[0]
    row = _iota(x.shape, 0)
    rank = jnp.zeros(x.shape, jnp.float32)
    for j in range(n):
        cand = x[j:j + 1, :]
        rank = rank + jnp.where(cand > x, 1.0, jnp.where((cand == x) & (row > j), 1.0, 0.0))
    return rank


def _att_scratch(tq, stacked=False):
    shape = (1, GROUP_HEADS * tq, LANES) if stacked else (GROUP_HEADS, tq, LANES)
    return [pltpu.VMEM(shape, jnp.float32), pltpu.VMEM(shape, jnp.float32)]


def _att_specs(B, H, T, tq, shared_kv):
    q = pl.BlockSpec((1, H, tq, LANES), lambda b, i: (b, 0, i, 0))
    if shared_kv:
        kt = pl.BlockSpec((1, LANES, T), lambda b, i: (b, 0, 0))
        v = pl.BlockSpec((1, T, LANES), lambda b, i: (b, 0, 0))
    else:
        kt = pl.BlockSpec((1, H, LANES, T), lambda b, i: (b, 0, 0, 0))
        v = pl.BlockSpec((1, H, T, LANES), lambda b, i: (b, 0, 0, 0))
    out = pl.BlockSpec((1, tq, H * HEAD_DIM), lambda b, i: (b, i, 0))
    return q, kt, v, out


def _mla_kernel(q_ref, kt_ref, v_ref, o_ref, m_scr, acc_scr, *, tq):
    H = GROUP_HEADS
    qi = pl.program_id(1)
    qs = [q_ref[0, h] for h in range(H)]
    _flash_init(m_scr, acc_scr)

    def chunk(k0, bias):
        for h in range(H):
            s = jnp.dot(qs[h], kt_ref[0, h, :, pl.ds(k0, tq)], preferred_element_type=jnp.float32)
            if bias is not None:
                s = s + bias
            _flash_update(h, s, v_ref[0, h, pl.ds(k0, tq), :], m_scr, acc_scr)

    def body(c, carry):
        chunk(pl.multiple_of(c * tq, tq), None)
        return carry

    lax.fori_loop(0, qi, body, 0)
    chunk(pl.multiple_of(qi * tq, tq), _causal_bias(tq))
    for h in range(H):
        o_ref[0, :, h * HEAD_DIM:(h + 1) * HEAD_DIM] = _flash_out(h, acc_scr)


def _mla_call(q, kt, v):
    B, H, T, _ = q.shape
    tq = ATT_TILE
    qs, ks, vs, out = _att_specs(B, H, T, tq, False)
    return pl.pallas_call(
        functools.partial(_mla_kernel, tq=tq),
        grid=(B, T // tq),
        in_specs=[qs, ks, vs],
        out_specs=out,
        out_shape=jax.ShapeDtypeStruct((B, T, H * HEAD_DIM), jnp.float32),
        scratch_shapes=_att_scratch(tq),
        compiler_params=_params("arbitrary", "arbitrary"),
        name="mla_attention",
    )(q, kt, v)


def _moba_kernel(q_ref, kt_ref, v_ref, o_ref, kmean_scr, m_scr, acc_scr, *, tq, nb, topk):
    H, hd = GROUP_HEADS, HEAD_DIM
    qi = pl.program_id(1)
    q0 = qi * tq
    T = nb * MOBA_BLOCK

    @pl.when(qi == 0)
    def _():
        avg = jnp.where((_iota((nb, T), 1) >> _log2(MOBA_BLOCK)) == _iota((nb, T), 0), 1.0 / MOBA_BLOCK, 0.0)
        for h in range(H):
            kmean_scr[h] = _dot_nt(avg, kt_ref[0, h])

    blk = _iota((nb, tq), 0)
    own = (q0 + _iota((nb, tq), 1)) >> _log2(MOBA_BLOCK)
    past = blk < own
    bias_rows = [jnp.zeros((hd, tq), jnp.float32)]
    for h in range(H):
        gate = jnp.where(past, _dot_nt(kmean_scr[h], q_ref[0, h]), NEG_INF)
        allowed = ((_rank_desc(gate) < topk) & past) | (blk == own)
        bias_rows.append(jnp.where(allowed, 0.0, NEG_INF))
    if H * nb < LANES - hd:
        bias_rows.append(jnp.zeros((LANES - hd - H * nb, tq), jnp.float32))
    bias = jnp.concatenate(bias_rows, axis=0).T
    lane = _iota((tq, LANES), 1)
    qs = [jnp.where(lane < hd, q_ref[0, h], bias.astype(q_ref.dtype)) for h in range(H)]
    _flash_init(m_scr, acc_scr)

    def chunk(k0, bias):
        for h in range(H):
            s = jnp.dot(qs[h], kt_ref[0, h, :, pl.ds(k0, tq)], preferred_element_type=jnp.float32)
            if bias is not None:
                s = s + bias
            _flash_update(h, s, v_ref[0, h, pl.ds(k0, tq), :], m_scr, acc_scr)

    def body(c, carry):
        chunk(pl.multiple_of(c * tq, tq), None)
        return carry

    lax.fori_loop(0, qi, body, 0)
    chunk(pl.multiple_of(q0, tq), _causal_bias(tq))
    for h in range(H):
        o_ref[0, :, h * hd:(h + 1) * hd] = _flash_out(h, acc_scr)


def _moba_call(q, kt, v):
    B, H, T, _ = q.shape
    tq = ATT_TILE
    nb = T // MOBA_BLOCK
    qs, ks, vs, out = _att_specs(B, H, T, tq, False)
    return pl.pallas_call(
        functools.partial(_moba_kernel, tq=tq, nb=nb, topk=min(MOBA_TOPK, nb - 1)),
        grid=(B, T // tq),
        in_specs=[qs, ks, vs],
        out_specs=out,
        out_shape=jax.ShapeDtypeStruct((B, T, H * HEAD_DIM), jnp.float32),
        scratch_shapes=[pltpu.VMEM((H, nb, LANES), jnp.float32)] + _att_scratch(tq),
        compiler_params=_params("arbitrary", "arbitrary"),
        name="moba_attention",
    )(q, kt, v)


def _cmp_kernel(k_ref, v_ref, pek_ref, pev_ref, kw1_ref, kw2_ref, vw1_ref, vw2_ref, ko_ref, vo_ref, *, rows):
    half = NSA_CMP_STRIDE * HEAD_DIM

    def compress(t_ref, pe_ref, w1_ref, w2_ref):
        t = t_ref[0].astype(jnp.float32)
        first = _dot(t + pe_ref[0:1, :], w1_ref[0:half, :])
        second = _dot(t + pe_ref[1:2, :], w1_ref[half:, :])
        hid = first + pltpu.roll(second, rows - 1, 0)
        return _dot(_silu(hid), w2_ref[...])

    ko_ref[0] = compress(k_ref, pek_ref, kw1_ref, kw2_ref)
    vo_ref[0] = compress(v_ref, pev_ref, vw1_ref, vw2_ref)


def _cmp_call(kc, vc, pe_k, pe_v, k_w1, k_w2, v_w1, v_w2):
    B, T, d = kc.shape
    rows = T // NSA_CMP_STRIDE
    wide = NSA_CMP_STRIDE * d
    assert NSA_CMP_LEN == 2 * NSA_CMP_STRIDE
    const2 = lambda b: (0, 0)
    blk = pl.BlockSpec((1, rows, wide), lambda b: (b, 0, 0))
    out = pl.BlockSpec((1, rows, d), lambda b: (b, 0, 0))
    return pl.pallas_call(
        functools.partial(_cmp_kernel, rows=rows),
        grid=(B,),
        in_specs=[blk, blk, pl.BlockSpec((2, wide), const2), pl.BlockSpec((2, wide), const2),
                  pl.BlockSpec(k_w1.shape, const2), pl.BlockSpec(k_w2.shape, const2),
                  pl.BlockSpec(v_w1.shape, const2), pl.BlockSpec(v_w2.shape, const2)],
        out_specs=[out, out],
        out_shape=[jax.ShapeDtypeStruct((B, rows, d), jnp.float32)] * 2,
        compiler_params=_params("arbitrary"),
        name="nsa_compress",
    )(kc.reshape(B, rows, wide), vc.reshape(B, rows, wide), pe_k.reshape(2, wide), pe_v.reshape(2, wide),
      k_w1.astype(MXU_DTYPE), k_w2.astype(MXU_DTYPE), v_w1.astype(MXU_DTYPE), v_w2.astype(MXU_DTYPE))


def _nsa_kernel(q_ref, kcmp_ref, vcmp_ref, kst_ref, vs_ref, kwt_ref, vw_ref, g_ref, o_ref,
                oc_scr, os_scr, m_scr, acc_scr, *, tq, ncp, n_sel, topn):
    H, hd = GROUP_HEADS, HEAD_DIM
    qi = pl.program_id(1)
    q0 = qi * tq
    tq_col = q0 + _iota((tq, 1), 0)
    qraw = [q_ref[0, h] for h in range(H)]

    cmp_end = _iota((tq, ncp), 1) * NSA_CMP_STRIDE + (NSA_CMP_LEN - 1)
    m_c = cmp_end <= tq_col
    p_sum = jnp.zeros((tq, ncp), jnp.float32)
    for h in range(H):
        s = jnp.where(m_c, _dot_nt(qraw[h][:, :hd], kcmp_ref[0]), NEG_INF)
        e = jnp.where(m_c, jnp.exp2(s - jnp.max(s, axis=-1, keepdims=True)), 0.0)
        l = jnp.sum(e, axis=-1, keepdims=True)
        p = e / jnp.where(l > 0.0, l, 1.0)
        p_sum = p_sum + p
        oc_scr[h] = _dot(p, vcmp_ref[0])

    cmp_start = _iota((n_sel, ncp), 1) * NSA_CMP_STRIDE
    sel_start = _iota((n_sel, ncp), 0) * NSA_SEL_BLOCK
    overlap = (cmp_start < sel_start + NSA_SEL_BLOCK) & (cmp_start + NSA_CMP_LEN > sel_start)
    imp = _dot_nt(jnp.where(overlap, 1.0, 0.0), p_sum)
    sel_id = _iota((n_sel, tq), 0)
    own = (q0 + _iota((n_sel, tq), 1)) >> _log2(NSA_SEL_BLOCK)
    causal = sel_id <= own
    forced = causal & ((sel_id == 0) | (sel_id >= own - 1))
    imp = jnp.where(forced, NSA_FORCE_SCORE, jnp.where(causal, imp, -NSA_FORCE_SCORE))
    bias_rows = [jnp.zeros((hd, tq), jnp.float32), jnp.where(_rank_desc(imp) < topn, 0.0, NEG_INF)]
    if n_sel < LANES - hd:
        bias_rows.append(jnp.zeros((LANES - hd - n_sel, tq), jnp.float32))
    bias = jnp.concatenate(bias_rows, axis=0).T.astype(qraw[0].dtype)
    lane = _iota((tq, LANES), 1)
    qsel = [jnp.where(lane < hd, qraw[h], bias) for h in range(H)]

    def stack(per_head):
        return jnp.concatenate(per_head, axis=0)

    def tall(bias):
        return jnp.tile(bias, (H, 1))

    q_sel, q_raw = stack(qsel), stack(qraw)
    _flash_init(m_scr, acc_scr)

    def sel_chunk(k0, extra):
        s = jnp.dot(q_sel, kst_ref[0, :, pl.ds(k0, tq)], preferred_element_type=jnp.float32)
        if extra is not None:
            s = s + tall(extra)
        _flash_update(0, s, vs_ref[0, pl.ds(k0, tq), :], m_scr, acc_scr)

    def sel_body(c, carry):
        sel_chunk(pl.multiple_of(c * tq, tq), None)
        return carry

    lax.fori_loop(0, qi, sel_body, 0)
    sel_chunk(pl.multiple_of(q0, tq), _causal_bias(tq))
    o_sel = _flash_out(0, acc_scr)
    for h in range(H):
        os_scr[h] = o_sel[h * tq:(h + 1) * tq]

    _flash_init(m_scr, acc_scr)
    first = jnp.maximum(q0 - NSA_WINDOW + 1, 0) // tq

    def win_body(c, carry):
        k0 = pl.multiple_of(c * tq, tq)
        qpos = q0 + _iota((tq, tq), 0)
        kpos = k0 + _iota((tq, tq), 1)
        band = jnp.where((kpos <= qpos) & (kpos > qpos - NSA_WINDOW), 0.0, NEG_INF)
        s = jnp.dot(q_raw, kwt_ref[0, :, pl.ds(k0, tq)], preferred_element_type=jnp.float32) + tall(band)
        _flash_update(0, s, vw_ref[0, pl.ds(k0, tq), :], m_scr, acc_scr)
        return carry

    lax.fori_loop(first, qi + 1, win_body, 0)
    o_win = _flash_out(0, acc_scr)
    for h in range(H):
        gates = 1.0 / (1.0 + jnp.exp(-g_ref[0][:, 3 * h:3 * h + 3]))
        o_ref[0, :, h * hd:(h + 1) * hd] = (gates[:, 0:1] * oc_scr[h] + gates[:, 1:2] * os_scr[h]
                                            + gates[:, 2:3] * o_win[h * tq:(h + 1) * tq])


def _nsa_call(q, kcmp, vcmp, kst, vs, kwt, vw, gate_logits):
    B, H, T, _ = q.shape
    tq = ATT_TILE
    ncp = kcmp.shape[1]
    n_sel = T // NSA_SEL_BLOCK
    qs, ks, vsp, out = _att_specs(B, H, T, tq, True)
    cmp_spec = pl.BlockSpec((1, ncp, HEAD_DIM), lambda b, i: (b, 0, 0))
    return pl.pallas_call(
        functools.partial(_nsa_kernel, tq=tq, ncp=ncp, n_sel=n_sel, topn=min(NSA_SEL_TOPN, n_sel)),
        grid=(B, T // tq),
        in_specs=[qs, cmp_spec, cmp_spec, ks, vsp, ks, vsp,
                  pl.BlockSpec((1, tq, LANES), lambda b, i: (b, i, 0))],
        out_specs=out,
        out_shape=jax.ShapeDtypeStruct((B, T, H * HEAD_DIM), jnp.float32),
        scratch_shapes=[pltpu.VMEM((H, tq, HEAD_DIM), jnp.float32), pltpu.VMEM((H, tq, HEAD_DIM), jnp.float32)]
        + _att_scratch(tq, stacked=True),
        compiler_params=_params("arbitrary", "arbitrary"),
        name="nsa_attention",
    )(q, kcmp, vcmp, kst, vs, kwt, vw, gate_logits)


def _sortable(x):
    b = int(np.float32(x).view(np.int32))
    return b ^ ((b >> 31) & 0x7FFFFFFF)


def _from_sortable(k):
    return lax.bitcast_convert_type(k ^ ((k >> 31) & 0x7FFFFFFF), jnp.float32)


def _to_sortable(x):
    k = lax.bitcast_convert_type(x, jnp.int32)
    return k ^ ((k >> 31) & 0x7FFFFFFF)


COUNT_ROWS = 64
VALUE_STEPS = 24


def _dsa_kernel(q_ref, kt_ref, v_ref, iq_ref, ikt_ref, iw_ref, o_ref,
                s_scr, t_scr, m_scr, acc_scr, *, tq, topk, idx_scale):
    H, hd = GROUP_HEADS, HEAD_DIM
    qi = pl.program_id(1)
    q0 = qi * tq
    n_kc = qi + 1
    T = s_scr.shape[1]
    reps = tq // LANES

    lane = _iota((tq, LANES), 1)
    quarter = lane >> _log2(DSA_IDX_DIM)
    per_group = LANES // DSA_IDX_DIM
    iq = iq_ref[0]
    iq_all = jnp.concatenate(
        [jnp.where(quarter == (h % per_group), iq[:, (h // per_group) * LANES:(h // per_group + 1) * LANES],
                   jnp.zeros((), iq.dtype)) for h in range(DSA_IDX_HEADS)], axis=0)
    iw = iw_ref[0]
    iwb = [jnp.broadcast_to(iw[:, h:h + 1], (tq, LANES)) for h in range(DSA_IDX_HEADS)]

    def score_body(c, ends):
        top, bot = ends
        k0 = pl.multiple_of(c * tq, tq)
        ikt = ikt_ref[0, :, pl.ds(k0, tq)]
        sh = jnp.dot(iq_all, ikt, preferred_element_type=jnp.float32)
        acc = jnp.zeros((tq, tq), jnp.float32)
        for h in range(DSA_IDX_HEADS):
            acc = acc + _lanes(iwb[h], tq) * jnp.maximum(sh[h * tq:(h + 1) * tq], 0.0)
        qpos = q0 + _iota((tq, tq), 0)
        kpos = k0 + _iota((tq, tq), 1)
        val = acc * idx_scale + 0.0
        sc = jnp.where(kpos <= qpos, val, NEG_INF)
        s_scr[:, pl.ds(k0, tq)] = sc
        low = jnp.where(kpos <= qpos, val, np.inf)
        for r in range(reps):
            top = jnp.maximum(top, sc[:, r * LANES:(r + 1) * LANES])
            bot = jnp.minimum(bot, low[:, r * LANES:(r + 1) * LANES])
        return top, bot

    top, bot = lax.fori_loop(0, n_kc, score_body, (jnp.full((tq, LANES), NEG_INF, jnp.float32),
                                                   jnp.full((tq, LANES), np.inf, jnp.float32)))
    groups = tq // LANES

    def spread(c):
        return jnp.concatenate([jnp.broadcast_to(c[g:g + 1, :], (LANES, LANES)).T for g in range(groups)], axis=0)

    def gather(x):
        return jnp.concatenate([x[g * LANES:(g + 1) * LANES].T[0:1] for g in range(groups)], axis=0)

    def totals(part):
        ones = jnp.ones((8, LANES), jnp.float32)
        return jnp.concatenate([_dot_nt(ones, part[g * LANES:(g + 1) * LANES])[0:1] for g in range(groups)], axis=0)

    row_max = gather(jnp.broadcast_to(jnp.max(top, axis=-1, keepdims=True), (tq, LANES)))
    row_min = gather(jnp.broadcast_to(jnp.min(bot, axis=-1, keepdims=True), (tq, LANES)))

    def count_ge(t, strict=False):
        above = (lambda a, b: a > b) if strict else (lambda a, b: a >= b)
        t_scr[...] = spread(t)
        blocks = [slice(rb * COUNT_ROWS, (rb + 1) * COUNT_ROWS) for rb in range(tq // COUNT_ROWS)]

        def body(c, parts):
            k0 = pl.multiple_of(c * tq, tq)
            out = []
            for rows, part in zip(blocks, parts):
                t_rb = t_scr[rows, :]
                sc = s_scr[rows, pl.ds(k0, tq)]
                for r in range(reps):
                    part = part + jnp.where(above(sc[:, r * LANES:(r + 1) * LANES], t_rb), 1.0, 0.0)
                out.append(part)
            return tuple(out)

        parts = lax.fori_loop(0, n_kc, body, tuple(jnp.zeros((COUNT_ROWS, LANES), jnp.float32) for _ in blocks))
        return totals(jnp.concatenate(parts, axis=0))

    kf = float(topk)
    floor_key = _sortable(NEG_INF)

    def bis_cond(c):
        return c[-1]

    def bis_body(c):
        it, lo, hi, n_lo, _ = c
        active = _any(lo < hi)
        key_mid = (lo | hi) - ((lo ^ hi) >> 1)
        val_mid = _to_sortable(0.5 * (_from_sortable(lo) + _from_sortable(hi)))
        steps = jnp.zeros(lo.shape, jnp.int32) + it
        use_val = (val_mid > lo) & (val_mid <= hi) & (lo > floor_key) & (steps < VALUE_STEPS)
        mid = jnp.where(use_val, val_mid, key_mid)
        cnt = count_ge(_from_sortable(mid))
        ge = cnt >= kf
        lo = jnp.where(ge, mid, lo)
        n_lo = jnp.where(ge, cnt, n_lo)
        hi = jnp.where(cnt == kf, mid, jnp.where(ge, hi, mid - 1))
        return it + 1, lo, hi, n_lo, active

    def count_signs():
        blocks = [slice(rb * COUNT_ROWS, (rb + 1) * COUNT_ROWS) for rb in range(tq // COUNT_ROWS)]

        def body(c, parts):
            k0 = pl.multiple_of(c * tq, tq)
            out = []
            for rows, (pos, nonneg) in zip(blocks, parts):
                sc = s_scr[rows, pl.ds(k0, tq)]
                for r in range(reps):
                    slab = sc[:, r * LANES:(r + 1) * LANES]
                    pos = pos + jnp.where(slab > 0.0, 1.0, 0.0)
                    nonneg = nonneg + jnp.where(slab >= 0.0, 1.0, 0.0)
                out.append((pos, nonneg))
            return tuple(out)

        zero = jnp.zeros((COUNT_ROWS, LANES), jnp.float32)
        parts = lax.fori_loop(0, n_kc, body, tuple((zero, zero) for _ in blocks))
        return (totals(jnp.concatenate([p for p, _ in parts], axis=0)),
                totals(jnp.concatenate([n for _, n in parts], axis=0)))

    n_pos, n_nonneg = count_signs()
    n_causal = q0 + _iota((groups, LANES), 0) * LANES + _iota((groups, LANES), 1) + 1
    lo_neg = jnp.where(n_causal >= topk, _to_sortable(row_min), floor_key)
    n_neg = jnp.where(n_causal >= topk, n_causal, n_kc * tq).astype(jnp.float32)
    is_pos = n_pos >= kf
    is_zero = n_nonneg >= kf
    pick = lambda p, z, n: jnp.where(is_pos, p, jnp.where(is_zero, z, n))
    lo0 = pick(_sortable(np.float32(1e-45)), _sortable(0.0), lo_neg)
    hi0 = pick(_to_sortable(row_max), _sortable(0.0), _sortable(-0.0) - 1)
    _, lo, _, n_ge, _ = lax.while_loop(
        bis_cond, bis_body, (jnp.int32(0), lo0, hi0, pick(n_pos, n_nonneg, n_neg), _any(lo0 < hi0)))
    thr = _lanes(spread(_from_sortable(lo)), tq)

    any_tie = _any(n_ge > kf)

    def causal_at(k0):
        return (k0 + _iota((tq, tq), 1)) <= (q0 + _iota((tq, tq), 0))

    @pl.when(jnp.logical_not(any_tie))
    def _():
        def body(c, carry):
            k0 = pl.multiple_of(c * tq, tq)
            sc = s_scr[:, pl.ds(k0, tq)]
            s_scr[:, pl.ds(k0, tq)] = jnp.where((sc >= thr) & causal_at(k0), 0.0, NEG_INF)
            return carry
        lax.fori_loop(0, n_kc, body, 0)

    @pl.when(any_tie)
    def _():
        need = _lanes(spread(kf - count_ge(_from_sortable(lo), strict=True)), tq)
        prefix = jnp.where(_iota((tq, tq), 0) <= _iota((tq, tq), 1), 1.0, 0.0)
        ones = jnp.ones((tq, LANES), jnp.float32)

        def body(c, seen):
            k0 = pl.multiple_of(c * tq, tq)
            sc = s_scr[:, pl.ds(k0, tq)]
            eq = jnp.where(sc == thr, 1.0, 0.0)
            rank_eq = _dot(eq, prefix) + _lanes(seen, tq)
            picked = (sc > thr) | ((sc == thr) & (rank_eq <= need))
            s_scr[:, pl.ds(k0, tq)] = jnp.where(picked & causal_at(k0), 0.0, NEG_INF)
            return seen + _dot(eq, ones)
        lax.fori_loop(0, n_kc, body, jnp.zeros((tq, LANES), jnp.float32))

    q_all = jnp.concatenate([q_ref[0, h] for h in range(H)], axis=0)
    _flash_init(m_scr, acc_scr)

    def att_body(c, carry):
        k0 = pl.multiple_of(c * tq, tq)
        bias = s_scr[:, pl.ds(k0, tq)]
        s = jnp.dot(q_all, kt_ref[0, :, pl.ds(k0, tq)], preferred_element_type=jnp.float32)
        _flash_update(0, s + jnp.tile(bias, (H, 1)), v_ref[0, pl.ds(k0, tq), :], m_scr, acc_scr)
        return carry

    lax.fori_loop(0, n_kc, att_body, 0)
    out = _flash_out(0, acc_scr)
    for h in range(H):
        o_ref[0, :, h * hd:(h + 1) * hd] = out[h * tq:(h + 1) * tq]


def _dsa_call(q, kt, v, iq, ikt, iw):
    B, H, T, _ = q.shape
    tq = ATT_TILE
    topk = min(DSA_TOPK, T // 4)
    assert tq >= topk
    qs, ks, vs, out = _att_specs(B, H, T, tq, True)
    return pl.pallas_call(
        functools.partial(_dsa_kernel, tq=tq, topk=topk, idx_scale=(DSA_IDX_HEADS * DSA_IDX_DIM) ** -0.5),
        grid=(B, T // tq),
        in_specs=[qs, ks, vs, pl.BlockSpec((1, tq, 2 * LANES), lambda b, i: (b, i, 0)), ks,
                  pl.BlockSpec((1, tq, LANES), lambda b, i: (b, i, 0))],
        out_specs=out,
        out_shape=jax.ShapeDtypeStruct((B, T, H * HEAD_DIM), jnp.float32),
        scratch_shapes=[pltpu.VMEM((tq, T), jnp.float32), pltpu.VMEM((tq, LANES), jnp.float32)]
        + _att_scratch(tq, stacked=True),
        compiler_params=_params("arbitrary", "arbitrary"),
        name="dsa_attention",
    )(q, kt, v, iq, ikt, iw)


def _out_kernel(x_ref, ada_ref, o1_ref, o2_ref, o3_ref, o4_ref, gn_ref, w_ref, y_ref):
    a = ada_ref[0]
    gn = gn_ref[...]
    y = jnp.concatenate([_rms(o[0], gn[i:i + 1]).astype(MXU_DTYPE)
                         for i, o in enumerate((o1_ref, o2_ref, o3_ref, o4_ref))], axis=-1)
    y_ref[...] = x_ref[...] + a[5:6] * jnp.dot(y, w_ref[...], preferred_element_type=jnp.float32)


def _out_call(x2d, ada_l, groups, group_norm, w_out, B, T):
    N, D = x2d.shape
    tm = ROW_TILE
    tpb = T // tm
    grp = pl.BlockSpec((1, tm, GROUP_WIDTH), lambda i: (i // tpb, i % tpb, 0))
    return pl.pallas_call(
        _out_kernel,
        grid=(N // tm,),
        in_specs=[pl.BlockSpec((tm, D), lambda i: (i, 0)),
                  pl.BlockSpec((1, N_ADA, D), lambda i: (i // tpb, 0, 0)),
                  grp, grp, grp, grp,
                  pl.BlockSpec((N_GROUPS, GROUP_WIDTH), lambda i: (0, 0)),
                  pl.BlockSpec((MIX_WIDTH, D), lambda i: (0, 0))],
        out_specs=pl.BlockSpec((tm, D), lambda i: (i, 0)),
        out_shape=jax.ShapeDtypeStruct((N, D), jnp.float32),
        compiler_params=_params("arbitrary"),
        name="mixer_out_proj",
    )(x2d, ada_l, *groups, group_norm, w_out.astype(MXU_DTYPE))


def _mixer_groups(x2d, ada_l, tables, mix_norm, w_in, mla_q_norm, mla_w_uq, mla_kv_norm, mla_w_uk, mla_w_uv,
                  nsa_pe_k, nsa_pe_v, nsa_cmp_k_w1, nsa_cmp_k_w2, nsa_cmp_v_w1, nsa_cmp_v_w2, B, T):
    (mq, mkt, mv, lq, lkt, lv, nq, nkc, nvc, nkst, nvs, nkwt, nvw, ngate,
     dq, dkt, dv, diq, dikt, diw) = _proj_call(
        x2d, ada_l, mix_norm, w_in, tables, mla_q_norm, mla_w_uq, mla_kv_norm, mla_w_uk, mla_w_uv, B, T)
    o_moba = _moba_call(mq, mkt, mv)
    o_mla = _mla_call(lq, lkt, lv)
    kcmp, vcmp = _cmp_call(nkc, nvc, nsa_pe_k, nsa_pe_v, nsa_cmp_k_w1, nsa_cmp_k_w2, nsa_cmp_v_w1, nsa_cmp_v_w2)
    o_nsa = _nsa_call(nq, kcmp, vcmp, nkst, nvs, nkwt, nvw, ngate)
    o_dsa = _dsa_call(dq, dkt, dv, diq, dikt, diw)
    return o_moba, o_mla, o_nsa, o_dsa


def kernel(x, c, ada_w, ada_b, ffn1_norm, ffn1_w_gate, ffn1_w_up, ffn1_w_down, mix_norm, w_in, mla_q_norm, mla_w_uq, mla_kv_norm, mla_w_uk, mla_w_uv, nsa_pe_k, nsa_pe_v, nsa_cmp_k_w1, nsa_cmp_k_w2, nsa_cmp_v_w1, nsa_cmp_v_w2, group_norm, w_out, ffn2_norm, ffn2_w_gate, ffn2_w_up, ffn2_w_down, final_norm):
    B, T, D = x.shape
    L = ada_w.shape[0]
    assert D == D_MODEL and T % ROW_TILE == 0 and T % ATT_TILE == 0
    assert ATT_TILE % MOBA_BLOCK == 0 and ATT_TILE % NSA_SEL_BLOCK == 0 and ATT_TILE >= NSA_WINDOW
    tpb = T // ROW_TILE
    ada = _ada_call(c, ada_w, ada_b)
    tables = _rope_tables(T)
    x2d = x.reshape(B * T, D)
    for l in range(L):
        x2d = _ffn_call(x2d, ada[l], ffn1_norm[l], ffn1_w_gate[l], ffn1_w_up[l], ffn1_w_down[l], 0, tpb)
        groups = _mixer_groups(x2d, ada[l], tables, mix_norm[l], w_in[l], mla_q_norm[l], mla_w_uq[l],
                               mla_kv_norm[l], mla_w_uk[l], mla_w_uv[l], nsa_pe_k[l], nsa_pe_v[l],
                               nsa_cmp_k_w1[l], nsa_cmp_k_w2[l], nsa_cmp_v_w1[l], nsa_cmp_v_w2[l], B, T)
        x2d = _out_call(x2d, ada[l], groups, group_norm[l], w_out[l], B, T)
        x2d = _ffn_call(x2d, ada[l], ffn2_norm[l], ffn2_w_gate[l], ffn2_w_up[l], ffn2_w_down[l], 6, tpb,
                        final_gain=final_norm if l == L - 1 else None)
    return x2d.reshape(B, T, D)
```

```python
import functools
import math

import numpy as np
import jax
import jax.numpy as jnp
from jax import lax
from jax.experimental import pallas as pl
from jax.experimental.pallas import tpu as pltpu

D_MODEL = 1024
N_GROUPS = 4
HEAD_DIM = 64
GROUP_HEADS = D_MODEL // (N_GROUPS * HEAD_DIM)
GROUP_WIDTH = GROUP_HEADS * HEAD_DIM
MIX_WIDTH = N_GROUPS * GROUP_WIDTH
D_FF = 256 * ((8 * D_MODEL + 3 * 256 - 1) // (3 * 256))
N_ADA = 9
FFN_RESIDUAL_WEIGHT = 0.5
ROPE_THETA = 10000.0
RMS_EPS = 1e-6
NEG_INF = -1e30

MOBA_BLOCK = 256
MOBA_TOPK = 3

MLA_Q_LORA = D_MODEL // 4
MLA_KV_LORA = D_MODEL // 8
MLA_NOPE = HEAD_DIM
MLA_ROPE = HEAD_DIM // 2
MLA_V = HEAD_DIM
MLA_QK = MLA_NOPE + MLA_ROPE

NSA_CMP_LEN = 32
NSA_CMP_STRIDE = 16
NSA_CMP_HIDDEN = 4 * HEAD_DIM
NSA_SEL_BLOCK = 64
NSA_SEL_TOPN = 16
NSA_WINDOW = 512
NSA_FORCE_SCORE = 1e4

DSA_TOPK = 256
DSA_IDX_HEADS = 8
DSA_IDX_DIM = 32

IN_NAMES = ("mq", "mk", "mv", "cq", "ckv", "kr", "nq", "nkc", "nvc", "nks", "nvs", "nkw", "nvw",
            "ngate", "dq", "dk", "dv", "diq", "dik", "diw")
IN_SIZES = (
    GROUP_WIDTH, GROUP_WIDTH, GROUP_WIDTH,
    MLA_Q_LORA, MLA_KV_LORA, MLA_ROPE,
    GROUP_WIDTH, HEAD_DIM, HEAD_DIM, HEAD_DIM, HEAD_DIM,
    HEAD_DIM, HEAD_DIM, 3 * GROUP_HEADS,
    GROUP_WIDTH, HEAD_DIM, HEAD_DIM,
    DSA_IDX_HEADS * DSA_IDX_DIM, DSA_IDX_DIM, DSA_IDX_HEADS,
)
N_IN = sum(IN_SIZES)

LANES = 128
MXU_DTYPE = jnp.bfloat16
VMEM_LIMIT = 56 * 1024 * 1024

ATT_TILE = 512
ROW_TILE = 512
MXU_TILE = 256
FF_CHUNK = 6 * MXU_TILE

LOG2E = math.log2(math.e)
M_INIT = -1e29
DEN_LANE = HEAD_DIM


def _params(*semantics):
    return pltpu.CompilerParams(dimension_semantics=semantics, vmem_limit_bytes=VMEM_LIMIT)


def _dot(a, b):
    return jnp.dot(a.astype(MXU_DTYPE), b.astype(MXU_DTYPE), preferred_element_type=jnp.float32)


def _dot_nt(a, b):
    return lax.dot_general(a.astype(MXU_DTYPE), b.astype(MXU_DTYPE), (((1,), (1,)), ((), ())),
                           preferred_element_type=jnp.float32)


def _rms(x, g):
    return x * lax.rsqrt(jnp.mean(x * x, axis=-1, keepdims=True) + RMS_EPS) * g


def _silu(x):
    return x * (1.0 / (1.0 + jnp.exp(-x)))


def _iota(shape, dim):
    return lax.broadcasted_iota(jnp.int32, shape, dim)


def _log2(n):
    assert n & (n - 1) == 0
    return n.bit_length() - 1


def _any(pred):
    return jnp.max(jnp.where(pred, 1.0, 0.0)) > 0.5


def _lanes(x, width):
    return x if width == LANES else jnp.tile(x, (1, width // LANES))


def _ada_kernel(c_ref, w_ref, b_ref, o_ref):
    o_ref[0] = _dot(_silu(c_ref[...]), w_ref[0]) + b_ref[0]


def _ada_call(c, ada_w, ada_b):
    L, D, _ = ada_w.shape
    B = c.shape[0]
    out = pl.pallas_call(
        _ada_kernel,
        grid=(L, N_ADA),
        in_specs=[
            pl.BlockSpec((B, D), lambda l, k: (0, 0)),
            pl.BlockSpec((1, D, D), lambda l, k: (l, 0, k)),
            pl.BlockSpec((1, 1, D), lambda l, k: (l, 0, k)),
        ],
        out_specs=pl.BlockSpec((1, B, D), lambda l, k: (l, 0, k)),
        out_shape=jax.ShapeDtypeStruct((L, B, N_ADA * D), jnp.float32),
        compiler_params=_params("arbitrary", "arbitrary"),
        name="ada_proj",
    )(c, ada_w, ada_b.reshape(L, 1, N_ADA * D))
    return out.reshape(L, B, N_ADA, D)


def _ffn_chunks(F):
    bounds = list(range(0, F, FF_CHUNK)) + [F]
    return list(zip(bounds[:-1], bounds[1:]))


def _ffn_kernel(x_ref, ada_ref, gn_ref, wg_ref, wu_ref, wd_ref, *rest, k0, final):
    if final:
        fg_ref, o_ref = rest
    else:
        (o_ref,) = rest
    a = ada_ref[0]
    h = (_rms(x_ref[...], gn_ref[...]) * (1.0 + a[k0 + 1:k0 + 2]) + a[k0:k0 + 1]).astype(MXU_DTYPE)
    acc = None
    for lo, hi in _ffn_chunks(wg_ref.shape[1]):
        g = jnp.dot(h, wg_ref[:, lo:hi], preferred_element_type=jnp.float32)
        u = jnp.dot(h, wu_ref[:, lo:hi], preferred_element_type=jnp.float32)
        part = jnp.dot((_silu(g) * u).astype(MXU_DTYPE), wd_ref[lo:hi, :], preferred_element_type=jnp.float32)
        acc = part if acc is None else acc + part
    y = x_ref[...] + (FFN_RESIDUAL_WEIGHT * a[k0 + 2:k0 + 3]) * acc
    if final:
        y = _rms(y, fg_ref[...])
    o_ref[...] = y


def _ffn_call(x2d, ada_l, norm_g, w_gate, w_up, w_down, k0, tiles_per_batch, final_gain=None):
    N, D = x2d.shape
    F = w_gate.shape[1]
    tm = ROW_TILE
    in_specs = [
        pl.BlockSpec((tm, D), lambda i: (i, 0)),
        pl.BlockSpec((1, N_ADA, D), lambda i: (i // tiles_per_batch, 0, 0)),
        pl.BlockSpec((1, D), lambda i: (0, 0)),
        pl.BlockSpec((D, F), lambda i: (0, 0)),
        pl.BlockSpec((D, F), lambda i: (0, 0)),
        pl.BlockSpec((F, D), lambda i: (0, 0)),
    ]
    args = [x2d, ada_l, norm_g.reshape(1, D), w_gate.astype(MXU_DTYPE), w_up.astype(MXU_DTYPE),
            w_down.astype(MXU_DTYPE)]
    final = final_gain is not None
    if final:
        in_specs.append(pl.BlockSpec((1, D), lambda i: (0, 0)))
        args.append(final_gain.reshape(1, D))
    return pl.pallas_call(
        functools.partial(_ffn_kernel, k0=k0, final=final),
        grid=(N // tm,),
        in_specs=in_specs,
        out_specs=pl.BlockSpec((tm, D), lambda i: (i, 0)),
        out_shape=jax.ShapeDtypeStruct((N, D), jnp.float32),
        compiler_params=_params("arbitrary"),
        name="ffn",
    )(*args)


G_QMAIN, G_QSWAP, G_KC, G_V, G_MISC, ROW_GROUPS = 0, 6, 12, 14, 18, 25
T_MK, T_MKS, T_SW, T_SWS, T_DK, T_DKS, T_IK, T_ROWS = 0, 256, 512, 640, 768, 896, 1024, 1152


def _swap_halves(c, width):
    return c.reshape(-1, 2, width // 2)[:, ::-1, :].reshape(-1)


def _proj_indices():
    off = dict(zip(IN_NAMES, np.cumsum((0,) + IN_SIZES[:-1]).tolist()))
    size = dict(zip(IN_NAMES, IN_SIZES))
    cols = lambda name: np.arange(off[name], off[name] + size[name])
    zero = lambda n: np.full((n,), N_IN)
    hd = HEAD_DIM

    def head_groups(c):
        return np.concatenate([np.concatenate([c[i:i + hd], zero(LANES - hd)]) for i in range(0, c.size, hd)])

    q = np.concatenate([cols("mq"), cols("nq"), cols("dq")])
    row = np.concatenate([
        q, _swap_halves(q, hd),
        head_groups(cols("nkc")), head_groups(_swap_halves(cols("nkc"), hd)),
        cols("mv"), cols("nvs"), cols("nvw"), cols("dv"), cols("nvc"),
        cols("cq"), cols("ckv"), cols("diq"),
        cols("ngate"), zero(LANES - size["ngate"]), cols("diw"), zero(LANES - size["diw"])])
    assert row.size == ROW_GROUPS * LANES
    sw = np.concatenate([cols("nks"), cols("nkw")])
    dk_main = np.concatenate([cols("dk"), cols("kr"), zero(LANES - hd - MLA_ROPE)])
    dk_swap = np.concatenate([_swap_halves(cols("dk"), hd), _swap_halves(cols("kr"), MLA_ROPE),
                              zero(LANES - hd - MLA_ROPE)])
    tr = np.concatenate([cols("mk"), _swap_halves(cols("mk"), hd), sw, _swap_halves(sw, hd),
                         dk_main, dk_swap, np.tile(cols("dik"), LANES // DSA_IDX_DIM)])
    assert tr.size == T_ROWS
    return row, tr


def _proj_kernel(x_ref, ada_ref, gn_ref, wr_ref, wt_ref, rtab_ref, ttab_ref,
                 qn_ref, wuq_ref, kvn_ref, wukt_ref, wuv_ref,
                 mq_ref, mkt_ref, mv_ref, lq_ref, lkt_ref, lv_ref,
                 nq_ref, nkc_ref, nvc_ref, nkst_ref, nvs_ref, nkwt_ref, nvw_ref, ng_ref,
                 dq_ref, dkt_ref, dv_ref, diq_ref, dikt_ref, diw_ref, *, tm, tpb, n_moba, n_sel):
    H, hd, G = GROUP_HEADS, HEAD_DIM, LANES
    t0 = (pl.program_id(0) % tpb) * tm
    a = ada_ref[0]
    h = (_rms(x_ref[...], gn_ref[...]) * (1.0 + a[4:5]) + a[3:4]).astype(MXU_DTYPE)

    def rows(g0, n):
        return jnp.dot(h, wr_ref[:, g0 * G:(g0 + n) * G], preferred_element_type=jnp.float32)

    def cols(r0, n):
        return _dot_nt(wt_ref[r0:r0 + n, :], h)

    low_half = _iota((tm, G), 1) < hd

    def split(pair, fill):
        return (jnp.where(low_half, pair, fill), jnp.where(low_half, pltpu.roll(pair, hd, 1), fill))

    n_pairs = 3 * H // 2
    roped_q = (rows(G_QMAIN, n_pairs) * _lanes(rtab_ref[0], n_pairs * G)
               + rows(G_QSWAP, n_pairs) * _lanes(rtab_ref[1], n_pairs * G))
    for i, ref in enumerate((mq_ref, nq_ref, dq_ref)):
        for p in range(H // 2):
            g = i * (H // 2) + p
            even, odd = split(roped_q[:, g * G:(g + 1) * G], 0.0)
            ref[0, 2 * p] = even.astype(ref.dtype)
            ref[0, 2 * p + 1] = odd.astype(ref.dtype)
    kc = rows(G_KC, 2)
    nkc_ref[0] = (kc[:, :G] * rtab_ref[2] + kc[:, G:] * rtab_ref[3])[:, :hd].astype(nkc_ref.dtype)

    v = rows(G_V, H // 2 + 2)
    for p in range(H // 2):
        even, odd = split(v[:, p * G:(p + 1) * G], 1.0)
        mv_ref[0, 2 * p] = even.astype(mv_ref.dtype)
        mv_ref[0, 2 * p + 1] = odd.astype(mv_ref.dtype)
    nvs, nvw = split(v[:, (H // 2) * G:(H // 2 + 1) * G], 1.0)
    nvs_ref[0] = nvs.astype(nvs_ref.dtype)
    nvw_ref[0] = nvw.astype(nvw_ref.dtype)
    dv, nvc = split(v[:, (H // 2 + 1) * G:(H // 2 + 2) * G], 1.0)
    dv_ref[0] = dv.astype(dv_ref.dtype)
    nvc_ref[0] = nvc[:, :hd].astype(nvc_ref.dtype)
    ones_hi = jnp.where(_iota((1, G), 1) >= DEN_LANE, 1.0, 0.0)

    misc = rows(G_MISC, 7)
    cq = misc[:, :MLA_Q_LORA]
    ckv = misc[:, MLA_Q_LORA:MLA_Q_LORA + MLA_KV_LORA]
    diq_ref[0] = misc[:, 3 * G:5 * G].astype(diq_ref.dtype)
    ng_ref[0] = misc[:, 5 * G:6 * G]
    diw_ref[0] = misc[:, 6 * G:7 * G]

    tok = t0 + _iota((hd, tm), 1)
    rid = _iota((hd, tm), 0)
    oh_moba = jnp.where((rid & (n_moba - 1)) == (tok >> _log2(MOBA_BLOCK)), 1.0, 0.0)
    oh_sel = jnp.where((rid == (tok >> _log2(NSA_SEL_BLOCK))) & (rid < n_sel), 1.0, 0.0)
    zeros_lo = jnp.zeros((hd, tm), jnp.float32)
    ta_c, ta_s, tb_c, tb_s = ttab_ref[0], ttab_ref[1], ttab_ref[2], ttab_ref[3]
    mkt = (cols(T_MK, H * hd) * jnp.tile(ta_c, (H * hd // G, 1))
           + cols(T_MKS, H * hd) * jnp.tile(ta_s, (H * hd // G, 1)))
    for hh in range(H):
        mine = (rid >> _log2(n_moba)) == hh
        mkt_ref[0, hh] = jnp.concatenate([mkt[hh * hd:(hh + 1) * hd], jnp.where(mine, oh_moba, 0.0)],
                                         axis=0).astype(mkt_ref.dtype)
    sw = cols(T_SW, G) * ta_c + cols(T_SWS, G) * ta_s
    nkst_ref[0] = jnp.concatenate([sw[:hd], oh_sel], axis=0).astype(nkst_ref.dtype)
    nkwt_ref[0] = jnp.concatenate([sw[hd:], zeros_lo], axis=0).astype(nkwt_ref.dtype)
    dkr = cols(T_DK, G) * tb_c + cols(T_DKS, G) * tb_s
    rid2 = _iota((G, tm), 0)
    dkt_ref[0] = jnp.where(rid2 < hd, dkr, 0.0).astype(dkt_ref.dtype)
    kpe_rows = jnp.where((rid2 >= MLA_NOPE) & (rid2 < MLA_QK), dkr, 0.0)
    dikt_ref[0] = cols(T_IK, G).astype(dikt_ref.dtype)

    cqn = _rms(cq, qn_ref[...]).astype(MXU_DTYPE)
    lq = (jnp.dot(cqn, wuq_ref[:, :H * G], preferred_element_type=jnp.float32) * _lanes(rtab_ref[4], H * G)
          + jnp.dot(cqn, wuq_ref[:, H * G:], preferred_element_type=jnp.float32) * _lanes(rtab_ref[5], H * G))
    ckvn = _rms(ckv, kvn_ref[...]).astype(MXU_DTYPE)
    knt = _dot_nt(wukt_ref[...], ckvn)
    lv = jnp.dot(ckvn, wuv_ref[...], preferred_element_type=jnp.float32) + _lanes(ones_hi, H * G)
    for hh in range(H):
        lq_ref[0, hh] = lq[:, hh * G:(hh + 1) * G].astype(lq_ref.dtype)
        lkt_ref[0, hh] = (knt[hh * G:(hh + 1) * G] + kpe_rows).astype(lkt_ref.dtype)
        lv_ref[0, hh] = lv[:, hh * G:(hh + 1) * G].astype(lv_ref.dtype)


def _rope_tables(T):
    def cs(dim):
        inv_freq = 1.0 / (ROPE_THETA ** (np.arange(0, dim, 2, dtype=np.float32) / dim))
        ang = jnp.arange(T, dtype=jnp.float32)[:, None] * jnp.asarray(inv_freq, jnp.float32)[None, :]
        cos, sin = jnp.cos(ang), jnp.sin(ang)
        return jnp.concatenate([cos, cos], axis=-1), jnp.concatenate([-sin, sin], axis=-1)

    c64, s64 = cs(HEAD_DIM)
    c32, s32 = cs(MLA_ROPE)
    pad = lambda t, n: jnp.concatenate([t, jnp.zeros((T, n), jnp.float32)], axis=-1)
    sc = HEAD_DIM ** -0.5 * LOG2E
    sl = MLA_QK ** -0.5 * LOG2E
    ones = jnp.ones((T, MLA_NOPE), jnp.float32)
    rest = LANES - MLA_QK
    rtab = jnp.stack([
        jnp.tile(c64 * sc, (1, LANES // HEAD_DIM)), jnp.tile(s64 * sc, (1, LANES // HEAD_DIM)),
        pad(c64, LANES - HEAD_DIM), pad(s64, LANES - HEAD_DIM),
        pad(jnp.concatenate([ones, c32], axis=-1) * sl, rest),
        pad(jnp.concatenate([0.0 * ones, s32], axis=-1) * sl, rest)])
    ttab = jnp.stack([
        jnp.concatenate([c64, c64], axis=-1).T, jnp.concatenate([s64, s64], axis=-1).T,
        pad(jnp.concatenate([c64, c32], axis=-1), rest).T, pad(jnp.concatenate([s64, s32], axis=-1), rest).T])
    return rtab, ttab


def _proj_call(x2d, ada_l, norm_g, w_in, tables, mla_q_norm, mla_w_uq, mla_kv_norm, mla_w_uk, mla_w_uv, B, T):
    N, D = x2d.shape
    H, G = GROUP_HEADS, LANES
    tm = ROW_TILE
    tpb = T // tm
    n_moba, n_sel = T // MOBA_BLOCK, T // NSA_SEL_BLOCK
    assert H * n_moba <= LANES - HEAD_DIM and n_sel <= LANES - HEAD_DIM and n_moba & (n_moba - 1) == 0
    zcol = lambda w: jnp.concatenate([w, jnp.zeros((w.shape[0], 1), w.dtype)], axis=1)
    row_idx, tr_idx = _proj_indices()
    w_ext = zcol(w_in)
    w_row = w_ext[:, row_idx].astype(MXU_DTYPE)
    w_tr = w_ext[:, tr_idx].T.astype(MXU_DTYPE)
    zq = mla_w_uq.shape[1]
    per_head = np.arange(H * MLA_QK).reshape(H, MLA_QK)
    main = np.concatenate([np.concatenate([per_head[i], np.full((G - MLA_QK,), zq)]) for i in range(H)])
    part = np.concatenate([np.concatenate([np.full((MLA_NOPE,), zq), _swap_halves(per_head[i, MLA_NOPE:], MLA_ROPE),
                                           np.full((G - MLA_QK,), zq)]) for i in range(H)])
    wuq = zcol(mla_w_uq)[:, np.concatenate([main, part])].astype(MXU_DTYPE)
    zv = mla_w_uk.shape[1]
    grp = np.concatenate([np.concatenate([np.arange(i * HEAD_DIM, (i + 1) * HEAD_DIM), np.full((G - HEAD_DIM,), zv)])
                          for i in range(H)])
    wukt = zcol(mla_w_uk)[:, grp].T.astype(MXU_DTYPE)
    wuv = zcol(mla_w_uv)[:, grp].astype(MXU_DTYPE)
    rtab, ttab = tables

    row = lambda i: (i, 0)
    const2 = lambda i: (0, 0)
    in_specs = [
        pl.BlockSpec((tm, D), row),
        pl.BlockSpec((1, N_ADA, D), lambda i: (i // tpb, 0, 0)),
        pl.BlockSpec((1, D), const2),
        pl.BlockSpec(w_row.shape, const2),
        pl.BlockSpec(w_tr.shape, const2),
        pl.BlockSpec((6, tm, G), lambda i: (0, i % tpb, 0)),
        pl.BlockSpec((4, G, tm), lambda i: (0, 0, i % tpb)),
        pl.BlockSpec((1, MLA_Q_LORA), const2),
        pl.BlockSpec(wuq.shape, const2),
        pl.BlockSpec((1, MLA_KV_LORA), const2),
        pl.BlockSpec(wukt.shape, const2),
        pl.BlockSpec(wuv.shape, const2),
    ]
    dt = MXU_DTYPE
    hq = (jax.ShapeDtypeStruct((B, H, T, G), dt), pl.BlockSpec((1, H, tm, G), lambda i: (i // tpb, 0, i % tpb, 0)))
    hkt = (jax.ShapeDtypeStruct((B, H, G, T), dt), pl.BlockSpec((1, H, G, tm), lambda i: (i // tpb, 0, 0, i % tpb)))
    srow = lambda d, t=dt: (jax.ShapeDtypeStruct((B, T, d), t), pl.BlockSpec((1, tm, d), lambda i: (i // tpb, i % tpb, 0)))
    skt = (jax.ShapeDtypeStruct((B, G, T), dt), pl.BlockSpec((1, G, tm), lambda i: (i // tpb, 0, i % tpb)))
    outs = [hq, hkt, hq,
            hq, hkt, hq,
            hq, srow(HEAD_DIM), srow(HEAD_DIM), skt, srow(G), skt, srow(G), srow(G, jnp.float32),
            hq, skt, srow(G), srow(2 * G), skt, srow(G, jnp.float32)]
    return pl.pallas_call(
        functools.partial(_proj_kernel, tm=tm, tpb=tpb, n_moba=n_moba, n_sel=n_sel),
        grid=(N // tm,),
        in_specs=in_specs,
        out_specs=[o[1] for o in outs],
        out_shape=[o[0] for o in outs],
        compiler_params=_params("arbitrary"),
        name="mixer_in_proj",
    )(x2d, ada_l, norm_g.reshape(1, D), w_row, w_tr, rtab, ttab,
      mla_q_norm.reshape(1, -1), wuq, mla_kv_norm.reshape(1, -1), wukt, wuv)


def _flash_init(m_scr, acc_scr):
    m_scr[...] = jnp.full(m_scr.shape, M_INIT, jnp.float32)
    acc_scr[...] = jnp.zeros_like(acc_scr)


def _flash_update(h, s, v, m_scr, acc_scr, rows=slice(None)):
    m_prev = m_scr[h, rows]
    m_new = jnp.maximum(m_prev, jnp.max(s, axis=-1, keepdims=True))
    p = jnp.exp2(s - _lanes(m_new, s.shape[1]))
    acc_scr[h, rows] = jnp.exp2(m_prev - m_new) * acc_scr[h, rows] + _dot(p, v)
    m_scr[h, rows] = m_new


def _flash_out(h, acc_scr):
    acc = acc_scr[h]
    den = acc[:, DEN_LANE:DEN_LANE + 1]
    return acc[:, :HEAD_DIM] / jnp.where(den > 0.0, den, 1.0)


def _causal_bias(t):
    return jnp.where(_iota((t, t), 1) <= _iota((t, t), 0), 0.0, NEG_INF)


def _rank_desc(x):
    n = x.shape[0]
    row = _iota(x.shape, 0)
    rank = jnp.zeros(x.shape, jnp.float32)
    for j in range(n):
        cand = x[j:j + 1, :]
        rank = rank + jnp.where(cand > x, 1.0, jnp.where((cand == x) & (row > j), 1.0, 0.0))
    return rank


def _att_scratch(tq, stacked=False):
    shape = (1, GROUP_HEADS * tq, LANES) if stacked else (GROUP_HEADS, tq, LANES)
    return [pltpu.VMEM(shape, jnp.float32), pltpu.VMEM(shape, jnp.float32)]


def _att_specs(B, H, T, tq, shared_kv):
    q = pl.BlockSpec((1, H, tq, LANES), lambda b, i: (b, 0, i, 0))
    if shared_kv:
        kt = pl.BlockSpec((1, LANES, T), lambda b, i: (b, 0, 0))
        v = pl.BlockSpec((1, T, LANES), lambda b, i: (b, 0, 0))
    else:
        kt = pl.BlockSpec((1, H, LANES, T), lambda b, i: (b, 0, 0, 0))
        v = pl.BlockSpec((1, H, T, LANES), lambda b, i: (b, 0, 0, 0))
    out = pl.BlockSpec((1, tq, H * HEAD_DIM), lambda b, i: (b, i, 0))
    return q, kt, v, out


def _causal_sweep(qs, kt_ref, v_ref, qi, tq, tk, m_scr, acc_scr):
    H = len(qs)
    _flash_init(m_scr, acc_scr)

    def body(c, carry):
        k0 = pl.multiple_of(c * tk, tk)
        for h in range(H):
            s = jnp.dot(qs[h], kt_ref[0, h, :, pl.ds(k0, tk)], preferred_element_type=jnp.float32)
            _flash_update(h, s, v_ref[0, h, pl.ds(k0, tk), :], m_scr, acc_scr)
        return carry

    lax.fori_loop(0, 2 * qi, body, 0)
    k0 = pl.multiple_of(qi * tq, tk)
    k1 = pl.multiple_of(qi * tq + tk, tk)
    left = jnp.where(_iota((tq, tk), 1) <= _iota((tq, tk), 0), 0.0, NEG_INF)
    right = _causal_bias(tk)
    for h in range(H):
        s = jnp.dot(qs[h], kt_ref[0, h, :, pl.ds(k0, tk)], preferred_element_type=jnp.float32) + left
        _flash_update(h, s, v_ref[0, h, pl.ds(k0, tk), :], m_scr, acc_scr)
        s = jnp.dot(qs[h][tk:], kt_ref[0, h, :, pl.ds(k1, tk)], preferred_element_type=jnp.float32) + right
        _flash_update(h, s, v_ref[0, h, pl.ds(k1, tk), :], m_scr, acc_scr, slice(tk, tq))


def _mla_kernel(q_ref, kt_ref, v_ref, o_ref, m_scr, acc_scr, *, tq, tk):
    H = GROUP_HEADS
    _causal_sweep([q_ref[0, h] for h in range(H)], kt_ref, v_ref, pl.program_id(1), tq, tk, m_scr, acc_scr)
    for h in range(H):
        o_ref[0, :, h * HEAD_DIM:(h + 1) * HEAD_DIM] = _flash_out(h, acc_scr)


def _mla_call(q, kt, v):
    B, H, T, _ = q.shape
    tk = ATT_TILE
    tq = 2 * tk
    assert T % tq == 0
    qs, ks, vs, out = _att_specs(B, H, T, tq, False)
    return pl.pallas_call(
        functools.partial(_mla_kernel, tq=tq, tk=tk),
        grid=(B, T // tq),
        in_specs=[qs, ks, vs],
        out_specs=out,
        out_shape=jax.ShapeDtypeStruct((B, T, H * HEAD_DIM), jnp.float32),
        scratch_shapes=_att_scratch(tq),
        compiler_params=_params("arbitrary", "arbitrary"),
        name="mla_attention",
    )(q, kt, v)


def _moba_kernel(q_ref, kt_ref, v_ref, o_ref, kmean_scr, m_scr, acc_scr, *, tq, tk, nb, topk):
    H, hd = GROUP_HEADS, HEAD_DIM
    qi = pl.program_id(1)
    q0 = qi * tq
    T = nb * MOBA_BLOCK

    @pl.when(qi == 0)
    def _():
        avg = jnp.where((_iota((nb, T), 1) >> _log2(MOBA_BLOCK)) == _iota((nb, T), 0), 1.0 / MOBA_BLOCK, 0.0)
        for h in range(H):
            kmean_scr[h] = _dot_nt(avg, kt_ref[0, h])

    blk = _iota((nb, tq), 0)
    own = (q0 + _iota((nb, tq), 1)) >> _log2(MOBA_BLOCK)
    past = blk < own
    bias_rows = [jnp.zeros((hd, tq), jnp.float32)]
    for h in range(H):
        gate = jnp.where(past, _dot_nt(kmean_scr[h], q_ref[0, h]), NEG_INF)
        allowed = ((_rank_desc(gate) < topk) & past) | (blk == own)
        bias_rows.append(jnp.where(allowed, 0.0, NEG_INF))
    if H * nb < LANES - hd:
        bias_rows.append(jnp.zeros((LANES - hd - H * nb, tq), jnp.float32))
    bias = jnp.concatenate(bias_rows, axis=0).T
    lane = _iota((tq, LANES), 1)
    qs = [jnp.where(lane < hd, q_ref[0, h], bias.astype(q_ref.dtype)) for h in range(H)]
    _causal_sweep(qs, kt_ref, v_ref, qi, tq, tk, m_scr, acc_scr)
    for h in range(H):
        o_ref[0, :, h * hd:(h + 1) * hd] = _flash_out(h, acc_scr)


def _moba_call(q, kt, v):
    B, H, T, _ = q.shape
    tk = ATT_TILE
    tq = 2 * tk
    assert T % tq == 0
    nb = T // MOBA_BLOCK
    qs, ks, vs, out = _att_specs(B, H, T, tq, False)
    return pl.pallas_call(
        functools.partial(_moba_kernel, tq=tq, tk=tk, nb=nb, topk=min(MOBA_TOPK, nb - 1)),
        grid=(B, T // tq),
        in_specs=[qs, ks, vs],
        out_specs=out,
        out_shape=jax.ShapeDtypeStruct((B, T, H * HEAD_DIM), jnp.float32),
        scratch_shapes=[pltpu.VMEM((H, nb, LANES), jnp.float32)] + _att_scratch(tq),
        compiler_params=_params("arbitrary", "arbitrary"),
        name="moba_attention",
    )(q, kt, v)


def _cmp_kernel(k_ref, v_ref, pek_ref, pev_ref, kw1_ref, kw2_ref, vw1_ref, vw2_ref, ko_ref, vo_ref, *, rows):
    half = NSA_CMP_STRIDE * HEAD_DIM

    def compress(t_ref, pe_ref, w1_ref, w2_ref):
        t = t_ref[0].astype(jnp.float32)
        first = _dot(t + pe_ref[0:1, :], w1_ref[0:half, :])
        second = _dot(t + pe_ref[1:2, :], w1_ref[half:, :])
        hid = first + pltpu.roll(second, rows - 1, 0)
        return _dot(_silu(hid), w2_ref[...])

    ko_ref[0] = compress(k_ref, pek_ref, kw1_ref, kw2_ref)
    vo_ref[0] = compress(v_ref, pev_ref, vw1_ref, vw2_ref)


def _cmp_call(kc, vc, pe_k, pe_v, k_w1, k_w2, v_w1, v_w2):
    B, T, d = kc.shape
    rows = T // NSA_CMP_STRIDE
    wide = NSA_CMP_STRIDE * d
    assert NSA_CMP_LEN == 2 * NSA_CMP_STRIDE
    const2 = lambda b: (0, 0)
    blk = pl.BlockSpec((1, rows, wide), lambda b: (b, 0, 0))
    out = pl.BlockSpec((1, rows, d), lambda b: (b, 0, 0))
    return pl.pallas_call(
        functools.partial(_cmp_kernel, rows=rows),
        grid=(B,),
        in_specs=[blk, blk, pl.BlockSpec((2, wide), const2), pl.BlockSpec((2, wide), const2),
                  pl.BlockSpec(k_w1.shape, const2), pl.BlockSpec(k_w2.shape, const2),
                  pl.BlockSpec(v_w1.shape, const2), pl.BlockSpec(v_w2.shape, const2)],
        out_specs=[out, out],
        out_shape=[jax.ShapeDtypeStruct((B, rows, d), jnp.float32)] * 2,
        compiler_params=_params("arbitrary"),
        name="nsa_compress",
    )(kc.reshape(B, rows, wide), vc.reshape(B, rows, wide), pe_k.reshape(2, wide), pe_v.reshape(2, wide),
      k_w1.astype(MXU_DTYPE), k_w2.astype(MXU_DTYPE), v_w1.astype(MXU_DTYPE), v_w2.astype(MXU_DTYPE))


def _nsa_kernel(q_ref, kcmp_ref, vcmp_ref, kst_ref, vs_ref, kwt_ref, vw_ref, g_ref, o_ref,
                oc_scr, os_scr, m_scr, acc_scr, *, tq, ncp, n_sel, topn):
    H, hd = GROUP_HEADS, HEAD_DIM
    qi = pl.program_id(1)
    q0 = qi * tq
    tq_col = q0 + _iota((tq, 1), 0)
    qraw = [q_ref[0, h] for h in range(H)]

    cmp_end = _iota((tq, ncp), 1) * NSA_CMP_STRIDE + (NSA_CMP_LEN - 1)
    m_c = cmp_end <= tq_col
    p_sum = jnp.zeros((tq, ncp), jnp.float32)
    for h in range(H):
        s = jnp.where(m_c, _dot_nt(qraw[h][:, :hd], kcmp_ref[0]), NEG_INF)
        e = jnp.where(m_c, jnp.exp2(s - jnp.max(s, axis=-1, keepdims=True)), 0.0)
        l = jnp.sum(e, axis=-1, keepdims=True)
        p = e / jnp.where(l > 0.0, l, 1.0)
        p_sum = p_sum + p
        oc_scr[h] = _dot(p, vcmp_ref[0])

    cmp_start = _iota((n_sel, ncp), 1) * NSA_CMP_STRIDE
    sel_start = _iota((n_sel, ncp), 0) * NSA_SEL_BLOCK
    overlap = (cmp_start < sel_start + NSA_SEL_BLOCK) & (cmp_start + NSA_CMP_LEN > sel_start)
    imp = _dot_nt(jnp.where(overlap, 1.0, 0.0), p_sum)
    sel_id = _iota((n_sel, tq), 0)
    own = (q0 + _iota((n_sel, tq), 1)) >> _log2(NSA_SEL_BLOCK)
    causal = sel_id <= own
    forced = causal & ((sel_id == 0) | (sel_id >= own - 1))
    imp = jnp.where(forced, NSA_FORCE_SCORE, jnp.where(causal, imp, -NSA_FORCE_SCORE))
    bias_rows = [jnp.zeros((hd, tq), jnp.float32), jnp.where(_rank_desc(imp) < topn, 0.0, NEG_INF)]
    if n_sel < LANES - hd:
        bias_rows.append(jnp.zeros((LANES - hd - n_sel, tq), jnp.float32))
    bias = jnp.concatenate(bias_rows, axis=0).T.astype(qraw[0].dtype)
    lane = _iota((tq, LANES), 1)
    qsel = [jnp.where(lane < hd, qraw[h], bias) for h in range(H)]

    def stack(per_head):
        return jnp.concatenate(per_head, axis=0)

    def tall(bias):
        return jnp.tile(bias, (H, 1))

    q_sel, q_raw = stack(qsel), stack(qraw)
    _flash_init(m_scr, acc_scr)

    def sel_chunk(k0, extra):
        s = jnp.dot(q_sel, kst_ref[0, :, pl.ds(k0, tq)], preferred_element_type=jnp.float32)
        if extra is not None:
            s = s + tall(extra)
        _flash_update(0, s, vs_ref[0, pl.ds(k0, tq), :], m_scr, acc_scr)

    def sel_body(c, carry):
        sel_chunk(pl.multiple_of(c * tq, tq), None)
        return carry

    lax.fori_loop(0, qi, sel_body, 0)
    sel_chunk(pl.multiple_of(q0, tq), _causal_bias(tq))
    o_sel = _flash_out(0, acc_scr)
    for h in range(H):
        os_scr[h] = o_sel[h * tq:(h + 1) * tq]

    _flash_init(m_scr, acc_scr)
    first = jnp.maximum(q0 - NSA_WINDOW + 1, 0) // tq

    def win_body(c, carry):
        k0 = pl.multiple_of(c * tq, tq)
        qpos = q0 + _iota((tq, tq), 0)
        kpos = k0 + _iota((tq, tq), 1)
        band = jnp.where((kpos <= qpos) & (kpos > qpos - NSA_WINDOW), 0.0, NEG_INF)
        s = jnp.dot(q_raw, kwt_ref[0, :, pl.ds(k0, tq)], preferred_element_type=jnp.float32) + tall(band)
        _flash_update(0, s, vw_ref[0, pl.ds(k0, tq), :], m_scr, acc_scr)
        return carry

    lax.fori_loop(first, qi + 1, win_body, 0)
    o_win = _flash_out(0, acc_scr)
    for h in range(H):
        gates = 1.0 / (1.0 + jnp.exp(-g_ref[0][:, 3 * h:3 * h + 3]))
        o_ref[0, :, h * hd:(h + 1) * hd] = (gates[:, 0:1] * oc_scr[h] + gates[:, 1:2] * os_scr[h]
                                            + gates[:, 2:3] * o_win[h * tq:(h + 1) * tq])


def _nsa_call(q, kcmp, vcmp, kst, vs, kwt, vw, gate_logits):
    B, H, T, _ = q.shape
    tq = ATT_TILE
    ncp = kcmp.shape[1]
    n_sel = T // NSA_SEL_BLOCK
    qs, ks, vsp, out = _att_specs(B, H, T, tq, True)
    cmp_spec = pl.BlockSpec((1, ncp, HEAD_DIM), lambda b, i: (b, 0, 0))
    return pl.pallas_call(
        functools.partial(_nsa_kernel, tq=tq, ncp=ncp, n_sel=n_sel, topn=min(NSA_SEL_TOPN, n_sel)),
        grid=(B, T // tq),
        in_specs=[qs, cmp_spec, cmp_spec, ks, vsp, ks, vsp,
                  pl.BlockSpec((1, tq, LANES), lambda b, i: (b, i, 0))],
        out_specs=out,
        out_shape=jax.ShapeDtypeStruct((B, T, H * HEAD_DIM), jnp.float32),
        scratch_shapes=[pltpu.VMEM((H, tq, HEAD_DIM), jnp.float32), pltpu.VMEM((H, tq, HEAD_DIM), jnp.float32)]
        + _att_scratch(tq, stacked=True),
        compiler_params=_params("arbitrary", "arbitrary"),
        name="nsa_attention",
    )(q, kcmp, vcmp, kst, vs, kwt, vw, gate_logits)


def _sortable(x):
    b = int(np.float32(x).view(np.int32))
    return b ^ ((b >> 31) & 0x7FFFFFFF)


def _from_sortable(k):
    return lax.bitcast_convert_type(k ^ ((k >> 31) & 0x7FFFFFFF), jnp.float32)


def _to_sortable(x):
    k = lax.bitcast_convert_type(x, jnp.int32)
    return k ^ ((k >> 31) & 0x7FFFFFFF)


COUNT_ROWS = 64
VALUE_STEPS = 24


def _dsa_kernel(q_ref, kt_ref, v_ref, iq_ref, ikt_ref, iw_ref, o_ref,
                s_scr, t_scr, m_scr, acc_scr, *, tq, topk, idx_scale):
    H, hd = GROUP_HEADS, HEAD_DIM
    qi = pl.program_id(1)
    q0 = qi * tq
    n_kc = qi + 1
    T = s_scr.shape[1]
    reps = tq // LANES

    lane = _iota((tq, LANES), 1)
    quarter = lane >> _log2(DSA_IDX_DIM)
    per_group = LANES // DSA_IDX_DIM
    iq = iq_ref[0]
    iq_all = jnp.concatenate(
        [jnp.where(quarter == (h % per_group), iq[:, (h // per_group) * LANES:(h // per_group + 1) * LANES],
                   jnp.zeros((), iq.dtype)) for h in range(DSA_IDX_HEADS)], axis=0)
    iw = iw_ref[0]
    iwb = [jnp.broadcast_to(iw[:, h:h + 1], (tq, LANES)) for h in range(DSA_IDX_HEADS)]

    def score_body(c, ends):
        top, bot = ends
        k0 = pl.multiple_of(c * tq, tq)
        ikt = ikt_ref[0, :, pl.ds(k0, tq)]
        sh = jnp.dot(iq_all, ikt, preferred_element_type=jnp.float32)
        acc = jnp.zeros((tq, tq), jnp.float32)
        for h in range(DSA_IDX_HEADS):
            acc = acc + _lanes(iwb[h], tq) * jnp.maximum(sh[h * tq:(h + 1) * tq], 0.0)
        qpos = q0 + _iota((tq, tq), 0)
        kpos = k0 + _iota((tq, tq), 1)
        val = acc * idx_scale + 0.0
        sc = jnp.where(kpos <= qpos, val, NEG_INF)
        s_scr[:, pl.ds(k0, tq)] = sc
        low = jnp.where(kpos <= qpos, val, np.inf)
        for r in range(reps):
            top = jnp.maximum(top, sc[:, r * LANES:(r + 1) * LANES])
            bot = jnp.minimum(bot, low[:, r * LANES:(r + 1) * LANES])
        return top, bot

    top, bot = lax.fori_loop(0, n_kc, score_body, (jnp.full((tq, LANES), NEG_INF, jnp.float32),
                                                   jnp.full((tq, LANES), np.inf, jnp.float32)))
    groups = tq // LANES

    def spread(c):
        return jnp.concatenate([jnp.broadcast_to(c[g:g + 1, :], (LANES, LANES)).T for g in range(groups)], axis=0)

    def gather(x):
        return jnp.concatenate([x[g * LANES:(g + 1) * LANES].T[0:1] for g in range(groups)], axis=0)

    def totals(part):
        ones = jnp.ones((8, LANES), jnp.float32)
        return jnp.concatenate([_dot_nt(ones, part[g * LANES:(g + 1) * LANES])[0:1] for g in range(groups)], axis=0)

    row_max = gather(jnp.broadcast_to(jnp.max(top, axis=-1, keepdims=True), (tq, LANES)))
    row_min = gather(jnp.broadcast_to(jnp.min(bot, axis=-1, keepdims=True), (tq, LANES)))

    def count_ge(t, strict=False):
        above = (lambda a, b: a > b) if strict else (lambda a, b: a >= b)
        t_scr[...] = spread(t)
        blocks = [slice(rb * COUNT_ROWS, (rb + 1) * COUNT_ROWS) for rb in range(tq // COUNT_ROWS)]

        def body(c, parts):
            k0 = pl.multiple_of(c * tq, tq)
            out = []
            for rows, part in zip(blocks, parts):
                t_rb = t_scr[rows, :]
                sc = s_scr[rows, pl.ds(k0, tq)]
                for r in range(reps):
                    part = part + jnp.where(above(sc[:, r * LANES:(r + 1) * LANES], t_rb), 1.0, 0.0)
                out.append(part)
            return tuple(out)

        parts = lax.fori_loop(0, n_kc, body, tuple(jnp.zeros((COUNT_ROWS, LANES), jnp.float32) for _ in blocks))
        return totals(jnp.concatenate(parts, axis=0))

    kf = float(topk)
    floor_key = _sortable(NEG_INF)

    def bis_cond(c):
        return c[-1]

    def bis_body(c):
        it, lo, hi, n_lo, _ = c
        active = _any(lo < hi)
        key_mid = (lo | hi) - ((lo ^ hi) >> 1)
        val_mid = _to_sortable(0.5 * (_from_sortable(lo) + _from_sortable(hi)))
        steps = jnp.zeros(lo.shape, jnp.int32) + it
        use_val = (val_mid > lo) & (val_mid <= hi) & (lo > floor_key) & (steps < VALUE_STEPS)
        mid = jnp.where(use_val, val_mid, key_mid)
        cnt = count_ge(_from_sortable(mid))
        ge = cnt >= kf
        lo = jnp.where(ge, mid, lo)
        n_lo = jnp.where(ge, cnt, n_lo)
        hi = jnp.where(cnt == kf, mid, jnp.where(ge, hi, mid - 1))
        return it + 1, lo, hi, n_lo, active

    def count_signs():
        blocks = [slice(rb * COUNT_ROWS, (rb + 1) * COUNT_ROWS) for rb in range(tq // COUNT_ROWS)]

        def body(c, parts):
            k0 = pl.multiple_of(c * tq, tq)
            out = []
            for rows, (pos, nonneg) in zip(blocks, parts):
                sc = s_scr[rows, pl.ds(k0, tq)]
                for r in range(reps):
                    slab = sc[:, r * LANES:(r + 1) * LANES]
                    pos = pos + jnp.where(slab > 0.0, 1.0, 0.0)
                    nonneg = nonneg + jnp.where(slab >= 0.0, 1.0, 0.0)
                out.append((pos, nonneg))
            return tuple(out)

        zero = jnp.zeros((COUNT_ROWS, LANES), jnp.float32)
        parts = lax.fori_loop(0, n_kc, body, tuple((zero, zero) for _ in blocks))
        return (totals(jnp.concatenate([p for p, _ in parts], axis=0)),
                totals(jnp.concatenate([n for _, n in parts], axis=0)))

    n_pos, n_nonneg = count_signs()
    n_causal = q0 + _iota((groups, LANES), 0) * LANES + _iota((groups, LANES), 1) + 1
    lo_neg = jnp.where(n_causal >= topk, _to_sortable(row_min), floor_key)
    n_neg = jnp.where(n_causal >= topk, n_causal, n_kc * tq).astype(jnp.float32)
    is_pos = n_pos >= kf
    is_zero = n_nonneg >= kf
    pick = lambda p, z, n: jnp.where(is_pos, p, jnp.where(is_zero, z, n))
    lo0 = pick(_sortable(np.float32(1e-45)), _sortable(0.0), lo_neg)
    hi0 = pick(_to_sortable(row_max), _sortable(0.0), _sortable(-0.0) - 1)
    _, lo, _, n_ge, _ = lax.while_loop(
        bis_cond, bis_body, (jnp.int32(0), lo0, hi0, pick(n_pos, n_nonneg, n_neg), _any(lo0 < hi0)))
    thr = _lanes(spread(_from_sortable(lo)), tq)

    any_tie = _any(n_ge > kf)

    def causal_at(k0):
        return (k0 + _iota((tq, tq), 1)) <= (q0 + _iota((tq, tq), 0))

    @pl.when(jnp.logical_not(any_tie))
    def _():
        def body(c, carry):
            k0 = pl.multiple_of(c * tq, tq)
            sc = s_scr[:, pl.ds(k0, tq)]
            s_scr[:, pl.ds(k0, tq)] = jnp.where((sc >= thr) & causal_at(k0), 0.0, NEG_INF)
            return carry
        lax.fori_loop(0, n_kc, body, 0)

    @pl.when(any_tie)
    def _():
        need = _lanes(spread(kf - count_ge(_from_sortable(lo), strict=True)), tq)
        prefix = jnp.where(_iota((tq, tq), 0) <= _iota((tq, tq), 1), 1.0, 0.0)
        ones = jnp.ones((tq, LANES), jnp.float32)

        def body(c, seen):
            k0 = pl.multiple_of(c * tq, tq)
            sc = s_scr[:, pl.ds(k0, tq)]
            eq = jnp.where(sc == thr, 1.0, 0.0)
            rank_eq = _dot(eq, prefix) + _lanes(seen, tq)
            picked = (sc > thr) | ((sc == thr) & (rank_eq <= need))
            s_scr[:, pl.ds(k0, tq)] = jnp.where(picked & causal_at(k0), 0.0, NEG_INF)
            return seen + _dot(eq, ones)
        lax.fori_loop(0, n_kc, body, jnp.zeros((tq, LANES), jnp.float32))

    q_all = jnp.concatenate([q_ref[0, h] for h in range(H)], axis=0)
    _flash_init(m_scr, acc_scr)

    def att_body(c, carry):
        k0 = pl.multiple_of(c * tq, tq)
        bias = s_scr[:, pl.ds(k0, tq)]
        s = jnp.dot(q_all, kt_ref[0, :, pl.ds(k0, tq)], preferred_element_type=jnp.float32)
        _flash_update(0, s + jnp.tile(bias, (H, 1)), v_ref[0, pl.ds(k0, tq), :], m_scr, acc_scr)
        return carry

    lax.fori_loop(0, n_kc, att_body, 0)
    out = _flash_out(0, acc_scr)
    for h in range(H):
        o_ref[0, :, h * hd:(h + 1) * hd] = out[h * tq:(h + 1) * tq]


def _dsa_call(q, kt, v, iq, ikt, iw):
    B, H, T, _ = q.shape
    tq = ATT_TILE
    topk = min(DSA_TOPK, T // 4)
    assert tq >= topk
    qs, ks, vs, out = _att_specs(B, H, T, tq, True)
    return pl.pallas_call(
        functools.partial(_dsa_kernel, tq=tq, topk=topk, idx_scale=(DSA_IDX_HEADS * DSA_IDX_DIM) ** -0.5),
        grid=(B, T // tq),
        in_specs=[qs, ks, vs, pl.BlockSpec((1, tq, 2 * LANES), lambda b, i: (b, i, 0)), ks,
                  pl.BlockSpec((1, tq, LANES), lambda b, i: (b, i, 0))],
        out_specs=out,
        out_shape=jax.ShapeDtypeStruct((B, T, H * HEAD_DIM), jnp.float32),
        scratch_shapes=[pltpu.VMEM((tq, T), jnp.float32), pltpu.VMEM((tq, LANES), jnp.float32)]
        + _att_scratch(tq, stacked=True),
        compiler_params=_params("arbitrary", "arbitrary"),
        name="dsa_attention",
    )(q, kt, v, iq, ikt, iw)


def _out_kernel(x_ref, ada_ref, o1_ref, o2_ref, o3_ref, o4_ref, gn_ref, w_ref, y_ref):
    a = ada_ref[0]
    gn = gn_ref[...]
    y = jnp.concatenate([_rms(o[0], gn[i:i + 1]).astype(MXU_DTYPE)
                         for i, o in enumerate((o1_ref, o2_ref, o3_ref, o4_ref))], axis=-1)
    y_ref[...] = x_ref[...] + a[5:6] * jnp.dot(y, w_ref[...], preferred_element_type=jnp.float32)


def _out_call(x2d, ada_l, groups, group_norm, w_out, B, T):
    N, D = x2d.shape
    tm = ROW_TILE
    tpb = T // tm
    grp = pl.BlockSpec((1, tm, GROUP_WIDTH), lambda i: (i // tpb, i % tpb, 0))
    return pl.pallas_call(
        _out_kernel,
        grid=(N // tm,),
        in_specs=[pl.BlockSpec((tm, D), lambda i: (i, 0)),
                  pl.BlockSpec((1, N_ADA, D), lambda i: (i // tpb, 0, 0)),
                  grp, grp, grp, grp,
                  pl.BlockSpec((N_GROUPS, GROUP_WIDTH), lambda i: (0, 0)),
                  pl.BlockSpec((MIX_WIDTH, D), lambda i: (0, 0))],
        out_specs=pl.BlockSpec((tm, D), lambda i: (i, 0)),
        out_shape=jax.ShapeDtypeStruct((N, D), jnp.float32),
        compiler_params=_params("arbitrary"),
        name="mixer_out_proj",
    )(x2d, ada_l, *groups, group_norm, w_out.astype(MXU_DTYPE))


def _mixer_groups(x2d, ada_l, tables, mix_norm, w_in, mla_q_norm, mla_w_uq, mla_kv_norm, mla_w_uk, mla_w_uv,
                  nsa_pe_k, nsa_pe_v, nsa_cmp_k_w1, nsa_cmp_k_w2, nsa_cmp_v_w1, nsa_cmp_v_w2, B, T):
    (mq, mkt, mv, lq, lkt, lv, nq, nkc, nvc, nkst, nvs, nkwt, nvw, ngate,
     dq, dkt, dv, diq, dikt, diw) = _proj_call(
        x2d, ada_l, mix_norm, w_in, tables, mla_q_norm, mla_w_uq, mla_kv_norm, mla_w_uk, mla_w_uv, B, T)
    o_moba = _moba_call(mq, mkt, mv)
    o_mla = _mla_call(lq, lkt, lv)
    kcmp, vcmp = _cmp_call(nkc, nvc, nsa_pe_k, nsa_pe_v, nsa_cmp_k_w1, nsa_cmp_k_w2, nsa_cmp_v_w1, nsa_cmp_v_w2)
    o_nsa = _nsa_call(nq, kcmp, vcmp, nkst, nvs, nkwt, nvw, ngate)
    o_dsa = _dsa_call(dq, dkt, dv, diq, dikt, diw)
    return o_moba, o_mla, o_nsa, o_dsa


def kernel(x, c, ada_w, ada_b, ffn1_norm, ffn1_w_gate, ffn1_w_up, ffn1_w_down, mix_norm, w_in, mla_q_norm, mla_w_uq, mla_kv_norm, mla_w_uk, mla_w_uv, nsa_pe_k, nsa_pe_v, nsa_cmp_k_w1, nsa_cmp_k_w2, nsa_cmp_v_w1, nsa_cmp_v_w2, group_norm, w_out, ffn2_norm, ffn2_w_gate, ffn2_w_up, ffn2_w_down, final_norm):
    B, T, D = x.shape
    L = ada_w.shape[0]
    assert D == D_MODEL and T % ROW_TILE == 0 and T % ATT_TILE == 0
    assert ATT_TILE % MOBA_BLOCK == 0 and ATT_TILE % NSA_SEL_BLOCK == 0 and ATT_TILE >= NSA_WINDOW
    tpb = T // ROW_TILE
    ada = _ada_call(c, ada_w, ada_b)
    tables = _rope_tables(T)
    x2d = x.reshape(B * T, D)
    for l in range(L):
        x2d = _ffn_call(x2d, ada[l], ffn1_norm[l], ffn1_w_gate[l], ffn1_w_up[l], ffn1_w_down[l], 0, tpb)
        groups = _mixer_groups(x2d, ada[l], tables, mix_norm[l], w_in[l], mla_q_norm[l], mla_w_uq[l],
                               mla_kv_norm[l], mla_w_uk[l], mla_w_uv[l], nsa_pe_k[l], nsa_pe_v[l],
                               nsa_cmp_k_w1[l], nsa_cmp_k_w2[l], nsa_cmp_v_w1[l], nsa_cmp_v_w2[l], B, T)
        x2d = _out_call(x2d, ada[l], groups, group_norm[l], w_out[l], B, T)
        x2d = _ffn_call(x2d, ada[l], ffn2_norm[l], ffn2_w_gate[l], ffn2_w_up[l], ffn2_w_down[l], 6, tpb,
                        final_gain=final_norm if l == L - 1 else None)
    return x2d.reshape(B, T, D)
```

```python
import functools
import math

import numpy as np
import jax
import jax.numpy as jnp
from jax import lax
from jax.experimental import pallas as pl
from jax.experimental.pallas import tpu as pltpu

D_MODEL = 1024
N_GROUPS = 4
HEAD_DIM = 64
GROUP_HEADS = D_MODEL // (N_GROUPS * HEAD_DIM)
GROUP_WIDTH = GROUP_HEADS * HEAD_DIM
MIX_WIDTH = N_GROUPS * GROUP_WIDTH
D_FF = 256 * ((8 * D_MODEL + 3 * 256 - 1) // (3 * 256))
N_ADA = 9
FFN_RESIDUAL_WEIGHT = 0.5
ROPE_THETA = 10000.0
RMS_EPS = 1e-6
NEG_INF = -1e30

MOBA_BLOCK = 256
MOBA_TOPK = 3

MLA_Q_LORA = D_MODEL // 4
MLA_KV_LORA = D_MODEL // 8
MLA_NOPE = HEAD_DIM
MLA_ROPE = HEAD_DIM // 2
MLA_V = HEAD_DIM
MLA_QK = MLA_NOPE + MLA_ROPE

NSA_CMP_LEN = 32
NSA_CMP_STRIDE = 16
NSA_CMP_HIDDEN = 4 * HEAD_DIM
NSA_SEL_BLOCK = 64
NSA_SEL_TOPN = 16
NSA_WINDOW = 512
NSA_FORCE_SCORE = 1e4

DSA_TOPK = 256
DSA_IDX_HEADS = 8
DSA_IDX_DIM = 32

IN_NAMES = ("mq", "mk", "mv", "cq", "ckv", "kr", "nq", "nkc", "nvc", "nks", "nvs", "nkw", "nvw",
            "ngate", "dq", "dk", "dv", "diq", "dik", "diw")
IN_SIZES = (
    GROUP_WIDTH, GROUP_WIDTH, GROUP_WIDTH,
    MLA_Q_LORA, MLA_KV_LORA, MLA_ROPE,
    GROUP_WIDTH, HEAD_DIM, HEAD_DIM, HEAD_DIM, HEAD_DIM,
    HEAD_DIM, HEAD_DIM, 3 * GROUP_HEADS,
    GROUP_WIDTH, HEAD_DIM, HEAD_DIM,
    DSA_IDX_HEADS * DSA_IDX_DIM, DSA_IDX_DIM, DSA_IDX_HEADS,
)
N_IN = sum(IN_SIZES)

LANES = 128
MXU_DTYPE = jnp.bfloat16
VMEM_LIMIT = 56 * 1024 * 1024

ATT_TILE = 512
ROW_TILE = 512
MXU_TILE = 256
FF_CHUNK = 6 * MXU_TILE

LOG2E = math.log2(math.e)
M_INIT = -1e29
DEN_LANE = HEAD_DIM


def _params(*semantics):
    return pltpu.CompilerParams(dimension_semantics=semantics, vmem_limit_bytes=VMEM_LIMIT)


def _dot(a, b):
    return jnp.dot(a.astype(MXU_DTYPE), b.astype(MXU_DTYPE), preferred_element_type=jnp.float32)


def _dot_nt(a, b):
    return lax.dot_general(a.astype(MXU_DTYPE), b.astype(MXU_DTYPE), (((1,), (1,)), ((), ())),
                           preferred_element_type=jnp.float32)


def _rms(x, g):
    return x * lax.rsqrt(jnp.mean(x * x, axis=-1, keepdims=True) + RMS_EPS) * g


def _silu(x):
    return x * (1.0 / (1.0 + jnp.exp(-x)))


def _iota(shape, dim):
    return lax.broadcasted_iota(jnp.int32, shape, dim)


def _log2(n):
    assert n & (n - 1) == 0
    return n.bit_length() - 1


def _any(pred):
    return jnp.max(jnp.where(pred, 1.0, 0.0)) > 0.5


def _lanes(x, width):
    return x if width == LANES else jnp.tile(x, (1, width // LANES))


def _ada_kernel(c_ref, w_ref, b_ref, o_ref):
    o_ref[0] = _dot(_silu(c_ref[...]), w_ref[0]) + b_ref[0]


def _ada_call(c, ada_w, ada_b):
    L, D, _ = ada_w.shape
    B = c.shape[0]
    out = pl.pallas_call(
        _ada_kernel,
        grid=(L, N_ADA),
        in_specs=[
            pl.BlockSpec((B, D), lambda l, k: (0, 0)),
            pl.BlockSpec((1, D, D), lambda l, k: (l, 0, k)),
            pl.BlockSpec((1, 1, D), lambda l, k: (l, 0, k)),
        ],
        out_specs=pl.BlockSpec((1, B, D), lambda l, k: (l, 0, k)),
        out_shape=jax.ShapeDtypeStruct((L, B, N_ADA * D), jnp.float32),
        compiler_params=_params("arbitrary", "arbitrary"),
        name="ada_proj",
    )(c, ada_w, ada_b.reshape(L, 1, N_ADA * D))
    return out.reshape(L, B, N_ADA, D)


def _ffn_chunks(F):
    bounds = list(range(0, F, FF_CHUNK)) + [F]
    return list(zip(bounds[:-1], bounds[1:]))


def _ffn_kernel(x_ref, ada_ref, gn_ref, wg_ref, wu_ref, wd_ref, *rest, k0, final):
    if final:
        fg_ref, o_ref = rest
    else:
        (o_ref,) = rest
    a = ada_ref[0]
    h = (_rms(x_ref[...], gn_ref[...]) * (1.0 + a[k0 + 1:k0 + 2]) + a[k0:k0 + 1]).astype(MXU_DTYPE)
    acc = None
    for lo, hi in _ffn_chunks(wg_ref.shape[1]):
        g = jnp.dot(h, wg_ref[:, lo:hi], preferred_element_type=jnp.float32)
        u = jnp.dot(h, wu_ref[:, lo:hi], preferred_element_type=jnp.float32)
        part = jnp.dot((_silu(g) * u).astype(MXU_DTYPE), wd_ref[lo:hi, :], preferred_element_type=jnp.float32)
        acc = part if acc is None else acc + part
    y = x_ref[...] + (FFN_RESIDUAL_WEIGHT * a[k0 + 2:k0 + 3]) * acc
    if final:
        y = _rms(y, fg_ref[...])
    o_ref[...] = y


def _ffn_call(x2d, ada_l, norm_g, w_gate, w_up, w_down, k0, tiles_per_batch, final_gain=None):
    N, D = x2d.shape
    F = w_gate.shape[1]
    tm = ROW_TILE
    in_specs = [
        pl.BlockSpec((tm, D), lambda i: (i, 0)),
        pl.BlockSpec((1, N_ADA, D), lambda i: (i // tiles_per_batch, 0, 0)),
        pl.BlockSpec((1, D), lambda i: (0, 0)),
        pl.BlockSpec((D, F), lambda i: (0, 0)),
        pl.BlockSpec((D, F), lambda i: (0, 0)),
        pl.BlockSpec((F, D), lambda i: (0, 0)),
    ]
    args = [x2d, ada_l, norm_g.reshape(1, D), w_gate.astype(MXU_DTYPE), w_up.astype(MXU_DTYPE),
            w_down.astype(MXU_DTYPE)]
    final = final_gain is not None
    if final:
        in_specs.append(pl.BlockSpec((1, D), lambda i: (0, 0)))
        args.append(final_gain.reshape(1, D))
    return pl.pallas_call(
        functools.partial(_ffn_kernel, k0=k0, final=final),
        grid=(N // tm,),
        in_specs=in_specs,
        out_specs=pl.BlockSpec((tm, D), lambda i: (i, 0)),
        out_shape=jax.ShapeDtypeStruct((N, D), jnp.float32),
        compiler_params=_params("arbitrary"),
        name="ffn",
    )(*args)


G_QMAIN, G_QSWAP, G_KC, G_V, G_MISC, ROW_GROUPS = 0, 6, 12, 14, 18, 25
T_MK, T_MKS, T_SW, T_SWS, T_DK, T_DKS, T_IK, T_ROWS = 0, 256, 512, 640, 768, 896, 1024, 1152


def _swap_halves(c, width):
    return c.reshape(-1, 2, width // 2)[:, ::-1, :].reshape(-1)


def _proj_indices():
    off = dict(zip(IN_NAMES, np.cumsum((0,) + IN_SIZES[:-1]).tolist()))
    size = dict(zip(IN_NAMES, IN_SIZES))
    cols = lambda name: np.arange(off[name], off[name] + size[name])
    zero = lambda n: np.full((n,), N_IN)
    hd = HEAD_DIM

    def head_groups(c):
        return np.concatenate([np.concatenate([c[i:i + hd], zero(LANES - hd)]) for i in range(0, c.size, hd)])

    q = np.concatenate([cols("mq"), cols("nq"), cols("dq")])
    row = np.concatenate([
        q, _swap_halves(q, hd),
        head_groups(cols("nkc")), head_groups(_swap_halves(cols("nkc"), hd)),
        cols("mv"), cols("nvs"), cols("nvw"), cols("dv"), cols("nvc"),
        cols("cq"), cols("ckv"), cols("diq"),
        cols("ngate"), zero(LANES - size["ngate"]), cols("diw"), zero(LANES - size["diw"])])
    assert row.size == ROW_GROUPS * LANES
    sw = np.concatenate([cols("nks"), cols("nkw")])
    dk_main = np.concatenate([cols("dk"), cols("kr"), zero(LANES - hd - MLA_ROPE)])
    dk_swap = np.concatenate([_swap_halves(cols("dk"), hd), _swap_halves(cols("kr"), MLA_ROPE),
                              zero(LANES - hd - MLA_ROPE)])
    tr = np.concatenate([cols("mk"), _swap_halves(cols("mk"), hd), sw, _swap_halves(sw, hd),
                         dk_main, dk_swap, np.tile(cols("dik"), LANES // DSA_IDX_DIM)])
    assert tr.size == T_ROWS
    return row, tr


def _proj_kernel(x_ref, ada_ref, gn_ref, wr_ref, wt_ref, rtab_ref, ttab_ref,
                 qn_ref, wuq_ref, kvn_ref, wukt_ref, wuv_ref,
                 mq_ref, mkt_ref, mv_ref, lq_ref, lkt_ref, lv_ref,
                 nq_ref, nkc_ref, nvc_ref, nkst_ref, nvs_ref, nkwt_ref, nvw_ref, ng_ref,
                 dq_ref, dkt_ref, dv_ref, diq_ref, dikt_ref, diw_ref, *, tm, tpb, n_moba, n_sel):
    H, hd, G = GROUP_HEADS, HEAD_DIM, LANES
    t0 = (pl.program_id(0) % tpb) * tm
    a = ada_ref[0]
    h = (_rms(x_ref[...], gn_ref[...]) * (1.0 + a[4:5]) + a[3:4]).astype(MXU_DTYPE)

    def rows(g0, n):
        return jnp.dot(h, wr_ref[:, g0 * G:(g0 + n) * G], preferred_element_type=jnp.float32)

    def cols(r0, n):
        return _dot_nt(wt_ref[r0:r0 + n, :], h)

    low_half = _iota((tm, G), 1) < hd

    def split(pair, fill):
        return (jnp.where(low_half, pair, fill), jnp.where(low_half, pltpu.roll(pair, hd, 1), fill))

    n_pairs = 3 * H // 2
    roped_q = (rows(G_QMAIN, n_pairs) * _lanes(rtab_ref[0], n_pairs * G)
               + rows(G_QSWAP, n_pairs) * _lanes(rtab_ref[1], n_pairs * G))
    for i, ref in enumerate((mq_ref, nq_ref, dq_ref)):
        for p in range(H // 2):
            g = i * (H // 2) + p
            even, odd = split(roped_q[:, g * G:(g + 1) * G], 0.0)
            ref[0, 2 * p] = even.astype(ref.dtype)
            ref[0, 2 * p + 1] = odd.astype(ref.dtype)
    kc = rows(G_KC, 2)
    nkc_ref[0] = (kc[:, :G] * rtab_ref[2] + kc[:, G:] * rtab_ref[3])[:, :hd].astype(nkc_ref.dtype)

    v = rows(G_V, H // 2 + 2)
    for p in range(H // 2):
        even, odd = split(v[:, p * G:(p + 1) * G], 1.0)
        mv_ref[0, 2 * p] = even.astype(mv_ref.dtype)
        mv_ref[0, 2 * p + 1] = odd.astype(mv_ref.dtype)
    nvs, nvw = split(v[:, (H // 2) * G:(H // 2 + 1) * G], 1.0)
    nvs_ref[0] = nvs.astype(nvs_ref.dtype)
    nvw_ref[0] = nvw.astype(nvw_ref.dtype)
    dv, nvc = split(v[:, (H // 2 + 1) * G:(H // 2 + 2) * G], 1.0)
    dv_ref[0] = dv.astype(dv_ref.dtype)
    nvc_ref[0] = nvc[:, :hd].astype(nvc_ref.dtype)
    ones_hi = jnp.where(_iota((1, G), 1) >= DEN_LANE, 1.0, 0.0)

    misc = rows(G_MISC, 7)
    cq = misc[:, :MLA_Q_LORA]
    ckv = misc[:, MLA_Q_LORA:MLA_Q_LORA + MLA_KV_LORA]
    diq_ref[0] = misc[:, 3 * G:5 * G].astype(diq_ref.dtype)
    ng_ref[0] = misc[:, 5 * G:6 * G]
    diw_ref[0] = misc[:, 6 * G:7 * G]

    tok = t0 + _iota((hd, tm), 1)
    rid = _iota((hd, tm), 0)
    oh_moba = jnp.where((rid & (n_moba - 1)) == (tok >> _log2(MOBA_BLOCK)), 1.0, 0.0)
    oh_sel = jnp.where((rid == (tok >> _log2(NSA_SEL_BLOCK))) & (rid < n_sel), 1.0, 0.0)
    zeros_lo = jnp.zeros((hd, tm), jnp.float32)
    ta_c, ta_s, tb_c, tb_s = ttab_ref[0], ttab_ref[1], ttab_ref[2], ttab_ref[3]
    mkt = (cols(T_MK, H * hd) * jnp.tile(ta_c, (H * hd // G, 1))
           + cols(T_MKS, H * hd) * jnp.tile(ta_s, (H * hd // G, 1)))
    for hh in range(H):
        mine = (rid >> _log2(n_moba)) == hh
        mkt_ref[0, hh] = jnp.concatenate([mkt[hh * hd:(hh + 1) * hd], jnp.where(mine, oh_moba, 0.0)],
                                         axis=0).astype(mkt_ref.dtype)
    sw = cols(T_SW, G) * ta_c + cols(T_SWS, G) * ta_s
    nkst_ref[0] = jnp.concatenate([sw[:hd], oh_sel], axis=0).astype(nkst_ref.dtype)
    nkwt_ref[0] = jnp.concatenate([sw[hd:], zeros_lo], axis=0).astype(nkwt_ref.dtype)
    dkr = cols(T_DK, G) * tb_c + cols(T_DKS, G) * tb_s
    rid2 = _iota((G, tm), 0)
    dkt_ref[0] = jnp.where(rid2 < hd, dkr, 0.0).astype(dkt_ref.dtype)
    kpe_rows = jnp.where((rid2 >= MLA_NOPE) & (rid2 < MLA_QK), dkr, 0.0)
    dikt_ref[0] = cols(T_IK, G).astype(dikt_ref.dtype)

    cqn = _rms(cq, qn_ref[...]).astype(MXU_DTYPE)
    lq = (jnp.dot(cqn, wuq_ref[:, :H * G], preferred_element_type=jnp.float32) * _lanes(rtab_ref[4], H * G)
          + jnp.dot(cqn, wuq_ref[:, H * G:], preferred_element_type=jnp.float32) * _lanes(rtab_ref[5], H * G))
    ckvn = _rms(ckv, kvn_ref[...]).astype(MXU_DTYPE)
    knt = _dot_nt(wukt_ref[...], ckvn)
    lv = jnp.dot(ckvn, wuv_ref[...], preferred_element_type=jnp.float32) + _lanes(ones_hi, H * G)
    for hh in range(H):
        lq_ref[0, hh] = lq[:, hh * G:(hh + 1) * G].astype(lq_ref.dtype)
        lkt_ref[0, hh] = (knt[hh * G:(hh + 1) * G] + kpe_rows).astype(lkt_ref.dtype)
        lv_ref[0, hh] = lv[:, hh * G:(hh + 1) * G].astype(lv_ref.dtype)


def _rope_tables(T):
    def cs(dim):
        inv_freq = 1.0 / (ROPE_THETA ** (np.arange(0, dim, 2, dtype=np.float32) / dim))
        ang = jnp.arange(T, dtype=jnp.float32)[:, None] * jnp.asarray(inv_freq, jnp.float32)[None, :]
        cos, sin = jnp.cos(ang), jnp.sin(ang)
        return jnp.concatenate([cos, cos], axis=-1), jnp.concatenate([-sin, sin], axis=-1)

    c64, s64 = cs(HEAD_DIM)
    c32, s32 = cs(MLA_ROPE)
    pad = lambda t, n: jnp.concatenate([t, jnp.zeros((T, n), jnp.float32)], axis=-1)
    sc = HEAD_DIM ** -0.5 * LOG2E
    sl = MLA_QK ** -0.5 * LOG2E
    ones = jnp.ones((T, MLA_NOPE), jnp.float32)
    rest = LANES - MLA_QK
    rtab = jnp.stack([
        jnp.tile(c64 * sc, (1, LANES // HEAD_DIM)), jnp.tile(s64 * sc, (1, LANES // HEAD_DIM)),
        pad(c64, LANES - HEAD_DIM), pad(s64, LANES - HEAD_DIM),
        pad(jnp.concatenate([ones, c32], axis=-1) * sl, rest),
        pad(jnp.concatenate([0.0 * ones, s32], axis=-1) * sl, rest)])
    ttab = jnp.stack([
        jnp.concatenate([c64, c64], axis=-1).T, jnp.concatenate([s64, s64], axis=-1).T,
        pad(jnp.concatenate([c64, c32], axis=-1), rest).T, pad(jnp.concatenate([s64, s32], axis=-1), rest).T])
    return rtab, ttab


def _proj_call(x2d, ada_l, norm_g, w_in, tables, mla_q_norm, mla_w_uq, mla_kv_norm, mla_w_uk, mla_w_uv, B, T):
    N, D = x2d.shape
    H, G = GROUP_HEADS, LANES
    tm = ROW_TILE
    tpb = T // tm
    n_moba, n_sel = T // MOBA_BLOCK, T // NSA_SEL_BLOCK
    assert H * n_moba <= LANES - HEAD_DIM and n_sel <= LANES - HEAD_DIM and n_moba & (n_moba - 1) == 0
    zcol = lambda w: jnp.concatenate([w, jnp.zeros((w.shape[0], 1), w.dtype)], axis=1)
    row_idx, tr_idx = _proj_indices()
    w_ext = zcol(w_in)
    w_row = w_ext[:, row_idx].astype(MXU_DTYPE)
    w_tr = w_ext[:, tr_idx].T.astype(MXU_DTYPE)
    zq = mla_w_uq.shape[1]
    per_head = np.arange(H * MLA_QK).reshape(H, MLA_QK)
    main = np.concatenate([np.concatenate([per_head[i], np.full((G - MLA_QK,), zq)]) for i in range(H)])
    part = np.concatenate([np.concatenate([np.full((MLA_NOPE,), zq), _swap_halves(per_head[i, MLA_NOPE:], MLA_ROPE),
                                           np.full((G - MLA_QK,), zq)]) for i in range(H)])
    wuq = zcol(mla_w_uq)[:, np.concatenate([main, part])].astype(MXU_DTYPE)
    zv = mla_w_uk.shape[1]
    grp = np.concatenate([np.concatenate([np.arange(i * HEAD_DIM, (i + 1) * HEAD_DIM), np.full((G - HEAD_DIM,), zv)])
                          for i in range(H)])
    wukt = zcol(mla_w_uk)[:, grp].T.astype(MXU_DTYPE)
    wuv = zcol(mla_w_uv)[:, grp].astype(MXU_DTYPE)
    rtab, ttab = tables

    row = lambda i: (i, 0)
    const2 = lambda i: (0, 0)
    in_specs = [
        pl.BlockSpec((tm, D), row),
        pl.BlockSpec((1, N_ADA, D), lambda i: (i // tpb, 0, 0)),
        pl.BlockSpec((1, D), const2),
        pl.BlockSpec(w_row.shape, const2),
        pl.BlockSpec(w_tr.shape, const2),
        pl.BlockSpec((6, tm, G), lambda i: (0, i % tpb, 0)),
        pl.BlockSpec((4, G, tm), lambda i: (0, 0, i % tpb)),
        pl.BlockSpec((1, MLA_Q_LORA), const2),
        pl.BlockSpec(wuq.shape, const2),
        pl.BlockSpec((1, MLA_KV_LORA), const2),
        pl.BlockSpec(wukt.shape, const2),
        pl.BlockSpec(wuv.shape, const2),
    ]
    dt = MXU_DTYPE
    hq = (jax.ShapeDtypeStruct((B, H, T, G), dt), pl.BlockSpec((1, H, tm, G), lambda i: (i // tpb, 0, i % tpb, 0)))
    hkt = (jax.ShapeDtypeStruct((B, H, G, T), dt), pl.BlockSpec((1, H, G, tm), lambda i: (i // tpb, 0, 0, i % tpb)))
    srow = lambda d, t=dt: (jax.ShapeDtypeStruct((B, T, d), t), pl.BlockSpec((1, tm, d), lambda i: (i // tpb, i % tpb, 0)))
    skt = (jax.ShapeDtypeStruct((B, G, T), dt), pl.BlockSpec((1, G, tm), lambda i: (i // tpb, 0, i % tpb)))
    outs = [hq, hkt, hq,
            hq, hkt, hq,
            hq, srow(HEAD_DIM), srow(HEAD_DIM), skt, srow(G), skt, srow(G), srow(G, jnp.float32),
            hq, skt, srow(G), srow(2 * G), skt, srow(G, jnp.float32)]
    return pl.pallas_call(
        functools.partial(_proj_kernel, tm=tm, tpb=tpb, n_moba=n_moba, n_sel=n_sel),
        grid=(N // tm,),
        in_specs=in_specs,
        out_specs=[o[1] for o in outs],
        out_shape=[o[0] for o in outs],
        compiler_params=_params("arbitrary"),
        name="mixer_in_proj",
    )(x2d, ada_l, norm_g.reshape(1, D), w_row, w_tr, rtab, ttab,
      mla_q_norm.reshape(1, -1), wuq, mla_kv_norm.reshape(1, -1), wukt, wuv)


def _flash_init(m_scr, acc_scr):
    m_scr[...] = jnp.full(m_scr.shape, M_INIT, jnp.float32)
    acc_scr[...] = jnp.zeros_like(acc_scr)


def _flash_update(h, s, v, m_scr, acc_scr, rows=slice(None)):
    m_prev = m_scr[h, rows]
    m_new = jnp.maximum(m_prev, jnp.max(s, axis=-1, keepdims=True))
    p = jnp.exp2(s - _lanes(m_new, s.shape[1]))
    acc_scr[h, rows] = jnp.exp2(m_prev - m_new) * acc_scr[h, rows] + _dot(p, v)
    m_scr[h, rows] = m_new


def _flash_out(h, acc_scr):
    acc = acc_scr[h]
    den = acc[:, DEN_LANE:DEN_LANE + 1]
    return acc[:, :HEAD_DIM] / jnp.where(den > 0.0, den, 1.0)


def _causal_bias(t):
    return jnp.where(_iota((t, t), 1) <= _iota((t, t), 0), 0.0, NEG_INF)


def _rank_desc(x):
    n = x.shape[0]
    row = _iota(x.shape, 0)
    rank = jnp.zeros(x.shape, jnp.float32)
    for j in range(n):
        cand = x[j:j + 1, :]
        rank = rank + jnp.where(cand > x, 1.0, jnp.where((cand == x) & (row > j), 1.0, 0.0))
    return rank


def _att_scratch(tq, stacked=False):
    shape = (1, GROUP_HEADS * tq, LANES) if stacked else (GROUP_HEADS, tq, LANES)
    return [pltpu.VMEM(shape, jnp.float32), pltpu.VMEM(shape, jnp.float32)]


def _att_specs(B, H, T, tq, shared_kv):
    q = pl.BlockSpec((1, H, tq, LANES), lambda b, i: (b, 0, i, 0))
    if shared_kv:
        kt = pl.BlockSpec((1, LANES, T), lambda b, i: (b, 0, 0))
        v = pl.BlockSpec((1, T, LANES), lambda b, i: (b, 0, 0))
    else:
        kt = pl.BlockSpec((1, H, LANES, T), lambda b, i: (b, 0, 0, 0))
        v = pl.BlockSpec((1, H, T, LANES), lambda b, i: (b, 0, 0, 0))
    out = pl.BlockSpec((1, tq, H * HEAD_DIM), lambda b, i: (b, i, 0))
    return q, kt, v, out


def _causal_sweep(qs, kt_ref, v_ref, qi, tq, tk, m_scr, acc_scr):
    H = len(qs)
    _flash_init(m_scr, acc_scr)

    def body(c, carry):
        k0 = pl.multiple_of(c * tk, tk)
        for h in range(H):
            s = jnp.dot(qs[h], kt_ref[0, h, :, pl.ds(k0, tk)], preferred_element_type=jnp.float32)
            _flash_update(h, s, v_ref[0, h, pl.ds(k0, tk), :], m_scr, acc_scr)
        return carry

    lax.fori_loop(0, 2 * qi, body, 0)
    k0 = pl.multiple_of(qi * tq, tk)
    k1 = pl.multiple_of(qi * tq + tk, tk)
    left = jnp.where(_iota((tq, tk), 1) <= _iota((tq, tk), 0), 0.0, NEG_INF)
    right = _causal_bias(tk)
    for h in range(H):
        s = jnp.dot(qs[h], kt_ref[0, h, :, pl.ds(k0, tk)], preferred_element_type=jnp.float32) + left
        _flash_update(h, s, v_ref[0, h, pl.ds(k0, tk), :], m_scr, acc_scr)
        s = jnp.dot(qs[h][tk:], kt_ref[0, h, :, pl.ds(k1, tk)], preferred_element_type=jnp.float32) + right
        _flash_update(h, s, v_ref[0, h, pl.ds(k1, tk), :], m_scr, acc_scr, slice(tk, tq))


def _mla_kernel(q_ref, kt_ref, v_ref, o_ref, m_scr, acc_scr, *, tq, tk):
    H = GROUP_HEADS
    _causal_sweep([q_ref[0, h] for h in range(H)], kt_ref, v_ref, pl.program_id(1), tq, tk, m_scr, acc_scr)
    for h in range(H):
        o_ref[0, :, h * HEAD_DIM:(h + 1) * HEAD_DIM] = _flash_out(h, acc_scr)


def _mla_call(q, kt, v):
    B, H, T, _ = q.shape
    tk = ATT_TILE
    tq = 2 * tk
    assert T % tq == 0
    qs, ks, vs, out = _att_specs(B, H, T, tq, False)
    return pl.pallas_call(
        functools.partial(_mla_kernel, tq=tq, tk=tk),
        grid=(B, T // tq),
        in_specs=[qs, ks, vs],
        out_specs=out,
        out_shape=jax.ShapeDtypeStruct((B, T, H * HEAD_DIM), jnp.float32),
        scratch_shapes=_att_scratch(tq),
        compiler_params=_params("arbitrary", "arbitrary"),
        name="mla_attention",
    )(q, kt, v)


def _moba_kernel(q_ref, kt_ref, v_ref, o_ref, kmean_scr, m_scr, acc_scr, *, tq, tk, nb, topk):
    H, hd = GROUP_HEADS, HEAD_DIM
    qi = pl.program_id(1)
    q0 = qi * tq
    T = nb * MOBA_BLOCK

    @pl.when(qi == 0)
    def _():
        avg = jnp.where((_iota((nb, T), 1) >> _log2(MOBA_BLOCK)) == _iota((nb, T), 0), 1.0 / MOBA_BLOCK, 0.0)
        for h in range(H):
            kmean_scr[h] = _dot_nt(avg, kt_ref[0, h])

    blk = _iota((nb, tq), 0)
    own = (q0 + _iota((nb, tq), 1)) >> _log2(MOBA_BLOCK)
    past = blk < own
    bias_rows = [jnp.zeros((hd, tq), jnp.float32)]
    for h in range(H):
        gate = jnp.where(past, _dot_nt(kmean_scr[h], q_ref[0, h]), NEG_INF)
        allowed = ((_rank_desc(gate) < topk) & past) | (blk == own)
        bias_rows.append(jnp.where(allowed, 0.0, NEG_INF))
    if H * nb < LANES - hd:
        bias_rows.append(jnp.zeros((LANES - hd - H * nb, tq), jnp.float32))
    bias = jnp.concatenate(bias_rows, axis=0).T
    lane = _iota((tq, LANES), 1)
    qs = [jnp.where(lane < hd, q_ref[0, h], bias.astype(q_ref.dtype)) for h in range(H)]
    _causal_sweep(qs, kt_ref, v_ref, qi, tq, tk, m_scr, acc_scr)
    for h in range(H):
        o_ref[0, :, h * hd:(h + 1) * hd] = _flash_out(h, acc_scr)


def _moba_call(q, kt, v):
    B, H, T, _ = q.shape
    tk = ATT_TILE
    tq = 2 * tk
    assert T % tq == 0
    nb = T // MOBA_BLOCK
    qs, ks, vs, out = _att_specs(B, H, T, tq, False)
    return pl.pallas_call(
        functools.partial(_moba_kernel, tq=tq, tk=tk, nb=nb, topk=min(MOBA_TOPK, nb - 1)),
        grid=(B, T // tq),
        in_specs=[qs, ks, vs],
        out_specs=out,
        out_shape=jax.ShapeDtypeStruct((B, T, H * HEAD_DIM), jnp.float32),
        scratch_shapes=[pltpu.VMEM((H, nb, LANES), jnp.float32)] + _att_scratch(tq),
        compiler_params=_params("arbitrary", "arbitrary"),
        name="moba_attention",
    )(q, kt, v)


def _cmp_kernel(k_ref, v_ref, pek_ref, pev_ref, kw1_ref, kw2_ref, vw1_ref, vw2_ref, ko_ref, vo_ref, *, rows):
    half = NSA_CMP_STRIDE * HEAD_DIM

    def compress(t_ref, pe_ref, w1_ref, w2_ref):
        t = t_ref[0].astype(jnp.float32)
        first = _dot(t + pe_ref[0:1, :], w1_ref[0:half, :])
        second = _dot(t + pe_ref[1:2, :], w1_ref[half:, :])
        hid = first + pltpu.roll(second, rows - 1, 0)
        return _dot(_silu(hid), w2_ref[...])

    ko_ref[0] = compress(k_ref, pek_ref, kw1_ref, kw2_ref)
    vo_ref[0] = compress(v_ref, pev_ref, vw1_ref, vw2_ref)


def _cmp_call(kc, vc, pe_k, pe_v, k_w1, k_w2, v_w1, v_w2):
    B, T, d = kc.shape
    rows = T // NSA_CMP_STRIDE
    wide = NSA_CMP_STRIDE * d
    assert NSA_CMP_LEN == 2 * NSA_CMP_STRIDE
    const2 = lambda b: (0, 0)
    blk = pl.BlockSpec((1, rows, wide), lambda b: (b, 0, 0))
    out = pl.BlockSpec((1, rows, d), lambda b: (b, 0, 0))
    return pl.pallas_call(
        functools.partial(_cmp_kernel, rows=rows),
        grid=(B,),
        in_specs=[blk, blk, pl.BlockSpec((2, wide), const2), pl.BlockSpec((2, wide), const2),
                  pl.BlockSpec(k_w1.shape, const2), pl.BlockSpec(k_w2.shape, const2),
                  pl.BlockSpec(v_w1.shape, const2), pl.BlockSpec(v_w2.shape, const2)],
        out_specs=[out, out],
        out_shape=[jax.ShapeDtypeStruct((B, rows, d), jnp.float32)] * 2,
        compiler_params=_params("arbitrary"),
        name="nsa_compress",
    )(kc.reshape(B, rows, wide), vc.reshape(B, rows, wide), pe_k.reshape(2, wide), pe_v.reshape(2, wide),
      k_w1.astype(MXU_DTYPE), k_w2.astype(MXU_DTYPE), v_w1.astype(MXU_DTYPE), v_w2.astype(MXU_DTYPE))


def _nsa_kernel(q_ref, kcmp_ref, vcmp_ref, kst_ref, vs_ref, kwt_ref, vw_ref, g_ref, o_ref,
                oc_scr, os_scr, m_scr, acc_scr, *, tq, ncp, n_sel, topn):
    H, hd = GROUP_HEADS, HEAD_DIM
    qi = pl.program_id(1)
    q0 = qi * tq
    tq_col = q0 + _iota((tq, 1), 0)
    qraw = [q_ref[0, h] for h in range(H)]

    cmp_end = _iota((tq, ncp), 1) * NSA_CMP_STRIDE + (NSA_CMP_LEN - 1)
    m_c = cmp_end <= tq_col
    p_sum = jnp.zeros((tq, ncp), jnp.float32)
    for h in range(H):
        s = jnp.where(m_c, _dot_nt(qraw[h][:, :hd], kcmp_ref[0]), NEG_INF)
        e = jnp.where(m_c, jnp.exp2(s - jnp.max(s, axis=-1, keepdims=True)), 0.0)
        l = jnp.sum(e, axis=-1, keepdims=True)
        p = e / jnp.where(l > 0.0, l, 1.0)
        p_sum = p_sum + p
        oc_scr[h] = _dot(p, vcmp_ref[0])

    cmp_start = _iota((n_sel, ncp), 1) * NSA_CMP_STRIDE
    sel_start = _iota((n_sel, ncp), 0) * NSA_SEL_BLOCK
    overlap = (cmp_start < sel_start + NSA_SEL_BLOCK) & (cmp_start + NSA_CMP_LEN > sel_start)
    imp = _dot_nt(jnp.where(overlap, 1.0, 0.0), p_sum)
    sel_id = _iota((n_sel, tq), 0)
    own = (q0 + _iota((n_sel, tq), 1)) >> _log2(NSA_SEL_BLOCK)
    causal = sel_id <= own
    forced = causal & ((sel_id == 0) | (sel_id >= own - 1))
    imp = jnp.where(forced, NSA_FORCE_SCORE, jnp.where(causal, imp, -NSA_FORCE_SCORE))
    bias_rows = [jnp.zeros((hd, tq), jnp.float32), jnp.where(_rank_desc(imp) < topn, 0.0, NEG_INF)]
    if n_sel < LANES - hd:
        bias_rows.append(jnp.zeros((LANES - hd - n_sel, tq), jnp.float32))
    bias = jnp.concatenate(bias_rows, axis=0).T.astype(qraw[0].dtype)
    lane = _iota((tq, LANES), 1)
    qsel = [jnp.where(lane < hd, qraw[h], bias) for h in range(H)]

    def stack(per_head):
        return jnp.concatenate(per_head, axis=0)

    def tall(bias):
        return jnp.tile(bias, (H, 1))

    q_sel, q_raw = stack(qsel), stack(qraw)
    _flash_init(m_scr, acc_scr)

    def sel_chunk(k0, extra):
        s = jnp.dot(q_sel, kst_ref[0, :, pl.ds(k0, tq)], preferred_element_type=jnp.float32)
        if extra is not None:
            s = s + tall(extra)
        _flash_update(0, s, vs_ref[0, pl.ds(k0, tq), :], m_scr, acc_scr)

    def sel_body(c, carry):
        sel_chunk(pl.multiple_of(c * tq, tq), None)
        return carry

    lax.fori_loop(0, qi, sel_body, 0)
    sel_chunk(pl.multiple_of(q0, tq), _causal_bias(tq))
    o_sel = _flash_out(0, acc_scr)
    for h in range(H):
        os_scr[h] = o_sel[h * tq:(h + 1) * tq]

    _flash_init(m_scr, acc_scr)
    first = jnp.maximum(q0 - NSA_WINDOW + 1, 0) // tq

    def win_body(c, carry):
        k0 = pl.multiple_of(c * tq, tq)
        qpos = q0 + _iota((tq, tq), 0)
        kpos = k0 + _iota((tq, tq), 1)
        band = jnp.where((kpos <= qpos) & (kpos > qpos - NSA_WINDOW), 0.0, NEG_INF)
        s = jnp.dot(q_raw, kwt_ref[0, :, pl.ds(k0, tq)], preferred_element_type=jnp.float32) + tall(band)
        _flash_update(0, s, vw_ref[0, pl.ds(k0, tq), :], m_scr, acc_scr)
        return carry

    lax.fori_loop(first, qi + 1, win_body, 0)
    o_win = _flash_out(0, acc_scr)
    for h in range(H):
        gates = 1.0 / (1.0 + jnp.exp(-g_ref[0][:, 3 * h:3 * h + 3]))
        o_ref[0, :, h * hd:(h + 1) * hd] = (gates[:, 0:1] * oc_scr[h] + gates[:, 1:2] * os_scr[h]
                                            + gates[:, 2:3] * o_win[h * tq:(h + 1) * tq])


def _nsa_call(q, kcmp, vcmp, kst, vs, kwt, vw, gate_logits):
    B, H, T, _ = q.shape
    tq = ATT_TILE
    ncp = kcmp.shape[1]
    n_sel = T // NSA_SEL_BLOCK
    qs, ks, vsp, out = _att_specs(B, H, T, tq, True)
    cmp_spec = pl.BlockSpec((1, ncp, HEAD_DIM), lambda b, i: (b, 0, 0))
    return pl.pallas_call(
        functools.partial(_nsa_kernel, tq=tq, ncp=ncp, n_sel=n_sel, topn=min(NSA_SEL_TOPN, n_sel)),
        grid=(B, T // tq),
        in_specs=[qs, cmp_spec, cmp_spec, ks, vsp, ks, vsp,
                  pl.BlockSpec((1, tq, LANES), lambda b, i: (b, i, 0))],
        out_specs=out,
        out_shape=jax.ShapeDtypeStruct((B, T, H * HEAD_DIM), jnp.float32),
        scratch_shapes=[pltpu.VMEM((H, tq, HEAD_DIM), jnp.float32), pltpu.VMEM((H, tq, HEAD_DIM), jnp.float32)]
        + _att_scratch(tq, stacked=True),
        compiler_params=_params("arbitrary", "arbitrary"),
        name="nsa_attention",
    )(q, kcmp, vcmp, kst, vs, kwt, vw, gate_logits)


def _sortable(x):
    b = int(np.float32(x).view(np.int32))
    return b ^ ((b >> 31) & 0x7FFFFFFF)


def _from_sortable(k):
    return lax.bitcast_convert_type(k ^ ((k >> 31) & 0x7FFFFFFF), jnp.float32)


def _to_sortable(x):
    k = lax.bitcast_convert_type(x, jnp.int32)
    return k ^ ((k >> 31) & 0x7FFFFFFF)


COUNT_ROWS = 64
VALUE_STEPS = 24


def _dsa_kernel(q_ref, kt_ref, v_ref, iq_ref, ikt_ref, iw_ref, o_ref,
                s_scr, t_scr, m_scr, acc_scr, *, tq, topk, idx_scale):
    H, hd = GROUP_HEADS, HEAD_DIM
    qi = pl.program_id(1)
    q0 = qi * tq
    n_kc = qi + 1
    T = s_scr.shape[1]
    reps = tq // LANES

    lane = _iota((tq, LANES), 1)
    quarter = lane >> _log2(DSA_IDX_DIM)
    per_group = LANES // DSA_IDX_DIM
    iq = iq_ref[0]
    iq_all = jnp.concatenate(
        [jnp.where(quarter == (h % per_group), iq[:, (h // per_group) * LANES:(h // per_group + 1) * LANES],
                   jnp.zeros((), iq.dtype)) for h in range(DSA_IDX_HEADS)], axis=0)
    iw = iw_ref[0]
    iwb = [jnp.broadcast_to(iw[:, h:h + 1], (tq, LANES)) for h in range(DSA_IDX_HEADS)]

    diag_causal = _iota((tq, tq), 1) <= _iota((tq, tq), 0)

    def score_chunk(c, ends, diagonal):
        top, bot = ends
        k0 = pl.multiple_of(c * tq, tq)
        ikt = ikt_ref[0, :, pl.ds(k0, tq)]
        sh = jnp.dot(iq_all, ikt, preferred_element_type=jnp.float32)
        acc = jnp.zeros((tq, tq), jnp.float32)
        for h in range(DSA_IDX_HEADS):
            acc = acc + _lanes(iwb[h], tq) * jnp.maximum(sh[h * tq:(h + 1) * tq], 0.0)
        sc = low = acc * idx_scale + 0.0
        if diagonal:
            sc = jnp.where(diag_causal, sc, NEG_INF)
            low = jnp.where(diag_causal, low, np.inf)
        s_scr[:, pl.ds(k0, tq)] = sc
        for r in range(reps):
            top = jnp.maximum(top, sc[:, r * LANES:(r + 1) * LANES])
            bot = jnp.minimum(bot, low[:, r * LANES:(r + 1) * LANES])
        return top, bot

    ends = lax.fori_loop(0, qi, functools.partial(score_chunk, diagonal=False),
                         (jnp.full((tq, LANES), NEG_INF, jnp.float32), jnp.full((tq, LANES), np.inf, jnp.float32)))
    top, bot = score_chunk(qi, ends, diagonal=True)
    groups = tq // LANES

    def spread(c):
        return jnp.concatenate([jnp.broadcast_to(c[g:g + 1, :], (LANES, LANES)).T for g in range(groups)], axis=0)

    def gather(x):
        return jnp.concatenate([x[g * LANES:(g + 1) * LANES].T[0:1] for g in range(groups)], axis=0)

    def totals(part):
        ones = jnp.ones((8, LANES), jnp.float32)
        return jnp.concatenate([_dot_nt(ones, part[g * LANES:(g + 1) * LANES])[0:1] for g in range(groups)], axis=0)

    row_max = gather(jnp.broadcast_to(jnp.max(top, axis=-1, keepdims=True), (tq, LANES)))
    row_min = gather(jnp.broadcast_to(jnp.min(bot, axis=-1, keepdims=True), (tq, LANES)))

    def count_ge(t, strict=False):
        above = (lambda a, b: a > b) if strict else (lambda a, b: a >= b)
        t_scr[...] = spread(t)
        blocks = [slice(rb * COUNT_ROWS, (rb + 1) * COUNT_ROWS) for rb in range(tq // COUNT_ROWS)]

        def body(c, parts):
            k0 = pl.multiple_of(c * tq, tq)
            out = []
            for rows, part in zip(blocks, parts):
                t_rb = t_scr[rows, :]
                sc = s_scr[rows, pl.ds(k0, tq)]
                for r in range(reps):
                    part = part + jnp.where(above(sc[:, r * LANES:(r + 1) * LANES], t_rb), 1.0, 0.0)
                out.append(part)
            return tuple(out)

        parts = lax.fori_loop(0, n_kc, body, tuple(jnp.zeros((COUNT_ROWS, LANES), jnp.float32) for _ in blocks))
        return totals(jnp.concatenate(parts, axis=0))

    kf = float(topk)
    floor_key = _sortable(NEG_INF)

    def bis_cond(c):
        return c[-1]

    def bis_body(c):
        it, lo, hi, n_lo, _ = c
        active = _any(lo < hi)
        key_mid = (lo | hi) - ((lo ^ hi) >> 1)
        val_mid = _to_sortable(0.5 * (_from_sortable(lo) + _from_sortable(hi)))
        steps = jnp.zeros(lo.shape, jnp.int32) + it
        use_val = (val_mid > lo) & (val_mid <= hi) & (lo > floor_key) & (steps < VALUE_STEPS)
        mid = jnp.where(use_val, val_mid, key_mid)
        cnt = count_ge(_from_sortable(mid))
        ge = cnt >= kf
        lo = jnp.where(ge, mid, lo)
        n_lo = jnp.where(ge, cnt, n_lo)
        hi = jnp.where(cnt == kf, mid, jnp.where(ge, hi, mid - 1))
        return it + 1, lo, hi, n_lo, active

    def count_signs():
        blocks = [slice(rb * COUNT_ROWS, (rb + 1) * COUNT_ROWS) for rb in range(tq // COUNT_ROWS)]

        def body(c, parts):
            k0 = pl.multiple_of(c * tq, tq)
            out = []
            for rows, (pos, nonneg) in zip(blocks, parts):
                sc = s_scr[rows, pl.ds(k0, tq)]
                for r in range(reps):
                    slab = sc[:, r * LANES:(r + 1) * LANES]
                    pos = pos + jnp.where(slab > 0.0, 1.0, 0.0)
                    nonneg = nonneg + jnp.where(slab >= 0.0, 1.0, 0.0)
                out.append((pos, nonneg))
            return tuple(out)

        zero = jnp.zeros((COUNT_ROWS, LANES), jnp.float32)
        parts = lax.fori_loop(0, n_kc, body, tuple((zero, zero) for _ in blocks))
        return (totals(jnp.concatenate([p for p, _ in parts], axis=0)),
                totals(jnp.concatenate([n for _, n in parts], axis=0)))

    n_pos, n_nonneg = count_signs()
    n_causal = q0 + _iota((groups, LANES), 0) * LANES + _iota((groups, LANES), 1) + 1
    lo_neg = jnp.where(n_causal >= topk, _to_sortable(row_min), floor_key)
    n_neg = jnp.where(n_causal >= topk, n_causal, n_kc * tq).astype(jnp.float32)
    is_pos = n_pos >= kf
    is_zero = n_nonneg >= kf
    pick = lambda p, z, n: jnp.where(is_pos, p, jnp.where(is_zero, z, n))
    lo0 = pick(_sortable(np.float32(1e-45)), _sortable(0.0), lo_neg)
    hi0 = pick(_to_sortable(row_max), _sortable(0.0), _sortable(-0.0) - 1)
    _, lo, _, n_ge, _ = lax.while_loop(
        bis_cond, bis_body, (jnp.int32(0), lo0, hi0, pick(n_pos, n_nonneg, n_neg), _any(lo0 < hi0)))
    thr = _lanes(spread(_from_sortable(lo)), tq)

    any_tie = _any(n_ge > kf)

    @pl.when(jnp.logical_not(any_tie))
    def _():
        def mask_chunk(c, carry, diagonal):
            k0 = pl.multiple_of(c * tq, tq)
            picked = s_scr[:, pl.ds(k0, tq)] >= thr
            if diagonal:
                picked = picked & diag_causal
            s_scr[:, pl.ds(k0, tq)] = jnp.where(picked, 0.0, NEG_INF)
            return carry
        lax.fori_loop(0, qi, functools.partial(mask_chunk, diagonal=False), 0)
        mask_chunk(qi, 0, diagonal=True)

    @pl.when(any_tie)
    def _():
        need = _lanes(spread(kf - count_ge(_from_sortable(lo), strict=True)), tq)
        prefix = jnp.where(_iota((tq, tq), 0) <= _iota((tq, tq), 1), 1.0, 0.0)
        ones = jnp.ones((tq, LANES), jnp.float32)

        def mask_chunk(c, seen, diagonal):
            k0 = pl.multiple_of(c * tq, tq)
            sc = s_scr[:, pl.ds(k0, tq)]
            eq = jnp.where(sc == thr, 1.0, 0.0)
            rank_eq = _dot(eq, prefix) + _lanes(seen, tq)
            picked = (sc > thr) | ((sc == thr) & (rank_eq <= need))
            if diagonal:
                picked = picked & diag_causal
            s_scr[:, pl.ds(k0, tq)] = jnp.where(picked, 0.0, NEG_INF)
            return seen + _dot(eq, ones)
        seen = lax.fori_loop(0, qi, functools.partial(mask_chunk, diagonal=False),
                             jnp.zeros((tq, LANES), jnp.float32))
        mask_chunk(qi, seen, diagonal=True)

    q_all = jnp.concatenate([q_ref[0, h] for h in range(H)], axis=0)
    _flash_init(m_scr, acc_scr)

    def att_body(c, carry):
        k0 = pl.multiple_of(c * tq, tq)
        bias = s_scr[:, pl.ds(k0, tq)]
        s = jnp.dot(q_all, kt_ref[0, :, pl.ds(k0, tq)], preferred_element_type=jnp.float32)
        _flash_update(0, s + jnp.tile(bias, (H, 1)), v_ref[0, pl.ds(k0, tq), :], m_scr, acc_scr)
        return carry

    lax.fori_loop(0, n_kc, att_body, 0)
    out = _flash_out(0, acc_scr)
    for h in range(H):
        o_ref[0, :, h * hd:(h + 1) * hd] = out[h * tq:(h + 1) * tq]


def _dsa_call(q, kt, v, iq, ikt, iw):
    B, H, T, _ = q.shape
    tq = ATT_TILE
    topk = min(DSA_TOPK, T // 4)
    assert tq >= topk
    qs, ks, vs, out = _att_specs(B, H, T, tq, True)
    return pl.pallas_call(
        functools.partial(_dsa_kernel, tq=tq, topk=topk, idx_scale=(DSA_IDX_HEADS * DSA_IDX_DIM) ** -0.5),
        grid=(B, T // tq),
        in_specs=[qs, ks, vs, pl.BlockSpec((1, tq, 2 * LANES), lambda b, i: (b, i, 0)), ks,
                  pl.BlockSpec((1, tq, LANES), lambda b, i: (b, i, 0))],
        out_specs=out,
        out_shape=jax.ShapeDtypeStruct((B, T, H * HEAD_DIM), jnp.float32),
        scratch_shapes=[pltpu.VMEM((tq, T), jnp.float32), pltpu.VMEM((tq, LANES), jnp.float32)]
        + _att_scratch(tq, stacked=True),
        compiler_params=_params("arbitrary", "arbitrary"),
        name="dsa_attention",
    )(q, kt, v, iq, ikt, iw)


def _out_kernel(x_ref, ada_ref, o1_ref, o2_ref, o3_ref, o4_ref, gn_ref, w_ref, y_ref):
    a = ada_ref[0]
    gn = gn_ref[...]
    y = jnp.concatenate([_rms(o[0], gn[i:i + 1]).astype(MXU_DTYPE)
                         for i, o in enumerate((o1_ref, o2_ref, o3_ref, o4_ref))], axis=-1)
    y_ref[...] = x_ref[...] + a[5:6] * jnp.dot(y, w_ref[...], preferred_element_type=jnp.float32)


def _out_call(x2d, ada_l, groups, group_norm, w_out, B, T):
    N, D = x2d.shape
    tm = ROW_TILE
    tpb = T // tm
    grp = pl.BlockSpec((1, tm, GROUP_WIDTH), lambda i: (i // tpb, i % tpb, 0))
    return pl.pallas_call(
        _out_kernel,
        grid=(N // tm,),
        in_specs=[pl.BlockSpec((tm, D), lambda i: (i, 0)),
                  pl.BlockSpec((1, N_ADA, D), lambda i: (i // tpb, 0, 0)),
                  grp, grp, grp, grp,
                  pl.BlockSpec((N_GROUPS, GROUP_WIDTH), lambda i: (0, 0)),
                  pl.BlockSpec((MIX_WIDTH, D), lambda i: (0, 0))],
        out_specs=pl.BlockSpec((tm, D), lambda i: (i, 0)),
        out_shape=jax.ShapeDtypeStruct((N, D), jnp.float32),
        compiler_params=_params("arbitrary"),
        name="mixer_out_proj",
    )(x2d, ada_l, *groups, group_norm, w_out.astype(MXU_DTYPE))


def _mixer_groups(x2d, ada_l, tables, mix_norm, w_in, mla_q_norm, mla_w_uq, mla_kv_norm, mla_w_uk, mla_w_uv,
                  nsa_pe_k, nsa_pe_v, nsa_cmp_k_w1, nsa_cmp_k_w2, nsa_cmp_v_w1, nsa_cmp_v_w2, B, T):
    (mq, mkt, mv, lq, lkt, lv, nq, nkc, nvc, nkst, nvs, nkwt, nvw, ngate,
     dq, dkt, dv, diq, dikt, diw) = _proj_call(
        x2d, ada_l, mix_norm, w_in, tables, mla_q_norm, mla_w_uq, mla_kv_norm, mla_w_uk, mla_w_uv, B, T)
    o_moba = _moba_call(mq, mkt, mv)
    o_mla = _mla_call(lq, lkt, lv)
    kcmp, vcmp = _cmp_call(nkc, nvc, nsa_pe_k, nsa_pe_v, nsa_cmp_k_w1, nsa_cmp_k_w2, nsa_cmp_v_w1, nsa_cmp_v_w2)
    o_nsa = _nsa_call(nq, kcmp, vcmp, nkst, nvs, nkwt, nvw, ngate)
    o_dsa = _dsa_call(dq, dkt, dv, diq, dikt, diw)
    return o_moba, o_mla, o_nsa, o_dsa


def kernel(x, c, ada_w, ada_b, ffn1_norm, ffn1_w_gate, ffn1_w_up, ffn1_w_down, mix_norm, w_in, mla_q_norm, mla_w_uq, mla_kv_norm, mla_w_uk, mla_w_uv, nsa_pe_k, nsa_pe_v, nsa_cmp_k_w1, nsa_cmp_k_w2, nsa_cmp_v_w1, nsa_cmp_v_w2, group_norm, w_out, ffn2_norm, ffn2_w_gate, ffn2_w_up, ffn2_w_down, final_norm):
    B, T, D = x.shape
    L = ada_w.shape[0]
    assert D == D_MODEL and T % ROW_TILE == 0 and T % ATT_TILE == 0
    assert ATT_TILE % MOBA_BLOCK == 0 and ATT_TILE % NSA_SEL_BLOCK == 0 and ATT_TILE >= NSA_WINDOW
    tpb = T // ROW_TILE
    ada = _ada_call(c, ada_w, ada_b)
    tables = _rope_tables(T)
    x2d = x.reshape(B * T, D)
    for l in range(L):
        x2d = _ffn_call(x2d, ada[l], ffn1_norm[l], ffn1_w_gate[l], ffn1_w_up[l], ffn1_w_down[l], 0, tpb)
        groups = _mixer_groups(x2d, ada[l], tables, mix_norm[l], w_in[l], mla_q_norm[l], mla_w_uq[l],
                               mla_kv_norm[l], mla_w_uk[l], mla_w_uv[l], nsa_pe_k[l], nsa_pe_v[l],
                               nsa_cmp_k_w1[l], nsa_cmp_k_w2[l], nsa_cmp_v_w1[l], nsa_cmp_v_w2[l], B, T)
        x2d = _out_call(x2d, ada[l], groups, group_norm[l], w_out[l], B, T)
        x2d = _ffn_call(x2d, ada[l], ffn2_norm[l], ffn2_w_gate[l], ffn2_w_up[l], ffn2_w_down[l], 6, tpb,
                        final_gain=final_norm if l == L - 1 else None)
    return x2d.reshape(B, T, D)
```

```python
import functools
import math

import numpy as np
import jax
import jax.numpy as jnp
from jax import lax
from jax.experimental import pallas as pl
from jax.experimental.pallas import tpu as pltpu

D_MODEL = 1024
N_GROUPS = 4
HEAD_DIM = 64
GROUP_HEADS = D_MODEL // (N_GROUPS * HEAD_DIM)
GROUP_WIDTH = GROUP_HEADS * HEAD_DIM
MIX_WIDTH = N_GROUPS * GROUP_WIDTH
D_FF = 256 * ((8 * D_MODEL + 3 * 256 - 1) // (3 * 256))
N_ADA = 9
FFN_RESIDUAL_WEIGHT = 0.5
ROPE_THETA = 10000.0
RMS_EPS = 1e-6
NEG_INF = -1e30

MOBA_BLOCK = 256
MOBA_TOPK = 3

MLA_Q_LORA = D_MODEL // 4
MLA_KV_LORA = D_MODEL // 8
MLA_NOPE = HEAD_DIM
MLA_ROPE = HEAD_DIM // 2
MLA_V = HEAD_DIM
MLA_QK = MLA_NOPE + MLA_ROPE

NSA_CMP_LEN = 32
NSA_CMP_STRIDE = 16
NSA_CMP_HIDDEN = 4 * HEAD_DIM
NSA_SEL_BLOCK = 64
NSA_SEL_TOPN = 16
NSA_WINDOW = 512
NSA_FORCE_SCORE = 1e4

DSA_TOPK = 256
DSA_IDX_HEADS = 8
DSA_IDX_DIM = 32

IN_NAMES = ("mq", "mk", "mv", "cq", "ckv", "kr", "nq", "nkc", "nvc", "nks", "nvs", "nkw", "nvw",
            "ngate", "dq", "dk", "dv", "diq", "dik", "diw")
IN_SIZES = (
    GROUP_WIDTH, GROUP_WIDTH, GROUP_WIDTH,
    MLA_Q_LORA, MLA_KV_LORA, MLA_ROPE,
    GROUP_WIDTH, HEAD_DIM, HEAD_DIM, HEAD_DIM, HEAD_DIM,
    HEAD_DIM, HEAD_DIM, 3 * GROUP_HEADS,
    GROUP_WIDTH, HEAD_DIM, HEAD_DIM,
    DSA_IDX_HEADS * DSA_IDX_DIM, DSA_IDX_DIM, DSA_IDX_HEADS,
)
N_IN = sum(IN_SIZES)

LANES = 128
MXU_DTYPE = jnp.bfloat16
VMEM_LIMIT = 56 * 1024 * 1024

ATT_TILE = 512
ROW_TILE = 512
MXU_TILE = 256
FF_CHUNK = 6 * MXU_TILE

LOG2E = math.log2(math.e)
M_INIT = -1e29
DEN_LANE = HEAD_DIM


def _params(*semantics):
    return pltpu.CompilerParams(dimension_semantics=semantics, vmem_limit_bytes=VMEM_LIMIT)


def _dot(a, b):
    return jnp.dot(a.astype(MXU_DTYPE), b.astype(MXU_DTYPE), preferred_element_type=jnp.float32)


def _dot_nt(a, b):
    return lax.dot_general(a.astype(MXU_DTYPE), b.astype(MXU_DTYPE), (((1,), (1,)), ((), ())),
                           preferred_element_type=jnp.float32)


def _rms(x, g):
    return x * lax.rsqrt(jnp.mean(x * x, axis=-1, keepdims=True) + RMS_EPS) * g


def _silu(x):
    return x * (1.0 / (1.0 + jnp.exp(-x)))


def _iota(shape, dim):
    return lax.broadcasted_iota(jnp.int32, shape, dim)


def _log2(n):
    assert n & (n - 1) == 0
    return n.bit_length() - 1


def _any(pred):
    return jnp.max(jnp.where(pred, 1.0, 0.0)) > 0.5


def _lanes(x, width):
    return x if width == LANES else jnp.tile(x, (1, width // LANES))


def _ada_kernel(c_ref, w_ref, b_ref, o_ref):
    o_ref[0] = _dot(_silu(c_ref[...]), w_ref[0]) + b_ref[0]


def _ada_call(c, ada_w, ada_b):
    L, D, _ = ada_w.shape
    B = c.shape[0]
    out = pl.pallas_call(
        _ada_kernel,
        grid=(L, N_ADA),
        in_specs=[
            pl.BlockSpec((B, D), lambda l, k: (0, 0)),
            pl.BlockSpec((1, D, D), lambda l, k: (l, 0, k)),
            pl.BlockSpec((1, 1, D), lambda l, k: (l, 0, k)),
        ],
        out_specs=pl.BlockSpec((1, B, D), lambda l, k: (l, 0, k)),
        out_shape=jax.ShapeDtypeStruct((L, B, N_ADA * D), jnp.float32),
        compiler_params=_params("arbitrary", "arbitrary"),
        name="ada_proj",
    )(c, ada_w, ada_b.reshape(L, 1, N_ADA * D))
    return out.reshape(L, B, N_ADA, D)


def _ffn_chunks(F):
    bounds = list(range(0, F, FF_CHUNK)) + [F]
    return list(zip(bounds[:-1], bounds[1:]))


def _ffn_kernel(x_ref, ada_ref, gn_ref, wg_ref, wu_ref, wd_ref, *rest, k0, final):
    if final:
        fg_ref, o_ref = rest
    else:
        (o_ref,) = rest
    a = ada_ref[0]
    h = (_rms(x_ref[...], gn_ref[...]) * (1.0 + a[k0 + 1:k0 + 2]) + a[k0:k0 + 1]).astype(MXU_DTYPE)
    acc = None
    for lo, hi in _ffn_chunks(wg_ref.shape[1]):
        g = jnp.dot(h, wg_ref[:, lo:hi], preferred_element_type=jnp.float32)
        u = jnp.dot(h, wu_ref[:, lo:hi], preferred_element_type=jnp.float32)
        part = jnp.dot((_silu(g) * u).astype(MXU_DTYPE), wd_ref[lo:hi, :], preferred_element_type=jnp.float32)
        acc = part if acc is None else acc + part
    y = x_ref[...] + (FFN_RESIDUAL_WEIGHT * a[k0 + 2:k0 + 3]) * acc
    if final:
        y = _rms(y, fg_ref[...])
    o_ref[...] = y


def _ffn_call(x2d, ada_l, norm_g, w_gate, w_up, w_down, k0, tiles_per_batch, final_gain=None):
    N, D = x2d.shape
    F = w_gate.shape[1]
    tm = ROW_TILE
    in_specs = [
        pl.BlockSpec((tm, D), lambda i: (i, 0)),
        pl.BlockSpec((1, N_ADA, D), lambda i: (i // tiles_per_batch, 0, 0)),
        pl.BlockSpec((1, D), lambda i: (0, 0)),
        pl.BlockSpec((D, F), lambda i: (0, 0)),
        pl.BlockSpec((D, F), lambda i: (0, 0)),
        pl.BlockSpec((F, D), lambda i: (0, 0)),
    ]
    args = [x2d, ada_l, norm_g.reshape(1, D), w_gate.astype(MXU_DTYPE), w_up.astype(MXU_DTYPE),
            w_down.astype(MXU_DTYPE)]
    final = final_gain is not None
    if final:
        in_specs.append(pl.BlockSpec((1, D), lambda i: (0, 0)))
        args.append(final_gain.reshape(1, D))
    return pl.pallas_call(
        functools.partial(_ffn_kernel, k0=k0, final=final),
        grid=(N // tm,),
        in_specs=in_specs,
        out_specs=pl.BlockSpec((tm, D), lambda i: (i, 0)),
        out_shape=jax.ShapeDtypeStruct((N, D), jnp.float32),
        compiler_params=_params("arbitrary"),
        name="ffn",
    )(*args)


G_QMAIN, G_QSWAP, G_KC, G_V, G_MISC, ROW_GROUPS = 0, 6, 12, 14, 18, 25
T_MK, T_MKS, T_SW, T_SWS, T_DK, T_DKS, T_IK, T_ROWS = 0, 256, 512, 640, 768, 896, 1024, 1152


def _swap_halves(c, width):
    return c.reshape(-1, 2, width // 2)[:, ::-1, :].reshape(-1)


def _proj_indices():
    off = dict(zip(IN_NAMES, np.cumsum((0,) + IN_SIZES[:-1]).tolist()))
    size = dict(zip(IN_NAMES, IN_SIZES))
    cols = lambda name: np.arange(off[name], off[name] + size[name])
    zero = lambda n: np.full((n,), N_IN)
    hd = HEAD_DIM

    def head_groups(c):
        return np.concatenate([np.concatenate([c[i:i + hd], zero(LANES - hd)]) for i in range(0, c.size, hd)])

    q = np.concatenate([cols("mq"), cols("nq"), cols("dq")])
    row = np.concatenate([
        q, _swap_halves(q, hd),
        head_groups(cols("nkc")), head_groups(_swap_halves(cols("nkc"), hd)),
        cols("mv"), cols("nvs"), cols("nvw"), cols("dv"), cols("nvc"),
        cols("cq"), cols("ckv"), cols("diq"),
        cols("ngate"), zero(LANES - size["ngate"]), cols("diw"), zero(LANES - size["diw"])])
    assert row.size == ROW_GROUPS * LANES
    sw = np.concatenate([cols("nks"), cols("nkw")])
    dk_main = np.concatenate([cols("dk"), cols("kr"), zero(LANES - hd - MLA_ROPE)])
    dk_swap = np.concatenate([_swap_halves(cols("dk"), hd), _swap_halves(cols("kr"), MLA_ROPE),
                              zero(LANES - hd - MLA_ROPE)])
    tr = np.concatenate([cols("mk"), _swap_halves(cols("mk"), hd), sw, _swap_halves(sw, hd),
                         dk_main, dk_swap, np.tile(cols("dik"), LANES // DSA_IDX_DIM)])
    assert tr.size == T_ROWS
    return row, tr


def _proj_kernel(x_ref, ada_ref, gn_ref, wr_ref, wt_ref, rtab_ref, ttab_ref,
                 qn_ref, wuq_ref, kvn_ref, wukt_ref, wuv_ref,
                 mq_ref, mkt_ref, mv_ref, lq_ref, lkt_ref, lv_ref,
                 nq_ref, nkc_ref, nvc_ref, nkst_ref, nvs_ref, nkwt_ref, nvw_ref, ng_ref,
                 dq_ref, dkt_ref, dv_ref, diq_ref, dikt_ref, diw_ref, *, tm, tpb, n_moba, n_sel):
    H, hd, G = GROUP_HEADS, HEAD_DIM, LANES
    t0 = (pl.program_id(0) % tpb) * tm
    a = ada_ref[0]
    h = (_rms(x_ref[...], gn_ref[...]) * (1.0 + a[4:5]) + a[3:4]).astype(MXU_DTYPE)

    def rows(g0, n):
        return jnp.dot(h, wr_ref[:, g0 * G:(g0 + n) * G], preferred_element_type=jnp.float32)

    def cols(r0, n):
        return _dot_nt(wt_ref[r0:r0 + n, :], h)

    low_half = _iota((tm, G), 1) < hd

    def split(pair, fill):
        return (jnp.where(low_half, pair, fill), jnp.where(low_half, pltpu.roll(pair, hd, 1), fill))

    n_pairs = 3 * H // 2
    roped_q = (rows(G_QMAIN, n_pairs) * _lanes(rtab_ref[0], n_pairs * G)
               + rows(G_QSWAP, n_pairs) * _lanes(rtab_ref[1], n_pairs * G))
    for i, ref in enumerate((mq_ref, nq_ref, dq_ref)):
        for p in range(H // 2):
            g = i * (H // 2) + p
            even, odd = split(roped_q[:, g * G:(g + 1) * G], 0.0)
            ref[0, 2 * p] = even.astype(ref.dtype)
            ref[0, 2 * p + 1] = odd.astype(ref.dtype)
    kc = rows(G_KC, 2)
    nkc_ref[0] = (kc[:, :G] * rtab_ref[2] + kc[:, G:] * rtab_ref[3])[:, :hd].astype(nkc_ref.dtype)

    v = rows(G_V, H // 2 + 2)
    for p in range(H // 2):
        even, odd = split(v[:, p * G:(p + 1) * G], 1.0)
        mv_ref[0, 2 * p] = even.astype(mv_ref.dtype)
        mv_ref[0, 2 * p + 1] = odd.astype(mv_ref.dtype)
    nvs, nvw = split(v[:, (H // 2) * G:(H // 2 + 1) * G], 1.0)
    nvs_ref[0] = nvs.astype(nvs_ref.dtype)
    nvw_ref[0] = nvw.astype(nvw_ref.dtype)
    dv, nvc = split(v[:, (H // 2 + 1) * G:(H // 2 + 2) * G], 1.0)
    dv_ref[0] = dv.astype(dv_ref.dtype)
    nvc_ref[0] = nvc[:, :hd].astype(nvc_ref.dtype)
    ones_hi = jnp.where(_iota((1, G), 1) >= DEN_LANE, 1.0, 0.0)

    misc = rows(G_MISC, 7)
    cq = misc[:, :MLA_Q_LORA]
    ckv = misc[:, MLA_Q_LORA:MLA_Q_LORA + MLA_KV_LORA]
    diq_ref[0] = misc[:, 3 * G:5 * G].astype(diq_ref.dtype)
    ng_ref[0] = misc[:, 5 * G:6 * G]
    diw_ref[0] = misc[:, 6 * G:7 * G]

    tok = t0 + _iota((hd, tm), 1)
    rid = _iota((hd, tm), 0)
    oh_moba = jnp.where((rid & (n_moba - 1)) == (tok >> _log2(MOBA_BLOCK)), 1.0, 0.0)
    oh_sel = jnp.where((rid == (tok >> _log2(NSA_SEL_BLOCK))) & (rid < n_sel), 1.0, 0.0)
    zeros_lo = jnp.zeros((hd, tm), jnp.float32)
    ta_c, ta_s, tb_c, tb_s = ttab_ref[0], ttab_ref[1], ttab_ref[2], ttab_ref[3]
    mkt = (cols(T_MK, H * hd) * jnp.tile(ta_c, (H * hd // G, 1))
           + cols(T_MKS, H * hd) * jnp.tile(ta_s, (H * hd // G, 1)))
    for hh in range(H):
        mine = (rid >> _log2(n_moba)) == hh
        mkt_ref[0, hh] = jnp.concatenate([mkt[hh * hd:(hh + 1) * hd], jnp.where(mine, oh_moba, 0.0)],
                                         axis=0).astype(mkt_ref.dtype)
    sw = cols(T_SW, G) * ta_c + cols(T_SWS, G) * ta_s
    nkst_ref[0] = jnp.concatenate([sw[:hd], oh_sel], axis=0).astype(nkst_ref.dtype)
    nkwt_ref[0] = jnp.concatenate([sw[hd:], zeros_lo], axis=0).astype(nkwt_ref.dtype)
    dkr = cols(T_DK, G) * tb_c + cols(T_DKS, G) * tb_s
    rid2 = _iota((G, tm), 0)
    dkt_ref[0] = jnp.where(rid2 < hd, dkr, 0.0).astype(dkt_ref.dtype)
    kpe_rows = jnp.where((rid2 >= MLA_NOPE) & (rid2 < MLA_QK), dkr, 0.0)
    dikt_ref[0] = cols(T_IK, G).astype(dikt_ref.dtype)

    cqn = _rms(cq, qn_ref[...]).astype(MXU_DTYPE)
    lq = (jnp.dot(cqn, wuq_ref[:, :H * G], preferred_element_type=jnp.float32) * _lanes(rtab_ref[4], H * G)
          + jnp.dot(cqn, wuq_ref[:, H * G:], preferred_element_type=jnp.float32) * _lanes(rtab_ref[5], H * G))
    ckvn = _rms(ckv, kvn_ref[...]).astype(MXU_DTYPE)
    knt = _dot_nt(wukt_ref[...], ckvn)
    lv = jnp.dot(ckvn, wuv_ref[...], preferred_element_type=jnp.float32) + _lanes(ones_hi, H * G)
    for hh in range(H):
        lq_ref[0, hh] = lq[:, hh * G:(hh + 1) * G].astype(lq_ref.dtype)
        lkt_ref[0, hh] = (knt[hh * G:(hh + 1) * G] + kpe_rows).astype(lkt_ref.dtype)
        lv_ref[0, hh] = lv[:, hh * G:(hh + 1) * G].astype(lv_ref.dtype)


def _rope_tables(T):
    def cs(dim):
        inv_freq = 1.0 / (ROPE_THETA ** (np.arange(0, dim, 2, dtype=np.float32) / dim))
        ang = jnp.arange(T, dtype=jnp.float32)[:, None] * jnp.asarray(inv_freq, jnp.float32)[None, :]
        cos, sin = jnp.cos(ang), jnp.sin(ang)
        return jnp.concatenate([cos, cos], axis=-1), jnp.concatenate([-sin, sin], axis=-1)

    c64, s64 = cs(HEAD_DIM)
    c32, s32 = cs(MLA_ROPE)
    pad = lambda t, n: jnp.concatenate([t, jnp.zeros((T, n), jnp.float32)], axis=-1)
    sc = HEAD_DIM ** -0.5 * LOG2E
    sl = MLA_QK ** -0.5 * LOG2E
    ones = jnp.ones((T, MLA_NOPE), jnp.float32)
    rest = LANES - MLA_QK
    rtab = jnp.stack([
        jnp.tile(c64 * sc, (1, LANES // HEAD_DIM)), jnp.tile(s64 * sc, (1, LANES // HEAD_DIM)),
        pad(c64, LANES - HEAD_DIM), pad(s64, LANES - HEAD_DIM),
        pad(jnp.concatenate([ones, c32], axis=-1) * sl, rest),
        pad(jnp.concatenate([0.0 * ones, s32], axis=-1) * sl, rest)])
    ttab = jnp.stack([
        jnp.concatenate([c64, c64], axis=-1).T, jnp.concatenate([s64, s64], axis=-1).T,
        pad(jnp.concatenate([c64, c32], axis=-1), rest).T, pad(jnp.concatenate([s64, s32], axis=-1), rest).T])
    return rtab, ttab


def _proj_call(x2d, ada_l, norm_g, w_in, tables, mla_q_norm, mla_w_uq, mla_kv_norm, mla_w_uk, mla_w_uv, B, T):
    N, D = x2d.shape
    H, G = GROUP_HEADS, LANES
    tm = ROW_TILE
    tpb = T // tm
    n_moba, n_sel = T // MOBA_BLOCK, T // NSA_SEL_BLOCK
    assert H * n_moba <= LANES - HEAD_DIM and n_sel <= LANES - HEAD_DIM and n_moba & (n_moba - 1) == 0
    zcol = lambda w: jnp.concatenate([w, jnp.zeros((w.shape[0], 1), w.dtype)], axis=1)
    row_idx, tr_idx = _proj_indices()
    w_ext = zcol(w_in)
    w_row = w_ext[:, row_idx].astype(MXU_DTYPE)
    w_tr = w_ext[:, tr_idx].T.astype(MXU_DTYPE)
    zq = mla_w_uq.shape[1]
    per_head = np.arange(H * MLA_QK).reshape(H, MLA_QK)
    main = np.concatenate([np.concatenate([per_head[i], np.full((G - MLA_QK,), zq)]) for i in range(H)])
    part = np.concatenate([np.concatenate([np.full((MLA_NOPE,), zq), _swap_halves(per_head[i, MLA_NOPE:], MLA_ROPE),
                                           np.full((G - MLA_QK,), zq)]) for i in range(H)])
    wuq = zcol(mla_w_uq)[:, np.concatenate([main, part])].astype(MXU_DTYPE)
    zv = mla_w_uk.shape[1]
    grp = np.concatenate([np.concatenate([np.arange(i * HEAD_DIM, (i + 1) * HEAD_DIM), np.full((G - HEAD_DIM,), zv)])
                          for i in range(H)])
    wukt = zcol(mla_w_uk)[:, grp].T.astype(MXU_DTYPE)
    wuv = zcol(mla_w_uv)[:, grp].astype(MXU_DTYPE)
    rtab, ttab = tables

    row = lambda i: (i, 0)
    const2 = lambda i: (0, 0)
    in_specs = [
        pl.BlockSpec((tm, D), row),
        pl.BlockSpec((1, N_ADA, D), lambda i: (i // tpb, 0, 0)),
        pl.BlockSpec((1, D), const2),
        pl.BlockSpec(w_row.shape, const2),
        pl.BlockSpec(w_tr.shape, const2),
        pl.BlockSpec((6, tm, G), lambda i: (0, i % tpb, 0)),
        pl.BlockSpec((4, G, tm), lambda i: (0, 0, i % tpb)),
        pl.BlockSpec((1, MLA_Q_LORA), const2),
        pl.BlockSpec(wuq.shape, const2),
        pl.BlockSpec((1, MLA_KV_LORA), const2),
        pl.BlockSpec(wukt.shape, const2),
        pl.BlockSpec(wuv.shape, const2),
    ]
    dt = MXU_DTYPE
    hq = (jax.ShapeDtypeStruct((B, H, T, G), dt), pl.BlockSpec((1, H, tm, G), lambda i: (i // tpb, 0, i % tpb, 0)))
    hkt = (jax.ShapeDtypeStruct((B, H, G, T), dt), pl.BlockSpec((1, H, G, tm), lambda i: (i // tpb, 0, 0, i % tpb)))
    srow = lambda d, t=dt: (jax.ShapeDtypeStruct((B, T, d), t), pl.BlockSpec((1, tm, d), lambda i: (i // tpb, i % tpb, 0)))
    skt = (jax.ShapeDtypeStruct((B, G, T), dt), pl.BlockSpec((1, G, tm), lambda i: (i // tpb, 0, i % tpb)))
    outs = [hq, hkt, hq,
            hq, hkt, hq,
            hq, srow(HEAD_DIM), srow(HEAD_DIM), skt, srow(G), skt, srow(G), srow(G, jnp.float32),
            hq, skt, srow(G), srow(2 * G), skt, srow(G, jnp.float32)]
    return pl.pallas_call(
        functools.partial(_proj_kernel, tm=tm, tpb=tpb, n_moba=n_moba, n_sel=n_sel),
        grid=(N // tm,),
        in_specs=in_specs,
        out_specs=[o[1] for o in outs],
        out_shape=[o[0] for o in outs],
        compiler_params=_params("arbitrary"),
        name="mixer_in_proj",
    )(x2d, ada_l, norm_g.reshape(1, D), w_row, w_tr, rtab, ttab,
      mla_q_norm.reshape(1, -1), wuq, mla_kv_norm.reshape(1, -1), wukt, wuv)


def _flash_init(m_scr, acc_scr):
    m_scr[...] = jnp.full(m_scr.shape, M_INIT, jnp.float32)
    acc_scr[...] = jnp.zeros_like(acc_scr)


def _flash_update(h, s, v, m_scr, acc_scr, rows=slice(None)):
    m_prev = m_scr[h, rows]
    m_new = jnp.maximum(m_prev, jnp.max(s, axis=-1, keepdims=True))
    p = jnp.exp2(s - _lanes(m_new, s.shape[1]))
    acc_scr[h, rows] = jnp.exp2(m_prev - m_new) * acc_scr[h, rows] + _dot(p, v)
    m_scr[h, rows] = m_new


def _flash_out(h, acc_scr):
    acc = acc_scr[h]
    den = acc[:, DEN_LANE:DEN_LANE + 1]
    return acc[:, :HEAD_DIM] / jnp.where(den > 0.0, den, 1.0)


def _causal_bias(t):
    return jnp.where(_iota((t, t), 1) <= _iota((t, t), 0), 0.0, NEG_INF)


def _rank_desc(x):
    n = x.shape[0]
    row = _iota(x.shape, 0)
    rank = jnp.zeros(x.shape, jnp.float32)
    for j in range(n):
        cand = x[j:j + 1, :]
        rank = rank + jnp.where(cand > x, 1.0, jnp.where((cand == x) & (row > j), 1.0, 0.0))
    return rank


def _att_scratch(tq, stacked=False):
    shape = (1, GROUP_HEADS * tq, LANES) if stacked else (GROUP_HEADS, tq, LANES)
    return [pltpu.VMEM(shape, jnp.float32), pltpu.VMEM(shape, jnp.float32)]


def _att_specs(B, H, T, tq, shared_kv):
    q = pl.BlockSpec((1, H, tq, LANES), lambda b, i: (b, 0, i, 0))
    if shared_kv:
        kt = pl.BlockSpec((1, LANES, T), lambda b, i: (b, 0, 0))
        v = pl.BlockSpec((1, T, LANES), lambda b, i: (b, 0, 0))
    else:
        kt = pl.BlockSpec((1, H, LANES, T), lambda b, i: (b, 0, 0, 0))
        v = pl.BlockSpec((1, H, T, LANES), lambda b, i: (b, 0, 0, 0))
    out = pl.BlockSpec((1, tq, H * HEAD_DIM), lambda b, i: (b, i, 0))
    return q, kt, v, out


def _causal_sweep(qs, kt_ref, v_ref, qi, tq, tk, m_scr, acc_scr):
    H = len(qs)
    _flash_init(m_scr, acc_scr)

    def body(c, carry):
        k0 = pl.multiple_of(c * tk, tk)
        for h in range(H):
            s = jnp.dot(qs[h], kt_ref[0, h, :, pl.ds(k0, tk)], preferred_element_type=jnp.float32)
            _flash_update(h, s, v_ref[0, h, pl.ds(k0, tk), :], m_scr, acc_scr)
        return carry

    lax.fori_loop(0, 2 * qi, body, 0)
    k0 = pl.multiple_of(qi * tq, tk)
    k1 = pl.multiple_of(qi * tq + tk, tk)
    left = jnp.where(_iota((tq, tk), 1) <= _iota((tq, tk), 0), 0.0, NEG_INF)
    right = _causal_bias(tk)
    for h in range(H):
        s = jnp.dot(qs[h], kt_ref[0, h, :, pl.ds(k0, tk)], preferred_element_type=jnp.float32) + left
        _flash_update(h, s, v_ref[0, h, pl.ds(k0, tk), :], m_scr, acc_scr)
        s = jnp.dot(qs[h][tk:], kt_ref[0, h, :, pl.ds(k1, tk)], preferred_element_type=jnp.float32) + right
        _flash_update(h, s, v_ref[0, h, pl.ds(k1, tk), :], m_scr, acc_scr, slice(tk, tq))


def _mla_kernel(q_ref, kt_ref, v_ref, o_ref, m_scr, acc_scr, *, tq, tk):
    H = GROUP_HEADS
    _causal_sweep([q_ref[0, h] for h in range(H)], kt_ref, v_ref, pl.program_id(1), tq, tk, m_scr, acc_scr)
    for h in range(H):
        o_ref[0, :, h * HEAD_DIM:(h + 1) * HEAD_DIM] = _flash_out(h, acc_scr)


def _mla_call(q, kt, v):
    B, H, T, _ = q.shape
    tk = ATT_TILE
    tq = 2 * tk
    assert T % tq == 0
    qs, ks, vs, out = _att_specs(B, H, T, tq, False)
    return pl.pallas_call(
        functools.partial(_mla_kernel, tq=tq, tk=tk),
        grid=(B, T // tq),
        in_specs=[qs, ks, vs],
        out_specs=out,
        out_shape=jax.ShapeDtypeStruct((B, T, H * HEAD_DIM), jnp.float32),
        scratch_shapes=_att_scratch(tq),
        compiler_params=_params("arbitrary", "arbitrary"),
        name="mla_attention",
    )(q, kt, v)


def _moba_kernel(q_ref, kt_ref, v_ref, o_ref, kmean_scr, m_scr, acc_scr, *, tq, tk, nb, topk):
    H, hd = GROUP_HEADS, HEAD_DIM
    qi = pl.program_id(1)
    q0 = qi * tq
    T = nb * MOBA_BLOCK

    @pl.when(qi == 0)
    def _():
        avg = jnp.where((_iota((nb, T), 1) >> _log2(MOBA_BLOCK)) == _iota((nb, T), 0), 1.0 / MOBA_BLOCK, 0.0)
        for h in range(H):
            kmean_scr[h] = _dot_nt(avg, kt_ref[0, h])

    blk = _iota((nb, tq), 0)
    own = (q0 + _iota((nb, tq), 1)) >> _log2(MOBA_BLOCK)
    past = blk < own
    bias_rows = [jnp.zeros((hd, tq), jnp.float32)]
    for h in range(H):
        gate = jnp.where(past, _dot_nt(kmean_scr[h], q_ref[0, h]), NEG_INF)
        allowed = ((_rank_desc(gate) < topk) & past) | (blk == own)
        bias_rows.append(jnp.where(allowed, 0.0, NEG_INF))
    if H * nb < LANES - hd:
        bias_rows.append(jnp.zeros((LANES - hd - H * nb, tq), jnp.float32))
    bias = jnp.concatenate(bias_rows, axis=0).T
    lane = _iota((tq, LANES), 1)
    qs = [jnp.where(lane < hd, q_ref[0, h], bias.astype(q_ref.dtype)) for h in range(H)]
    _causal_sweep(qs, kt_ref, v_ref, qi, tq, tk, m_scr, acc_scr)
    for h in range(H):
        o_ref[0, :, h * hd:(h + 1) * hd] = _flash_out(h, acc_scr)


def _moba_call(q, kt, v):
    B, H, T, _ = q.shape
    tk = ATT_TILE
    tq = 2 * tk
    assert T % tq == 0
    nb = T // MOBA_BLOCK
    qs, ks, vs, out = _att_specs(B, H, T, tq, False)
    return pl.pallas_call(
        functools.partial(_moba_kernel, tq=tq, tk=tk, nb=nb, topk=min(MOBA_TOPK, nb - 1)),
        grid=(B, T // tq),
        in_specs=[qs, ks, vs],
        out_specs=out,
        out_shape=jax.ShapeDtypeStruct((B, T, H * HEAD_DIM), jnp.float32),
        scratch_shapes=[pltpu.VMEM((H, nb, LANES), jnp.float32)] + _att_scratch(tq),
        compiler_params=_params("arbitrary", "arbitrary"),
        name="moba_attention",
    )(q, kt, v)


def _cmp_kernel(k_ref, v_ref, pek_ref, pev_ref, kw1_ref, kw2_ref, vw1_ref, vw2_ref, ko_ref, vo_ref, *, rows):
    half = NSA_CMP_STRIDE * HEAD_DIM

    def compress(t_ref, pe_ref, w1_ref, w2_ref):
        t = t_ref[0].astype(jnp.float32)
        first = _dot(t + pe_ref[0:1, :], w1_ref[0:half, :])
        second = _dot(t + pe_ref[1:2, :], w1_ref[half:, :])
        hid = first + pltpu.roll(second, rows - 1, 0)
        return _dot(_silu(hid), w2_ref[...])

    ko_ref[0] = compress(k_ref, pek_ref, kw1_ref, kw2_ref)
    vo_ref[0] = compress(v_ref, pev_ref, vw1_ref, vw2_ref)


def _cmp_call(kc, vc, pe_k, pe_v, k_w1, k_w2, v_w1, v_w2):
    B, T, d = kc.shape
    rows = T // NSA_CMP_STRIDE
    wide = NSA_CMP_STRIDE * d
    assert NSA_CMP_LEN == 2 * NSA_CMP_STRIDE
    const2 = lambda b: (0, 0)
    blk = pl.BlockSpec((1, rows, wide), lambda b: (b, 0, 0))
    out = pl.BlockSpec((1, rows, d), lambda b: (b, 0, 0))
    return pl.pallas_call(
        functools.partial(_cmp_kernel, rows=rows),
        grid=(B,),
        in_specs=[blk, blk, pl.BlockSpec((2, wide), const2), pl.BlockSpec((2, wide), const2),
                  pl.BlockSpec(k_w1.shape, const2), pl.BlockSpec(k_w2.shape, const2),
                  pl.BlockSpec(v_w1.shape, const2), pl.BlockSpec(v_w2.shape, const2)],
        out_specs=[out, out],
        out_shape=[jax.ShapeDtypeStruct((B, rows, d), jnp.float32)] * 2,
        compiler_params=_params("arbitrary"),
        name="nsa_compress",
    )(kc.reshape(B, rows, wide), vc.reshape(B, rows, wide), pe_k.reshape(2, wide), pe_v.reshape(2, wide),
      k_w1.astype(MXU_DTYPE), k_w2.astype(MXU_DTYPE), v_w1.astype(MXU_DTYPE), v_w2.astype(MXU_DTYPE))


def _nsa_kernel(q_ref, kcmp_ref, vcmp_ref, kst_ref, vs_ref, kwt_ref, vw_ref, g_ref, o_ref,
                oc_scr, os_scr, m_scr, acc_scr, *, tq, ncp, n_sel, topn):
    H, hd = GROUP_HEADS, HEAD_DIM
    qi = pl.program_id(1)
    q0 = qi * tq
    tq_col = q0 + _iota((tq, 1), 0)
    qraw = [q_ref[0, h] for h in range(H)]

    cmp_end = _iota((tq, ncp), 1) * NSA_CMP_STRIDE + (NSA_CMP_LEN - 1)
    m_c = cmp_end <= tq_col
    p_sum = jnp.zeros((tq, ncp), jnp.float32)
    for h in range(H):
        s = jnp.where(m_c, _dot_nt(qraw[h][:, :hd], kcmp_ref[0]), NEG_INF)
        e = jnp.where(m_c, jnp.exp2(s - jnp.max(s, axis=-1, keepdims=True)), 0.0)
        l = jnp.sum(e, axis=-1, keepdims=True)
        p = e / jnp.where(l > 0.0, l, 1.0)
        p_sum = p_sum + p
        oc_scr[h] = _dot(p, vcmp_ref[0])

    cmp_start = _iota((n_sel, ncp), 1) * NSA_CMP_STRIDE
    sel_start = _iota((n_sel, ncp), 0) * NSA_SEL_BLOCK
    overlap = (cmp_start < sel_start + NSA_SEL_BLOCK) & (cmp_start + NSA_CMP_LEN > sel_start)
    imp = _dot_nt(jnp.where(overlap, 1.0, 0.0), p_sum)
    sel_id = _iota((n_sel, tq), 0)
    own = (q0 + _iota((n_sel, tq), 1)) >> _log2(NSA_SEL_BLOCK)
    causal = sel_id <= own
    forced = causal & ((sel_id == 0) | (sel_id >= own - 1))
    imp = jnp.where(forced, NSA_FORCE_SCORE, jnp.where(causal, imp, -NSA_FORCE_SCORE))
    bias_rows = [jnp.zeros((hd, tq), jnp.float32), jnp.where(_rank_desc(imp) < topn, 0.0, NEG_INF)]
    if n_sel < LANES - hd:
        bias_rows.append(jnp.zeros((LANES - hd - n_sel, tq), jnp.float32))
    bias = jnp.concatenate(bias_rows, axis=0).T.astype(qraw[0].dtype)
    lane = _iota((tq, LANES), 1)
    qsel = [jnp.where(lane < hd, qraw[h], bias) for h in range(H)]

    def stack(per_head):
        return jnp.concatenate(per_head, axis=0)

    def tall(bias):
        return jnp.tile(bias, (H, 1))

    q_sel, q_raw = stack(qsel), stack(qraw)
    _flash_init(m_scr, acc_scr)

    def sel_chunk(k0, extra):
        s = jnp.dot(q_sel, kst_ref[0, :, pl.ds(k0, tq)], preferred_element_type=jnp.float32)
        if extra is not None:
            s = s + tall(extra)
        _flash_update(0, s, vs_ref[0, pl.ds(k0, tq), :], m_scr, acc_scr)

    def sel_body(c, carry):
        sel_chunk(pl.multiple_of(c * tq, tq), None)
        return carry

    lax.fori_loop(0, qi, sel_body, 0)
    sel_chunk(pl.multiple_of(q0, tq), _causal_bias(tq))
    o_sel = _flash_out(0, acc_scr)
    for h in range(H):
        os_scr[h] = o_sel[h * tq:(h + 1) * tq]

    _flash_init(m_scr, acc_scr)
    first = jnp.maximum(q0 - NSA_WINDOW + 1, 0) // tq

    def win_body(c, carry):
        k0 = pl.multiple_of(c * tq, tq)
        qpos = q0 + _iota((tq, tq), 0)
        kpos = k0 + _iota((tq, tq), 1)
        band = jnp.where((kpos <= qpos) & (kpos > qpos - NSA_WINDOW), 0.0, NEG_INF)
        s = jnp.dot(q_raw, kwt_ref[0, :, pl.ds(k0, tq)], preferred_element_type=jnp.float32) + tall(band)
        _flash_update(0, s, vw_ref[0, pl.ds(k0, tq), :], m_scr, acc_scr)
        return carry

    lax.fori_loop(first, qi + 1, win_body, 0)
    o_win = _flash_out(0, acc_scr)
    for h in range(H):
        gates = 1.0 / (1.0 + jnp.exp(-g_ref[0][:, 3 * h:3 * h + 3]))
        o_ref[0, :, h * hd:(h + 1) * hd] = (gates[:, 0:1] * oc_scr[h] + gates[:, 1:2] * os_scr[h]
                                            + gates[:, 2:3] * o_win[h * tq:(h + 1) * tq])


def _nsa_call(q, kcmp, vcmp, kst, vs, kwt, vw, gate_logits):
    B, H, T, _ = q.shape
    tq = ATT_TILE
    ncp = kcmp.shape[1]
    n_sel = T // NSA_SEL_BLOCK
    qs, ks, vsp, out = _att_specs(B, H, T, tq, True)
    cmp_spec = pl.BlockSpec((1, ncp, HEAD_DIM), lambda b, i: (b, 0, 0))
    return pl.pallas_call(
        functools.partial(_nsa_kernel, tq=tq, ncp=ncp, n_sel=n_sel, topn=min(NSA_SEL_TOPN, n_sel)),
        grid=(B, T // tq),
        in_specs=[qs, cmp_spec, cmp_spec, ks, vsp, ks, vsp,
                  pl.BlockSpec((1, tq, LANES), lambda b, i: (b, i, 0))],
        out_specs=out,
        out_shape=jax.ShapeDtypeStruct((B, T, H * HEAD_DIM), jnp.float32),
        scratch_shapes=[pltpu.VMEM((H, tq, HEAD_DIM), jnp.float32), pltpu.VMEM((H, tq, HEAD_DIM), jnp.float32)]
        + _att_scratch(tq, stacked=True),
        compiler_params=_params("arbitrary", "arbitrary"),
        name="nsa_attention",
    )(q, kcmp, vcmp, kst, vs, kwt, vw, gate_logits)


def _sortable(x):
    b = int(np.float32(x).view(np.int32))
    return b ^ ((b >> 31) & 0x7FFFFFFF)


def _from_sortable(k):
    return lax.bitcast_convert_type(k ^ ((k >> 31) & 0x7FFFFFFF), jnp.float32)


def _to_sortable(x):
    k = lax.bitcast_convert_type(x, jnp.int32)
    return k ^ ((k >> 31) & 0x7FFFFFFF)


VALUE_STEPS = 24


def _dsa_kernel(q_ref, kt_ref, v_ref, iq_ref, ikt_ref, iw_ref, o_ref,
                st_scr, b_scr, m_scr, acc_scr, *, tq, topk, idx_scale):
    H, hd = GROUP_HEADS, HEAD_DIM
    qi = pl.program_id(1)
    q0 = qi * tq
    n_kc = qi + 1
    sub = 8

    def fold(x, reduce):
        return reduce(x.reshape(x.shape[0] // sub, sub, tq), axis=0)

    lane = _iota((tq, LANES), 1)
    quarter = lane >> _log2(DSA_IDX_DIM)
    per_group = LANES // DSA_IDX_DIM
    iq = iq_ref[0].astype(jnp.float32)
    iqt = jnp.concatenate(
        [jnp.where(quarter == (h % per_group), iq[:, (h // per_group) * LANES:(h // per_group + 1) * LANES], 0.0).T
         for h in range(DSA_IDX_HEADS)], axis=1).astype(MXU_DTYPE)
    iwt = iw_ref[0].T

    diag_causal = _iota((tq, tq), 0) <= _iota((tq, tq), 1)

    def score_chunk(c, ends, diagonal):
        top, bot = ends
        k0 = pl.multiple_of(c * tq, tq)
        ik = ikt_ref[0, :, pl.ds(k0, tq)].astype(jnp.float32).T
        sh = _dot(ik, iqt)
        acc = jnp.zeros((tq, tq), jnp.float32)
        for h in range(DSA_IDX_HEADS):
            acc = acc + iwt[h:h + 1, :] * jnp.maximum(sh[:, h * tq:(h + 1) * tq], 0.0)
        sc = low = acc * idx_scale + 0.0
        if diagonal:
            sc = jnp.where(diag_causal, sc, NEG_INF)
            low = jnp.where(diag_causal, low, np.inf)
        st_scr[pl.ds(k0, tq), :] = sc
        return jnp.maximum(top, fold(sc, jnp.max)), jnp.minimum(bot, fold(low, jnp.min))

    ends = lax.fori_loop(0, qi, functools.partial(score_chunk, diagonal=False),
                         (jnp.full((sub, tq), NEG_INF, jnp.float32), jnp.full((sub, tq), np.inf, jnp.float32)))
    top, bot = score_chunk(qi, ends, diagonal=True)
    row_max = jnp.max(top, axis=0, keepdims=True)
    row_min = jnp.min(bot, axis=0, keepdims=True)

    def count_ge(t, strict=False):
        above = (lambda a, b: a > b) if strict else (lambda a, b: a >= b)
        t_sub = jnp.broadcast_to(t, (sub, tq))

        def body(c, part):
            k0 = pl.multiple_of(c * tq, tq)
            sc = st_scr[pl.ds(k0, tq), :].reshape(tq // sub, sub, tq)
            return part + jnp.sum(jnp.where(above(sc, t_sub[None]), 1.0, 0.0), axis=0)

        part = lax.fori_loop(0, n_kc, body, jnp.zeros((sub, tq), jnp.float32))
        return jnp.sum(part, axis=0, keepdims=True)

    kf = float(topk)
    floor_key = _sortable(NEG_INF)

    def bis_cond(c):
        return c[-1]

    def bis_body(c):
        it, lo, hi, n_lo, _ = c
        active = _any(lo < hi)
        key_mid = (lo | hi) - ((lo ^ hi) >> 1)
        val_mid = _to_sortable(0.5 * (_from_sortable(lo) + _from_sortable(hi)))
        steps = jnp.zeros(lo.shape, jnp.int32) + it
        use_val = (val_mid > lo) & (val_mid <= hi) & (lo > floor_key) & (steps < VALUE_STEPS)
        mid = jnp.where(use_val, val_mid, key_mid)
        cnt = count_ge(_from_sortable(mid))
        ge = cnt >= kf
        lo = jnp.where(ge, mid, lo)
        n_lo = jnp.where(ge, cnt, n_lo)
        hi = jnp.where(cnt == kf, mid, jnp.where(ge, hi, mid - 1))
        return it + 1, lo, hi, n_lo, active

    zero = jnp.zeros((1, tq), jnp.float32)
    n_pos = count_ge(zero, strict=True)
    n_nonneg = count_ge(zero)
    n_causal = q0 + _iota((1, tq), 1) + 1
    lo_neg = jnp.where(n_causal >= topk, _to_sortable(row_min), floor_key)
    n_neg = jnp.where(n_causal >= topk, n_causal, n_kc * tq).astype(jnp.float32)
    is_pos = n_pos >= kf
    is_zero = n_nonneg >= kf
    pick = lambda p, z, n: jnp.where(is_pos, p, jnp.where(is_zero, z, n))
    lo0 = pick(_sortable(np.float32(1e-45)), _sortable(0.0), lo_neg)
    hi0 = pick(_to_sortable(row_max), _sortable(0.0), _sortable(-0.0) - 1)
    _, lo, _, n_ge, _ = lax.while_loop(
        bis_cond, bis_body, (jnp.int32(0), lo0, hi0, pick(n_pos, n_nonneg, n_neg), _any(lo0 < hi0)))
    thr = _from_sortable(lo)

    any_tie = _any(n_ge > kf)

    @pl.when(jnp.logical_not(any_tie))
    def _():
        def mask_chunk(c, carry, diagonal):
            k0 = pl.multiple_of(c * tq, tq)
            picked = st_scr[pl.ds(k0, tq), :] >= thr
            if diagonal:
                picked = picked & diag_causal
            b_scr[:, pl.ds(k0, tq)] = jnp.where(picked, 0.0, NEG_INF).T
            return carry
        lax.fori_loop(0, qi, functools.partial(mask_chunk, diagonal=False), 0)
        mask_chunk(qi, 0, diagonal=True)

    @pl.when(any_tie)
    def _():
        need = kf - count_ge(thr, strict=True)
        prefix = jnp.where(_iota((tq, tq), 1) <= _iota((tq, tq), 0), 1.0, 0.0)

        def mask_chunk(c, seen, diagonal):
            k0 = pl.multiple_of(c * tq, tq)
            sc = st_scr[pl.ds(k0, tq), :]
            eq = jnp.where(sc == thr, 1.0, 0.0)
            rank_eq = _dot(prefix, eq) + seen
            picked = (sc > thr) | ((sc == thr) & (rank_eq <= need))
            if diagonal:
                picked = picked & diag_causal
            b_scr[:, pl.ds(k0, tq)] = jnp.where(picked, 0.0, NEG_INF).T
            return seen + jnp.sum(fold(eq, jnp.sum), axis=0, keepdims=True)
        seen = lax.fori_loop(0, qi, functools.partial(mask_chunk, diagonal=False), jnp.zeros((1, tq), jnp.float32))
        mask_chunk(qi, seen, diagonal=True)

    q_all = jnp.concatenate([q_ref[0, h] for h in range(H)], axis=0)
    _flash_init(m_scr, acc_scr)

    def att_body(c, carry):
        k0 = pl.multiple_of(c * tq, tq)
        bias = b_scr[:, pl.ds(k0, tq)]
        s = jnp.dot(q_all, kt_ref[0, :, pl.ds(k0, tq)], preferred_element_type=jnp.float32)
        _flash_update(0, s + jnp.tile(bias, (H, 1)), v_ref[0, pl.ds(k0, tq), :], m_scr, acc_scr)
        return carry

    lax.fori_loop(0, n_kc, att_body, 0)
    out = _flash_out(0, acc_scr)
    for h in range(H):
        o_ref[0, :, h * hd:(h + 1) * hd] = out[h * tq:(h + 1) * tq]


def _dsa_call(q, kt, v, iq, ikt, iw):
    B, H, T, _ = q.shape
    tq = ATT_TILE
    topk = min(DSA_TOPK, T // 4)
    assert tq >= topk
    qs, ks, vs, out = _att_specs(B, H, T, tq, True)
    return pl.pallas_call(
        functools.partial(_dsa_kernel, tq=tq, topk=topk, idx_scale=(DSA_IDX_HEADS * DSA_IDX_DIM) ** -0.5),
        grid=(B, T // tq),
        in_specs=[qs, ks, vs, pl.BlockSpec((1, tq, 2 * LANES), lambda b, i: (b, i, 0)), ks,
                  pl.BlockSpec((1, tq, LANES), lambda b, i: (b, i, 0))],
        out_specs=out,
        out_shape=jax.ShapeDtypeStruct((B, T, H * HEAD_DIM), jnp.float32),
        scratch_shapes=[pltpu.VMEM((T, tq), jnp.float32), pltpu.VMEM((tq, T), jnp.float32)]
        + _att_scratch(tq, stacked=True),
        compiler_params=_params("arbitrary", "arbitrary"),
        name="dsa_attention",
    )(q, kt, v, iq, ikt, iw)


def _out_kernel(x_ref, ada_ref, o1_ref, o2_ref, o3_ref, o4_ref, gn_ref, w_ref, y_ref):
    a = ada_ref[0]
    gn = gn_ref[...]
    y = jnp.concatenate([_rms(o[0], gn[i:i + 1]).astype(MXU_DTYPE)
                         for i, o in enumerate((o1_ref, o2_ref, o3_ref, o4_ref))], axis=-1)
    y_ref[...] = x_ref[...] + a[5:6] * jnp.dot(y, w_ref[...], preferred_element_type=jnp.float32)


def _out_call(x2d, ada_l, groups, group_norm, w_out, B, T):
    N, D = x2d.shape
    tm = ROW_TILE
    tpb = T // tm
    grp = pl.BlockSpec((1, tm, GROUP_WIDTH), lambda i: (i // tpb, i % tpb, 0))
    return pl.pallas_call(
        _out_kernel,
        grid=(N // tm,),
        in_specs=[pl.BlockSpec((tm, D), lambda i: (i, 0)),
                  pl.BlockSpec((1, N_ADA, D), lambda i: (i // tpb, 0, 0)),
                  grp, grp, grp, grp,
                  pl.BlockSpec((N_GROUPS, GROUP_WIDTH), lambda i: (0, 0)),
                  pl.BlockSpec((MIX_WIDTH, D), lambda i: (0, 0))],
        out_specs=pl.BlockSpec((tm, D), lambda i: (i, 0)),
        out_shape=jax.ShapeDtypeStruct((N, D), jnp.float32),
        compiler_params=_params("arbitrary"),
        name="mixer_out_proj",
    )(x2d, ada_l, *groups, group_norm, w_out.astype(MXU_DTYPE))


def _mixer_groups(x2d, ada_l, tables, mix_norm, w_in, mla_q_norm, mla_w_uq, mla_kv_norm, mla_w_uk, mla_w_uv,
                  nsa_pe_k, nsa_pe_v, nsa_cmp_k_w1, nsa_cmp_k_w2, nsa_cmp_v_w1, nsa_cmp_v_w2, B, T):
    (mq, mkt, mv, lq, lkt, lv, nq, nkc, nvc, nkst, nvs, nkwt, nvw, ngate,
     dq, dkt, dv, diq, dikt, diw) = _proj_call(
        x2d, ada_l, mix_norm, w_in, tables, mla_q_norm, mla_w_uq, mla_kv_norm, mla_w_uk, mla_w_uv, B, T)
    o_moba = _moba_call(mq, mkt, mv)
    o_mla = _mla_call(lq, lkt, lv)
    kcmp, vcmp = _cmp_call(nkc, nvc, nsa_pe_k, nsa_pe_v, nsa_cmp_k_w1, nsa_cmp_k_w2, nsa_cmp_v_w1, nsa_cmp_v_w2)
    o_nsa = _nsa_call(nq, kcmp, vcmp, nkst, nvs, nkwt, nvw, ngate)
    o_dsa = _dsa_call(dq, dkt, dv, diq, dikt, diw)
    return o_moba, o_mla, o_nsa, o_dsa


def kernel(x, c, ada_w, ada_b, ffn1_norm, ffn1_w_gate, ffn1_w_up, ffn1_w_down, mix_norm, w_in, mla_q_norm, mla_w_uq, mla_kv_norm, mla_w_uk, mla_w_uv, nsa_pe_k, nsa_pe_v, nsa_cmp_k_w1, nsa_cmp_k_w2, nsa_cmp_v_w1, nsa_cmp_v_w2, group_norm, w_out, ffn2_norm, ffn2_w_gate, ffn2_w_up, ffn2_w_down, final_norm):
    B, T, D = x.shape
    L = ada_w.shape[0]
    assert D == D_MODEL and T % ROW_TILE == 0 and T % ATT_TILE == 0
    assert ATT_TILE % MOBA_BLOCK == 0 and ATT_TILE % NSA_SEL_BLOCK == 0 and ATT_TILE >= NSA_WINDOW
    tpb = T // ROW_TILE
    ada = _ada_call(c, ada_w, ada_b)
    tables = _rope_tables(T)
    x2d = x.reshape(B * T, D)
    for l in range(L):
        x2d = _ffn_call(x2d, ada[l], ffn1_norm[l], ffn1_w_gate[l], ffn1_w_up[l], ffn1_w_down[l], 0, tpb)
        groups = _mixer_groups(x2d, ada[l], tables, mix_norm[l], w_in[l], mla_q_norm[l], mla_w_uq[l],
                               mla_kv_norm[l], mla_w_uk[l], mla_w_uv[l], nsa_pe_k[l], nsa_pe_v[l],
                               nsa_cmp_k_w1[l], nsa_cmp_k_w2[l], nsa_cmp_v_w1[l], nsa_cmp_v_w2[l], B, T)
        x2d = _out_call(x2d, ada[l], groups, group_norm[l], w_out[l], B, T)
        x2d = _ffn_call(x2d, ada[l], ffn2_norm[l], ffn2_w_gate[l], ffn2_w_up[l], ffn2_w_down[l], 6, tpb,
                        final_gain=final_norm if l == L - 1 else None)
    return x2d.reshape(B, T, D)
```

```python
import functools
import math

import numpy as np
import jax
import jax.numpy as jnp
from jax import lax
from jax.experimental import pallas as pl
from jax.experimental.pallas import tpu as pltpu

D_MODEL = 1024
N_GROUPS = 4
HEAD_DIM = 64
GROUP_HEADS = D_MODEL // (N_GROUPS * HEAD_DIM)
GROUP_WIDTH = GROUP_HEADS * HEAD_DIM
MIX_WIDTH = N_GROUPS * GROUP_WIDTH
D_FF = 256 * ((8 * D_MODEL + 3 * 256 - 1) // (3 * 256))
N_ADA = 9
FFN_RESIDUAL_WEIGHT = 0.5
ROPE_THETA = 10000.0
RMS_EPS = 1e-6
NEG_INF = -1e30

MOBA_BLOCK = 256
MOBA_TOPK = 3

MLA_Q_LORA = D_MODEL // 4
MLA_KV_LORA = D_MODEL // 8
MLA_NOPE = HEAD_DIM
MLA_ROPE = HEAD_DIM // 2
MLA_V = HEAD_DIM
MLA_QK = MLA_NOPE + MLA_ROPE

NSA_CMP_LEN = 32
NSA_CMP_STRIDE = 16
NSA_CMP_HIDDEN = 4 * HEAD_DIM
NSA_SEL_BLOCK = 64
NSA_SEL_TOPN = 16
NSA_WINDOW = 512
NSA_FORCE_SCORE = 1e4

DSA_TOPK = 256
DSA_IDX_HEADS = 8
DSA_IDX_DIM = 32

IN_NAMES = ("mq", "mk", "mv", "cq", "ckv", "kr", "nq", "nkc", "nvc", "nks", "nvs", "nkw", "nvw",
            "ngate", "dq", "dk", "dv", "diq", "dik", "diw")
IN_SIZES = (
    GROUP_WIDTH, GROUP_WIDTH, GROUP_WIDTH,
    MLA_Q_LORA, MLA_KV_LORA, MLA_ROPE,
    GROUP_WIDTH, HEAD_DIM, HEAD_DIM, HEAD_DIM, HEAD_DIM,
    HEAD_DIM, HEAD_DIM, 3 * GROUP_HEADS,
    GROUP_WIDTH, HEAD_DIM, HEAD_DIM,
    DSA_IDX_HEADS * DSA_IDX_DIM, DSA_IDX_DIM, DSA_IDX_HEADS,
)
N_IN = sum(IN_SIZES)

LANES = 128
MXU_DTYPE = jnp.bfloat16
VMEM_LIMIT = 56 * 1024 * 1024

ATT_TILE = 512
ROW_TILE = 512
MXU_TILE = 256
FF_CHUNK = 6 * MXU_TILE

LOG2E = math.log2(math.e)
M_INIT = -1e29
DEN_LANE = HEAD_DIM


def _params(*semantics):
    return pltpu.CompilerParams(dimension_semantics=semantics, vmem_limit_bytes=VMEM_LIMIT)


def _dot(a, b):
    return jnp.dot(a.astype(MXU_DTYPE), b.astype(MXU_DTYPE), preferred_element_type=jnp.float32)


def _dot_nt(a, b):
    return lax.dot_general(a.astype(MXU_DTYPE), b.astype(MXU_DTYPE), (((1,), (1,)), ((), ())),
                           preferred_element_type=jnp.float32)


def _rms(x, g):
    return x * lax.rsqrt(jnp.mean(x * x, axis=-1, keepdims=True) + RMS_EPS) * g


def _silu(x):
    return x * (1.0 / (1.0 + jnp.exp(-x)))


def _iota(shape, dim):
    return lax.broadcasted_iota(jnp.int32, shape, dim)


def _log2(n):
    assert n & (n - 1) == 0
    return n.bit_length() - 1


def _any(pred):
    return jnp.max(jnp.where(pred, 1.0, 0.0)) > 0.5


def _lanes(x, width):
    return x if width == LANES else jnp.tile(x, (1, width // LANES))


def _ada_kernel(c_ref, w_ref, b_ref, o_ref):
    o_ref[0] = _dot(_silu(c_ref[...]), w_ref[0]) + b_ref[0]


def _ada_call(c, ada_w, ada_b):
    L, D, _ = ada_w.shape
    B = c.shape[0]
    out = pl.pallas_call(
        _ada_kernel,
        grid=(L, N_ADA),
        in_specs=[
            pl.BlockSpec((B, D), lambda l, k: (0, 0)),
            pl.BlockSpec((1, D, D), lambda l, k: (l, 0, k)),
            pl.BlockSpec((1, 1, D), lambda l, k: (l, 0, k)),
        ],
        out_specs=pl.BlockSpec((1, B, D), lambda l, k: (l, 0, k)),
        out_shape=jax.ShapeDtypeStruct((L, B, N_ADA * D), jnp.float32),
        compiler_params=_params("arbitrary", "arbitrary"),
        name="ada_proj",
    )(c, ada_w, ada_b.reshape(L, 1, N_ADA * D))
    return out.reshape(L, B, N_ADA, D)


def _ffn_chunks(F):
    bounds = list(range(0, F, FF_CHUNK)) + [F]
    return list(zip(bounds[:-1], bounds[1:]))


def _ffn_block(x, a, gn, wg_ref, wu_ref, wd_ref, k0):
    h = (_rms(x, gn) * (1.0 + a[k0 + 1:k0 + 2]) + a[k0:k0 + 1]).astype(MXU_DTYPE)
    acc = None
    for lo, hi in _ffn_chunks(wg_ref.shape[1]):
        g = jnp.dot(h, wg_ref[:, lo:hi], preferred_element_type=jnp.float32)
        u = jnp.dot(h, wu_ref[:, lo:hi], preferred_element_type=jnp.float32)
        part = jnp.dot((_silu(g) * u).astype(MXU_DTYPE), wd_ref[lo:hi, :], preferred_element_type=jnp.float32)
        acc = part if acc is None else acc + part
    return x + (FFN_RESIDUAL_WEIGHT * a[k0 + 2:k0 + 3]) * acc


def _ffn_kernel(x_ref, ada_ref, gn_ref, wg_ref, wu_ref, wd_ref, o_ref, *, k0):
    o_ref[...] = _ffn_block(x_ref[...], ada_ref[0], gn_ref[...], wg_ref, wu_ref, wd_ref, k0)


def _resident(shape):
    return pl.BlockSpec(shape, lambda i: (0,) * len(shape), pipeline_mode=pl.Buffered(1))


def _ffn_call(x2d, ada_l, norm_g, w_gate, w_up, w_down, k0, tiles_per_batch):
    N, D = x2d.shape
    F = w_gate.shape[1]
    tm = ROW_TILE
    return pl.pallas_call(
        functools.partial(_ffn_kernel, k0=k0),
        grid=(N // tm,),
        in_specs=[pl.BlockSpec((tm, D), lambda i: (i, 0)),
                  pl.BlockSpec((1, N_ADA, D), lambda i: (i // tiles_per_batch, 0, 0)),
                  _resident((1, D)), _resident((D, F)), _resident((D, F)), _resident((F, D))],
        out_specs=pl.BlockSpec((tm, D), lambda i: (i, 0)),
        out_shape=jax.ShapeDtypeStruct((N, D), jnp.float32),
        compiler_params=_params("arbitrary"),
        name="ffn",
    )(x2d, ada_l, norm_g.reshape(1, D), w_gate.astype(MXU_DTYPE), w_up.astype(MXU_DTYPE),
      w_down.astype(MXU_DTYPE))


G_QMAIN, G_QSWAP, G_KC, G_V, G_MISC, ROW_GROUPS = 0, 6, 12, 14, 18, 25
T_MK, T_MKS, T_SW, T_SWS, T_DK, T_DKS, T_IK, T_ROWS = 0, 256, 512, 640, 768, 896, 1024, 1152


def _swap_halves(c, width):
    return c.reshape(-1, 2, width // 2)[:, ::-1, :].reshape(-1)


def _proj_indices():
    off = dict(zip(IN_NAMES, np.cumsum((0,) + IN_SIZES[:-1]).tolist()))
    size = dict(zip(IN_NAMES, IN_SIZES))
    cols = lambda name: np.arange(off[name], off[name] + size[name])
    zero = lambda n: np.full((n,), N_IN)
    hd = HEAD_DIM

    def head_groups(c):
        return np.concatenate([np.concatenate([c[i:i + hd], zero(LANES - hd)]) for i in range(0, c.size, hd)])

    q = np.concatenate([cols("mq"), cols("nq"), cols("dq")])
    row = np.concatenate([
        q, _swap_halves(q, hd),
        head_groups(cols("nkc")), head_groups(_swap_halves(cols("nkc"), hd)),
        cols("mv"), cols("nvs"), cols("nvw"), cols("dv"), cols("nvc"),
        cols("cq"), cols("ckv"), cols("diq"),
        cols("ngate"), zero(LANES - size["ngate"]), cols("diw"), zero(LANES - size["diw"])])
    assert row.size == ROW_GROUPS * LANES
    sw = np.concatenate([cols("nks"), cols("nkw")])
    dk_main = np.concatenate([cols("dk"), cols("kr"), zero(LANES - hd - MLA_ROPE)])
    dk_swap = np.concatenate([_swap_halves(cols("dk"), hd), _swap_halves(cols("kr"), MLA_ROPE),
                              zero(LANES - hd - MLA_ROPE)])
    tr = np.concatenate([cols("mk"), _swap_halves(cols("mk"), hd), sw, _swap_halves(sw, hd),
                         dk_main, dk_swap, np.tile(cols("dik"), LANES // DSA_IDX_DIM)])
    assert tr.size == T_ROWS
    return row, tr


def _proj_kernel(x_ref, ada_ref, gn_ref, wr_ref, wt_ref, rtab_ref, ttab_ref,
                 qn_ref, wuq_ref, kvn_ref, wukt_ref, wuv_ref,
                 mq_ref, mkt_ref, mv_ref, lq_ref, lkt_ref, lv_ref,
                 nq_ref, nkc_ref, nvc_ref, nkst_ref, nvs_ref, nkwt_ref, nvw_ref, ng_ref,
                 dq_ref, dkt_ref, dv_ref, diq_ref, dikt_ref, diw_ref, *, tm, tpb, n_moba, n_sel):
    H, hd, G = GROUP_HEADS, HEAD_DIM, LANES
    t0 = (pl.program_id(0) % tpb) * tm
    a = ada_ref[0]
    h = (_rms(x_ref[...], gn_ref[...]) * (1.0 + a[4:5]) + a[3:4]).astype(MXU_DTYPE)

    def rows(g0, n):
        return jnp.dot(h, wr_ref[:, g0 * G:(g0 + n) * G], preferred_element_type=jnp.float32)

    def cols(r0, n):
        return _dot_nt(wt_ref[r0:r0 + n, :], h)

    low_half = _iota((tm, G), 1) < hd

    def split(pair, fill):
        return (jnp.where(low_half, pair, fill), jnp.where(low_half, pltpu.roll(pair, hd, 1), fill))

    n_pairs = 3 * H // 2
    roped_q = (rows(G_QMAIN, n_pairs) * _lanes(rtab_ref[0], n_pairs * G)
               + rows(G_QSWAP, n_pairs) * _lanes(rtab_ref[1], n_pairs * G))
    for i, ref in enumerate((mq_ref, nq_ref, dq_ref)):
        for p in range(H // 2):
            g = i * (H // 2) + p
            even, odd = split(roped_q[:, g * G:(g + 1) * G], 0.0)
            ref[0, 2 * p] = even.astype(ref.dtype)
            ref[0, 2 * p + 1] = odd.astype(ref.dtype)
    kc = rows(G_KC, 2)
    nkc_ref[0] = (kc[:, :G] * rtab_ref[2] + kc[:, G:] * rtab_ref[3])[:, :hd].astype(nkc_ref.dtype)

    v = rows(G_V, H // 2 + 2)
    for p in range(H // 2):
        even, odd = split(v[:, p * G:(p + 1) * G], 1.0)
        mv_ref[0, 2 * p] = even.astype(mv_ref.dtype)
        mv_ref[0, 2 * p + 1] = odd.astype(mv_ref.dtype)
    nvs, nvw = split(v[:, (H // 2) * G:(H // 2 + 1) * G], 1.0)
    nvs_ref[0] = nvs.astype(nvs_ref.dtype)
    nvw_ref[0] = nvw.astype(nvw_ref.dtype)
    dv, nvc = split(v[:, (H // 2 + 1) * G:(H // 2 + 2) * G], 1.0)
    dv_ref[0] = dv.astype(dv_ref.dtype)
    nvc_ref[0] = nvc[:, :hd].astype(nvc_ref.dtype)
    ones_hi = jnp.where(_iota((1, G), 1) >= DEN_LANE, 1.0, 0.0)

    misc = rows(G_MISC, 7)
    cq = misc[:, :MLA_Q_LORA]
    ckv = misc[:, MLA_Q_LORA:MLA_Q_LORA + MLA_KV_LORA]
    diq_ref[0] = misc[:, 3 * G:5 * G].astype(diq_ref.dtype)
    ng_ref[0] = misc[:, 5 * G:6 * G]
    diw_ref[0] = misc[:, 6 * G:7 * G]

    tok = t0 + _iota((hd, tm), 1)
    rid = _iota((hd, tm), 0)
    oh_moba = jnp.where((rid & (n_moba - 1)) == (tok >> _log2(MOBA_BLOCK)), 1.0, 0.0)
    oh_sel = jnp.where((rid == (tok >> _log2(NSA_SEL_BLOCK))) & (rid < n_sel), 1.0, 0.0)
    zeros_lo = jnp.zeros((hd, tm), jnp.float32)
    ta_c, ta_s, tb_c, tb_s = ttab_ref[0], ttab_ref[1], ttab_ref[2], ttab_ref[3]
    mkt = (cols(T_MK, H * hd) * jnp.tile(ta_c, (H * hd // G, 1))
           + cols(T_MKS, H * hd) * jnp.tile(ta_s, (H * hd // G, 1)))
    for hh in range(H):
        mine = (rid >> _log2(n_moba)) == hh
        mkt_ref[0, hh] = jnp.concatenate([mkt[hh * hd:(hh + 1) * hd], jnp.where(mine, oh_moba, 0.0)],
                                         axis=0).astype(mkt_ref.dtype)
    sw = cols(T_SW, G) * ta_c + cols(T_SWS, G) * ta_s
    nkst_ref[0] = jnp.concatenate([sw[:hd], oh_sel], axis=0).astype(nkst_ref.dtype)
    nkwt_ref[0] = jnp.concatenate([sw[hd:], zeros_lo], axis=0).astype(nkwt_ref.dtype)
    dkr = cols(T_DK, G) * tb_c + cols(T_DKS, G) * tb_s
    rid2 = _iota((G, tm), 0)
    dkt_ref[0] = jnp.where(rid2 < hd, dkr, 0.0).astype(dkt_ref.dtype)
    kpe_rows = jnp.where((rid2 >= MLA_NOPE) & (rid2 < MLA_QK), dkr, 0.0)
    dikt_ref[0] = cols(T_IK, G).astype(dikt_ref.dtype)

    cqn = _rms(cq, qn_ref[...]).astype(MXU_DTYPE)
    lq = (jnp.dot(cqn, wuq_ref[:, :H * G], preferred_element_type=jnp.float32) * _lanes(rtab_ref[4], H * G)
          + jnp.dot(cqn, wuq_ref[:, H * G:], preferred_element_type=jnp.float32) * _lanes(rtab_ref[5], H * G))
    ckvn = _rms(ckv, kvn_ref[...]).astype(MXU_DTYPE)
    knt = _dot_nt(wukt_ref[...], ckvn)
    lv = jnp.dot(ckvn, wuv_ref[...], preferred_element_type=jnp.float32) + _lanes(ones_hi, H * G)
    for hh in range(H):
        lq_ref[0, hh] = lq[:, hh * G:(hh + 1) * G].astype(lq_ref.dtype)
        lkt_ref[0, hh] = (knt[hh * G:(hh + 1) * G] + kpe_rows).astype(lkt_ref.dtype)
        lv_ref[0, hh] = lv[:, hh * G:(hh + 1) * G].astype(lv_ref.dtype)


def _rope_tables(T):
    def cs(dim):
        inv_freq = 1.0 / (ROPE_THETA ** (np.arange(0, dim, 2, dtype=np.float32) / dim))
        ang = jnp.arange(T, dtype=jnp.float32)[:, None] * jnp.asarray(inv_freq, jnp.float32)[None, :]
        cos, sin = jnp.cos(ang), jnp.sin(ang)
        return jnp.concatenate([cos, cos], axis=-1), jnp.concatenate([-sin, sin], axis=-1)

    c64, s64 = cs(HEAD_DIM)
    c32, s32 = cs(MLA_ROPE)
    pad = lambda t, n: jnp.concatenate([t, jnp.zeros((T, n), jnp.float32)], axis=-1)
    sc = HEAD_DIM ** -0.5 * LOG2E
    sl = MLA_QK ** -0.5 * LOG2E
    ones = jnp.ones((T, MLA_NOPE), jnp.float32)
    rest = LANES - MLA_QK
    rtab = jnp.stack([
        jnp.tile(c64 * sc, (1, LANES // HEAD_DIM)), jnp.tile(s64 * sc, (1, LANES // HEAD_DIM)),
        pad(c64, LANES - HEAD_DIM), pad(s64, LANES - HEAD_DIM),
        pad(jnp.concatenate([ones, c32], axis=-1) * sl, rest),
        pad(jnp.concatenate([0.0 * ones, s32], axis=-1) * sl, rest)])
    ttab = jnp.stack([
        jnp.concatenate([c64, c64], axis=-1).T, jnp.concatenate([s64, s64], axis=-1).T,
        pad(jnp.concatenate([c64, c32], axis=-1), rest).T, pad(jnp.concatenate([s64, s32], axis=-1), rest).T])
    return rtab, ttab


def _proj_call(x2d, ada_l, norm_g, w_in, tables, mla_q_norm, mla_w_uq, mla_kv_norm, mla_w_uk, mla_w_uv, B, T):
    N, D = x2d.shape
    H, G = GROUP_HEADS, LANES
    tm = ROW_TILE
    tpb = T // tm
    n_moba, n_sel = T // MOBA_BLOCK, T // NSA_SEL_BLOCK
    assert H * n_moba <= LANES - HEAD_DIM and n_sel <= LANES - HEAD_DIM and n_moba & (n_moba - 1) == 0
    zcol = lambda w: jnp.concatenate([w, jnp.zeros((w.shape[0], 1), w.dtype)], axis=1)
    row_idx, tr_idx = _proj_indices()
    w_ext = zcol(w_in)
    w_row = w_ext[:, row_idx].astype(MXU_DTYPE)
    w_tr = w_ext[:, tr_idx].T.astype(MXU_DTYPE)
    zq = mla_w_uq.shape[1]
    per_head = np.arange(H * MLA_QK).reshape(H, MLA_QK)
    main = np.concatenate([np.concatenate([per_head[i], np.full((G - MLA_QK,), zq)]) for i in range(H)])
    part = np.concatenate([np.concatenate([np.full((MLA_NOPE,), zq), _swap_halves(per_head[i, MLA_NOPE:], MLA_ROPE),
                                           np.full((G - MLA_QK,), zq)]) for i in range(H)])
    wuq = zcol(mla_w_uq)[:, np.concatenate([main, part])].astype(MXU_DTYPE)
    zv = mla_w_uk.shape[1]
    grp = np.concatenate([np.concatenate([np.arange(i * HEAD_DIM, (i + 1) * HEAD_DIM), np.full((G - HEAD_DIM,), zv)])
                          for i in range(H)])
    wukt = zcol(mla_w_uk)[:, grp].T.astype(MXU_DTYPE)
    wuv = zcol(mla_w_uv)[:, grp].astype(MXU_DTYPE)
    rtab, ttab = tables

    row = lambda i: (i, 0)
    const2 = lambda i: (0, 0)
    in_specs = [
        pl.BlockSpec((tm, D), row),
        pl.BlockSpec((1, N_ADA, D), lambda i: (i // tpb, 0, 0)),
        pl.BlockSpec((1, D), const2),
        pl.BlockSpec(w_row.shape, const2),
        pl.BlockSpec(w_tr.shape, const2),
        pl.BlockSpec((6, tm, G), lambda i: (0, i % tpb, 0)),
        pl.BlockSpec((4, G, tm), lambda i: (0, 0, i % tpb)),
        pl.BlockSpec((1, MLA_Q_LORA), const2),
        pl.BlockSpec(wuq.shape, const2),
        pl.BlockSpec((1, MLA_KV_LORA), const2),
        pl.BlockSpec(wukt.shape, const2),
        pl.BlockSpec(wuv.shape, const2),
    ]
    dt = MXU_DTYPE
    hq = (jax.ShapeDtypeStruct((B, H, T, G), dt), pl.BlockSpec((1, H, tm, G), lambda i: (i // tpb, 0, i % tpb, 0)))
    hkt = (jax.ShapeDtypeStruct((B, H, G, T), dt), pl.BlockSpec((1, H, G, tm), lambda i: (i // tpb, 0, 0, i % tpb)))
    srow = lambda d, t=dt: (jax.ShapeDtypeStruct((B, T, d), t), pl.BlockSpec((1, tm, d), lambda i: (i // tpb, i % tpb, 0)))
    skt = (jax.ShapeDtypeStruct((B, G, T), dt), pl.BlockSpec((1, G, tm), lambda i: (i // tpb, 0, i % tpb)))
    outs = [hq, hkt, hq,
            hq, hkt, hq,
            hq, srow(HEAD_DIM), srow(HEAD_DIM), skt, srow(G), skt, srow(G), srow(G, jnp.float32),
            hq, skt, srow(G), srow(2 * G), skt, srow(G, jnp.float32)]
    return pl.pallas_call(
        functools.partial(_proj_kernel, tm=tm, tpb=tpb, n_moba=n_moba, n_sel=n_sel),
        grid=(N // tm,),
        in_specs=in_specs,
        out_specs=[o[1] for o in outs],
        out_shape=[o[0] for o in outs],
        compiler_params=_params("arbitrary"),
        name="mixer_in_proj",
    )(x2d, ada_l, norm_g.reshape(1, D), w_row, w_tr, rtab, ttab,
      mla_q_norm.reshape(1, -1), wuq, mla_kv_norm.reshape(1, -1), wukt, wuv)


def _flash_init(m_scr, acc_scr):
    m_scr[...] = jnp.full(m_scr.shape, M_INIT, jnp.float32)
    acc_scr[...] = jnp.zeros_like(acc_scr)


def _flash_update(h, s, v, m_scr, acc_scr, rows=slice(None)):
    m_prev = m_scr[h, rows]
    m_new = jnp.maximum(m_prev, jnp.max(s, axis=-1, keepdims=True))
    p = jnp.exp2(s - _lanes(m_new, s.shape[1]))
    acc_scr[h, rows] = jnp.exp2(m_prev - m_new) * acc_scr[h, rows] + _dot(p, v)
    m_scr[h, rows] = m_new


def _flash_out(h, acc_scr):
    acc = acc_scr[h]
    den = acc[:, DEN_LANE:DEN_LANE + 1]
    return acc[:, :HEAD_DIM] / jnp.where(den > 0.0, den, 1.0)


def _causal_bias(t):
    return jnp.where(_iota((t, t), 1) <= _iota((t, t), 0), 0.0, NEG_INF)


def _rank_desc(x):
    n = x.shape[0]
    row = _iota(x.shape, 0)
    rank = jnp.zeros(x.shape, jnp.float32)
    for j in range(n):
        cand = x[j:j + 1, :]
        rank = rank + jnp.where(cand > x, 1.0, jnp.where((cand == x) & (row > j), 1.0, 0.0))
    return rank


def _att_scratch(tq, stacked=False):
    shape = (1, GROUP_HEADS * tq, LANES) if stacked else (GROUP_HEADS, tq, LANES)
    return [pltpu.VMEM(shape, jnp.float32), pltpu.VMEM(shape, jnp.float32)]


def _att_specs(B, H, T, tq, shared_kv):
    q = pl.BlockSpec((1, H, tq, LANES), lambda b, i: (b, 0, i, 0))
    if shared_kv:
        kt = pl.BlockSpec((1, LANES, T), lambda b, i: (b, 0, 0))
        v = pl.BlockSpec((1, T, LANES), lambda b, i: (b, 0, 0))
    else:
        kt = pl.BlockSpec((1, H, LANES, T), lambda b, i: (b, 0, 0, 0))
        v = pl.BlockSpec((1, H, T, LANES), lambda b, i: (b, 0, 0, 0))
    out = pl.BlockSpec((1, tq, H * HEAD_DIM), lambda b, i: (b, i, 0))
    return q, kt, v, out


def _causal_sweep(qs, kt_ref, v_ref, qi, tq, tk, m_scr, acc_scr):
    H = len(qs)
    _flash_init(m_scr, acc_scr)

    def body(c, carry):
        k0 = pl.multiple_of(c * tk, tk)
        for h in range(H):
            s = jnp.dot(qs[h], kt_ref[0, h, :, pl.ds(k0, tk)], preferred_element_type=jnp.float32)
            _flash_update(h, s, v_ref[0, h, pl.ds(k0, tk), :], m_scr, acc_scr)
        return carry

    lax.fori_loop(0, 2 * qi, body, 0)
    k0 = pl.multiple_of(qi * tq, tk)
    k1 = pl.multiple_of(qi * tq + tk, tk)
    left = jnp.where(_iota((tq, tk), 1) <= _iota((tq, tk), 0), 0.0, NEG_INF)
    right = _causal_bias(tk)
    for h in range(H):
        s = jnp.dot(qs[h], kt_ref[0, h, :, pl.ds(k0, tk)], preferred_element_type=jnp.float32) + left
        _flash_update(h, s, v_ref[0, h, pl.ds(k0, tk), :], m_scr, acc_scr)
        s = jnp.dot(qs[h][tk:], kt_ref[0, h, :, pl.ds(k1, tk)], preferred_element_type=jnp.float32) + right
        _flash_update(h, s, v_ref[0, h, pl.ds(k1, tk), :], m_scr, acc_scr, slice(tk, tq))


def _mla_kernel(q_ref, kt_ref, v_ref, o_ref, m_scr, acc_scr, *, tq, tk):
    H = GROUP_HEADS
    _causal_sweep([q_ref[0, h] for h in range(H)], kt_ref, v_ref, pl.program_id(1), tq, tk, m_scr, acc_scr)
    for h in range(H):
        o_ref[0, :, h * HEAD_DIM:(h + 1) * HEAD_DIM] = _flash_out(h, acc_scr)


def _mla_call(q, kt, v):
    B, H, T, _ = q.shape
    tk = ATT_TILE
    tq = 2 * tk
    assert T % tq == 0
    qs, ks, vs, out = _att_specs(B, H, T, tq, False)
    return pl.pallas_call(
        functools.partial(_mla_kernel, tq=tq, tk=tk),
        grid=(B, T // tq),
        in_specs=[qs, ks, vs],
        out_specs=out,
        out_shape=jax.ShapeDtypeStruct((B, T, H * HEAD_DIM), jnp.float32),
        scratch_shapes=_att_scratch(tq),
        compiler_params=_params("arbitrary", "arbitrary"),
        name="mla_attention",
    )(q, kt, v)


def _moba_kernel(q_ref, kt_ref, v_ref, o_ref, kmean_scr, m_scr, acc_scr, *, tq, tk, nb, topk):
    H, hd = GROUP_HEADS, HEAD_DIM
    qi = pl.program_id(1)
    q0 = qi * tq
    T = nb * MOBA_BLOCK

    @pl.when(qi == 0)
    def _():
        avg = jnp.where((_iota((nb, T), 1) >> _log2(MOBA_BLOCK)) == _iota((nb, T), 0), 1.0 / MOBA_BLOCK, 0.0)
        for h in range(H):
            kmean_scr[h] = _dot_nt(avg, kt_ref[0, h])

    blk = _iota((nb, tq), 0)
    own = (q0 + _iota((nb, tq), 1)) >> _log2(MOBA_BLOCK)
    past = blk < own
    bias_rows = [jnp.zeros((hd, tq), jnp.float32)]
    for h in range(H):
        gate = jnp.where(past, _dot_nt(kmean_scr[h], q_ref[0, h]), NEG_INF)
        allowed = ((_rank_desc(gate) < topk) & past) | (blk == own)
        bias_rows.append(jnp.where(allowed, 0.0, NEG_INF))
    if H * nb < LANES - hd:
        bias_rows.append(jnp.zeros((LANES - hd - H * nb, tq), jnp.float32))
    bias = jnp.concatenate(bias_rows, axis=0).T
    lane = _iota((tq, LANES), 1)
    qs = [jnp.where(lane < hd, q_ref[0, h], bias.astype(q_ref.dtype)) for h in range(H)]
    _causal_sweep(qs, kt_ref, v_ref, qi, tq, tk, m_scr, acc_scr)
    for h in range(H):
        o_ref[0, :, h * hd:(h + 1) * hd] = _flash_out(h, acc_scr)


def _moba_call(q, kt, v):
    B, H, T, _ = q.shape
    tk = ATT_TILE
    tq = 2 * tk
    assert T % tq == 0
    nb = T // MOBA_BLOCK
    qs, ks, vs, out = _att_specs(B, H, T, tq, False)
    return pl.pallas_call(
        functools.partial(_moba_kernel, tq=tq, tk=tk, nb=nb, topk=min(MOBA_TOPK, nb - 1)),
        grid=(B, T // tq),
        in_specs=[qs, ks, vs],
        out_specs=out,
        out_shape=jax.ShapeDtypeStruct((B, T, H * HEAD_DIM), jnp.float32),
        scratch_shapes=[pltpu.VMEM((H, nb, LANES), jnp.float32)] + _att_scratch(tq),
        compiler_params=_params("arbitrary", "arbitrary"),
        name="moba_attention",
    )(q, kt, v)


def _cmp_kernel(k_ref, v_ref, pek_ref, pev_ref, kw1_ref, kw2_ref, vw1_ref, vw2_ref, ko_ref, vo_ref, *, rows):
    half = NSA_CMP_STRIDE * HEAD_DIM

    def compress(t_ref, pe_ref, w1_ref, w2_ref):
        t = t_ref[0].astype(jnp.float32)
        first = _dot(t + pe_ref[0:1, :], w1_ref[0:half, :])
        second = _dot(t + pe_ref[1:2, :], w1_ref[half:, :])
        hid = first + pltpu.roll(second, rows - 1, 0)
        return _dot(_silu(hid), w2_ref[...])

    ko_ref[0] = compress(k_ref, pek_ref, kw1_ref, kw2_ref)
    vo_ref[0] = compress(v_ref, pev_ref, vw1_ref, vw2_ref)


def _cmp_call(kc, vc, pe_k, pe_v, k_w1, k_w2, v_w1, v_w2):
    B, T, d = kc.shape
    rows = T // NSA_CMP_STRIDE
    wide = NSA_CMP_STRIDE * d
    assert NSA_CMP_LEN == 2 * NSA_CMP_STRIDE
    const2 = lambda b: (0, 0)
    blk = pl.BlockSpec((1, rows, wide), lambda b: (b, 0, 0))
    out = pl.BlockSpec((1, rows, d), lambda b: (b, 0, 0))
    return pl.pallas_call(
        functools.partial(_cmp_kernel, rows=rows),
        grid=(B,),
        in_specs=[blk, blk, pl.BlockSpec((2, wide), const2), pl.BlockSpec((2, wide), const2),
                  pl.BlockSpec(k_w1.shape, const2), pl.BlockSpec(k_w2.shape, const2),
                  pl.BlockSpec(v_w1.shape, const2), pl.BlockSpec(v_w2.shape, const2)],
        out_specs=[out, out],
        out_shape=[jax.ShapeDtypeStruct((B, rows, d), jnp.float32)] * 2,
        compiler_params=_params("arbitrary"),
        name="nsa_compress",
    )(kc.reshape(B, rows, wide), vc.reshape(B, rows, wide), pe_k.reshape(2, wide), pe_v.reshape(2, wide),
      k_w1.astype(MXU_DTYPE), k_w2.astype(MXU_DTYPE), v_w1.astype(MXU_DTYPE), v_w2.astype(MXU_DTYPE))


def _nsa_kernel(q_ref, kcmp_ref, vcmp_ref, kst_ref, vs_ref, kwt_ref, vw_ref, g_ref, o_ref,
                oc_scr, os_scr, m_scr, acc_scr, *, tq, ncp, n_sel, topn):
    H, hd = GROUP_HEADS, HEAD_DIM
    qi = pl.program_id(1)
    q0 = qi * tq
    tq_col = q0 + _iota((tq, 1), 0)
    qraw = [q_ref[0, h] for h in range(H)]

    cmp_end = _iota((tq, ncp), 1) * NSA_CMP_STRIDE + (NSA_CMP_LEN - 1)
    m_c = cmp_end <= tq_col
    p_sum = jnp.zeros((tq, ncp), jnp.float32)
    for h in range(H):
        s = jnp.where(m_c, _dot_nt(qraw[h][:, :hd], kcmp_ref[0]), NEG_INF)
        e = jnp.where(m_c, jnp.exp2(s - jnp.max(s, axis=-1, keepdims=True)), 0.0)
        l = jnp.sum(e, axis=-1, keepdims=True)
        p = e / jnp.where(l > 0.0, l, 1.0)
        p_sum = p_sum + p
        oc_scr[h] = _dot(p, vcmp_ref[0])

    cmp_start = _iota((n_sel, ncp), 1) * NSA_CMP_STRIDE
    sel_start = _iota((n_sel, ncp), 0) * NSA_SEL_BLOCK
    overlap = (cmp_start < sel_start + NSA_SEL_BLOCK) & (cmp_start + NSA_CMP_LEN > sel_start)
    imp = _dot_nt(jnp.where(overlap, 1.0, 0.0), p_sum)
    sel_id = _iota((n_sel, tq), 0)
    own = (q0 + _iota((n_sel, tq), 1)) >> _log2(NSA_SEL_BLOCK)
    causal = sel_id <= own
    forced = causal & ((sel_id == 0) | (sel_id >= own - 1))
    imp = jnp.where(forced, NSA_FORCE_SCORE, jnp.where(causal, imp, -NSA_FORCE_SCORE))
    bias_rows = [jnp.zeros((hd, tq), jnp.float32), jnp.where(_rank_desc(imp) < topn, 0.0, NEG_INF)]
    if n_sel < LANES - hd:
        bias_rows.append(jnp.zeros((LANES - hd - n_sel, tq), jnp.float32))
    bias = jnp.concatenate(bias_rows, axis=0).T.astype(qraw[0].dtype)
    lane = _iota((tq, LANES), 1)
    qsel = [jnp.where(lane < hd, qraw[h], bias) for h in range(H)]

    def stack(per_head):
        return jnp.concatenate(per_head, axis=0)

    def tall(bias):
        return jnp.tile(bias, (H, 1))

    q_sel, q_raw = stack(qsel), stack(qraw)
    _flash_init(m_scr, acc_scr)

    def sel_chunk(k0, extra):
        s = jnp.dot(q_sel, kst_ref[0, :, pl.ds(k0, tq)], preferred_element_type=jnp.float32)
        if extra is not None:
            s = s + tall(extra)
        _flash_update(0, s, vs_ref[0, pl.ds(k0, tq), :], m_scr, acc_scr)

    def sel_body(c, carry):
        sel_chunk(pl.multiple_of(c * tq, tq), None)
        return carry

    lax.fori_loop(0, qi, sel_body, 0)
    sel_chunk(pl.multiple_of(q0, tq), _causal_bias(tq))
    o_sel = _flash_out(0, acc_scr)
    for h in range(H):
        os_scr[h] = o_sel[h * tq:(h + 1) * tq]

    _flash_init(m_scr, acc_scr)
    first = jnp.maximum(q0 - NSA_WINDOW + 1, 0) // tq

    def win_body(c, carry):
        k0 = pl.multiple_of(c * tq, tq)
        qpos = q0 + _iota((tq, tq), 0)
        kpos = k0 + _iota((tq, tq), 1)
        band = jnp.where((kpos <= qpos) & (kpos > qpos - NSA_WINDOW), 0.0, NEG_INF)
        s = jnp.dot(q_raw, kwt_ref[0, :, pl.ds(k0, tq)], preferred_element_type=jnp.float32) + tall(band)
        _flash_update(0, s, vw_ref[0, pl.ds(k0, tq), :], m_scr, acc_scr)
        return carry

    lax.fori_loop(first, qi + 1, win_body, 0)
    o_win = _flash_out(0, acc_scr)
    for h in range(H):
        gates = 1.0 / (1.0 + jnp.exp(-g_ref[0][:, 3 * h:3 * h + 3]))
        o_ref[0, :, h * hd:(h + 1) * hd] = (gates[:, 0:1] * oc_scr[h] + gates[:, 1:2] * os_scr[h]
                                            + gates[:, 2:3] * o_win[h * tq:(h + 1) * tq])


def _nsa_call(q, kcmp, vcmp, kst, vs, kwt, vw, gate_logits):
    B, H, T, _ = q.shape
    tq = ATT_TILE
    ncp = kcmp.shape[1]
    n_sel = T // NSA_SEL_BLOCK
    qs, ks, vsp, out = _att_specs(B, H, T, tq, True)
    cmp_spec = pl.BlockSpec((1, ncp, HEAD_DIM), lambda b, i: (b, 0, 0))
    return pl.pallas_call(
        functools.partial(_nsa_kernel, tq=tq, ncp=ncp, n_sel=n_sel, topn=min(NSA_SEL_TOPN, n_sel)),
        grid=(B, T // tq),
        in_specs=[qs, cmp_spec, cmp_spec, ks, vsp, ks, vsp,
                  pl.BlockSpec((1, tq, LANES), lambda b, i: (b, i, 0))],
        out_specs=out,
        out_shape=jax.ShapeDtypeStruct((B, T, H * HEAD_DIM), jnp.float32),
        scratch_shapes=[pltpu.VMEM((H, tq, HEAD_DIM), jnp.float32), pltpu.VMEM((H, tq, HEAD_DIM), jnp.float32)]
        + _att_scratch(tq, stacked=True),
        compiler_params=_params("arbitrary", "arbitrary"),
        name="nsa_attention",
    )(q, kcmp, vcmp, kst, vs, kwt, vw, gate_logits)


def _sortable(x):
    b = int(np.float32(x).view(np.int32))
    return b ^ ((b >> 31) & 0x7FFFFFFF)


def _from_sortable(k):
    return lax.bitcast_convert_type(k ^ ((k >> 31) & 0x7FFFFFFF), jnp.float32)


def _to_sortable(x):
    k = lax.bitcast_convert_type(x, jnp.int32)
    return k ^ ((k >> 31) & 0x7FFFFFFF)


VALUE_STEPS = 24


def _dsa_kernel(q_ref, kt_ref, v_ref, iq_ref, ikt_ref, iw_ref, o_ref,
                st_scr, b_scr, m_scr, acc_scr, *, tq, topk, idx_scale):
    H, hd = GROUP_HEADS, HEAD_DIM
    qi = pl.program_id(1)
    q0 = qi * tq
    n_kc = qi + 1
    sub = 8

    def fold(x, reduce):
        return reduce(x.reshape(x.shape[0] // sub, sub, tq), axis=0)

    lane = _iota((tq, LANES), 1)
    quarter = lane >> _log2(DSA_IDX_DIM)
    per_group = LANES // DSA_IDX_DIM
    iq = iq_ref[0].astype(jnp.float32)
    iqt = jnp.concatenate(
        [jnp.where(quarter == (h % per_group), iq[:, (h // per_group) * LANES:(h // per_group + 1) * LANES], 0.0).T
         for h in range(DSA_IDX_HEADS)], axis=1).astype(MXU_DTYPE)
    iwt = iw_ref[0].T

    diag_causal = _iota((tq, tq), 0) <= _iota((tq, tq), 1)

    def score_chunk(c, ends, diagonal):
        top, bot = ends
        k0 = pl.multiple_of(c * tq, tq)
        ik = ikt_ref[0, :, pl.ds(k0, tq)].astype(jnp.float32).T
        sh = _dot(ik, iqt)
        acc = jnp.zeros((tq, tq), jnp.float32)
        for h in range(DSA_IDX_HEADS):
            acc = acc + iwt[h:h + 1, :] * jnp.maximum(sh[:, h * tq:(h + 1) * tq], 0.0)
        sc = low = acc * idx_scale + 0.0
        if diagonal:
            sc = jnp.where(diag_causal, sc, NEG_INF)
            low = jnp.where(diag_causal, low, np.inf)
        st_scr[pl.ds(k0, tq), :] = sc
        return jnp.maximum(top, fold(sc, jnp.max)), jnp.minimum(bot, fold(low, jnp.min))

    ends = lax.fori_loop(0, qi, functools.partial(score_chunk, diagonal=False),
                         (jnp.full((sub, tq), NEG_INF, jnp.float32), jnp.full((sub, tq), np.inf, jnp.float32)))
    top, bot = score_chunk(qi, ends, diagonal=True)
    row_max = jnp.max(top, axis=0, keepdims=True)
    row_min = jnp.min(bot, axis=0, keepdims=True)

    def count_ge(t, strict=False):
        above = (lambda a, b: a > b) if strict else (lambda a, b: a >= b)
        t_sub = jnp.broadcast_to(t, (sub, tq))

        def body(c, part):
            k0 = pl.multiple_of(c * tq, tq)
            sc = st_scr[pl.ds(k0, tq), :].reshape(tq // sub, sub, tq)
            return part + jnp.sum(jnp.where(above(sc, t_sub[None]), 1.0, 0.0), axis=0)

        part = lax.fori_loop(0, n_kc, body, jnp.zeros((sub, tq), jnp.float32))
        return jnp.sum(part, axis=0, keepdims=True)

    kf = float(topk)
    floor_key = _sortable(NEG_INF)

    def bis_cond(c):
        return c[-1]

    def bis_body(c):
        it, lo, hi, n_lo, _ = c
        active = _any(lo < hi)
        key_mid = (lo | hi) - ((lo ^ hi) >> 1)
        val_mid = _to_sortable(0.5 * (_from_sortable(lo) + _from_sortable(hi)))
        steps = jnp.zeros(lo.shape, jnp.int32) + it
        use_val = (val_mid > lo) & (val_mid <= hi) & (lo > floor_key) & (steps < VALUE_STEPS)
        mid = jnp.where(use_val, val_mid, key_mid)
        cnt = count_ge(_from_sortable(mid))
        ge = cnt >= kf
        lo = jnp.where(ge, mid, lo)
        n_lo = jnp.where(ge, cnt, n_lo)
        hi = jnp.where(cnt == kf, mid, jnp.where(ge, hi, mid - 1))
        return it + 1, lo, hi, n_lo, active

    zero = jnp.zeros((1, tq), jnp.float32)
    n_pos = count_ge(zero, strict=True)
    n_nonneg = count_ge(zero)
    n_causal = q0 + _iota((1, tq), 1) + 1
    lo_neg = jnp.where(n_causal >= topk, _to_sortable(row_min), floor_key)
    n_neg = jnp.where(n_causal >= topk, n_causal, n_kc * tq).astype(jnp.float32)
    is_pos = n_pos >= kf
    is_zero = n_nonneg >= kf
    pick = lambda p, z, n: jnp.where(is_pos, p, jnp.where(is_zero, z, n))
    lo0 = pick(_sortable(np.float32(1e-45)), _sortable(0.0), lo_neg)
    hi0 = pick(_to_sortable(row_max), _sortable(0.0), _sortable(-0.0) - 1)
    _, lo, _, n_ge, _ = lax.while_loop(
        bis_cond, bis_body, (jnp.int32(0), lo0, hi0, pick(n_pos, n_nonneg, n_neg), _any(lo0 < hi0)))
    thr = _from_sortable(lo)

    any_tie = _any(n_ge > kf)

    @pl.when(jnp.logical_not(any_tie))
    def _():
        def mask_chunk(c, carry, diagonal):
            k0 = pl.multiple_of(c * tq, tq)
            picked = st_scr[pl.ds(k0, tq), :] >= thr
            if diagonal:
                picked = picked & diag_causal
            b_scr[:, pl.ds(k0, tq)] = jnp.where(picked, 0.0, NEG_INF).T
            return carry
        lax.fori_loop(0, qi, functools.partial(mask_chunk, diagonal=False), 0)
        mask_chunk(qi, 0, diagonal=True)

    @pl.when(any_tie)
    def _():
        need = kf - count_ge(thr, strict=True)
        prefix = jnp.where(_iota((tq, tq), 1) <= _iota((tq, tq), 0), 1.0, 0.0)

        def mask_chunk(c, seen, diagonal):
            k0 = pl.multiple_of(c * tq, tq)
            sc = st_scr[pl.ds(k0, tq), :]
            eq = jnp.where(sc == thr, 1.0, 0.0)
            rank_eq = _dot(prefix, eq) + seen
            picked = (sc > thr) | ((sc == thr) & (rank_eq <= need))
            if diagonal:
                picked = picked & diag_causal
            b_scr[:, pl.ds(k0, tq)] = jnp.where(picked, 0.0, NEG_INF).T
            return seen + jnp.sum(fold(eq, jnp.sum), axis=0, keepdims=True)
        seen = lax.fori_loop(0, qi, functools.partial(mask_chunk, diagonal=False), jnp.zeros((1, tq), jnp.float32))
        mask_chunk(qi, seen, diagonal=True)

    q_all = jnp.concatenate([q_ref[0, h] for h in range(H)], axis=0)
    _flash_init(m_scr, acc_scr)

    def att_body(c, carry):
        k0 = pl.multiple_of(c * tq, tq)
        bias = b_scr[:, pl.ds(k0, tq)]
        s = jnp.dot(q_all, kt_ref[0, :, pl.ds(k0, tq)], preferred_element_type=jnp.float32)
        _flash_update(0, s + jnp.tile(bias, (H, 1)), v_ref[0, pl.ds(k0, tq), :], m_scr, acc_scr)
        return carry

    lax.fori_loop(0, n_kc, att_body, 0)
    out = _flash_out(0, acc_scr)
    for h in range(H):
        o_ref[0, :, h * hd:(h + 1) * hd] = out[h * tq:(h + 1) * tq]


def _dsa_call(q, kt, v, iq, ikt, iw):
    B, H, T, _ = q.shape
    tq = ATT_TILE
    topk = min(DSA_TOPK, T // 4)
    assert tq >= topk
    qs, ks, vs, out = _att_specs(B, H, T, tq, True)
    return pl.pallas_call(
        functools.partial(_dsa_kernel, tq=tq, topk=topk, idx_scale=(DSA_IDX_HEADS * DSA_IDX_DIM) ** -0.5),
        grid=(B, T // tq),
        in_specs=[qs, ks, vs, pl.BlockSpec((1, tq, 2 * LANES), lambda b, i: (b, i, 0)), ks,
                  pl.BlockSpec((1, tq, LANES), lambda b, i: (b, i, 0))],
        out_specs=out,
        out_shape=jax.ShapeDtypeStruct((B, T, H * HEAD_DIM), jnp.float32),
        scratch_shapes=[pltpu.VMEM((T, tq), jnp.float32), pltpu.VMEM((tq, T), jnp.float32)]
        + _att_scratch(tq, stacked=True),
        compiler_params=_params("arbitrary", "arbitrary"),
        name="dsa_attention",
    )(q, kt, v, iq, ikt, iw)


def _out_ffn_kernel(x_ref, ada_ref, o1_ref, o2_ref, o3_ref, o4_ref, gn_ref, w_ref,
                    fn_ref, wg_ref, wu_ref, wd_ref, *rest, final):
    if final:
        fg_ref, y_ref = rest
    else:
        (y_ref,) = rest
    a = ada_ref[0]
    gn = gn_ref[...]
    y = jnp.concatenate([_rms(o[0], gn[i:i + 1]).astype(MXU_DTYPE)
                         for i, o in enumerate((o1_ref, o2_ref, o3_ref, o4_ref))], axis=-1)
    x = x_ref[...] + a[5:6] * jnp.dot(y, w_ref[...], preferred_element_type=jnp.float32)
    x = _ffn_block(x, a, fn_ref[...], wg_ref, wu_ref, wd_ref, 6)
    if final:
        x = _rms(x, fg_ref[...])
    y_ref[...] = x


def _out_ffn_call(x2d, ada_l, groups, group_norm, w_out, norm_g, w_gate, w_up, w_down, B, T, final_gain=None):
    N, D = x2d.shape
    F = w_gate.shape[1]
    tm = ROW_TILE
    tpb = T // tm
    grp = pl.BlockSpec((1, tm, GROUP_WIDTH), lambda i: (i // tpb, i % tpb, 0))
    in_specs = [pl.BlockSpec((tm, D), lambda i: (i, 0)),
                pl.BlockSpec((1, N_ADA, D), lambda i: (i // tpb, 0, 0)),
                grp, grp, grp, grp,
                _resident((N_GROUPS, GROUP_WIDTH)), _resident((MIX_WIDTH, D)),
                _resident((1, D)), _resident((D, F)), _resident((D, F)), _resident((F, D))]
    args = [x2d, ada_l, *groups, group_norm, w_out.astype(MXU_DTYPE), norm_g.reshape(1, D),
            w_gate.astype(MXU_DTYPE), w_up.astype(MXU_DTYPE), w_down.astype(MXU_DTYPE)]
    final = final_gain is not None
    if final:
        in_specs.append(_resident((1, D)))
        args.append(final_gain.reshape(1, D))
    return pl.pallas_call(
        functools.partial(_out_ffn_kernel, final=final),
        grid=(N // tm,),
        in_specs=in_specs,
        out_specs=pl.BlockSpec((tm, D), lambda i: (i, 0)),
        out_shape=jax.ShapeDtypeStruct((N, D), jnp.float32),
        compiler_params=_params("arbitrary"),
        name="mixer_out_ffn",
    )(*args)


def _mixer_groups(x2d, ada_l, tables, mix_norm, w_in, mla_q_norm, mla_w_uq, mla_kv_norm, mla_w_uk, mla_w_uv,
                  nsa_pe_k, nsa_pe_v, nsa_cmp_k_w1, nsa_cmp_k_w2, nsa_cmp_v_w1, nsa_cmp_v_w2, B, T):
    (mq, mkt, mv, lq, lkt, lv, nq, nkc, nvc, nkst, nvs, nkwt, nvw, ngate,
     dq, dkt, dv, diq, dikt, diw) = _proj_call(
        x2d, ada_l, mix_norm, w_in, tables, mla_q_norm, mla_w_uq, mla_kv_norm, mla_w_uk, mla_w_uv, B, T)
    o_moba = _moba_call(mq, mkt, mv)
    o_mla = _mla_call(lq, lkt, lv)
    kcmp, vcmp = _cmp_call(nkc, nvc, nsa_pe_k, nsa_pe_v, nsa_cmp_k_w1, nsa_cmp_k_w2, nsa_cmp_v_w1, nsa_cmp_v_w2)
    o_nsa = _nsa_call(nq, kcmp, vcmp, nkst, nvs, nkwt, nvw, ngate)
    o_dsa = _dsa_call(dq, dkt, dv, diq, dikt, diw)
    return o_moba, o_mla, o_nsa, o_dsa


def kernel(x, c, ada_w, ada_b, ffn1_norm, ffn1_w_gate, ffn1_w_up, ffn1_w_down, mix_norm, w_in, mla_q_norm, mla_w_uq, mla_kv_norm, mla_w_uk, mla_w_uv, nsa_pe_k, nsa_pe_v, nsa_cmp_k_w1, nsa_cmp_k_w2, nsa_cmp_v_w1, nsa_cmp_v_w2, group_norm, w_out, ffn2_norm, ffn2_w_gate, ffn2_w_up, ffn2_w_down, final_norm):
    B, T, D = x.shape
    L = ada_w.shape[0]
    assert D == D_MODEL and T % ROW_TILE == 0 and T % ATT_TILE == 0
    assert ATT_TILE % MOBA_BLOCK == 0 and ATT_TILE % NSA_SEL_BLOCK == 0 and ATT_TILE >= NSA_WINDOW
    tpb = T // ROW_TILE
    ada = _ada_call(c, ada_w, ada_b)
    tables = _rope_tables(T)
    x2d = x.reshape(B * T, D)
    for l in range(L):
        x2d = _ffn_call(x2d, ada[l], ffn1_norm[l], ffn1_w_gate[l], ffn1_w_up[l], ffn1_w_down[l], 0, tpb)
        groups = _mixer_groups(x2d, ada[l], tables, mix_norm[l], w_in[l], mla_q_norm[l], mla_w_uq[l],
                               mla_kv_norm[l], mla_w_uk[l], mla_w_uv[l], nsa_pe_k[l], nsa_pe_v[l],
                               nsa_cmp_k_w1[l], nsa_cmp_k_w2[l], nsa_cmp_v_w1[l], nsa_cmp_v_w2[l], B, T)
        x2d = _out_ffn_call(x2d, ada[l], groups, group_norm[l], w_out[l], ffn2_norm[l], ffn2_w_gate[l],
                            ffn2_w_up[l], ffn2_w_down[l], B, T, final_gain=final_norm if l == L - 1 else None)
    return x2d.reshape(B, T, D)
```
